```python
import jax, jax.numpy as jnp
from jax import lax
import numpy as np

D_MODEL = 1024
BATCH = 8
SEQ = 4096
DEPTH = 1

D_MIX = D_MODEL
D_LRU = 512
LRU_BLOCKS = 8
LRU_BLOCK_W = D_LRU // LRU_BLOCKS
LRU_C = 8.0
D_GDN = D_MIX - D_LRU
GDN_HEADS = 4
GDN_HEAD_DIM = D_GDN // GDN_HEADS
CONV_W = 4
CHUNK = 64
N_GROUPS = 4
EXPERTS_PER_GROUP = 8
N_EXPERTS = N_GROUPS * EXPERTS_PER_GROUP
TOP_K = 2
D_EXPERT = 512
MOE_BLOCK = 256
D_PLE = 256
EPS = 1e-6
IN_COLS = 2 * D_LRU + 4 * D_GDN + 2 * GDN_HEADS

kernel_name = 'hymba_rglru_gdn_hmoe_layer'


def rmsnorm(x, g):
    xf = x.astype(jnp.float32)
    return xf * lax.rsqrt(jnp.mean(xf * xf, axis=-1, keepdims=True) + EPS) * g


def l2norm(x):
    return x * lax.rsqrt(jnp.sum(x * x, axis=-1, keepdims=True) + EPS)


def causal_depthwise_conv(x, w):
    s = x.shape[1]
    kw = w.shape[0]
    xp = jnp.pad(x, ((0, 0), (kw - 1, 0), (0, 0)))
    return sum(xp[:, j:j + s] * w[j] for j in range(kw))


def rglru_group(xb, gb, conv_w, conv_b, wa, ba, wi, bi, lam, out_g):
    bsz, s, _ = xb.shape
    xc = causal_depthwise_conv(xb, conv_w) + conv_b
    xblk = xc.reshape(bsz, s, LRU_BLOCKS, LRU_BLOCK_W)
    r = jax.nn.sigmoid(jnp.einsum('bsnc,ncd->bsnd', xblk, wa).reshape(bsz, s, D_LRU) + ba)
    i = jax.nn.sigmoid(jnp.einsum('bsnc,ncd->bsnd', xblk, wi).reshape(bsz, s, D_LRU) + bi)
    log_a = LRU_C * r * jax.nn.log_sigmoid(lam)
    a = jnp.exp(log_a)
    u = jnp.sqrt(-jnp.expm1(2.0 * log_a)) * (i * xc)

    def combine(c1, c2):
        a1, b1 = c1
        a2, b2 = c2
        return a1 * a2, a2 * b1 + b2

    _, h = lax.associative_scan(combine, (a, u), axis=1)
    return rmsnorm(h * jax.nn.gelu(gb), out_g)


def chunked_gated_delta_rule(q, k, v, beta, g):
    bsz, s, nh, dk = q.shape
    dv = v.shape[-1]
    n = s // CHUNK

    def to_chunks(t):
        return t.reshape(bsz, n, CHUNK, nh, -1).transpose(1, 0, 3, 2, 4)

    qc = to_chunks(q) * (dk ** -0.5)
    kc = to_chunks(k)
    vc = to_chunks(v)
    bc = to_chunks(beta[..., None])[..., 0]
    gc = jnp.cumsum(to_chunks(g[..., None])[..., 0], axis=-1)
    causal = jnp.tril(jnp.ones((CHUNK, CHUNK), dtype=bool))
    strict = jnp.tril(jnp.ones((CHUNK, CHUNK), dtype=bool), k=-1)
    diff = gc[..., :, None] - gc[..., None, :]
    decay = jnp.where(causal, jnp.exp(jnp.where(causal, diff, 0.0)), 0.0)
    kb = kc * bc[..., None]
    kkt = jnp.einsum('nbhid,nbhjd->nbhij', kb, kc) * decay
    m = jnp.eye(CHUNK, dtype=kkt.dtype) + jnp.where(strict, kkt, 0.0)
    rhs = jnp.concatenate([vc * bc[..., None], kb * jnp.exp(gc)[..., None]], axis=-1)
    sol = lax.linalg.triangular_solve(m, rhs, left_side=True, lower=True, unit_diagonal=True)
    u, w = sol[..., :dv], sol[..., dv:]
    qk = jnp.einsum('nbhid,nbhjd->nbhij', qc, kc) * decay
    q_dec = qc * jnp.exp(gc)[..., None]
    g_last = gc[..., -1]
    k_dec = kc * jnp.exp(g_last[..., None] - gc)[..., None]

    def step(state, xs):
        u_n, w_n, q_n, qk_n, k_n, gl_n = xs
        v_new = u_n - jnp.einsum('bhcd,bhde->bhce', w_n, state)
        o = jnp.einsum('bhcd,bhde->bhce', q_n, state) + jnp.einsum('bhij,bhje->bhie', qk_n, v_new)
        state = state * jnp.exp(gl_n)[..., None, None] + jnp.einsum('bhcd,bhce->bhde', k_n, v_new)
        return state, o

    s0 = jnp.zeros((bsz, nh, dk, dv), dtype=u.dtype)
    _, o = lax.scan(step, s0, (u, w, q_dec, qk, k_dec, g_last))
    return o.transpose(1, 0, 3, 2, 4).reshape(bsz, s, nh, dv)


def gdn_group(qkv, z, b, a, conv_w, a_log, dt_bias, out_g):
    bsz, s, _ = qkv.shape
    qkv = jax.nn.silu(causal_depthwise_conv(qkv, conv_w))
    q, k, v = jnp.split(qkv, 3, axis=-1)
    shp = (bsz, s, GDN_HEADS, GDN_HEAD_DIM)
    q = l2norm(q.reshape(shp))
    k = l2norm(k.reshape(shp))
    v = v.reshape(shp)
    beta = jax.nn.sigmoid(b)
    g = -jnp.exp(a_log) * jax.nn.softplus(a + dt_bias)
    o = chunked_gated_delta_rule(q, k, v, beta, g)
    o = rmsnorm(o, out_g) * jax.nn.silu(z.reshape(shp))
    return o.reshape(bsz, s, D_GDN)


def expert_dispatch(xf, expert_id, gate, w_g, w_u, w_d):
    t, d = xf.shape
    n_assign = t * TOP_K
    flat_e = expert_id.reshape(-1)
    order = jnp.argsort(flat_e)
    sorted_e = flat_e[order]
    tok = order // TOP_K
    w_sorted = gate.reshape(-1)[order]
    counts = jax.ops.segment_sum(jnp.ones_like(flat_e), flat_e, num_segments=N_EXPERTS)
    padded = (counts + MOE_BLOCK - 1) // MOE_BLOCK * MOE_BLOCK
    pad_end = jnp.cumsum(padded)
    pad_start = pad_end - padded
    start = jnp.cumsum(counts) - counts
    dest = pad_start[sorted_e] + jnp.arange(n_assign, dtype=sorted_e.dtype) - start[sorted_e]
    n_blocks = -(-n_assign // MOE_BLOCK) + N_EXPERTS
    buf_tok = jnp.full((n_blocks * MOE_BLOCK,), t, dtype=tok.dtype).at[dest].set(tok)
    block_expert = jnp.minimum(
        jnp.searchsorted(pad_end, jnp.arange(n_blocks, dtype=pad_end.dtype) * MOE_BLOCK, side='right'),
        N_EXPERTS - 1)
    xpad = jnp.concatenate([xf, jnp.zeros((1, d), xf.dtype)], axis=0)
    xs = xpad[buf_tok].reshape(n_blocks, MOE_BLOCK, d)

    def expert_block(args):
        xb, e = args
        hb = jax.nn.silu(xb @ w_g[e]) * (xb @ w_u[e])
        return hb @ w_d[e]

    ys = lax.map(expert_block, (xs, block_expert)).reshape(n_blocks * MOE_BLOCK, d)
    return jax.ops.segment_sum(ys[dest] * w_sorted[:, None], tok, num_segments=t)


def hier_moe(xn, w_rg, b_rg, w_re, b_re, w_g, w_u, w_d):
    bsz, s, d = xn.shape
    xf = xn.reshape(bsz * s, d)
    rows = jnp.arange(bsz * s)
    group_logits = (xf @ w_rg + b_rg).astype(jnp.float32)
    g_sel = jnp.argmax(group_logits, axis=-1)
    p_group = jax.nn.softmax(group_logits, axis=-1)[rows, g_sel][:, None]
    e_logits = (xf @ w_re + b_re).astype(jnp.float32).reshape(-1, N_GROUPS, EXPERTS_PER_GROUP)
    in_group = e_logits[rows, g_sel]
    top_p, top_i = lax.top_k(jax.nn.softmax(in_group, axis=-1), TOP_K)
    gate = p_group * top_p / jnp.sum(top_p, axis=-1, keepdims=True)
    expert_id = g_sel[:, None] * EXPERTS_PER_GROUP + top_i
    y = expert_dispatch(xf, expert_id, gate, w_g, w_u, w_d)
    return y.reshape(bsz, s, d)


def setup_inputs(seed: int = 0) -> dict:
    key = jax.random.key(seed)
    ks = jax.random.split(key, 40)
    L = DEPTH

    def nrm(i, shape, scale):
        return jax.random.normal(ks[i], shape, jnp.float32) * scale

    def gain(i, shape):
        return 1.0 + nrm(i, shape, 0.02)

    a0 = jax.random.uniform(ks[10], (L, D_LRU), jnp.float32, 0.9, 0.999)
    s0 = a0 ** (1.0 / LRU_C)
    lru_lambda = jnp.log(s0) - jnp.log1p(-s0)
    gdn_a_log = jnp.log(jax.random.uniform(ks[13], (L, GDN_HEADS), jnp.float32, 1.0, 16.0))
    dt = jnp.exp(jax.random.uniform(ks[14], (L, GDN_HEADS), jnp.float32, float(np.log(1e-3)), float(np.log(1e-1))))
    gdn_dt_bias = dt + jnp.log(-jnp.expm1(-dt))
    return {
        'x': nrm(0, (BATCH, SEQ, D_MODEL), 1.0),
        'p': nrm(1, (L, BATCH, SEQ, D_PLE), 1.0),
        'norm_mix': gain(2, (L, D_MODEL)),
        'w_in': nrm(3, (L, D_MODEL, IN_COLS), D_MODEL ** -0.5),
        'lru_conv_w': nrm(4, (L, CONV_W, D_LRU), CONV_W ** -0.5),
        'lru_conv_b': nrm(5, (L, D_LRU), 0.02),
        'lru_wa': nrm(6, (L, LRU_BLOCKS, LRU_BLOCK_W, LRU_BLOCK_W), LRU_BLOCK_W ** -0.5),
        'lru_ba': nrm(7, (L, D_LRU), 0.02),
        'lru_wi': nrm(8, (L, LRU_BLOCKS, LRU_BLOCK_W, LRU_BLOCK_W), LRU_BLOCK_W ** -0.5),
        'lru_bi': nrm(9, (L, D_LRU), 0.02),
        'lru_lambda': lru_lambda,
        'lru_out_norm': gain(11, (L, D_LRU)),
        'gdn_conv_w': nrm(12, (L, CONV_W, 3 * D_GDN), CONV_W ** -0.5),
        'gdn_a_log': gdn_a_log,
        'gdn_dt_bias': gdn_dt_bias,
        'gdn_out_norm': gain(15, (L, GDN_HEAD_DIM)),
        'w_out': nrm(16, (L, D_MIX, D_MODEL), D_MIX ** -0.5),
        'norm_ffn': gain(17, (L, D_MODEL)),
        'w_router_group': nrm(18, (L, D_MODEL, N_GROUPS), D_MODEL ** -0.5),
        'b_router_group': nrm(19, (L, N_GROUPS), 0.01),
        'w_router_expert': nrm(20, (L, D_MODEL, N_EXPERTS), D_MODEL ** -0.5),
        'b_router_expert': nrm(21, (L, N_EXPERTS), 0.01),
        'w_exp_gate': nrm(22, (L, N_EXPERTS, D_MODEL, D_EXPERT), D_MODEL ** -0.5),
        'w_exp_up': nrm(23, (L, N_EXPERTS, D_MODEL, D_EXPERT), D_MODEL ** -0.5),
        'w_exp_down': nrm(24, (L, N_EXPERTS, D_EXPERT, D_MODEL), D_EXPERT ** -0.5),
        'norm_ple': gain(25, (L, D_MODEL)),
        'w_ple_gate': nrm(26, (L, D_MODEL, D_MODEL), D_MODEL ** -0.5),
        'w_ple': nrm(27, (L, D_PLE, D_MODEL), D_PLE ** -0.5),
        'norm_final': gain(28, (D_MODEL,)),
    }


def reference(x, p, norm_mix, w_in, lru_conv_w, lru_conv_b, lru_wa, lru_ba, lru_wi, lru_bi, lru_lambda,
              lru_out_norm, gdn_conv_w, gdn_a_log, gdn_dt_bias, gdn_out_norm, w_out, norm_ffn,
              w_router_group, b_router_group, w_router_expert, b_router_expert, w_exp_gate, w_exp_up,
              w_exp_down, norm_ple, w_ple_gate, w_ple, norm_final):
    h = x.astype(jnp.float32)
    o_gate = D_LRU
    o_qkv = 2 * D_LRU
    o_z = o_qkv + 3 * D_GDN
    o_beta = o_z + D_GDN
    o_alpha = o_beta + GDN_HEADS
    for l in range(DEPTH):
        xn = rmsnorm(h, norm_mix[l])
        proj = xn @ w_in[l]
        y_lru = rglru_group(proj[..., :o_gate], proj[..., o_gate:o_qkv], lru_conv_w[l], lru_conv_b[l],
                            lru_wa[l], lru_ba[l], lru_wi[l], lru_bi[l], lru_lambda[l], lru_out_norm[l])
        y_gdn = gdn_group(proj[..., o_qkv:o_z], proj[..., o_z:o_beta], proj[..., o_beta:o_alpha],
                          proj[..., o_alpha:], gdn_conv_w[l], gdn_a_log[l], gdn_dt_bias[l], gdn_out_norm[l])
        h = h + jnp.concatenate([y_lru, y_gdn], axis=-1) @ w_out[l]
        h = h + hier_moe(rmsnorm(h, norm_ffn[l]), w_router_group[l], b_router_group[l], w_router_expert[l],
                         b_router_expert[l], w_exp_gate[l], w_exp_up[l], w_exp_down[l])
        gate = jax.nn.sigmoid(rmsnorm(h, norm_ple[l]) @ w_ple_gate[l])
        h = h + (p[l].astype(jnp.float32) @ w_ple[l]) * gate
    return rmsnorm(h, norm_final).astype(x.dtype)
```

```python
import functools

import jax
import jax.numpy as jnp
import numpy as np
from jax import lax
from jax.experimental import pallas as pl
from jax.experimental.pallas import tpu as pltpu

D_MODEL = 1024
D_LRU = 512
LRU_BLOCKS = 8
LRU_BLOCK_W = D_LRU // LRU_BLOCKS
LRU_C = 8.0
D_GDN = 512
GDN_HEADS = 4
GDN_HEAD_DIM = D_GDN // GDN_HEADS
CONV_W = 4
CHUNK = 64
N_GROUPS = 4
EXPERTS_PER_GROUP = 8
N_EXPERTS = N_GROUPS * EXPERTS_PER_GROUP
TOP_K = 2
D_EXPERT = 512
MOE_BLOCK = 256
D_PLE = 256
EPS = 1e-6

LANES = 128
SUBLANES = 8
PROJ_COLS = 2 * D_LRU + 4 * D_GDN + LANES
O_BA = 2 * D_LRU + 4 * D_GDN
VMEM_LIMIT = 56 * 1024 * 1024

BF16 = jnp.bfloat16
F32 = jnp.float32


def _cparams(*sem):
    return pltpu.CompilerParams(dimension_semantics=sem, vmem_limit_bytes=VMEM_LIMIT)


def _rms(x, g):
    return x * lax.rsqrt(jnp.mean(x * x, axis=-1, keepdims=True) + EPS) * g


def _dot(a, b):
    return jnp.dot(a.astype(BF16), b.astype(BF16), preferred_element_type=F32)


def _dot_nt(a, b):
    return lax.dot_general(a.astype(BF16), b.astype(BF16), (((1,), (1,)), ((), ())),
                           preferred_element_type=F32)


def _dot_tn(a, b):
    return lax.dot_general(a.astype(BF16), b.astype(BF16), (((0,), (0,)), ((), ())),
                           preferred_element_type=F32)


def _shift_rows(x, tail, j):
    r = pltpu.roll(x, j, axis=0)
    tr = pltpu.roll(tail, j, axis=0)
    row = lax.broadcasted_iota(jnp.int32, tail.shape, 0)
    top = jnp.where(row < j, tr, r[:SUBLANES])
    return jnp.concatenate([top, r[SUBLANES:]], axis=0)


def _causal_conv(x, tail, w):
    acc = x * w[CONV_W - 1:CONV_W]
    for j in range(1, CONV_W):
        acc = acc + _shift_rows(x, tail, j) * w[CONV_W - 1 - j:CONV_W - j]
    return acc


def _in_proj_kernel(x_ref, g_ref, w_ref, o_ref, *, col_tile):
    xn = _rms(x_ref[...], g_ref[...]).astype(BF16)
    for j in range(PROJ_COLS // col_tile):
        cols = slice(j * col_tile, (j + 1) * col_tile)
        o_ref[:, cols] = jnp.dot(xn, w_ref[:, cols], preferred_element_type=F32)


def _in_proj(x2, g, w, tm):
    t = x2.shape[0]
    return pl.pallas_call(
        functools.partial(_in_proj_kernel, col_tile=PROJ_COLS // 5),
        grid=(t // tm,),
        in_specs=[pl.BlockSpec((tm, D_MODEL), lambda i: (i, 0)),
                  pl.BlockSpec((1, D_MODEL), lambda i: (0, 0)),
                  pl.BlockSpec((D_MODEL, PROJ_COLS), lambda i: (0, 0))],
        out_specs=pl.BlockSpec((tm, PROJ_COLS), lambda i: (i, 0)),
        out_shape=jax.ShapeDtypeStruct((t, PROJ_COLS), F32),
        compiler_params=_cparams("parallel"),
        name="in_proj",
    )(x2, g, w)


def _lru_kernel(x_ref, gate_ref, cw_ref, cb_ref, wg_ref, bg_ref, lam_ref, og_ref, o_ref,
                tail_ref, h_ref):
    ts = x_ref.shape[0]

    @pl.when(pl.program_id(1) == 0)
    def _():
        tail_ref[...] = jnp.zeros_like(tail_ref)
        h_ref[...] = jnp.zeros_like(h_ref)

    x = x_ref[...]
    xc = _causal_conv(x, tail_ref[...], cw_ref[...]) + cb_ref[...]
    tail_ref[...] = x[ts - SUBLANES:]
    gates = jax.nn.sigmoid(_dot(xc, wg_ref[...]) + bg_ref[...])
    r = gates[:, :D_LRU]
    i = gates[:, D_LRU:]
    log_a = LRU_C * r * jax.nn.log_sigmoid(lam_ref[...])
    a = jnp.exp(log_a)
    th = jnp.tanh(log_a)
    u = jnp.sqrt(-2.0 * th / (1.0 - th)) * (i * xc)
    row = lax.broadcasted_iota(jnp.int32, a.shape, 0)
    d = 1
    while d < ts:
        keep = row >= d
        a_prev = jnp.where(keep, pltpu.roll(a, d, axis=0), 1.0)
        u_prev = jnp.where(keep, pltpu.roll(u, d, axis=0), 0.0)
        u = a * u_prev + u
        a = a * a_prev
        d *= 2
    h = a * h_ref[...] + u
    h_ref[...] = h[ts - 1:]
    y = h * jax.nn.gelu(gate_ref[...])
    o_ref[...] = _rms(y, og_ref[...]).astype(o_ref.dtype)


def _lru(proj, cw, cb, wg, bg, lam, og, bsz, seq, ts):
    nt = seq // ts
    row = lambda b, s: b * nt + s
    vec = lambda n: pl.BlockSpec((1, n), lambda b, s: (0, 0))
    return pl.pallas_call(
        _lru_kernel,
        grid=(bsz, nt),
        in_specs=[pl.BlockSpec((ts, D_LRU), lambda b, s: (row(b, s), 0)),
                  pl.BlockSpec((ts, D_LRU), lambda b, s: (row(b, s), 1)),
                  pl.BlockSpec((CONV_W, D_LRU), lambda b, s: (0, 0)),
                  vec(D_LRU),
                  pl.BlockSpec((D_LRU, 2 * D_LRU), lambda b, s: (0, 0)),
                  vec(2 * D_LRU), vec(D_LRU), vec(D_LRU)],
        out_specs=pl.BlockSpec((ts, D_LRU), lambda b, s: (row(b, s), 0)),
        out_shape=jax.ShapeDtypeStruct((bsz * seq, D_LRU), BF16),
        scratch_shapes=[pltpu.VMEM((SUBLANES, D_LRU), F32), pltpu.VMEM((1, D_LRU), F32)],
        compiler_params=_cparams("parallel", "arbitrary"),
        name="rglru",
    )(proj, proj, cw, cb, wg, bg, lam, og)


def _gdn_kernel(q_ref, k_ref, v_ref, z_ref, ba_ref, cw_ref, alog_ref, dtb_ref, og_ref, o_ref,
                qt_ref, kt_ref, vt_ref, qs_ref, ks_ref, vs_ref, bs_ref, gs_ref, st_ref):
    ts = q_ref.shape[0]
    dk = GDN_HEAD_DIM

    @pl.when(pl.program_id(1) == 0)
    def _():
        qt_ref[...] = jnp.zeros_like(qt_ref)
        kt_ref[...] = jnp.zeros_like(kt_ref)
        vt_ref[...] = jnp.zeros_like(vt_ref)
        st_ref[...] = jnp.zeros_like(st_ref)

    def conv_silu(x_ref, tail_ref, part):
        x = x_ref[...]
        y = _causal_conv(x, tail_ref[...], cw_ref[part])
        tail_ref[...] = x[ts - SUBLANES:]
        return jax.nn.silu(y)

    def l2n(x, scale):
        parts = []
        for h in range(GDN_HEADS):
            xh = x[:, h * dk:(h + 1) * dk]
            parts.append(xh * (lax.rsqrt(jnp.sum(xh * xh, axis=-1, keepdims=True) + EPS) * scale))
        return jnp.concatenate(parts, axis=1)

    qs_ref[...] = l2n(conv_silu(q_ref, qt_ref, 0), dk ** -0.5)
    ks_ref[...] = l2n(conv_silu(k_ref, kt_ref, 1), 1.0)
    vs_ref[...] = conv_silu(v_ref, vt_ref, 2)
    ba = ba_ref[...]
    bs_ref[...] = jax.nn.sigmoid(ba)
    gs_ref[...] = -jnp.exp(alog_ref[...]) * jax.nn.softplus(ba + dtb_ref[...])

    ri = lax.broadcasted_iota(jnp.int32, (CHUNK, CHUNK), 0)
    ci = lax.broadcasted_iota(jnp.int32, (CHUNK, CHUNK), 1)
    causal = ri >= ci
    strict = ri > ci
    tril = causal.astype(F32)
    og = og_ref[...]

    def chunk_body(c, carry):
        rows = pl.ds(pl.multiple_of(c * CHUNK, CHUNK), CHUNK)
        g_all = gs_ref[rows, :]
        gc_all = jnp.dot(tril, g_all, precision=lax.Precision.HIGHEST, preferred_element_type=F32)
        gc_t = gc_all.T
        eg_all = jnp.exp(gc_all)
        gl_all = gc_all[CHUNK - 1:CHUNK, :]
        kd_all = jnp.exp(gl_all - gc_all)
        egl_all = jnp.exp(gl_all)
        b_all = bs_ref[rows, :]
        for h in range(GDN_HEADS):
            cols = slice(h * dk, (h + 1) * dk)
            gl = GDN_HEADS + h
            qh = qs_ref[rows, cols]
            kh = ks_ref[rows, cols]
            vh = vs_ref[rows, cols]
            beta = b_all[:, h:h + 1]
            diff = gc_all[:, gl:gl + 1] - gc_t[gl:gl + 1, :]
            decay = jnp.where(causal, jnp.exp(jnp.where(causal, diff, 0.0)), 0.0)
            kb = kh * beta
            a1 = jnp.where(strict, _dot_nt(kb, kh) * decay, 0.0)
            x = jnp.concatenate([vh * beta, kb * eg_all[:, gl:gl + 1]], axis=1)
            x = x - _dot(a1, x)
            ap = a1
            p = 2
            while p < CHUNK:
                ap = _dot(ap, ap)
                x = x + _dot(ap, x)
                p *= 2
            u = x[:, :dk]
            w = x[:, dk:]
            qk = _dot_nt(qh, kh) * decay
            s = st_ref[h]
            v_new = u - _dot(w, s)
            o = _dot(qh * eg_all[:, gl:gl + 1], s) + _dot(qk, v_new)
            st_ref[h] = s * egl_all[:, gl:gl + 1] + _dot_tn(kh * kd_all[:, gl:gl + 1], v_new)
            zh = z_ref[rows, cols]
            o_ref[rows, cols] = (_rms(o, og) * jax.nn.silu(zh)).astype(o_ref.dtype)
        return carry

    lax.fori_loop(0, ts // CHUNK, chunk_body, 0)


def _gdn(proj, cw, alog, dtb, og, bsz, seq, ts):
    nt = seq // ts
    row = lambda b, s: b * nt + s
    col = lambda c: pl.BlockSpec((ts, D_GDN), lambda b, s: (row(b, s), c))
    c0 = 2 * D_LRU // D_GDN
    return pl.pallas_call(
        _gdn_kernel,
        grid=(bsz, nt),
        in_specs=[col(c0), col(c0 + 1), col(c0 + 2), col(c0 + 3),
                  pl.BlockSpec((ts, LANES), lambda b, s: (row(b, s), O_BA // LANES)),
                  pl.BlockSpec((3, CONV_W, D_GDN), lambda b, s: (0, 0, 0)),
                  pl.BlockSpec((1, LANES), lambda b, s: (0, 0)),
                  pl.BlockSpec((1, LANES), lambda b, s: (0, 0)),
                  pl.BlockSpec((1, GDN_HEAD_DIM), lambda b, s: (0, 0))],
        out_specs=pl.BlockSpec((ts, D_GDN), lambda b, s: (row(b, s), 0)),
        out_shape=jax.ShapeDtypeStruct((bsz * seq, D_GDN), BF16),
        scratch_shapes=[pltpu.VMEM((SUBLANES, D_GDN), F32)] * 3
        + [pltpu.VMEM((ts, D_GDN), F32)] * 3
        + [pltpu.VMEM((ts, LANES), F32)] * 2
        + [pltpu.VMEM((GDN_HEADS, GDN_HEAD_DIM, GDN_HEAD_DIM), F32)],
        compiler_params=_cparams("parallel", "arbitrary"),
        name="gdn",
    )(proj, proj, proj, proj, proj, cw, alog, dtb, og)


def _out_router_kernel(x_ref, yl_ref, yg_ref, wo_ref, g_ref, wr_ref, br_ref, h_ref, xn_ref, rt_ref):
    h = x_ref[...] + jnp.dot(yl_ref[...], wo_ref[:D_LRU, :], preferred_element_type=F32) \
        + jnp.dot(yg_ref[...], wo_ref[D_LRU:, :], preferred_element_type=F32)
    h_ref[...] = h
    xn = _rms(h, g_ref[...])
    xn_ref[...] = xn
    logits = jnp.dot(xn, wr_ref[...], precision=lax.Precision.HIGHEST,
                     preferred_element_type=F32) + br_ref[...]
    lane = lax.broadcasted_iota(jnp.int32, logits.shape, 1).astype(F32)
    big = jnp.float32(2 * LANES)
    ninf = jnp.float32(-jnp.inf)

    def top1(vals):
        m = jnp.max(vals, axis=-1, keepdims=True)
        return m, jnp.min(jnp.where(vals == m, lane, big), axis=-1, keepdims=True)

    gl = jnp.where(lane < N_GROUPS, logits, ninf)
    gmax, gsel = top1(gl)
    p_group = 1.0 / jnp.sum(jnp.exp(gl - gmax), axis=-1, keepdims=True)
    lo = N_GROUPS + EXPERTS_PER_GROUP * gsel
    el = jnp.where((lane >= lo) & (lane < lo + EXPERTS_PER_GROUP), logits, ninf)
    m1, i1 = top1(el)
    m2, i2 = top1(jnp.where(lane == i1, ninf, el))
    r = jnp.exp(m2 - m1)
    g1 = p_group / (1.0 + r)
    g2 = p_group * r / (1.0 + r)
    rt = jnp.where(lane == 0, i1 - N_GROUPS,
                   jnp.where(lane == 1, i2 - N_GROUPS,
                             jnp.where(lane == 2, g1, jnp.where(lane == 3, g2, 0.0))))
    rt_ref[...] = rt


def _out_router(x2, yl, yg, wo, g, wr, br, tm):
    t = x2.shape[0]
    tile = lambda n: pl.BlockSpec((tm, n), lambda i: (i, 0))
    full = lambda a, b: pl.BlockSpec((a, b), lambda i: (0, 0))
    return pl.pallas_call(
        _out_router_kernel,
        grid=(t // tm,),
        in_specs=[tile(D_MODEL), tile(D_LRU), tile(D_GDN), full(D_MODEL, D_MODEL), full(1, D_MODEL),
                  full(D_MODEL, LANES), full(1, LANES)],
        out_specs=[tile(D_MODEL), tile(D_MODEL), tile(LANES)],
        out_shape=[jax.ShapeDtypeStruct((t, D_MODEL), F32), jax.ShapeDtypeStruct((t, D_MODEL), F32),
                   jax.ShapeDtypeStruct((t, LANES), F32)],
        compiler_params=_cparams("parallel"),
        name="out_router",
    )(x2, yl, yg, wo, g, wr, br)


def _slots_kernel(rt_ref, dest_ref, pend_ref, cnt_ref):
    phase = pl.program_id(0)
    i = pl.program_id(1)
    tm = rt_ref.shape[0]
    rt = rt_ref[...]
    lane = lax.broadcasted_iota(jnp.int32, rt.shape, 1)
    e0 = rt[:, 0:1].astype(jnp.int32)
    e1 = rt[:, 1:2].astype(jnp.int32)
    member = ((lane == e0) | (lane == e1)).astype(BF16)

    @pl.when((phase == 0) & (i == 0))
    def _():
        cnt_ref[...] = jnp.zeros_like(cnt_ref)

    @pl.when(phase == 0)
    def _():
        ones = jnp.ones((SUBLANES, tm), BF16)
        cnt_ref[...] += jnp.dot(ones, member, preferred_element_type=F32)

    @pl.when((phase == 1) & (i == 0))
    def _():
        cnt = cnt_ref[...]
        padded = jnp.ceil(cnt / MOE_BLOCK) * MOE_BLOCK
        l8 = lax.broadcasted_iota(jnp.int32, cnt.shape, 1)
        incl = padded
        d = 1
        while d < LANES:
            incl = incl + jnp.where(l8 >= d, pltpu.roll(incl, d, axis=1), 0.0)
            d *= 2
        pend_ref[...] = incl
        cnt_ref[...] = incl - padded

    @pl.when(phase == 1)
    def _():
        ri = lax.broadcasted_iota(jnp.int32, (tm, tm), 0)
        ci = lax.broadcasted_iota(jnp.int32, (tm, tm), 1)
        before = (ri > ci).astype(BF16)
        start = cnt_ref[...]
        pos = start[0:1, :] + jnp.dot(before, member, preferred_element_type=F32)
        d0 = jnp.sum(jnp.where(lane == e0, pos, 0.0), axis=-1, keepdims=True)
        d1 = jnp.sum(jnp.where(lane == e1, pos, 0.0), axis=-1, keepdims=True)
        dest_ref[...] = jnp.where(lane == 0, d0, jnp.where(lane == 1, d1, 0.0)).astype(jnp.int32)
        cnt_ref[...] = start + jnp.dot(jnp.ones((SUBLANES, tm), BF16), member, preferred_element_type=F32)


def _slots(rt, tm):
    t = rt.shape[0]
    return pl.pallas_call(
        _slots_kernel,
        grid=(2, t // tm),
        in_specs=[pl.BlockSpec((tm, LANES), lambda p, i: (i, 0))],
        out_specs=[pl.BlockSpec((tm, LANES), lambda p, i: (i * p, 0)),
                   pl.BlockSpec((SUBLANES, LANES), lambda p, i: (0, 0))],
        out_shape=[jax.ShapeDtypeStruct((t, LANES), jnp.int32),
                   jax.ShapeDtypeStruct((SUBLANES, LANES), F32)],
        scratch_shapes=[pltpu.VMEM((SUBLANES, LANES), F32)],
        compiler_params=_cparams("arbitrary", "arbitrary"),
        name="moe_slots",
    )(rt)


def _row_copy(src_ref, src_row, dst_ref, dst_row, sem):
    return pltpu.make_async_copy(src_ref.at[pl.ds(src_row, 1)], dst_ref.at[pl.ds(dst_row, 1)], sem)


def _dispatch_kernel(dest_ref, xn_ref, xs_in_ref, xs_ref, sem):
    del xs_in_ref
    tm = xn_ref.shape[0]
    base = pl.program_id(0) * tm * TOP_K

    def start(r, carry):
        for k in range(TOP_K):
            _row_copy(xn_ref, r, xs_ref, dest_ref[base + TOP_K * r + k], sem).start()
        return carry

    lax.fori_loop(0, tm, start, 0)

    def wait(r, carry):
        for k in range(TOP_K):
            _row_copy(xn_ref, r, xs_ref, dest_ref[base + TOP_K * r + k], sem).wait()
        return carry

    lax.fori_loop(0, tm, wait, 0)


def _dispatch(dest, xn, xs_init, tm):
    t = xn.shape[0]
    return pl.pallas_call(
        _dispatch_kernel,
        grid_spec=pltpu.PrefetchScalarGridSpec(
            num_scalar_prefetch=1,
            grid=(t // tm,),
            in_specs=[pl.BlockSpec((tm, D_MODEL), lambda i, d: (i, 0)),
                      pl.BlockSpec(memory_space=pl.ANY)],
            out_specs=pl.BlockSpec(memory_space=pl.ANY),
            scratch_shapes=[pltpu.SemaphoreType.DMA]),
        out_shape=jax.ShapeDtypeStruct(xs_init.shape, xs_init.dtype),
        input_output_aliases={2: 0},
        compiler_params=_cparams("arbitrary"),
        name="moe_dispatch",
    )(dest, xn, xs_init)


def _experts_kernel(be_ref, na_ref, xs_ref, wg_ref, wu_ref, wd_ref, ys_ref, wgb_ref, wub_ref, wdb_ref):
    b = pl.program_id(0)
    prev = be_ref[jnp.maximum(b - 1, 0)]

    @pl.when((b == 0) | (be_ref[b] != prev))
    def _():
        wgb_ref[...] = wg_ref[0].astype(BF16)
        wub_ref[...] = wu_ref[0].astype(BF16)
        wdb_ref[...] = wd_ref[0].astype(BF16)

    @pl.when(b < na_ref[0])
    def _():
        xb = xs_ref[...].astype(BF16)
        hg = jnp.dot(xb, wgb_ref[...], preferred_element_type=F32)
        hu = jnp.dot(xb, wub_ref[...], preferred_element_type=F32)
        hb = (jax.nn.silu(hg) * hu).astype(BF16)
        ys_ref[...] = jnp.dot(hb, wdb_ref[...], preferred_element_type=F32)

    @pl.when(b >= na_ref[0])
    def _():
        ys_ref[...] = jnp.zeros_like(ys_ref)


def _experts(block_expert, n_active, xs, wg, wu, wd):
    nb = xs.shape[0] // MOE_BLOCK
    blk = lambda b, be, na: (jnp.minimum(b, na[0] - 1), 0)
    wspec = lambda a, c: pl.BlockSpec((1, a, c), lambda b, be, na: (be[b], 0, 0))
    return pl.pallas_call(
        _experts_kernel,
        grid_spec=pltpu.PrefetchScalarGridSpec(
            num_scalar_prefetch=2,
            grid=(nb,),
            in_specs=[pl.BlockSpec((MOE_BLOCK, D_MODEL), blk),
                      wspec(D_MODEL, D_EXPERT), wspec(D_MODEL, D_EXPERT), wspec(D_EXPERT, D_MODEL)],
            out_specs=pl.BlockSpec((MOE_BLOCK, D_MODEL), lambda b, be, na: (b, 0)),
            scratch_shapes=[pltpu.VMEM((D_MODEL, D_EXPERT), BF16), pltpu.VMEM((D_MODEL, D_EXPERT), BF16),
                            pltpu.VMEM((D_EXPERT, D_MODEL), BF16)]),
        out_shape=jax.ShapeDtypeStruct(xs.shape, F32),
        compiler_params=_cparams("arbitrary"),
        name="moe_experts",
    )(block_expert, n_active, xs, wg, wu, wd)


def _combine_kernel(dest_ref, h_ref, rt_ref, p_ref, ys_ref, gp_ref, wpg_ref, wp_ref, gf_ref, o_ref,
                    y0_ref, y1_ref, sem):
    tm = h_ref.shape[0]
    base = pl.program_id(0) * tm * TOP_K
    bufs = (y0_ref, y1_ref)

    def start(r, carry):
        for k in range(TOP_K):
            _row_copy(ys_ref, dest_ref[base + TOP_K * r + k], bufs[k], r, sem).start()
        return carry

    lax.fori_loop(0, tm, start, 0)

    def wait(r, carry):
        for k in range(TOP_K):
            _row_copy(ys_ref, dest_ref[base + TOP_K * r + k], bufs[k], r, sem).wait()
        return carry

    lax.fori_loop(0, tm, wait, 0)

    rt = rt_ref[...]
    h = h_ref[...] + (y0_ref[...] * rt[:, 2:3] + y1_ref[...] * rt[:, 3:4])
    gate = jax.nn.sigmoid(_dot(_rms(h, gp_ref[...]), wpg_ref[...]))
    h = h + _dot(p_ref[...], wp_ref[...]) * gate
    o_ref[...] = _rms(h, gf_ref[...])


def _combine(dest, h1, rt, p2, ys, gp, wpg, wp, gf, tm):
    t = h1.shape[0]
    tile = lambda n: pl.BlockSpec((tm, n), lambda i, d: (i, 0))
    full = lambda a, b: pl.BlockSpec((a, b), lambda i, d: (0, 0))
    return pl.pallas_call(
        _combine_kernel,
        grid_spec=pltpu.PrefetchScalarGridSpec(
            num_scalar_prefetch=1,
            grid=(t // tm,),
            in_specs=[tile(D_MODEL), tile(LANES), tile(D_PLE), pl.BlockSpec(memory_space=pl.ANY),
                      full(1, D_MODEL), full(D_MODEL, D_MODEL), full(D_PLE, D_MODEL), full(1, D_MODEL)],
            out_specs=tile(D_MODEL),
            scratch_shapes=[pltpu.VMEM((tm, D_MODEL), F32), pltpu.VMEM((tm, D_MODEL), F32),
                            pltpu.SemaphoreType.DMA]),
        out_shape=jax.ShapeDtypeStruct((t, D_MODEL), F32),
        compiler_params=_cparams("arbitrary"),
        name="moe_combine_ple",
    )(dest, h1, rt, p2, ys, gp, wpg, wp, gf)


def _block_diag(w):
    n, c, d = w.shape
    eye = jnp.eye(n, dtype=w.dtype)
    return (w[:, :, None, :] * eye[:, None, :, None]).reshape(n * c, n * d)


def _lane_row(vals, offset):
    return jnp.zeros((1, LANES), F32).at[0, offset:offset + vals.shape[0]].set(vals)


def kernel(x, p, norm_mix, w_in, lru_conv_w, lru_conv_b, lru_wa, lru_ba, lru_wi, lru_bi, lru_lambda,
           lru_out_norm, gdn_conv_w, gdn_a_log, gdn_dt_bias, gdn_out_norm, w_out, norm_ffn,
           w_router_group, b_router_group, w_router_expert, b_router_expert, w_exp_gate, w_exp_up,
           w_exp_down, norm_ple, w_ple_gate, w_ple, norm_final):
    bsz, seq, d = x.shape
    t = bsz * seq
    depth = w_in.shape[0]
    assert depth == 1, "the final norm is fused into the last layer's combine kernel"
    n_blocks = -(-t * TOP_K // MOE_BLOCK) + N_EXPERTS
    row = lambda v: v.reshape(1, -1).astype(F32)
    h = x.reshape(t, d).astype(F32)
    for l in range(depth):
        w_in_p = jnp.pad(w_in[l], ((0, 0), (0, PROJ_COLS - w_in.shape[2]))).astype(BF16)
        proj = _in_proj(h, row(norm_mix[l]), w_in_p, 512)
        w_gates = jnp.concatenate([_block_diag(lru_wa[l]), _block_diag(lru_wi[l])], axis=1).astype(BF16)
        b_gates = jnp.concatenate([lru_ba[l], lru_bi[l]]).reshape(1, -1)
        y_lru = _lru(proj, lru_conv_w[l], row(lru_conv_b[l]), w_gates, b_gates, row(lru_lambda[l]),
                     row(lru_out_norm[l]), bsz, seq, 512)
        cw = gdn_conv_w[l].reshape(CONV_W, 3, D_GDN).transpose(1, 0, 2)
        y_gdn = _gdn(proj, cw, _lane_row(gdn_a_log[l], GDN_HEADS), _lane_row(gdn_dt_bias[l], GDN_HEADS),
                     row(gdn_out_norm[l]), bsz, seq, 512)
        w_r = jnp.pad(jnp.concatenate([w_router_group[l], w_router_expert[l]], axis=1),
                      ((0, 0), (0, LANES - N_GROUPS - N_EXPERTS)))
        b_r = _lane_row(jnp.concatenate([b_router_group[l], b_router_expert[l]]), 0)
        h1, xn2, rt = _out_router(h, y_lru, y_gdn, w_out[l].astype(BF16), row(norm_ffn[l]), w_r, b_r, 512)
        dest2, pad_end = _slots(rt, 512)
        dest = dest2[:, :TOP_K].reshape(-1)
        pad_end = pad_end[0, :N_EXPERTS].astype(jnp.int32)
        block_start = jnp.arange(n_blocks, dtype=jnp.int32) * MOE_BLOCK
        block_expert = jnp.minimum(jnp.sum(pad_end[None, :] <= block_start[:, None], axis=1),
                                   N_EXPERTS - 1).astype(jnp.int32)
        n_active = (pad_end[N_EXPERTS - 1:] // MOE_BLOCK).astype(jnp.int32)
        xs = _dispatch(dest, xn2, jnp.zeros((n_blocks * MOE_BLOCK, d), F32), 256)
        ys = _experts(block_expert, n_active, xs, w_exp_gate[l], w_exp_up[l], w_exp_down[l])
        h = _combine(dest, h1, rt, p[l].reshape(t, -1).astype(F32), ys, row(norm_ple[l]),
                     w_ple_gate[l].astype(BF16), w_ple[l].astype(BF16), row(norm_final), 256)
    return h.reshape(bsz, seq, d).astype(x.dtype)
```

```python
import functools

import jax
import jax.numpy as jnp
import numpy as np
from jax import lax
from jax.experimental import pallas as pl
from jax.experimental.pallas import tpu as pltpu

D_MODEL = 1024
D_LRU = 512
LRU_BLOCKS = 8
LRU_BLOCK_W = D_LRU // LRU_BLOCKS
LRU_C = 8.0
D_GDN = 512
GDN_HEADS = 4
GDN_HEAD_DIM = D_GDN // GDN_HEADS
CONV_W = 4
CHUNK = 64
N_GROUPS = 4
EXPERTS_PER_GROUP = 8
N_EXPERTS = N_GROUPS * EXPERTS_PER_GROUP
TOP_K = 2
D_EXPERT = 512
MOE_BLOCK = 256
D_PLE = 256
EPS = 1e-6

LANES = 128
SUBLANES = 8
PROJ_COLS = 2 * D_LRU + 4 * D_GDN + LANES
O_BA = 2 * D_LRU + 4 * D_GDN
VMEM_LIMIT = 56 * 1024 * 1024

BF16 = jnp.bfloat16
F32 = jnp.float32


def _cparams(*sem):
    return pltpu.CompilerParams(dimension_semantics=sem, vmem_limit_bytes=VMEM_LIMIT)


def _rms(x, g):
    return x * lax.rsqrt(jnp.mean(x * x, axis=-1, keepdims=True) + EPS) * g


def _dot(a, b):
    return jnp.dot(a.astype(BF16), b.astype(BF16), preferred_element_type=F32)


def _dot_nt(a, b):
    return lax.dot_general(a.astype(BF16), b.astype(BF16), (((1,), (1,)), ((), ())),
                           preferred_element_type=F32)


def _dot_tn(a, b):
    return lax.dot_general(a.astype(BF16), b.astype(BF16), (((0,), (0,)), ((), ())),
                           preferred_element_type=F32)


def _shift_rows(x, tail, j):
    r = pltpu.roll(x, j, axis=0)
    tr = pltpu.roll(tail, j, axis=0)
    row = lax.broadcasted_iota(jnp.int32, tail.shape, 0)
    top = jnp.where(row < j, tr, r[:SUBLANES])
    return jnp.concatenate([top, r[SUBLANES:]], axis=0)


def _causal_conv(x, tail, w):
    acc = x * w[CONV_W - 1:CONV_W]
    for j in range(1, CONV_W):
        acc = acc + _shift_rows(x, tail, j) * w[CONV_W - 1 - j:CONV_W - j]
    return acc


def _in_proj_kernel(x_ref, g_ref, w_ref, o_ref, *, col_tile):
    xn = _rms(x_ref[...], g_ref[...]).astype(BF16)
    for j in range(PROJ_COLS // col_tile):
        cols = slice(j * col_tile, (j + 1) * col_tile)
        o_ref[:, cols] = jnp.dot(xn, w_ref[:, cols], preferred_element_type=F32)


def _in_proj(x2, g, w, tm):
    t = x2.shape[0]
    return pl.pallas_call(
        functools.partial(_in_proj_kernel, col_tile=PROJ_COLS // 5),
        grid=(t // tm,),
        in_specs=[pl.BlockSpec((tm, D_MODEL), lambda i: (i, 0)),
                  pl.BlockSpec((1, D_MODEL), lambda i: (0, 0)),
                  pl.BlockSpec((D_MODEL, PROJ_COLS), lambda i: (0, 0))],
        out_specs=pl.BlockSpec((tm, PROJ_COLS), lambda i: (i, 0)),
        out_shape=jax.ShapeDtypeStruct((t, PROJ_COLS), F32),
        compiler_params=_cparams("parallel"),
        name="in_proj",
    )(x2, g, w)


def _lru_kernel(x_ref, gate_ref, cw_ref, cb_ref, wg_ref, bg_ref, lam_ref, og_ref, o_ref,
                tail_ref, h_ref):
    ts = x_ref.shape[0]

    @pl.when(pl.program_id(1) == 0)
    def _():
        tail_ref[...] = jnp.zeros_like(tail_ref)
        h_ref[...] = jnp.zeros_like(h_ref)

    x = x_ref[...]
    xc = _causal_conv(x, tail_ref[...], cw_ref[...]) + cb_ref[...]
    tail_ref[...] = x[ts - SUBLANES:]
    gates = jax.nn.sigmoid(_dot(xc, wg_ref[...]) + bg_ref[...])
    r = gates[:, :D_LRU]
    i = gates[:, D_LRU:]
    log_a = LRU_C * r * jax.nn.log_sigmoid(lam_ref[...])
    a = jnp.exp(log_a)
    th = jnp.tanh(log_a)
    u = jnp.sqrt(-2.0 * th / (1.0 - th)) * (i * xc)
    row = lax.broadcasted_iota(jnp.int32, a.shape, 0)
    d = 1
    while d < ts:
        keep = row >= d
        a_prev = jnp.where(keep, pltpu.roll(a, d, axis=0), 1.0)
        u_prev = jnp.where(keep, pltpu.roll(u, d, axis=0), 0.0)
        u = a * u_prev + u
        a = a * a_prev
        d *= 2
    h = a * h_ref[...] + u
    h_ref[...] = h[ts - 1:]
    y = h * jax.nn.gelu(gate_ref[...])
    o_ref[...] = _rms(y, og_ref[...]).astype(o_ref.dtype)


def _lru(proj, cw, cb, wg, bg, lam, og, bsz, seq, ts):
    nt = seq // ts
    row = lambda b, s: b * nt + s
    vec = lambda n: pl.BlockSpec((1, n), lambda b, s: (0, 0))
    return pl.pallas_call(
        _lru_kernel,
        grid=(bsz, nt),
        in_specs=[pl.BlockSpec((ts, D_LRU), lambda b, s: (row(b, s), 0)),
                  pl.BlockSpec((ts, D_LRU), lambda b, s: (row(b, s), 1)),
                  pl.BlockSpec((CONV_W, D_LRU), lambda b, s: (0, 0)),
                  vec(D_LRU),
                  pl.BlockSpec((D_LRU, 2 * D_LRU), lambda b, s: (0, 0)),
                  vec(2 * D_LRU), vec(D_LRU), vec(D_LRU)],
        out_specs=pl.BlockSpec((ts, D_LRU), lambda b, s: (row(b, s), 0)),
        out_shape=jax.ShapeDtypeStruct((bsz * seq, D_LRU), BF16),
        scratch_shapes=[pltpu.VMEM((SUBLANES, D_LRU), F32), pltpu.VMEM((1, D_LRU), F32)],
        compiler_params=_cparams("parallel", "arbitrary"),
        name="rglru",
    )(proj, proj, cw, cb, wg, bg, lam, og)


def _gdn_kernel(q_ref, k_ref, v_ref, z_ref, ba_ref, cw_ref, alog_ref, dtb_ref, og_ref, o_ref,
                qt_ref, kt_ref, vt_ref, qs_ref, ks_ref, vs_ref, bs_ref, gs_ref, gc_ref, gct_ref, st_ref,
                *, group_chunks):
    ts = q_ref.shape[0]
    dk = GDN_HEAD_DIM

    @pl.when(pl.program_id(1) == 0)
    def _():
        qt_ref[...] = jnp.zeros_like(qt_ref)
        kt_ref[...] = jnp.zeros_like(kt_ref)
        vt_ref[...] = jnp.zeros_like(vt_ref)
        st_ref[...] = jnp.zeros_like(st_ref)

    def conv_silu(x_ref, tail_ref, part):
        x = x_ref[...]
        y = _causal_conv(x, tail_ref[...], cw_ref[part])
        tail_ref[...] = x[ts - SUBLANES:]
        return jax.nn.silu(y)

    def l2n(x, scale):
        parts = []
        for h in range(GDN_HEADS):
            xh = x[:, h * dk:(h + 1) * dk]
            parts.append(xh * (lax.rsqrt(jnp.sum(xh * xh, axis=-1, keepdims=True) + EPS) * scale))
        return jnp.concatenate(parts, axis=1)

    qs_ref[...] = l2n(conv_silu(q_ref, qt_ref, 0), dk ** -0.5)
    ks_ref[...] = l2n(conv_silu(k_ref, kt_ref, 1), 1.0)
    vs_ref[...] = conv_silu(v_ref, vt_ref, 2)
    ba = ba_ref[...]
    bs_ref[...] = jax.nn.sigmoid(ba)
    gs_ref[...] = -jnp.exp(alog_ref[...]) * jax.nn.softplus(ba + dtb_ref[...])

    ri = lax.broadcasted_iota(jnp.int32, (CHUNK, CHUNK), 0)
    ci = lax.broadcasted_iota(jnp.int32, (CHUNK, CHUNK), 1)
    causal = ri >= ci
    strict = ri > ci
    tril = causal.astype(F32)
    eye = (ri == ci).astype(F32)
    og = og_ref[...]

    def chunk_terms(pairs):
        n = range(len(pairs))
        rows = [slice(c * CHUNK, (c + 1) * CHUNK) for c, _ in pairs]
        cols = [slice(h * dk, (h + 1) * dk) for _, h in pairs]
        gl = [GDN_HEADS + h for _, h in pairs]
        kh = [ks_ref[rows[i], cols[i]] for i in n]
        kb = [kh[i] * bs_ref[rows[i], pairs[i][1]:pairs[i][1] + 1] for i in n]
        r = [_dot_nt(jnp.concatenate([kb[i], qs_ref[rows[i], cols[i]]], axis=0), kh[i]) for i in n]
        gcol = [gc_ref[rows[i], gl[i]:gl[i] + 1] for i in n]
        decay = []
        for i in n:
            diff = gcol[i] - gct_ref[gl[i]:gl[i] + 1, rows[i]]
            decay.append(jnp.where(causal, jnp.exp(jnp.where(causal, diff, 0.0)), 0.0))
        a = [jnp.where(strict, r[i][:CHUNK] * decay[i], 0.0) for i in n]
        qk = [(r[i][CHUNK:] * decay[i]).astype(BF16) for i in n]
        tinv = [eye - a[i] for i in n]
        p = 2
        while p < CHUNK:
            a = [_dot(a[i], a[i]) for i in n]
            tinv = [tinv[i] + _dot(tinv[i], a[i]) for i in n]
            p *= 2
        eg = [jnp.exp(gcol[i]) for i in n]
        rhs = [jnp.concatenate([vs_ref[rows[i], cols[i]] * bs_ref[rows[i], pairs[i][1]:pairs[i][1] + 1],
                                kb[i] * eg[i]], axis=1) for i in n]
        uw = [_dot(tinv[i], rhs[i]).astype(BF16) for i in n]
        qk_uw = [jnp.dot(qk[i], uw[i], preferred_element_type=F32) for i in n]
        glast = [gc_ref[(c + 1) * CHUNK - 1:(c + 1) * CHUNK, gl[i]:gl[i] + 1] for i, (c, _) in enumerate(pairs)]
        kd_uw = [_dot_tn(kh[i] * jnp.exp(glast[i] - gcol[i]), uw[i]) for i in n]
        out = {}
        for i in n:
            lhs = jnp.concatenate([kd_uw[i][:, dk:], qs_ref[rows[i], cols[i]] * eg[i] - qk_uw[i][:, dk:]],
                                  axis=0).astype(BF16)
            out[pairs[i]] = (lhs, kd_uw[i][:, :dk], qk_uw[i][:, :dk], jnp.exp(glast[i]))
        return out

    nc = ts // CHUNK
    for c in range(nc):
        rows = slice(c * CHUNK, (c + 1) * CHUNK)
        gc_ref[rows, :] = jnp.dot(tril, gs_ref[rows, :], precision=lax.Precision.HIGHEST,
                                  preferred_element_type=F32)
    gct_ref[...] = gc_ref[...].T
    terms = {}
    for c0 in range(0, nc, group_chunks):
        terms.update(chunk_terms([(c, h) for c in range(c0, c0 + group_chunks) for h in range(GDN_HEADS)]))
    s = [st_ref[h] for h in range(GDN_HEADS)]
    for c in range(nc):
        rows = slice(c * CHUNK, (c + 1) * CHUNK)
        r = [jnp.dot(terms[c, h][0], s[h].astype(BF16), preferred_element_type=F32) for h in range(GDN_HEADS)]
        for h in range(GDN_HEADS):
            cols = slice(h * dk, (h + 1) * dk)
            _, c_add, o_add, egl = terms[c, h]
            o = r[h][dk:] + o_add
            s[h] = egl * s[h] - r[h][:dk] + c_add
            o_ref[rows, cols] = (_rms(o, og) * jax.nn.silu(z_ref[rows, cols])).astype(o_ref.dtype)
    for h in range(GDN_HEADS):
        st_ref[h] = s[h]


def _gdn(proj, cw, alog, dtb, og, bsz, seq, ts):
    nt = seq // ts
    row = lambda b, s: b * nt + s
    col = lambda c: pl.BlockSpec((ts, D_GDN), lambda b, s: (row(b, s), c))
    c0 = 2 * D_LRU // D_GDN
    return pl.pallas_call(
        functools.partial(_gdn_kernel, group_chunks=4),
        grid=(bsz, nt),
        in_specs=[col(c0), col(c0 + 1), col(c0 + 2), col(c0 + 3),
                  pl.BlockSpec((ts, LANES), lambda b, s: (row(b, s), O_BA // LANES)),
                  pl.BlockSpec((3, CONV_W, D_GDN), lambda b, s: (0, 0, 0)),
                  pl.BlockSpec((1, LANES), lambda b, s: (0, 0)),
                  pl.BlockSpec((1, LANES), lambda b, s: (0, 0)),
                  pl.BlockSpec((1, GDN_HEAD_DIM), lambda b, s: (0, 0))],
        out_specs=pl.BlockSpec((ts, D_GDN), lambda b, s: (row(b, s), 0)),
        out_shape=jax.ShapeDtypeStruct((bsz * seq, D_GDN), BF16),
        scratch_shapes=[pltpu.VMEM((SUBLANES, D_GDN), F32)] * 3
        + [pltpu.VMEM((ts, D_GDN), F32)] * 3
        + [pltpu.VMEM((ts, LANES), F32)] * 3
        + [pltpu.VMEM((LANES, ts), F32)]
        + [pltpu.VMEM((GDN_HEADS, GDN_HEAD_DIM, GDN_HEAD_DIM), F32)],
        compiler_params=_cparams("parallel", "arbitrary"),
        name="gdn",
    )(proj, proj, proj, proj, proj, cw, alog, dtb, og)


def _out_router_kernel(x_ref, yl_ref, yg_ref, wo_ref, g_ref, wr_ref, br_ref, h_ref, xn_ref, rt_ref):
    h = x_ref[...] + jnp.dot(yl_ref[...], wo_ref[:D_LRU, :], preferred_element_type=F32) \
        + jnp.dot(yg_ref[...], wo_ref[D_LRU:, :], preferred_element_type=F32)
    h_ref[...] = h
    xn = _rms(h, g_ref[...])
    xn_ref[...] = xn
    xh = xn.astype(BF16)
    xl = (xn - xh.astype(F32)).astype(BF16)
    logits = jnp.dot(jnp.concatenate([xh, xl, xh], axis=1), wr_ref[...],
                     preferred_element_type=F32) + br_ref[...]
    lane = lax.broadcasted_iota(jnp.int32, logits.shape, 1).astype(F32)
    big = jnp.float32(2 * LANES)
    ninf = jnp.float32(-jnp.inf)

    def top1(vals):
        m = jnp.max(vals, axis=-1, keepdims=True)
        return m, jnp.min(jnp.where(vals == m, lane, big), axis=-1, keepdims=True)

    gl = jnp.where(lane < N_GROUPS, logits, ninf)
    gmax, gsel = top1(gl)
    p_group = 1.0 / jnp.sum(jnp.exp(gl - gmax), axis=-1, keepdims=True)
    lo = N_GROUPS + EXPERTS_PER_GROUP * gsel
    el = jnp.where((lane >= lo) & (lane < lo + EXPERTS_PER_GROUP), logits, ninf)
    m1, i1 = top1(el)
    m2, i2 = top1(jnp.where(lane == i1, ninf, el))
    r = jnp.exp(m2 - m1)
    g1 = p_group / (1.0 + r)
    g2 = p_group * r / (1.0 + r)
    rt = jnp.where(lane == 0, i1 - N_GROUPS,
                   jnp.where(lane == 1, i2 - N_GROUPS,
                             jnp.where(lane == 2, g1, jnp.where(lane == 3, g2, 0.0))))
    rt_ref[...] = rt


def _out_router(x2, yl, yg, wo, g, wr, br, tm):
    t = x2.shape[0]
    tile = lambda n: pl.BlockSpec((tm, n), lambda i: (i, 0))
    full = lambda a, b: pl.BlockSpec((a, b), lambda i: (0, 0))
    return pl.pallas_call(
        _out_router_kernel,
        grid=(t // tm,),
        in_specs=[tile(D_MODEL), tile(D_LRU), tile(D_GDN), full(D_MODEL, D_MODEL), full(1, D_MODEL),
                  full(3 * D_MODEL, LANES), full(1, LANES)],
        out_specs=[tile(D_MODEL), tile(D_MODEL), tile(LANES)],
        out_shape=[jax.ShapeDtypeStruct((t, D_MODEL), F32), jax.ShapeDtypeStruct((t, D_MODEL), F32),
                   jax.ShapeDtypeStruct((t, LANES), F32)],
        compiler_params=_cparams("parallel"),
        name="out_router",
    )(x2, yl, yg, wo, g, wr, br)


def _slots_kernel(rt_ref, dest_ref, pend_ref, cnt_ref):
    phase = pl.program_id(0)
    i = pl.program_id(1)
    tm = rt_ref.shape[0]
    rt = rt_ref[...]
    lane = lax.broadcasted_iota(jnp.int32, rt.shape, 1)
    e0 = rt[:, 0:1].astype(jnp.int32)
    e1 = rt[:, 1:2].astype(jnp.int32)
    member = ((lane == e0) | (lane == e1)).astype(BF16)

    @pl.when((phase == 0) & (i == 0))
    def _():
        cnt_ref[...] = jnp.zeros_like(cnt_ref)

    @pl.when(phase == 0)
    def _():
        ones = jnp.ones((SUBLANES, tm), BF16)
        cnt_ref[...] += jnp.dot(ones, member, preferred_element_type=F32)

    @pl.when((phase == 1) & (i == 0))
    def _():
        cnt = cnt_ref[...]
        padded = jnp.ceil(cnt / MOE_BLOCK) * MOE_BLOCK
        l8 = lax.broadcasted_iota(jnp.int32, cnt.shape, 1)
        incl = padded
        d = 1
        while d < LANES:
            incl = incl + jnp.where(l8 >= d, pltpu.roll(incl, d, axis=1), 0.0)
            d *= 2
        pend_ref[...] = incl
        cnt_ref[...] = incl - padded

    @pl.when(phase == 1)
    def _():
        ri = lax.broadcasted_iota(jnp.int32, (tm, tm), 0)
        ci = lax.broadcasted_iota(jnp.int32, (tm, tm), 1)
        before = (ri > ci).astype(BF16)
        start = cnt_ref[...]
        pos = start[0:1, :] + jnp.dot(before, member, preferred_element_type=F32)
        d0 = jnp.sum(jnp.where(lane == e0, pos, 0.0), axis=-1, keepdims=True)
        d1 = jnp.sum(jnp.where(lane == e1, pos, 0.0), axis=-1, keepdims=True)
        dest_ref[...] = jnp.where(lane == 0, d0, jnp.where(lane == 1, d1, 0.0)).astype(jnp.int32)
        cnt_ref[...] = start + jnp.dot(jnp.ones((SUBLANES, tm), BF16), member, preferred_element_type=F32)


def _slots(rt, tm):
    t = rt.shape[0]
    return pl.pallas_call(
        _slots_kernel,
        grid=(2, t // tm),
        in_specs=[pl.BlockSpec((tm, LANES), lambda p, i: (i, 0))],
        out_specs=[pl.BlockSpec((tm, LANES), lambda p, i: (i * p, 0)),
                   pl.BlockSpec((SUBLANES, LANES), lambda p, i: (0, 0))],
        out_shape=[jax.ShapeDtypeStruct((t, LANES), jnp.int32),
                   jax.ShapeDtypeStruct((SUBLANES, LANES), F32)],
        scratch_shapes=[pltpu.VMEM((SUBLANES, LANES), F32)],
        compiler_params=_cparams("arbitrary", "arbitrary"),
        name="moe_slots",
    )(rt)


def _row_copy(src_ref, src_row, dst_ref, dst_row, sem):
    return pltpu.make_async_copy(src_ref.at[pl.ds(src_row, 1)], dst_ref.at[pl.ds(dst_row, 1)], sem)


def _dispatch_kernel(dest_ref, xn_ref, xs_in_ref, xs_ref, sem):
    del xs_in_ref
    tm = xn_ref.shape[0]
    base = pl.program_id(0) * tm * TOP_K

    def start(r, carry):
        for k in range(TOP_K):
            _row_copy(xn_ref, r, xs_ref, dest_ref[base + TOP_K * r + k], sem).start()
        return carry

    lax.fori_loop(0, tm, start, 0)

    def wait(r, carry):
        for k in range(TOP_K):
            _row_copy(xn_ref, r, xs_ref, dest_ref[base + TOP_K * r + k], sem).wait()
        return carry

    lax.fori_loop(0, tm, wait, 0)


def _dispatch(dest, xn, xs_init, tm):
    t = xn.shape[0]
    return pl.pallas_call(
        _dispatch_kernel,
        grid_spec=pltpu.PrefetchScalarGridSpec(
            num_scalar_prefetch=1,
            grid=(t // tm,),
            in_specs=[pl.BlockSpec((tm, D_MODEL), lambda i, d: (i, 0)),
                      pl.BlockSpec(memory_space=pl.ANY)],
            out_specs=pl.BlockSpec(memory_space=pl.ANY),
            scratch_shapes=[pltpu.SemaphoreType.DMA]),
        out_shape=jax.ShapeDtypeStruct(xs_init.shape, xs_init.dtype),
        input_output_aliases={2: 0},
        compiler_params=_cparams("arbitrary"),
        name="moe_dispatch",
    )(dest, xn, xs_init)


def _experts_kernel(be_ref, na_ref, xs_ref, wg_ref, wu_ref, wd_ref, ys_ref, wgb_ref, wub_ref, wdb_ref):
    b = pl.program_id(0)
    prev = be_ref[jnp.maximum(b - 1, 0)]

    @pl.when((b == 0) | (be_ref[b] != prev))
    def _():
        wgb_ref[...] = wg_ref[0].astype(BF16)
        wub_ref[...] = wu_ref[0].astype(BF16)
        wdb_ref[...] = wd_ref[0].astype(BF16)

    @pl.when(b < na_ref[0])
    def _():
        xb = xs_ref[...].astype(BF16)
        hg = jnp.dot(xb, wgb_ref[...], preferred_element_type=F32)
        hu = jnp.dot(xb, wub_ref[...], preferred_element_type=F32)
        hb = (jax.nn.silu(hg) * hu).astype(BF16)
        ys_ref[...] = jnp.dot(hb, wdb_ref[...], preferred_element_type=F32)

    @pl.when(b >= na_ref[0])
    def _():
        ys_ref[...] = jnp.zeros_like(ys_ref)


def _experts(block_expert, n_active, xs, wg, wu, wd):
    nb = xs.shape[0] // MOE_BLOCK
    blk = lambda b, be, na: (jnp.minimum(b, na[0] - 1), 0)
    wspec = lambda a, c: pl.BlockSpec((1, a, c), lambda b, be, na: (be[b], 0, 0))
    return pl.pallas_call(
        _experts_kernel,
        grid_spec=pltpu.PrefetchScalarGridSpec(
            num_scalar_prefetch=2,
            grid=(nb,),
            in_specs=[pl.BlockSpec((MOE_BLOCK, D_MODEL), blk),
                      wspec(D_MODEL, D_EXPERT), wspec(D_MODEL, D_EXPERT), wspec(D_EXPERT, D_MODEL)],
            out_specs=pl.BlockSpec((MOE_BLOCK, D_MODEL), lambda b, be, na: (b, 0)),
            scratch_shapes=[pltpu.VMEM((D_MODEL, D_EXPERT), BF16), pltpu.VMEM((D_MODEL, D_EXPERT), BF16),
                            pltpu.VMEM((D_EXPERT, D_MODEL), BF16)]),
        out_shape=jax.ShapeDtypeStruct(xs.shape, F32),
        compiler_params=_cparams("arbitrary"),
        name="moe_experts",
    )(block_expert, n_active, xs, wg, wu, wd)


def _combine_kernel(dest_ref, h_ref, rt_ref, p_ref, ys_ref, gp_ref, wpg_ref, wp_ref, gf_ref, o_ref,
                    y0_ref, y1_ref, sem):
    tm = h_ref.shape[0]
    base = pl.program_id(0) * tm * TOP_K
    bufs = (y0_ref, y1_ref)

    def start(r, carry):
        for k in range(TOP_K):
            _row_copy(ys_ref, dest_ref[base + TOP_K * r + k], bufs[k], r, sem).start()
        return carry

    lax.fori_loop(0, tm, start, 0)

    def wait(r, carry):
        for k in range(TOP_K):
            _row_copy(ys_ref, dest_ref[base + TOP_K * r + k], bufs[k], r, sem).wait()
        return carry

    lax.fori_loop(0, tm, wait, 0)

    rt = rt_ref[...]
    h = h_ref[...] + (y0_ref[...] * rt[:, 2:3] + y1_ref[...] * rt[:, 3:4])
    gate = jax.nn.sigmoid(_dot(_rms(h, gp_ref[...]), wpg_ref[...]))
    h = h + _dot(p_ref[...], wp_ref[...]) * gate
    o_ref[...] = _rms(h, gf_ref[...])


def _combine(dest, h1, rt, p2, ys, gp, wpg, wp, gf, tm):
    t = h1.shape[0]
    tile = lambda n: pl.BlockSpec((tm, n), lambda i, d: (i, 0))
    full = lambda a, b: pl.BlockSpec((a, b), lambda i, d: (0, 0))
    return pl.pallas_call(
        _combine_kernel,
        grid_spec=pltpu.PrefetchScalarGridSpec(
            num_scalar_prefetch=1,
            grid=(t // tm,),
            in_specs=[tile(D_MODEL), tile(LANES), tile(D_PLE), pl.BlockSpec(memory_space=pl.ANY),
                      full(1, D_MODEL), full(D_MODEL, D_MODEL), full(D_PLE, D_MODEL), full(1, D_MODEL)],
            out_specs=tile(D_MODEL),
            scratch_shapes=[pltpu.VMEM((tm, D_MODEL), F32), pltpu.VMEM((tm, D_MODEL), F32),
                            pltpu.SemaphoreType.DMA]),
        out_shape=jax.ShapeDtypeStruct((t, D_MODEL), F32),
        compiler_params=_cparams("arbitrary"),
        name="moe_combine_ple",
    )(dest, h1, rt, p2, ys, gp, wpg, wp, gf)


def _block_diag(w):
    n, c, d = w.shape
    eye = jnp.eye(n, dtype=w.dtype)
    return (w[:, :, None, :] * eye[:, None, :, None]).reshape(n * c, n * d)


def _lane_row(vals, offset):
    return jnp.zeros((1, LANES), F32).at[0, offset:offset + vals.shape[0]].set(vals)


def kernel(x, p, norm_mix, w_in, lru_conv_w, lru_conv_b, lru_wa, lru_ba, lru_wi, lru_bi, lru_lambda,
           lru_out_norm, gdn_conv_w, gdn_a_log, gdn_dt_bias, gdn_out_norm, w_out, norm_ffn,
           w_router_group, b_router_group, w_router_expert, b_router_expert, w_exp_gate, w_exp_up,
           w_exp_down, norm_ple, w_ple_gate, w_ple, norm_final):
    bsz, seq, d = x.shape
    t = bsz * seq
    depth = w_in.shape[0]
    assert depth == 1, "the final norm is fused into the last layer's combine kernel"
    n_blocks = -(-t * TOP_K // MOE_BLOCK) + N_EXPERTS
    row = lambda v: v.reshape(1, -1).astype(F32)
    h = x.reshape(t, d).astype(F32)
    for l in range(depth):
        w_in_p = jnp.pad(w_in[l], ((0, 0), (0, PROJ_COLS - w_in.shape[2]))).astype(BF16)
        proj = _in_proj(h, row(norm_mix[l]), w_in_p, 512)
        w_gates = jnp.concatenate([_block_diag(lru_wa[l]), _block_diag(lru_wi[l])], axis=1).astype(BF16)
        b_gates = jnp.concatenate([lru_ba[l], lru_bi[l]]).reshape(1, -1)
        y_lru = _lru(proj, lru_conv_w[l], row(lru_conv_b[l]), w_gates, b_gates, row(lru_lambda[l]),
                     row(lru_out_norm[l]), bsz, seq, 512)
        cw = gdn_conv_w[l].reshape(CONV_W, 3, D_GDN).transpose(1, 0, 2)
        y_gdn = _gdn(proj, cw, _lane_row(gdn_a_log[l], GDN_HEADS), _lane_row(gdn_dt_bias[l], GDN_HEADS),
                     row(gdn_out_norm[l]), bsz, seq, 512)
        w_r = jnp.pad(jnp.concatenate([w_router_group[l], w_router_expert[l]], axis=1),
                      ((0, 0), (0, LANES - N_GROUPS - N_EXPERTS)))
        w_r_hi = w_r.astype(BF16)
        w_r_lo = (w_r - w_r_hi.astype(F32)).astype(BF16)
        w_r = jnp.concatenate([w_r_hi, w_r_hi, w_r_lo], axis=0)
        b_r = _lane_row(jnp.concatenate([b_router_group[l], b_router_expert[l]]), 0)
        h1, xn2, rt = _out_router(h, y_lru, y_gdn, w_out[l].astype(BF16), row(norm_ffn[l]), w_r, b_r, 512)
        dest2, pad_end = _slots(rt, 512)
        dest = dest2[:, :TOP_K].reshape(-1)
        pad_end = pad_end[0, :N_EXPERTS].astype(jnp.int32)
        block_start = jnp.arange(n_blocks, dtype=jnp.int32) * MOE_BLOCK
        block_expert = jnp.minimum(jnp.sum(pad_end[None, :] <= block_start[:, None], axis=1),
                                   N_EXPERTS - 1).astype(jnp.int32)
        n_active = (pad_end[N_EXPERTS - 1:] // MOE_BLOCK).astype(jnp.int32)
        xs = _dispatch(dest, xn2, jnp.zeros((n_blocks * MOE_BLOCK, d), F32), 256)
        ys = _experts(block_expert, n_active, xs, w_exp_gate[l], w_exp_up[l], w_exp_down[l])
        h = _combine(dest, h1, rt, p[l].reshape(t, -1).astype(F32), ys, row(norm_ple[l]),
                     w_ple_gate[l].astype(BF16), w_ple[l].astype(BF16), row(norm_final), 256)
    return h.reshape(bsz, seq, d).astype(x.dtype)
```

```python
import functools

import jax
import jax.numpy as jnp
import numpy as np
from jax import lax
from jax.experimental import pallas as pl
from jax.experimental.pallas import tpu as pltpu
from jax.experimental.pallas import tpu_sc as plsc

D_MODEL = 1024
D_LRU = 512
LRU_BLOCKS = 8
LRU_BLOCK_W = D_LRU // LRU_BLOCKS
LRU_C = 8.0
D_GDN = 512
GDN_HEADS = 4
GDN_HEAD_DIM = D_GDN // GDN_HEADS
CONV_W = 4
CHUNK = 64
N_GROUPS = 4
EXPERTS_PER_GROUP = 8
N_EXPERTS = N_GROUPS * EXPERTS_PER_GROUP
TOP_K = 2
D_EXPERT = 512
MOE_BLOCK = 256
D_PLE = 256
EPS = 1e-6

LANES = 128
SUBLANES = 8
ROW_TILES = D_MODEL // LANES
SC_CORES = 2
SC_SUBCORES = 16
SC_WORKERS = SC_CORES * SC_SUBCORES
SC_WINDOW = 32
PROJ_COLS = 2 * D_LRU + 4 * D_GDN + LANES
O_BA = 2 * D_LRU + 4 * D_GDN
VMEM_LIMIT = 56 * 1024 * 1024

BF16 = jnp.bfloat16
F32 = jnp.float32


def _cparams(*sem):
    return pltpu.CompilerParams(dimension_semantics=sem, vmem_limit_bytes=VMEM_LIMIT)


def _rms(x, g):
    return x * lax.rsqrt(jnp.mean(x * x, axis=-1, keepdims=True) + EPS) * g


def _dot(a, b):
    return jnp.dot(a.astype(BF16), b.astype(BF16), preferred_element_type=F32)


def _dot_nt(a, b):
    return lax.dot_general(a.astype(BF16), b.astype(BF16), (((1,), (1,)), ((), ())),
                           preferred_element_type=F32)


def _dot_tn(a, b):
    return lax.dot_general(a.astype(BF16), b.astype(BF16), (((0,), (0,)), ((), ())),
                           preferred_element_type=F32)


def _shift_rows(x, tail, j):
    r = pltpu.roll(x, j, axis=0)
    tr = pltpu.roll(tail, j, axis=0)
    row = lax.broadcasted_iota(jnp.int32, tail.shape, 0)
    top = jnp.where(row < j, tr, r[:SUBLANES])
    return jnp.concatenate([top, r[SUBLANES:]], axis=0)


def _causal_conv(x, tail, w):
    acc = x * w[CONV_W - 1:CONV_W]
    for j in range(1, CONV_W):
        acc = acc + _shift_rows(x, tail, j) * w[CONV_W - 1 - j:CONV_W - j]
    return acc


def _store_token_tiles(ref, x):
    rows = x.shape[0]
    for j in range(ROW_TILES):
        ref[pl.ds(j, rows, stride=ROW_TILES), :] = x[:, j * LANES:(j + 1) * LANES]


def _load_token_tiles(ref, rows, dtype=F32):
    return jnp.concatenate([ref[pl.ds(j, rows, stride=ROW_TILES), :].astype(dtype) for j in range(ROW_TILES)],
                           axis=1)


def _in_proj_kernel(x_ref, g_ref, w_ref, o_ref, *, col_tile):
    xn = _rms(x_ref[...], g_ref[...]).astype(BF16)
    for j in range(PROJ_COLS // col_tile):
        cols = slice(j * col_tile, (j + 1) * col_tile)
        o_ref[:, cols] = jnp.dot(xn, w_ref[:, cols], preferred_element_type=F32)


def _in_proj(x2, g, w, tm):
    t = x2.shape[0]
    return pl.pallas_call(
        functools.partial(_in_proj_kernel, col_tile=PROJ_COLS // 5),
        grid=(t // tm,),
        in_specs=[pl.BlockSpec((tm, D_MODEL), lambda i: (i, 0)),
                  pl.BlockSpec((1, D_MODEL), lambda i: (0, 0)),
                  pl.BlockSpec((D_MODEL, PROJ_COLS), lambda i: (0, 0))],
        out_specs=pl.BlockSpec((tm, PROJ_COLS), lambda i: (i, 0)),
        out_shape=jax.ShapeDtypeStruct((t, PROJ_COLS), F32),
        compiler_params=_cparams("parallel"),
        name="in_proj",
    )(x2, g, w)


def _lru_kernel(x_ref, gate_ref, cw_ref, cb_ref, wg_ref, bg_ref, lam_ref, og_ref, o_ref,
                tail_ref, h_ref):
    ts = x_ref.shape[0]

    @pl.when(pl.program_id(1) == 0)
    def _():
        tail_ref[...] = jnp.zeros_like(tail_ref)
        h_ref[...] = jnp.zeros_like(h_ref)

    x = x_ref[...]
    xc = _causal_conv(x, tail_ref[...], cw_ref[...]) + cb_ref[...]
    tail_ref[...] = x[ts - SUBLANES:]
    gates = jax.nn.sigmoid(_dot(xc, wg_ref[...]) + bg_ref[...])
    r = gates[:, :D_LRU]
    i = gates[:, D_LRU:]
    log_a = LRU_C * r * jax.nn.log_sigmoid(lam_ref[...])
    a = jnp.exp(log_a)
    th = jnp.tanh(log_a)
    u = jnp.sqrt(-2.0 * th / (1.0 - th)) * (i * xc)
    row = lax.broadcasted_iota(jnp.int32, a.shape, 0)
    d = 1
    while d < ts:
        keep = row >= d
        a_prev = jnp.where(keep, pltpu.roll(a, d, axis=0), 1.0)
        u_prev = jnp.where(keep, pltpu.roll(u, d, axis=0), 0.0)
        u = a * u_prev + u
        a = a * a_prev
        d *= 2
    h = a * h_ref[...] + u
    h_ref[...] = h[ts - 1:]
    y = h * jax.nn.gelu(gate_ref[...])
    o_ref[...] = _rms(y, og_ref[...]).astype(o_ref.dtype)


def _lru(proj, cw, cb, wg, bg, lam, og, bsz, seq, ts):
    nt = seq // ts
    row = lambda b, s: b * nt + s
    vec = lambda n: pl.BlockSpec((1, n), lambda b, s: (0, 0))
    return pl.pallas_call(
        _lru_kernel,
        grid=(bsz, nt),
        in_specs=[pl.BlockSpec((ts, D_LRU), lambda b, s: (row(b, s), 0)),
                  pl.BlockSpec((ts, D_LRU), lambda b, s: (row(b, s), 1)),
                  pl.BlockSpec((CONV_W, D_LRU), lambda b, s: (0, 0)),
                  vec(D_LRU),
                  pl.BlockSpec((D_LRU, 2 * D_LRU), lambda b, s: (0, 0)),
                  vec(2 * D_LRU), vec(D_LRU), vec(D_LRU)],
        out_specs=pl.BlockSpec((ts, D_LRU), lambda b, s: (row(b, s), 0)),
        out_shape=jax.ShapeDtypeStruct((bsz * seq, D_LRU), BF16),
        scratch_shapes=[pltpu.VMEM((SUBLANES, D_LRU), F32), pltpu.VMEM((1, D_LRU), F32)],
        compiler_params=_cparams("parallel", "arbitrary"),
        name="rglru",
    )(proj, proj, cw, cb, wg, bg, lam, og)


def _gdn_kernel(q_ref, k_ref, v_ref, z_ref, ba_ref, cw_ref, alog_ref, dtb_ref, og_ref, o_ref,
                qt_ref, kt_ref, vt_ref, qs_ref, ks_ref, vs_ref, bs_ref, gs_ref, gc_ref, gct_ref, st_ref,
                *, group_chunks):
    ts = q_ref.shape[0]
    dk = GDN_HEAD_DIM

    @pl.when(pl.program_id(1) == 0)
    def _():
        qt_ref[...] = jnp.zeros_like(qt_ref)
        kt_ref[...] = jnp.zeros_like(kt_ref)
        vt_ref[...] = jnp.zeros_like(vt_ref)
        st_ref[...] = jnp.zeros_like(st_ref)

    def conv_silu(x_ref, tail_ref, part):
        x = x_ref[...]
        y = _causal_conv(x, tail_ref[...], cw_ref[part])
        tail_ref[...] = x[ts - SUBLANES:]
        return jax.nn.silu(y)

    def l2n(x, scale):
        parts = []
        for h in range(GDN_HEADS):
            xh = x[:, h * dk:(h + 1) * dk]
            parts.append(xh * (lax.rsqrt(jnp.sum(xh * xh, axis=-1, keepdims=True) + EPS) * scale))
        return jnp.concatenate(parts, axis=1)

    qs_ref[...] = l2n(conv_silu(q_ref, qt_ref, 0), dk ** -0.5)
    ks_ref[...] = l2n(conv_silu(k_ref, kt_ref, 1), 1.0)
    vs_ref[...] = conv_silu(v_ref, vt_ref, 2)
    ba = ba_ref[...]
    bs_ref[...] = jax.nn.sigmoid(ba)
    gs_ref[...] = -jnp.exp(alog_ref[...]) * jax.nn.softplus(ba + dtb_ref[...])

    ri = lax.broadcasted_iota(jnp.int32, (CHUNK, CHUNK), 0)
    ci = lax.broadcasted_iota(jnp.int32, (CHUNK, CHUNK), 1)
    causal = ri >= ci
    strict = ri > ci
    tril = causal.astype(F32)
    eye = (ri == ci).astype(F32)
    og = og_ref[...]

    def chunk_terms(pairs):
        n = range(len(pairs))
        rows = [slice(c * CHUNK, (c + 1) * CHUNK) for c, _ in pairs]
        cols = [slice(h * dk, (h + 1) * dk) for _, h in pairs]
        gl = [GDN_HEADS + h for _, h in pairs]
        kh = [ks_ref[rows[i], cols[i]] for i in n]
        kb = [kh[i] * bs_ref[rows[i], pairs[i][1]:pairs[i][1] + 1] for i in n]
        r = [_dot_nt(jnp.concatenate([kb[i], qs_ref[rows[i], cols[i]]], axis=0), kh[i]) for i in n]
        gcol = [gc_ref[rows[i], gl[i]:gl[i] + 1] for i in n]
        decay = []
        for i in n:
            diff = gcol[i] - gct_ref[gl[i]:gl[i] + 1, rows[i]]
            decay.append(jnp.where(causal, jnp.exp(jnp.where(causal, diff, 0.0)), 0.0))
        a = [jnp.where(strict, r[i][:CHUNK] * decay[i], 0.0) for i in n]
        qk = [(r[i][CHUNK:] * decay[i]).astype(BF16) for i in n]
        tinv = [eye - a[i] for i in n]
        p = 2
        while p < CHUNK:
            a = [_dot(a[i], a[i]) for i in n]
            tinv = [tinv[i] + _dot(tinv[i], a[i]) for i in n]
            p *= 2
        eg = [jnp.exp(gcol[i]) for i in n]
        rhs = [jnp.concatenate([vs_ref[rows[i], cols[i]] * bs_ref[rows[i], pairs[i][1]:pairs[i][1] + 1],
                                kb[i] * eg[i]], axis=1) for i in n]
        uw = [_dot(tinv[i], rhs[i]).astype(BF16) for i in n]
        qk_uw = [jnp.dot(qk[i], uw[i], preferred_element_type=F32) for i in n]
        glast = [gc_ref[(c + 1) * CHUNK - 1:(c + 1) * CHUNK, gl[i]:gl[i] + 1] for i, (c, _) in enumerate(pairs)]
        kd_uw = [_dot_tn(kh[i] * jnp.exp(glast[i] - gcol[i]), uw[i]) for i in n]
        out = {}
        for i in n:
            lhs = jnp.concatenate([kd_uw[i][:, dk:], qs_ref[rows[i], cols[i]] * eg[i] - qk_uw[i][:, dk:]],
                                  axis=0).astype(BF16)
            out[pairs[i]] = (lhs, kd_uw[i][:, :dk], qk_uw[i][:, :dk], jnp.exp(glast[i]))
        return out

    nc = ts // CHUNK
    for c in range(nc):
        rows = slice(c * CHUNK, (c + 1) * CHUNK)
        gc_ref[rows, :] = jnp.dot(tril, gs_ref[rows, :], precision=lax.Precision.HIGHEST,
                                  preferred_element_type=F32)
    gct_ref[...] = gc_ref[...].T
    terms = {}
    for c0 in range(0, nc, group_chunks):
        terms.update(chunk_terms([(c, h) for c in range(c0, c0 + group_chunks) for h in range(GDN_HEADS)]))
    s = [st_ref[h] for h in range(GDN_HEADS)]
    for c in range(nc):
        rows = slice(c * CHUNK, (c + 1) * CHUNK)
        r = [jnp.dot(terms[c, h][0], s[h].astype(BF16), preferred_element_type=F32) for h in range(GDN_HEADS)]
        for h in range(GDN_HEADS):
            cols = slice(h * dk, (h + 1) * dk)
            _, c_add, o_add, egl = terms[c, h]
            o = r[h][dk:] + o_add
            s[h] = egl * s[h] - r[h][:dk] + c_add
            o_ref[rows, cols] = (_rms(o, og) * jax.nn.silu(z_ref[rows, cols])).astype(o_ref.dtype)
    for h in range(GDN_HEADS):
        st_ref[h] = s[h]


def _gdn(proj, cw, alog, dtb, og, bsz, seq, ts):
    nt = seq // ts
    row = lambda b, s: b * nt + s
    col = lambda c: pl.BlockSpec((ts, D_GDN), lambda b, s: (row(b, s), c))
    c0 = 2 * D_LRU // D_GDN
    return pl.pallas_call(
        functools.partial(_gdn_kernel, group_chunks=4),
        grid=(bsz, nt),
        in_specs=[col(c0), col(c0 + 1), col(c0 + 2), col(c0 + 3),
                  pl.BlockSpec((ts, LANES), lambda b, s: (row(b, s), O_BA // LANES)),
                  pl.BlockSpec((3, CONV_W, D_GDN), lambda b, s: (0, 0, 0)),
                  pl.BlockSpec((1, LANES), lambda b, s: (0, 0)),
                  pl.BlockSpec((1, LANES), lambda b, s: (0, 0)),
                  pl.BlockSpec((1, GDN_HEAD_DIM), lambda b, s: (0, 0))],
        out_specs=pl.BlockSpec((ts, D_GDN), lambda b, s: (row(b, s), 0)),
        out_shape=jax.ShapeDtypeStruct((bsz * seq, D_GDN), BF16),
        scratch_shapes=[pltpu.VMEM((SUBLANES, D_GDN), F32)] * 3
        + [pltpu.VMEM((ts, D_GDN), F32)] * 3
        + [pltpu.VMEM((ts, LANES), F32)] * 3
        + [pltpu.VMEM((LANES, ts), F32)]
        + [pltpu.VMEM((GDN_HEADS, GDN_HEAD_DIM, GDN_HEAD_DIM), F32)],
        compiler_params=_cparams("parallel", "arbitrary"),
        name="gdn",
    )(proj, proj, proj, proj, proj, cw, alog, dtb, og)


def _out_router_kernel(x_ref, yl_ref, yg_ref, wo_ref, g_ref, wr_ref, br_ref, h_ref, xn_ref, rt_ref):
    h = x_ref[...] + jnp.dot(yl_ref[...], wo_ref[:D_LRU, :], preferred_element_type=F32) \
        + jnp.dot(yg_ref[...], wo_ref[D_LRU:, :], preferred_element_type=F32)
    h_ref[...] = h
    xn = _rms(h, g_ref[...])
    _store_token_tiles(xn_ref, xn)
    xh = xn.astype(BF16)
    xl = (xn - xh.astype(F32)).astype(BF16)
    logits = jnp.dot(jnp.concatenate([xh, xl, xh], axis=1), wr_ref[...],
                     preferred_element_type=F32) + br_ref[...]
    lane = lax.broadcasted_iota(jnp.int32, logits.shape, 1).astype(F32)
    big = jnp.float32(2 * LANES)
    ninf = jnp.float32(-jnp.inf)

    def top1(vals):
        m = jnp.max(vals, axis=-1, keepdims=True)
        return m, jnp.min(jnp.where(vals == m, lane, big), axis=-1, keepdims=True)

    gl = jnp.where(lane < N_GROUPS, logits, ninf)
    gmax, gsel = top1(gl)
    p_group = 1.0 / jnp.sum(jnp.exp(gl - gmax), axis=-1, keepdims=True)
    lo = N_GROUPS + EXPERTS_PER_GROUP * gsel
    el = jnp.where((lane >= lo) & (lane < lo + EXPERTS_PER_GROUP), logits, ninf)
    m1, i1 = top1(el)
    m2, i2 = top1(jnp.where(lane == i1, ninf, el))
    r = jnp.exp(m2 - m1)
    g1 = p_group / (1.0 + r)
    g2 = p_group * r / (1.0 + r)
    rt = jnp.where(lane == 0, i1 - N_GROUPS,
                   jnp.where(lane == 1, i2 - N_GROUPS,
                             jnp.where(lane == 2, g1, jnp.where(lane == 3, g2, 0.0))))
    rt_ref[...] = rt


def _out_router(x2, yl, yg, wo, g, wr, br, tm):
    t = x2.shape[0]
    tile = lambda n: pl.BlockSpec((tm, n), lambda i: (i, 0))
    full = lambda a, b: pl.BlockSpec((a, b), lambda i: (0, 0))
    return pl.pallas_call(
        _out_router_kernel,
        grid=(t // tm,),
        in_specs=[tile(D_MODEL), tile(D_LRU), tile(D_GDN), full(D_MODEL, D_MODEL), full(1, D_MODEL),
                  full(3 * D_MODEL, LANES), full(1, LANES)],
        out_specs=[tile(D_MODEL), pl.BlockSpec((tm * ROW_TILES, LANES), lambda i: (i, 0)), tile(LANES)],
        out_shape=[jax.ShapeDtypeStruct((t, D_MODEL), F32),
                   jax.ShapeDtypeStruct((t * ROW_TILES, LANES), F32),
                   jax.ShapeDtypeStruct((t, LANES), F32)],
        compiler_params=_cparams("parallel"),
        name="out_router",
    )(x2, yl, yg, wo, g, wr, br)


def _slots_kernel(rt_ref, dest_ref, pend_ref, cnt_ref):
    phase = pl.program_id(0)
    i = pl.program_id(1)
    tm = rt_ref.shape[0]
    rt = rt_ref[...]
    lane = lax.broadcasted_iota(jnp.int32, rt.shape, 1)
    e0 = rt[:, 0:1].astype(jnp.int32)
    e1 = rt[:, 1:2].astype(jnp.int32)
    member = ((lane == e0) | (lane == e1)).astype(BF16)

    @pl.when((phase == 0) & (i == 0))
    def _():
        cnt_ref[...] = jnp.zeros_like(cnt_ref)

    @pl.when(phase == 0)
    def _():
        ones = jnp.ones((SUBLANES, tm), BF16)
        cnt_ref[...] += jnp.dot(ones, member, preferred_element_type=F32)

    @pl.when((phase == 1) & (i == 0))
    def _():
        cnt = cnt_ref[...]
        padded = jnp.ceil(cnt / MOE_BLOCK) * MOE_BLOCK
        l8 = lax.broadcasted_iota(jnp.int32, cnt.shape, 1)
        incl = padded
        d = 1
        while d < LANES:
            incl = incl + jnp.where(l8 >= d, pltpu.roll(incl, d, axis=1), 0.0)
            d *= 2
        sub = lax.broadcasted_iota(jnp.int32, cnt.shape, 0)
        pend_ref[...] = jnp.where(sub == 0, incl, cnt)
        cnt_ref[...] = incl - padded

    @pl.when(phase == 1)
    def _():
        ri = lax.broadcasted_iota(jnp.int32, (tm, tm), 0)
        ci = lax.broadcasted_iota(jnp.int32, (tm, tm), 1)
        before = (ri > ci).astype(BF16)
        start = cnt_ref[...]
        pos = start[0:1, :] + jnp.dot(before, member, preferred_element_type=F32)
        d0 = jnp.sum(jnp.where(lane == e0, pos, 0.0), axis=-1, keepdims=True)
        d1 = jnp.sum(jnp.where(lane == e1, pos, 0.0), axis=-1, keepdims=True)
        dest_ref[...] = jnp.where(lane == 0, d0, jnp.where(lane == 1, d1, 0.0)).astype(jnp.int32)
        cnt_ref[...] = start + jnp.dot(jnp.ones((SUBLANES, tm), BF16), member, preferred_element_type=F32)


def _slots(rt, tm):
    t = rt.shape[0]
    return pl.pallas_call(
        _slots_kernel,
        grid=(2, t // tm),
        in_specs=[pl.BlockSpec((tm, LANES), lambda p, i: (i, 0))],
        out_specs=[pl.BlockSpec((tm, LANES), lambda p, i: (i * p, 0)),
                   pl.BlockSpec((SUBLANES, LANES), lambda p, i: (0, 0))],
        out_shape=[jax.ShapeDtypeStruct((t, LANES), jnp.int32),
                   jax.ShapeDtypeStruct((SUBLANES, LANES), F32)],
        scratch_shapes=[pltpu.VMEM((SUBLANES, LANES), F32)],
        compiler_params=_cparams("arbitrary", "arbitrary"),
        name="moe_slots",
    )(rt)


def _sc_mesh():
    return plsc.VectorSubcoreMesh(core_axis_name="c", subcore_axis_name="s")


def _sc_worker():
    return lax.axis_index("s") * SC_CORES + lax.axis_index("c")


def _sc_scatter_rows(src, idx, n_rows):
    t = src.shape[0]
    per_w = t // SC_WORKERS
    n_win = per_w // SC_WINDOW
    idx = idx.reshape(TOP_K, SC_WORKERS, n_win, SC_WINDOW).transpose(1, 0, 2, 3)

    @functools.partial(
        pl.kernel, mesh=_sc_mesh(),
        out_type=jax.ShapeDtypeStruct((n_rows,) + src.shape[1:], src.dtype),
        scratch_types=[pltpu.VMEM((TOP_K, n_win, SC_WINDOW), jnp.int32),
                       pltpu.VMEM((2, SC_WINDOW) + src.shape[1:], src.dtype),
                       pltpu.SemaphoreType.DMA((2,)), pltpu.SemaphoreType.DMA((2,))],
        compiler_params=pltpu.CompilerParams(use_tc_tiling_on_sc=True),
        name="sc_dispatch")
    def scatter(src_hbm, idx_hbm, out_hbm, idx_v, rows_v, lsem, ssem):
        wid = _sc_worker()
        base = wid * per_w
        pltpu.sync_copy(idx_hbm.at[wid], idx_v)

        def load(w, slot):
            return pltpu.make_async_copy(src_hbm.at[pl.ds(base + w * SC_WINDOW, SC_WINDOW)], rows_v.at[slot],
                                         lsem.at[slot])

        def put(w, slot, k):
            return pltpu.make_async_copy(rows_v.at[slot], out_hbm.at[idx_v.at[k, w]], ssem.at[slot])

        load(0, 0).start()

        @pl.loop(0, n_win, step=2)
        def _(w0):
            for s in range(2):
                w = w0 + s

                @pl.when(w + 1 < n_win)
                def _():
                    @pl.when(w >= 1)
                    def _():
                        for k in range(TOP_K):
                            put(w - 1, 1 - s, k).wait()

                    load(w + 1, 1 - s).start()

                load(w, s).wait()
                for k in range(TOP_K):
                    put(w, s, k).start()

        for s in range(2):
            for k in range(TOP_K):
                put(n_win - 2 + s, s, k).wait()

    return scatter(src, idx)


def _sc_gather_rows(table, idx):
    b = idx.shape[0]
    per_w = b // SC_WORKERS
    n_win = per_w // SC_WINDOW
    idx = idx.reshape(SC_WORKERS, n_win, SC_WINDOW)

    @functools.partial(
        pl.kernel, mesh=_sc_mesh(),
        out_type=jax.ShapeDtypeStruct((b,) + table.shape[1:], table.dtype),
        scratch_types=[pltpu.VMEM((n_win, SC_WINDOW), jnp.int32),
                       pltpu.VMEM((2, SC_WINDOW) + table.shape[1:], table.dtype),
                       pltpu.SemaphoreType.DMA((2,)), pltpu.SemaphoreType.DMA((2,))],
        compiler_params=pltpu.CompilerParams(use_tc_tiling_on_sc=True),
        name="sc_combine_gather")
    def gather(table_hbm, idx_hbm, out_hbm, idx_v, rows_v, gsem, psem):
        wid = _sc_worker()
        base = wid * per_w
        pltpu.sync_copy(idx_hbm.at[wid], idx_v)

        def get(w, slot):
            return pltpu.make_async_copy(table_hbm.at[idx_v.at[w]], rows_v.at[slot], gsem.at[slot])

        def put(w, slot):
            return pltpu.make_async_copy(rows_v.at[slot], out_hbm.at[pl.ds(base + w * SC_WINDOW, SC_WINDOW)],
                                         psem.at[slot])

        get(0, 0).start()

        @pl.loop(0, n_win, step=2)
        def _(w0):
            for s in range(2):
                w = w0 + s

                @pl.when(w + 1 < n_win)
                def _():
                    @pl.when(w >= 1)
                    def _():
                        put(w - 1, 1 - s).wait()

                    get(w + 1, 1 - s).start()

                get(w, s).wait()
                put(w, s).start()

        for s in range(2):
            put(n_win - 2 + s, s).wait()

    return gather(table, idx)


def _experts_kernel(be_ref, nv_ref, xs_ref, wg_ref, wu_ref, wd_ref, ys_ref, wgb_ref, wub_ref, wdb_ref):
    b = pl.program_id(0)
    prev = be_ref[jnp.maximum(b - 1, 0)]

    @pl.when((b == 0) | (be_ref[b] != prev))
    def _():
        wgb_ref[...] = wg_ref[0].astype(BF16)
        wub_ref[...] = wu_ref[0].astype(BF16)
        wdb_ref[...] = wd_ref[0].astype(BF16)

    @pl.when(nv_ref[b] > 0)
    def _():
        row = lax.broadcasted_iota(jnp.int32, (MOE_BLOCK, D_MODEL), 0)
        xb = jnp.where(row < nv_ref[b], _load_token_tiles(xs_ref, MOE_BLOCK), 0.0).astype(BF16)
        hg = jnp.dot(xb, wgb_ref[...], preferred_element_type=F32)
        hu = jnp.dot(xb, wub_ref[...], preferred_element_type=F32)
        hb = (jax.nn.silu(hg) * hu).astype(BF16)
        _store_token_tiles(ys_ref, jnp.dot(hb, wdb_ref[...], preferred_element_type=F32))

    @pl.when(nv_ref[b] == 0)
    def _():
        ys_ref[...] = jnp.zeros_like(ys_ref)


def _experts(block_expert, n_valid, xs, wg, wu, wd):
    nb = xs.shape[0] // (MOE_BLOCK * ROW_TILES)
    blk = pl.BlockSpec((MOE_BLOCK * ROW_TILES, LANES), lambda b, be, nv: (b, 0))
    wspec = lambda a, c: pl.BlockSpec((1, a, c), lambda b, be, nv: (be[b], 0, 0))
    return pl.pallas_call(
        _experts_kernel,
        grid_spec=pltpu.PrefetchScalarGridSpec(
            num_scalar_prefetch=2,
            grid=(nb,),
            in_specs=[blk, wspec(D_MODEL, D_EXPERT), wspec(D_MODEL, D_EXPERT), wspec(D_EXPERT, D_MODEL)],
            out_specs=blk,
            scratch_shapes=[pltpu.VMEM((D_MODEL, D_EXPERT), BF16), pltpu.VMEM((D_MODEL, D_EXPERT), BF16),
                            pltpu.VMEM((D_EXPERT, D_MODEL), BF16)]),
        out_shape=jax.ShapeDtypeStruct(xs.shape, F32),
        compiler_params=_cparams("arbitrary"),
        name="moe_experts",
    )(block_expert, n_valid, xs, wg, wu, wd)


def _combine_kernel(h_ref, rt_ref, p_ref, y0_ref, y1_ref, gp_ref, wpg_ref, wp_ref, gf_ref, o_ref):
    tm = h_ref.shape[0]
    rt = rt_ref[...]
    h = h_ref[...] + (_load_token_tiles(y0_ref, tm) * rt[:, 2:3] + _load_token_tiles(y1_ref, tm) * rt[:, 3:4])
    gate = jax.nn.sigmoid(_dot(_rms(h, gp_ref[...]), wpg_ref[...]))
    h = h + _dot(p_ref[...], wp_ref[...]) * gate
    o_ref[...] = _rms(h, gf_ref[...])


def _combine(h1, rt, p2, y, gp, wpg, wp, gf, tm):
    t = h1.shape[0]
    nt = t // tm
    tile = lambda n: pl.BlockSpec((tm, n), lambda i: (i, 0))
    full = lambda a, b: pl.BlockSpec((a, b), lambda i: (0, 0))
    ytile = lambda k: pl.BlockSpec((tm * ROW_TILES, LANES), lambda i: (i + k * nt, 0))
    return pl.pallas_call(
        _combine_kernel,
        grid=(nt,),
        in_specs=[tile(D_MODEL), tile(LANES), tile(D_PLE), ytile(0), ytile(1),
                  full(1, D_MODEL), full(D_MODEL, D_MODEL), full(D_PLE, D_MODEL), full(1, D_MODEL)],
        out_specs=tile(D_MODEL),
        out_shape=jax.ShapeDtypeStruct((t, D_MODEL), F32),
        compiler_params=_cparams("parallel"),
        name="moe_combine_ple",
    )(h1, rt, p2, y, y, gp, wpg, wp, gf)


def _block_diag(w):
    n, c, d = w.shape
    eye = jnp.eye(n, dtype=w.dtype)
    return (w[:, :, None, :] * eye[:, None, :, None]).reshape(n * c, n * d)


def _lane_row(vals, offset):
    return jnp.zeros((1, LANES), F32).at[0, offset:offset + vals.shape[0]].set(vals)


def kernel(x, p, norm_mix, w_in, lru_conv_w, lru_conv_b, lru_wa, lru_ba, lru_wi, lru_bi, lru_lambda,
           lru_out_norm, gdn_conv_w, gdn_a_log, gdn_dt_bias, gdn_out_norm, w_out, norm_ffn,
           w_router_group, b_router_group, w_router_expert, b_router_expert, w_exp_gate, w_exp_up,
           w_exp_down, norm_ple, w_ple_gate, w_ple, norm_final):
    bsz, seq, d = x.shape
    t = bsz * seq
    depth = w_in.shape[0]
    assert depth == 1, "the final norm is fused into the last layer's combine kernel"
    n_blocks = -(-t * TOP_K // MOE_BLOCK) + N_EXPERTS
    row = lambda v: v.reshape(1, -1).astype(F32)
    h = x.reshape(t, d).astype(F32)
    for l in range(depth):
        w_in_p = jnp.pad(w_in[l], ((0, 0), (0, PROJ_COLS - w_in.shape[2]))).astype(BF16)
        proj = _in_proj(h, row(norm_mix[l]), w_in_p, 512)
        w_gates = jnp.concatenate([_block_diag(lru_wa[l]), _block_diag(lru_wi[l])], axis=1).astype(BF16)
        b_gates = jnp.concatenate([lru_ba[l], lru_bi[l]]).reshape(1, -1)
        y_lru = _lru(proj, lru_conv_w[l], row(lru_conv_b[l]), w_gates, b_gates, row(lru_lambda[l]),
                     row(lru_out_norm[l]), bsz, seq, 512)
        cw = gdn_conv_w[l].reshape(CONV_W, 3, D_GDN).transpose(1, 0, 2)
        y_gdn = _gdn(proj, cw, _lane_row(gdn_a_log[l], GDN_HEADS), _lane_row(gdn_dt_bias[l], GDN_HEADS),
                     row(gdn_out_norm[l]), bsz, seq, 512)
        w_r = jnp.pad(jnp.concatenate([w_router_group[l], w_router_expert[l]], axis=1),
                      ((0, 0), (0, LANES - N_GROUPS - N_EXPERTS)))
        w_r_hi = w_r.astype(BF16)
        w_r_lo = (w_r - w_r_hi.astype(F32)).astype(BF16)
        w_r = jnp.concatenate([w_r_hi, w_r_hi, w_r_lo], axis=0)
        b_r = _lane_row(jnp.concatenate([b_router_group[l], b_router_expert[l]]), 0)
        h1, xn2, rt = _out_router(h, y_lru, y_gdn, w_out[l].astype(BF16), row(norm_ffn[l]), w_r, b_r, 512)
        dest2, seg = _slots(rt, 512)
        dest = dest2[:, :TOP_K].T
        pad_end = seg[0, :N_EXPERTS].astype(jnp.int32)
        counts = seg[1, :N_EXPERTS].astype(jnp.int32)
        block_start = jnp.arange(n_blocks, dtype=jnp.int32) * MOE_BLOCK
        block_expert = jnp.minimum(jnp.sum(pad_end[None, :] <= block_start[:, None], axis=1),
                                   N_EXPERTS - 1).astype(jnp.int32)
        seg_end = (pad_end - (-counts % MOE_BLOCK))[block_expert]
        n_valid = jnp.where(block_start < pad_end[N_EXPERTS - 1],
                            jnp.clip(seg_end - block_start, 0, MOE_BLOCK), 0).astype(jnp.int32)
        tiles = lambda a: a.reshape(-1, ROW_TILES, LANES)
        xs = _sc_scatter_rows(tiles(xn2), dest, n_blocks * MOE_BLOCK)
        ys = _experts(block_expert, n_valid, xs.reshape(-1, LANES), w_exp_gate[l], w_exp_up[l], w_exp_down[l])
        y = _sc_gather_rows(tiles(ys), dest.reshape(-1))
        h = _combine(h1, rt, p[l].reshape(t, -1).astype(F32), y.reshape(-1, LANES), row(norm_ple[l]),
                     w_ple_gate[l].astype(BF16), w_ple[l].astype(BF16), row(norm_final), 512)
    return h.reshape(bsz, seq, d).astype(x.dtype)
```

```python
import functools

import jax
import jax.numpy as jnp
import numpy as np
from jax import lax
from jax.experimental import pallas as pl
from jax.experimental.pallas import tpu as pltpu
from jax.experimental.pallas import tpu_sc as plsc

D_MODEL = 1024
D_LRU = 512
LRU_BLOCKS = 8
LRU_BLOCK_W = D_LRU // LRU_BLOCKS
LRU_C = 8.0
D_GDN = 512
GDN_HEADS = 4
GDN_HEAD_DIM = D_GDN // GDN_HEADS
CONV_W = 4
CHUNK = 64
N_GROUPS = 4
EXPERTS_PER_GROUP = 8
N_EXPERTS = N_GROUPS * EXPERTS_PER_GROUP
TOP_K = 2
D_EXPERT = 512
MOE_BLOCK = 512
D_PLE = 256
EPS = 1e-6

LANES = 128
SUBLANES = 8
ROW_TILES = D_MODEL // LANES
SC_CORES = 2
SC_SUBCORES = 16
SC_WORKERS = SC_CORES * SC_SUBCORES
SC_WINDOW = 32
PROJ_COLS = 2 * D_LRU + 4 * D_GDN + LANES
O_BA = 2 * D_LRU + 4 * D_GDN
VMEM_LIMIT = 56 * 1024 * 1024

BF16 = jnp.bfloat16
F32 = jnp.float32


def _cparams(*sem):
    return pltpu.CompilerParams(dimension_semantics=sem, vmem_limit_bytes=VMEM_LIMIT)


def _rms(x, g):
    return x * lax.rsqrt(jnp.mean(x * x, axis=-1, keepdims=True) + EPS) * g


def _dot(a, b):
    return jnp.dot(a.astype(BF16), b.astype(BF16), preferred_element_type=F32)


def _dot_nt(a, b):
    return lax.dot_general(a.astype(BF16), b.astype(BF16), (((1,), (1,)), ((), ())),
                           preferred_element_type=F32)


def _dot_tn(a, b):
    return lax.dot_general(a.astype(BF16), b.astype(BF16), (((0,), (0,)), ((), ())),
                           preferred_element_type=F32)


def _causal_conv(xp_ref, x, w):
    rows = x.shape[0]
    xp_ref[SUBLANES:, :] = x
    acc = x * w[CONV_W - 1:CONV_W]
    for j in range(1, CONV_W):
        acc = acc + xp_ref[pl.ds(SUBLANES - j, rows), :] * w[CONV_W - 1 - j:CONV_W - j]
    xp_ref[:SUBLANES, :] = x[rows - SUBLANES:]
    return acc


def _store_token_tiles(ref, x):
    rows = x.shape[0]
    for j in range(ROW_TILES):
        ref[pl.ds(j, rows, stride=ROW_TILES), :] = x[:, j * LANES:(j + 1) * LANES]


def _load_token_tiles(ref, rows, dtype=F32):
    return jnp.concatenate([ref[pl.ds(j, rows, stride=ROW_TILES), :].astype(dtype) for j in range(ROW_TILES)],
                           axis=1)


def _in_proj_kernel(x_ref, g_ref, w_ref, o_ref, *, col_tile):
    xn = _rms(x_ref[...], g_ref[...]).astype(BF16)
    for j in range(PROJ_COLS // col_tile):
        cols = slice(j * col_tile, (j + 1) * col_tile)
        o_ref[:, cols] = jnp.dot(xn, w_ref[:, cols], preferred_element_type=F32)


def _in_proj(x2, g, w, tm):
    t = x2.shape[0]
    return pl.pallas_call(
        functools.partial(_in_proj_kernel, col_tile=PROJ_COLS // 5),
        grid=(t // tm,),
        in_specs=[pl.BlockSpec((tm, D_MODEL), lambda i: (i, 0)),
                  pl.BlockSpec((1, D_MODEL), lambda i: (0, 0)),
                  pl.BlockSpec((D_MODEL, PROJ_COLS), lambda i: (0, 0))],
        out_specs=pl.BlockSpec((tm, PROJ_COLS), lambda i: (i, 0)),
        out_shape=jax.ShapeDtypeStruct((t, PROJ_COLS), F32),
        compiler_params=_cparams("parallel"),
        name="in_proj",
    )(x2, g, w)


def _lru_kernel(x_ref, gate_ref, cw_ref, cb_ref, wg_ref, bg_ref, lam_ref, og_ref, o_ref,
                tail_ref, h_ref):
    ts = x_ref.shape[0]

    @pl.when(pl.program_id(1) == 0)
    def _():
        tail_ref[:SUBLANES, :] = jnp.zeros((SUBLANES, D_LRU), F32)
        h_ref[...] = jnp.zeros_like(h_ref)

    xc = _causal_conv(tail_ref, x_ref[...], cw_ref[...]) + cb_ref[...]
    gates = jax.nn.sigmoid(_dot(xc, wg_ref[...]) + bg_ref[...])
    r = gates[:, :D_LRU]
    i = gates[:, D_LRU:]
    log_a = LRU_C * r * jax.nn.log_sigmoid(lam_ref[...])
    a = jnp.exp(log_a)
    th = jnp.tanh(log_a)
    u = jnp.sqrt(-2.0 * th / (1.0 - th)) * (i * xc)
    row = lax.broadcasted_iota(jnp.int32, a.shape, 0)
    d = 1
    while d < ts:
        keep = row >= d
        a_prev = jnp.where(keep, pltpu.roll(a, d, axis=0), 1.0)
        u_prev = jnp.where(keep, pltpu.roll(u, d, axis=0), 0.0)
        u = a * u_prev + u
        a = a * a_prev
        d *= 2
    h = a * h_ref[...] + u
    h_ref[...] = h[ts - 1:]
    y = h * jax.nn.gelu(gate_ref[...])
    o_ref[...] = _rms(y, og_ref[...]).astype(o_ref.dtype)


def _lru(proj, cw, cb, wg, bg, lam, og, bsz, seq, ts):
    nt = seq // ts
    row = lambda b, s: b * nt + s
    vec = lambda n: pl.BlockSpec((1, n), lambda b, s: (0, 0))
    return pl.pallas_call(
        _lru_kernel,
        grid=(bsz, nt),
        in_specs=[pl.BlockSpec((ts, D_LRU), lambda b, s: (row(b, s), 0)),
                  pl.BlockSpec((ts, D_LRU), lambda b, s: (row(b, s), 1)),
                  pl.BlockSpec((CONV_W, D_LRU), lambda b, s: (0, 0)),
                  vec(D_LRU),
                  pl.BlockSpec((D_LRU, 2 * D_LRU), lambda b, s: (0, 0)),
                  vec(2 * D_LRU), vec(D_LRU), vec(D_LRU)],
        out_specs=pl.BlockSpec((ts, D_LRU), lambda b, s: (row(b, s), 0)),
        out_shape=jax.ShapeDtypeStruct((bsz * seq, D_LRU), BF16),
        scratch_shapes=[pltpu.VMEM((SUBLANES + ts, D_LRU), F32), pltpu.VMEM((1, D_LRU), F32)],
        compiler_params=_cparams("parallel", "arbitrary"),
        name="rglru",
    )(proj, proj, cw, cb, wg, bg, lam, og)


def _gdn_kernel(q_ref, k_ref, v_ref, z_ref, ba_ref, cw_ref, alog_ref, dtb_ref, og_ref, o_ref,
                qt_ref, kt_ref, vt_ref, qs_ref, ks_ref, vs_ref, bs_ref, gs_ref, gc_ref, gct_ref, st_ref,
                *, group_chunks):
    ts = q_ref.shape[0]
    dk = GDN_HEAD_DIM

    @pl.when(pl.program_id(1) == 0)
    def _():
        for tail_ref in (qt_ref, kt_ref, vt_ref):
            tail_ref[:SUBLANES, :] = jnp.zeros((SUBLANES, D_GDN), F32)
        st_ref[...] = jnp.zeros_like(st_ref)

    def conv_silu(x_ref, tail_ref, part):
        return jax.nn.silu(_causal_conv(tail_ref, x_ref[...], cw_ref[part]))

    def l2n(x, scale):
        parts = []
        for h in range(GDN_HEADS):
            xh = x[:, h * dk:(h + 1) * dk]
            parts.append(xh * (lax.rsqrt(jnp.sum(xh * xh, axis=-1, keepdims=True) + EPS) * scale))
        return jnp.concatenate(parts, axis=1)

    qs_ref[...] = l2n(conv_silu(q_ref, qt_ref, 0), dk ** -0.5)
    ks_ref[...] = l2n(conv_silu(k_ref, kt_ref, 1), 1.0)
    vs_ref[...] = conv_silu(v_ref, vt_ref, 2)
    ba = ba_ref[...]
    bs_ref[...] = jax.nn.sigmoid(ba)
    gs_ref[...] = -jnp.exp(alog_ref[...]) * jax.nn.softplus(ba + dtb_ref[...])

    ri = lax.broadcasted_iota(jnp.int32, (CHUNK, CHUNK), 0)
    ci = lax.broadcasted_iota(jnp.int32, (CHUNK, CHUNK), 1)
    causal = ri >= ci
    strict = ri > ci
    tril = causal.astype(F32)
    eye = (ri == ci).astype(F32)
    og = og_ref[...]

    def chunk_terms(pairs):
        n = range(len(pairs))
        rows = [slice(c * CHUNK, (c + 1) * CHUNK) for c, _ in pairs]
        cols = [slice(h * dk, (h + 1) * dk) for _, h in pairs]
        gl = [GDN_HEADS + h for _, h in pairs]
        kh = [ks_ref[rows[i], cols[i]] for i in n]
        kb = [kh[i] * bs_ref[rows[i], pairs[i][1]:pairs[i][1] + 1] for i in n]
        r = [_dot_nt(jnp.concatenate([kb[i], qs_ref[rows[i], cols[i]]], axis=0), kh[i]) for i in n]
        gcol = [gc_ref[rows[i], gl[i]:gl[i] + 1] for i in n]
        decay = []
        for i in n:
            diff = gcol[i] - gct_ref[gl[i]:gl[i] + 1, rows[i]]
            decay.append(jnp.where(causal, jnp.exp(jnp.where(causal, diff, 0.0)), 0.0))
        a = [jnp.where(strict, r[i][:CHUNK] * decay[i], 0.0) for i in n]
        qk = [(r[i][CHUNK:] * decay[i]).astype(BF16) for i in n]
        tinv = [eye - a[i] for i in n]
        p = 2
        while p < CHUNK:
            a = [_dot(a[i], a[i]) for i in n]
            tinv = [tinv[i] + _dot(tinv[i], a[i]) for i in n]
            p *= 2
        eg = [jnp.exp(gcol[i]) for i in n]
        rhs = [jnp.concatenate([vs_ref[rows[i], cols[i]] * bs_ref[rows[i], pairs[i][1]:pairs[i][1] + 1],
                                kb[i] * eg[i]], axis=1) for i in n]
        uw = [_dot(tinv[i], rhs[i]).astype(BF16) for i in n]
        qk_uw = [jnp.dot(qk[i], uw[i], preferred_element_type=F32) for i in n]
        glast = [gc_ref[(c + 1) * CHUNK - 1:(c + 1) * CHUNK, gl[i]:gl[i] + 1] for i, (c, _) in enumerate(pairs)]
        kd_uw = [_dot_tn(kh[i] * jnp.exp(glast[i] - gcol[i]), uw[i]) for i in n]
        out = {}
        for i in n:
            lhs = jnp.concatenate([kd_uw[i][:, dk:], qs_ref[rows[i], cols[i]] * eg[i] - qk_uw[i][:, dk:]],
                                  axis=0).astype(BF16)
            out[pairs[i]] = (lhs, kd_uw[i][:, :dk], qk_uw[i][:, :dk], jnp.exp(glast[i]))
        return out

    nc = ts // CHUNK
    for c in range(nc):
        rows = slice(c * CHUNK, (c + 1) * CHUNK)
        gc_ref[rows, :] = jnp.dot(tril, gs_ref[rows, :], precision=lax.Precision.HIGHEST,
                                  preferred_element_type=F32)
    gct_ref[...] = gc_ref[...].T
    terms = {}
    for c0 in range(0, nc, group_chunks):
        terms.update(chunk_terms([(c, h) for c in range(c0, c0 + group_chunks) for h in range(GDN_HEADS)]))
    s = [st_ref[h] for h in range(GDN_HEADS)]
    for c in range(nc):
        rows = slice(c * CHUNK, (c + 1) * CHUNK)
        r = [jnp.dot(terms[c, h][0], s[h].astype(BF16), preferred_element_type=F32) for h in range(GDN_HEADS)]
        for h in range(GDN_HEADS):
            cols = slice(h * dk, (h + 1) * dk)
            _, c_add, o_add, egl = terms[c, h]
            o = r[h][dk:] + o_add
            s[h] = egl * s[h] - r[h][:dk] + c_add
            o_ref[rows, cols] = (_rms(o, og) * jax.nn.silu(z_ref[rows, cols])).astype(o_ref.dtype)
    for h in range(GDN_HEADS):
        st_ref[h] = s[h]


def _gdn(proj, cw, alog, dtb, og, bsz, seq, ts):
    nt = seq // ts
    row = lambda b, s: b * nt + s
    col = lambda c: pl.BlockSpec((ts, D_GDN), lambda b, s: (row(b, s), c))
    c0 = 2 * D_LRU // D_GDN
    return pl.pallas_call(
        functools.partial(_gdn_kernel, group_chunks=4),
        grid=(bsz, nt),
        in_specs=[col(c0), col(c0 + 1), col(c0 + 2), col(c0 + 3),
                  pl.BlockSpec((ts, LANES), lambda b, s: (row(b, s), O_BA // LANES)),
                  pl.BlockSpec((3, CONV_W, D_GDN), lambda b, s: (0, 0, 0)),
                  pl.BlockSpec((1, LANES), lambda b, s: (0, 0)),
                  pl.BlockSpec((1, LANES), lambda b, s: (0, 0)),
                  pl.BlockSpec((1, GDN_HEAD_DIM), lambda b, s: (0, 0))],
        out_specs=pl.BlockSpec((ts, D_GDN), lambda b, s: (row(b, s), 0)),
        out_shape=jax.ShapeDtypeStruct((bsz * seq, D_GDN), BF16),
        scratch_shapes=[pltpu.VMEM((SUBLANES + ts, D_GDN), F32)] * 3
        + [pltpu.VMEM((ts, D_GDN), F32)] * 3
        + [pltpu.VMEM((ts, LANES), F32)] * 3
        + [pltpu.VMEM((LANES, ts), F32)]
        + [pltpu.VMEM((GDN_HEADS, GDN_HEAD_DIM, GDN_HEAD_DIM), F32)],
        compiler_params=_cparams("parallel", "arbitrary"),
        name="gdn",
    )(proj, proj, proj, proj, proj, cw, alog, dtb, og)


def _out_router_kernel(x_ref, yl_ref, yg_ref, wo_ref, g_ref, wr_ref, br_ref, h_ref, xn_ref, rt_ref):
    h = x_ref[...] + jnp.dot(yl_ref[...], wo_ref[:D_LRU, :], preferred_element_type=F32) \
        + jnp.dot(yg_ref[...], wo_ref[D_LRU:, :], preferred_element_type=F32)
    h_ref[...] = h
    xn = _rms(h, g_ref[...])
    _store_token_tiles(xn_ref, xn)
    xh = xn.astype(BF16)
    xl = (xn - xh.astype(F32)).astype(BF16)
    logits = jnp.dot(jnp.concatenate([xh, xl, xh], axis=1), wr_ref[...],
                     preferred_element_type=F32) + br_ref[...]
    lane = lax.broadcasted_iota(jnp.int32, logits.shape, 1).astype(F32)
    big = jnp.float32(2 * LANES)
    ninf = jnp.float32(-jnp.inf)

    def top1(vals):
        m = jnp.max(vals, axis=-1, keepdims=True)
        return m, jnp.min(jnp.where(vals == m, lane, big), axis=-1, keepdims=True)

    gl = jnp.where(lane < N_GROUPS, logits, ninf)
    gmax, gsel = top1(gl)
    p_group = 1.0 / jnp.sum(jnp.exp(gl - gmax), axis=-1, keepdims=True)
    lo = N_GROUPS + EXPERTS_PER_GROUP * gsel
    el = jnp.where((lane >= lo) & (lane < lo + EXPERTS_PER_GROUP), logits, ninf)
    m1, i1 = top1(el)
    m2, i2 = top1(jnp.where(lane == i1, ninf, el))
    r = jnp.exp(m2 - m1)
    g1 = p_group / (1.0 + r)
    g2 = p_group * r / (1.0 + r)
    rt = jnp.where(lane == 0, i1 - N_GROUPS,
                   jnp.where(lane == 1, i2 - N_GROUPS,
                             jnp.where(lane == 2, g1, jnp.where(lane == 3, g2, 0.0))))
    rt_ref[...] = rt


def _out_router(x2, yl, yg, wo, g, wr, br, tm):
    t = x2.shape[0]
    tile = lambda n: pl.BlockSpec((tm, n), lambda i: (i, 0))
    full = lambda a, b: pl.BlockSpec((a, b), lambda i: (0, 0))
    return pl.pallas_call(
        _out_router_kernel,
        grid=(t // tm,),
        in_specs=[tile(D_MODEL), tile(D_LRU), tile(D_GDN), full(D_MODEL, D_MODEL), full(1, D_MODEL),
                  full(3 * D_MODEL, LANES), full(1, LANES)],
        out_specs=[tile(D_MODEL), pl.BlockSpec((tm * ROW_TILES, LANES), lambda i: (i, 0)), tile(LANES)],
        out_shape=[jax.ShapeDtypeStruct((t, D_MODEL), F32),
                   jax.ShapeDtypeStruct((t * ROW_TILES, LANES), F32),
                   jax.ShapeDtypeStruct((t, LANES), F32)],
        compiler_params=_cparams("parallel"),
        name="out_router",
    )(x2, yl, yg, wo, g, wr, br)


def _slots_kernel(rt_ref, dest_ref, pend_ref, cnt_ref, before_ref):
    phase = pl.program_id(0)
    i = pl.program_id(1)
    tm = rt_ref.shape[0]
    rt = rt_ref[...]
    lane = lax.broadcasted_iota(jnp.int32, rt.shape, 1)
    e0 = rt[:, 0:1].astype(jnp.int32)
    e1 = rt[:, 1:2].astype(jnp.int32)
    member = ((lane == e0) | (lane == e1)).astype(BF16)
    tile_counts = jnp.dot(jnp.ones((SUBLANES, tm), BF16), member, preferred_element_type=F32)

    @pl.when((phase == 0) & (i == 0))
    def _():
        cnt_ref[...] = jnp.zeros_like(cnt_ref)
        ri = lax.broadcasted_iota(jnp.int32, (tm, tm), 0)
        ci = lax.broadcasted_iota(jnp.int32, (tm, tm), 1)
        before_ref[...] = (ri > ci).astype(BF16)

    @pl.when(phase == 0)
    def _():
        cnt_ref[...] += tile_counts

    @pl.when((phase == 1) & (i == 0))
    def _():
        cnt = cnt_ref[...]
        padded = jnp.ceil(cnt / MOE_BLOCK) * MOE_BLOCK
        l8 = lax.broadcasted_iota(jnp.int32, cnt.shape, 1)
        incl = padded
        d = 1
        while d < LANES:
            incl = incl + jnp.where(l8 >= d, pltpu.roll(incl, d, axis=1), 0.0)
            d *= 2
        sub = lax.broadcasted_iota(jnp.int32, cnt.shape, 0)
        pend_ref[...] = jnp.where(sub == 0, incl, cnt)
        cnt_ref[...] = incl - padded

    @pl.when(phase == 1)
    def _():
        start = cnt_ref[...]
        pos = start[0:1, :] + jnp.dot(before_ref[...], member, preferred_element_type=F32)
        d0 = jnp.sum(jnp.where(lane == e0, pos, 0.0), axis=-1, keepdims=True)
        d1 = jnp.sum(jnp.where(lane == e1, pos, 0.0), axis=-1, keepdims=True)
        dest = jnp.where(lane == 0, d0, jnp.where(lane == 1, d1, 0.0))
        dest_ref[...] = dest.T[:SUBLANES].astype(jnp.int32)
        cnt_ref[...] = start + tile_counts


def _slots(rt, tm):
    t = rt.shape[0]
    return pl.pallas_call(
        _slots_kernel,
        grid=(2, t // tm),
        in_specs=[pl.BlockSpec((tm, LANES), lambda p, i: (i, 0))],
        out_specs=[pl.BlockSpec((SUBLANES, tm), lambda p, i: (0, i * p)),
                   pl.BlockSpec((SUBLANES, LANES), lambda p, i: (0, 0))],
        out_shape=[jax.ShapeDtypeStruct((SUBLANES, t), jnp.int32),
                   jax.ShapeDtypeStruct((SUBLANES, LANES), F32)],
        scratch_shapes=[pltpu.VMEM((SUBLANES, LANES), F32), pltpu.VMEM((tm, tm), BF16)],
        compiler_params=_cparams("arbitrary", "arbitrary"),
        name="moe_slots",
    )(rt)


def _sc_mesh():
    return plsc.VectorSubcoreMesh(core_axis_name="c", subcore_axis_name="s")


def _sc_worker():
    return lax.axis_index("s") * SC_CORES + lax.axis_index("c")


def _sc_scatter_rows(src, idx, n_rows):
    t = src.shape[0]
    per_w = t // SC_WORKERS
    n_win = per_w // SC_WINDOW
    idx = idx.reshape(TOP_K, SC_WORKERS, n_win, SC_WINDOW).transpose(1, 0, 2, 3)

    @functools.partial(
        pl.kernel, mesh=_sc_mesh(),
        out_type=jax.ShapeDtypeStruct((n_rows,) + src.shape[1:], src.dtype),
        scratch_types=[pltpu.VMEM((TOP_K, n_win, SC_WINDOW), jnp.int32),
                       pltpu.VMEM((2, SC_WINDOW) + src.shape[1:], src.dtype),
                       pltpu.SemaphoreType.DMA((2,)), pltpu.SemaphoreType.DMA((2,))],
        compiler_params=pltpu.CompilerParams(use_tc_tiling_on_sc=True),
        name="sc_dispatch")
    def scatter(src_hbm, idx_hbm, out_hbm, idx_v, rows_v, lsem, ssem):
        wid = _sc_worker()
        base = wid * per_w
        pltpu.sync_copy(idx_hbm.at[wid], idx_v)

        def load(w, slot):
            return pltpu.make_async_copy(src_hbm.at[pl.ds(base + w * SC_WINDOW, SC_WINDOW)], rows_v.at[slot],
                                         lsem.at[slot])

        def put(w, slot, k):
            return pltpu.make_async_copy(rows_v.at[slot], out_hbm.at[idx_v.at[k, w]], ssem.at[slot])

        load(0, 0).start()

        @pl.loop(0, n_win, step=2)
        def _(w0):
            for s in range(2):
                w = w0 + s

                @pl.when(w + 1 < n_win)
                def _():
                    @pl.when(w >= 1)
                    def _():
                        for k in range(TOP_K):
                            put(w - 1, 1 - s, k).wait()

                    load(w + 1, 1 - s).start()

                load(w, s).wait()
                for k in range(TOP_K):
                    put(w, s, k).start()

        for s in range(2):
            for k in range(TOP_K):
                put(n_win - 2 + s, s, k).wait()

    return scatter(src, idx)


def _sc_gather_rows(table, idx):
    b = idx.shape[0]
    per_w = b // SC_WORKERS
    n_win = per_w // SC_WINDOW
    idx = idx.reshape(SC_WORKERS, n_win, SC_WINDOW)

    @functools.partial(
        pl.kernel, mesh=_sc_mesh(),
        out_type=jax.ShapeDtypeStruct((b,) + table.shape[1:], table.dtype),
        scratch_types=[pltpu.VMEM((n_win, SC_WINDOW), jnp.int32),
                       pltpu.VMEM((2, SC_WINDOW) + table.shape[1:], table.dtype),
                       pltpu.SemaphoreType.DMA((2,)), pltpu.SemaphoreType.DMA((2,))],
        compiler_params=pltpu.CompilerParams(use_tc_tiling_on_sc=True),
        name="sc_combine_gather")
    def gather(table_hbm, idx_hbm, out_hbm, idx_v, rows_v, gsem, psem):
        wid = _sc_worker()
        base = wid * per_w
        pltpu.sync_copy(idx_hbm.at[wid], idx_v)

        def get(w, slot):
            return pltpu.make_async_copy(table_hbm.at[idx_v.at[w]], rows_v.at[slot], gsem.at[slot])

        def put(w, slot):
            return pltpu.make_async_copy(rows_v.at[slot], out_hbm.at[pl.ds(base + w * SC_WINDOW, SC_WINDOW)],
                                         psem.at[slot])

        get(0, 0).start()

        @pl.loop(0, n_win, step=2)
        def _(w0):
            for s in range(2):
                w = w0 + s

                @pl.when(w + 1 < n_win)
                def _():
                    @pl.when(w >= 1)
                    def _():
                        put(w - 1, 1 - s).wait()

                    get(w + 1, 1 - s).start()

                get(w, s).wait()
                put(w, s).start()

        for s in range(2):
            put(n_win - 2 + s, s).wait()

    return gather(table, idx)


def _experts_kernel(be_ref, nv_ref, xs_ref, wg_ref, wu_ref, wd_ref, ys_ref, wgb_ref, wub_ref, wdb_ref):
    b = pl.program_id(0)
    prev = be_ref[jnp.maximum(b - 1, 0)]

    @pl.when((b == 0) | (be_ref[b] != prev))
    def _():
        wgb_ref[...] = wg_ref[0].astype(BF16)
        wub_ref[...] = wu_ref[0].astype(BF16)
        wdb_ref[...] = wd_ref[0].astype(BF16)

    @pl.when(nv_ref[b] > 0)
    def _():
        row = lax.broadcasted_iota(jnp.int32, (MOE_BLOCK, D_MODEL), 0)
        xb = jnp.where(row < nv_ref[b], _load_token_tiles(xs_ref, MOE_BLOCK), 0.0).astype(BF16)
        hg = jnp.dot(xb, wgb_ref[...], preferred_element_type=F32)
        hu = jnp.dot(xb, wub_ref[...], preferred_element_type=F32)
        hb = (jax.nn.silu(hg) * hu).astype(BF16)
        _store_token_tiles(ys_ref, jnp.dot(hb, wdb_ref[...], preferred_element_type=F32))

    @pl.when(nv_ref[b] == 0)
    def _():
        ys_ref[...] = jnp.zeros_like(ys_ref)


def _experts(block_expert, n_valid, xs, wg, wu, wd):
    nb = xs.shape[0] // (MOE_BLOCK * ROW_TILES)
    blk = pl.BlockSpec((MOE_BLOCK * ROW_TILES, LANES), lambda b, be, nv: (b, 0))
    wspec = lambda a, c: pl.BlockSpec((1, a, c), lambda b, be, nv: (be[b], 0, 0))
    return pl.pallas_call(
        _experts_kernel,
        grid_spec=pltpu.PrefetchScalarGridSpec(
            num_scalar_prefetch=2,
            grid=(nb,),
            in_specs=[blk, wspec(D_MODEL, D_EXPERT), wspec(D_MODEL, D_EXPERT), wspec(D_EXPERT, D_MODEL)],
            out_specs=blk,
            scratch_shapes=[pltpu.VMEM((D_MODEL, D_EXPERT), BF16), pltpu.VMEM((D_MODEL, D_EXPERT), BF16),
                            pltpu.VMEM((D_EXPERT, D_MODEL), BF16)]),
        out_shape=jax.ShapeDtypeStruct(xs.shape, F32),
        compiler_params=_cparams("arbitrary"),
        name="moe_experts",
    )(block_expert, n_valid, xs, wg, wu, wd)


def _combine_kernel(h_ref, rt_ref, p_ref, y0_ref, y1_ref, gp_ref, wpg_ref, wp_ref, gf_ref, o_ref):
    tm = h_ref.shape[0]
    rt = rt_ref[...]
    h = h_ref[...] + (_load_token_tiles(y0_ref, tm) * rt[:, 2:3] + _load_token_tiles(y1_ref, tm) * rt[:, 3:4])
    gate = jax.nn.sigmoid(_dot(_rms(h, gp_ref[...]), wpg_ref[...]))
    h = h + _dot(p_ref[...], wp_ref[...]) * gate
    o_ref[...] = _rms(h, gf_ref[...])


def _combine(h1, rt, p2, y, gp, wpg, wp, gf, tm):
    t = h1.shape[0]
    nt = t // tm
    tile = lambda n: pl.BlockSpec((tm, n), lambda i: (i, 0))
    full = lambda a, b: pl.BlockSpec((a, b), lambda i: (0, 0))
    ytile = lambda k: pl.BlockSpec((tm * ROW_TILES, LANES), lambda i: (i + k * nt, 0))
    return pl.pallas_call(
        _combine_kernel,
        grid=(nt,),
        in_specs=[tile(D_MODEL), tile(LANES), tile(D_PLE), ytile(0), ytile(1),
                  full(1, D_MODEL), full(D_MODEL, D_MODEL), full(D_PLE, D_MODEL), full(1, D_MODEL)],
        out_specs=tile(D_MODEL),
        out_shape=jax.ShapeDtypeStruct((t, D_MODEL), F32),
        compiler_params=_cparams("parallel"),
        name="moe_combine_ple",
    )(h1, rt, p2, y, y, gp, wpg, wp, gf)


def _block_diag(w):
    n, c, d = w.shape
    eye = jnp.eye(n, dtype=w.dtype)
    return (w[:, :, None, :] * eye[:, None, :, None]).reshape(n * c, n * d)


def _lane_row(vals, offset):
    return jnp.zeros((1, LANES), F32).at[0, offset:offset + vals.shape[0]].set(vals)


def kernel(x, p, norm_mix, w_in, lru_conv_w, lru_conv_b, lru_wa, lru_ba, lru_wi, lru_bi, lru_lambda,
           lru_out_norm, gdn_conv_w, gdn_a_log, gdn_dt_bias, gdn_out_norm, w_out, norm_ffn,
           w_router_group, b_router_group, w_router_expert, b_router_expert, w_exp_gate, w_exp_up,
           w_exp_down, norm_ple, w_ple_gate, w_ple, norm_final):
    bsz, seq, d = x.shape
    t = bsz * seq
    depth = w_in.shape[0]
    assert depth == 1, "the final norm is fused into the last layer's combine kernel"
    n_blocks = -(-t * TOP_K // MOE_BLOCK) + N_EXPERTS
    row = lambda v: v.reshape(1, -1).astype(F32)
    h = x.reshape(t, d).astype(F32)
    for l in range(depth):
        w_in_p = jnp.pad(w_in[l], ((0, 0), (0, PROJ_COLS - w_in.shape[2]))).astype(BF16)
        proj = _in_proj(h, row(norm_mix[l]), w_in_p, 512)
        w_gates = jnp.concatenate([_block_diag(lru_wa[l]), _block_diag(lru_wi[l])], axis=1).astype(BF16)
        b_gates = jnp.concatenate([lru_ba[l], lru_bi[l]]).reshape(1, -1)
        y_lru = _lru(proj, lru_conv_w[l], row(lru_conv_b[l]), w_gates, b_gates, row(lru_lambda[l]),
                     row(lru_out_norm[l]), bsz, seq, 512)
        cw = gdn_conv_w[l].reshape(CONV_W, 3, D_GDN).transpose(1, 0, 2)
        y_gdn = _gdn(proj, cw, _lane_row(gdn_a_log[l], GDN_HEADS), _lane_row(gdn_dt_bias[l], GDN_HEADS),
                     row(gdn_out_norm[l]), bsz, seq, 512)
        w_r = jnp.pad(jnp.concatenate([w_router_group[l], w_router_expert[l]], axis=1),
                      ((0, 0), (0, LANES - N_GROUPS - N_EXPERTS)))
        w_r_hi = w_r.astype(BF16)
        w_r_lo = (w_r - w_r_hi.astype(F32)).astype(BF16)
        w_r = jnp.concatenate([w_r_hi, w_r_hi, w_r_lo], axis=0)
        b_r = _lane_row(jnp.concatenate([b_router_group[l], b_router_expert[l]]), 0)
        h1, xn2, rt = _out_router(h, y_lru, y_gdn, w_out[l].astype(BF16), row(norm_ffn[l]), w_r, b_r, 512)
        dest8, seg = _slots(rt, 1024)
        dest = dest8[:TOP_K]
        pad_end = seg[0, :N_EXPERTS].astype(jnp.int32)
        counts = seg[1, :N_EXPERTS].astype(jnp.int32)
        block_start = jnp.arange(n_blocks, dtype=jnp.int32) * MOE_BLOCK
        block_expert = jnp.minimum(jnp.sum(pad_end[None, :] <= block_start[:, None], axis=1),
                                   N_EXPERTS - 1).astype(jnp.int32)
        seg_end = (pad_end - (-counts % MOE_BLOCK))[block_expert]
        n_valid = jnp.where(block_start < pad_end[N_EXPERTS - 1],
                            jnp.clip(seg_end - block_start, 0, MOE_BLOCK), 0).astype(jnp.int32)
        tiles = lambda a: a.reshape(-1, ROW_TILES, LANES)
        xs = _sc_scatter_rows(tiles(xn2), dest, n_blocks * MOE_BLOCK)
        ys = _experts(block_expert, n_valid, xs.reshape(-1, LANES), w_exp_gate[l], w_exp_up[l], w_exp_down[l])
        y = _sc_gather_rows(tiles(ys), dest.reshape(-1))
        h = _combine(h1, rt, p[l].reshape(t, -1).astype(F32), y.reshape(-1, LANES), row(norm_ple[l]),
                     w_ple_gate[l].astype(BF16), w_ple[l].astype(BF16), row(norm_final), 512)
    return h.reshape(bsz, seq, d).astype(x.dtype)
```

```python
import functools

import jax
import jax.numpy as jnp
import numpy as np
from jax import lax
from jax.experimental import pallas as pl
from jax.experimental.pallas import tpu as pltpu
from jax.experimental.pallas import tpu_sc as plsc

D_MODEL = 1024
D_LRU = 512
LRU_BLOCKS = 8
LRU_BLOCK_W = D_LRU // LRU_BLOCKS
LRU_C = 8.0
D_GDN = 512
GDN_HEADS = 4
GDN_HEAD_DIM = D_GDN // GDN_HEADS
CONV_W = 4
CHUNK = 64
N_GROUPS = 4
EXPERTS_PER_GROUP = 8
N_EXPERTS = N_GROUPS * EXPERTS_PER_GROUP
TOP_K = 2
D_EXPERT = 512
MOE_BLOCK = 512
D_PLE = 256
EPS = 1e-6

LANES = 128
SUBLANES = 8
ROW_TILES = D_MODEL // (2 * LANES)
SC_CORES = 2
SC_SUBCORES = 16
SC_WORKERS = SC_CORES * SC_SUBCORES
SC_WINDOW = 64
PROJ_COLS = 2 * D_LRU + 4 * D_GDN + LANES
O_BA = 2 * D_LRU + 4 * D_GDN
VMEM_LIMIT = 56 * 1024 * 1024

BF16 = jnp.bfloat16
F32 = jnp.float32


def _cparams(*sem):
    return pltpu.CompilerParams(dimension_semantics=sem, vmem_limit_bytes=VMEM_LIMIT)


def _rms(x, g):
    return x * lax.rsqrt(jnp.mean(x * x, axis=-1, keepdims=True) + EPS) * g


def _dot(a, b):
    return jnp.dot(a.astype(BF16), b.astype(BF16), preferred_element_type=F32)


def _dot_nt(a, b):
    return lax.dot_general(a.astype(BF16), b.astype(BF16), (((1,), (1,)), ((), ())),
                           preferred_element_type=F32)


def _dot_tn(a, b):
    return lax.dot_general(a.astype(BF16), b.astype(BF16), (((0,), (0,)), ((), ())),
                           preferred_element_type=F32)


def _causal_conv(xp_ref, x, w):
    rows = x.shape[0]
    xp_ref[SUBLANES:, :] = x
    acc = x * w[CONV_W - 1:CONV_W]
    for j in range(1, CONV_W):
        acc = acc + xp_ref[pl.ds(SUBLANES - j, rows), :] * w[CONV_W - 1 - j:CONV_W - j]
    xp_ref[:SUBLANES, :] = x[rows - SUBLANES:]
    return acc


def _store_token_rows(ref, x):
    rows = x.shape[0]
    bits = lax.bitcast_convert_type(x.astype(BF16).astype(F32), jnp.uint32)
    words = (bits[:, :D_MODEL // 2] >> 16) | bits[:, D_MODEL // 2:]
    for j in range(ROW_TILES):
        ref[pl.ds(j, rows, stride=ROW_TILES), :] = words[:, j * LANES:(j + 1) * LANES]


def _load_token_rows(ref, rows, dtype):
    words = [ref[pl.ds(j, rows, stride=ROW_TILES), :] for j in range(ROW_TILES)]
    lo = [lax.bitcast_convert_type(w << 16, F32).astype(dtype) for w in words]
    hi = [lax.bitcast_convert_type(w & jnp.uint32(0xFFFF0000), F32).astype(dtype) for w in words]
    return jnp.concatenate(lo + hi, axis=1)


def _in_proj_kernel(x_ref, g_ref, w_ref, o_ref, *, col_tile):
    xn = _rms(x_ref[...], g_ref[...]).astype(BF16)
    for j in range(PROJ_COLS // col_tile):
        cols = slice(j * col_tile, (j + 1) * col_tile)
        o_ref[:, cols] = jnp.dot(xn, w_ref[:, cols], preferred_element_type=F32)


def _in_proj(x2, g, w, tm):
    t = x2.shape[0]
    return pl.pallas_call(
        functools.partial(_in_proj_kernel, col_tile=PROJ_COLS // 5),
        grid=(t // tm,),
        in_specs=[pl.BlockSpec((tm, D_MODEL), lambda i: (i, 0)),
                  pl.BlockSpec((1, D_MODEL), lambda i: (0, 0)),
                  pl.BlockSpec((D_MODEL, PROJ_COLS), lambda i: (0, 0))],
        out_specs=pl.BlockSpec((tm, PROJ_COLS), lambda i: (i, 0)),
        out_shape=jax.ShapeDtypeStruct((t, PROJ_COLS), F32),
        compiler_params=_cparams("parallel"),
        name="in_proj",
    )(x2, g, w)


def _lru_kernel(x_ref, gate_ref, cw_ref, cb_ref, wg_ref, bg_ref, lam_ref, og_ref, o_ref,
                tail_ref, h_ref):
    ts = x_ref.shape[0]

    @pl.when(pl.program_id(1) == 0)
    def _():
        tail_ref[:SUBLANES, :] = jnp.zeros((SUBLANES, D_LRU), F32)
        h_ref[...] = jnp.zeros_like(h_ref)

    xc = _causal_conv(tail_ref, x_ref[...], cw_ref[...]) + cb_ref[...]
    gates = jax.nn.sigmoid(_dot(xc, wg_ref[...]) + bg_ref[...])
    r = gates[:, :D_LRU]
    i = gates[:, D_LRU:]
    log_a = LRU_C * r * jax.nn.log_sigmoid(lam_ref[...])
    a = jnp.exp(log_a)
    th = jnp.tanh(log_a)
    u = jnp.sqrt(-2.0 * th / (1.0 - th)) * (i * xc)
    row = lax.broadcasted_iota(jnp.int32, a.shape, 0)
    d = 1
    while d < ts:
        keep = row >= d
        a_prev = jnp.where(keep, pltpu.roll(a, d, axis=0), 1.0)
        u_prev = jnp.where(keep, pltpu.roll(u, d, axis=0), 0.0)
        u = a * u_prev + u
        a = a * a_prev
        d *= 2
    h = a * h_ref[...] + u
    h_ref[...] = h[ts - 1:]
    y = h * jax.nn.gelu(gate_ref[...])
    o_ref[...] = _rms(y, og_ref[...]).astype(o_ref.dtype)


def _lru(proj, cw, cb, wg, bg, lam, og, bsz, seq, ts):
    nt = seq // ts
    row = lambda b, s: b * nt + s
    vec = lambda n: pl.BlockSpec((1, n), lambda b, s: (0, 0))
    return pl.pallas_call(
        _lru_kernel,
        grid=(bsz, nt),
        in_specs=[pl.BlockSpec((ts, D_LRU), lambda b, s: (row(b, s), 0)),
                  pl.BlockSpec((ts, D_LRU), lambda b, s: (row(b, s), 1)),
                  pl.BlockSpec((CONV_W, D_LRU), lambda b, s: (0, 0)),
                  vec(D_LRU),
                  pl.BlockSpec((D_LRU, 2 * D_LRU), lambda b, s: (0, 0)),
                  vec(2 * D_LRU), vec(D_LRU), vec(D_LRU)],
        out_specs=pl.BlockSpec((ts, D_LRU), lambda b, s: (row(b, s), 0)),
        out_shape=jax.ShapeDtypeStruct((bsz * seq, D_LRU), BF16),
        scratch_shapes=[pltpu.VMEM((SUBLANES + ts, D_LRU), F32), pltpu.VMEM((1, D_LRU), F32)],
        compiler_params=_cparams("parallel", "arbitrary"),
        name="rglru",
    )(proj, proj, cw, cb, wg, bg, lam, og)


def _gdn_kernel(q_ref, k_ref, v_ref, z_ref, ba_ref, cw_ref, alog_ref, dtb_ref, og_ref, o_ref,
                qt_ref, kt_ref, vt_ref, qs_ref, ks_ref, vs_ref, bs_ref, gs_ref, gc_ref, gct_ref, st_ref,
                *, group_chunks):
    ts = q_ref.shape[0]
    dk = GDN_HEAD_DIM

    @pl.when(pl.program_id(1) == 0)
    def _():
        for tail_ref in (qt_ref, kt_ref, vt_ref):
            tail_ref[:SUBLANES, :] = jnp.zeros((SUBLANES, D_GDN), F32)
        st_ref[...] = jnp.zeros_like(st_ref)

    def conv_silu(x_ref, tail_ref, part):
        return jax.nn.silu(_causal_conv(tail_ref, x_ref[...], cw_ref[part]))

    def l2n(x, scale):
        parts = []
        for h in range(GDN_HEADS):
            xh = x[:, h * dk:(h + 1) * dk]
            parts.append(xh * (lax.rsqrt(jnp.sum(xh * xh, axis=-1, keepdims=True) + EPS) * scale))
        return jnp.concatenate(parts, axis=1)

    qs_ref[...] = l2n(conv_silu(q_ref, qt_ref, 0), dk ** -0.5)
    ks_ref[...] = l2n(conv_silu(k_ref, kt_ref, 1), 1.0)
    vs_ref[...] = conv_silu(v_ref, vt_ref, 2)
    ba = ba_ref[...]
    bs_ref[...] = jax.nn.sigmoid(ba)
    gs_ref[...] = -jnp.exp(alog_ref[...]) * jax.nn.softplus(ba + dtb_ref[...])

    ri = lax.broadcasted_iota(jnp.int32, (CHUNK, CHUNK), 0)
    ci = lax.broadcasted_iota(jnp.int32, (CHUNK, CHUNK), 1)
    causal = ri >= ci
    strict = ri > ci
    tril = causal.astype(F32)
    eye = (ri == ci).astype(F32)
    og = og_ref[...]

    def chunk_terms(pairs):
        n = range(len(pairs))
        rows = [slice(c * CHUNK, (c + 1) * CHUNK) for c, _ in pairs]
        cols = [slice(h * dk, (h + 1) * dk) for _, h in pairs]
        gl = [GDN_HEADS + h for _, h in pairs]
        kh = [ks_ref[rows[i], cols[i]] for i in n]
        kb = [kh[i] * bs_ref[rows[i], pairs[i][1]:pairs[i][1] + 1] for i in n]
        r = [_dot_nt(jnp.concatenate([kb[i], qs_ref[rows[i], cols[i]]], axis=0), kh[i]) for i in n]
        gcol = [gc_ref[rows[i], gl[i]:gl[i] + 1] for i in n]
        decay = []
        for i in n:
            diff = gcol[i] - gct_ref[gl[i]:gl[i] + 1, rows[i]]
            decay.append(jnp.where(causal, jnp.exp(jnp.where(causal, diff, 0.0)), 0.0))
        a = [jnp.where(strict, r[i][:CHUNK] * decay[i], 0.0) for i in n]
        qk = [(r[i][CHUNK:] * decay[i]).astype(BF16) for i in n]
        tinv = [eye - a[i] for i in n]
        p = 2
        while p < CHUNK:
            a = [_dot(a[i], a[i]) for i in n]
            tinv = [tinv[i] + _dot(tinv[i], a[i]) for i in n]
            p *= 2
        eg = [jnp.exp(gcol[i]) for i in n]
        rhs = [jnp.concatenate([vs_ref[rows[i], cols[i]] * bs_ref[rows[i], pairs[i][1]:pairs[i][1] + 1],
                                kb[i] * eg[i]], axis=1) for i in n]
        uw = [_dot(tinv[i], rhs[i]).astype(BF16) for i in n]
        qk_uw = [jnp.dot(qk[i], uw[i], preferred_element_type=F32) for i in n]
        glast = [gc_ref[(c + 1) * CHUNK - 1:(c + 1) * CHUNK, gl[i]:gl[i] + 1] for i, (c, _) in enumerate(pairs)]
        kd_uw = [_dot_tn(kh[i] * jnp.exp(glast[i] - gcol[i]), uw[i]) for i in n]
        out = {}
        for i in n:
            lhs = jnp.concatenate([kd_uw[i][:, dk:], qs_ref[rows[i], cols[i]] * eg[i] - qk_uw[i][:, dk:]],
                                  axis=0).astype(BF16)
            out[pairs[i]] = (lhs, kd_uw[i][:, :dk], qk_uw[i][:, :dk], jnp.exp(glast[i]))
        return out

    nc = ts // CHUNK
    for c in range(nc):
        rows = slice(c * CHUNK, (c + 1) * CHUNK)
        gc_ref[rows, :] = jnp.dot(tril, gs_ref[rows, :], precision=lax.Precision.HIGHEST,
                                  preferred_element_type=F32)
    gct_ref[...] = gc_ref[...].T
    terms = {}
    for c0 in range(0, nc, group_chunks):
        terms.update(chunk_terms([(c, h) for c in range(c0, c0 + group_chunks) for h in range(GDN_HEADS)]))
    s = [st_ref[h] for h in range(GDN_HEADS)]
    for c in range(nc):
        rows = slice(c * CHUNK, (c + 1) * CHUNK)
        r = [jnp.dot(terms[c, h][0], s[h].astype(BF16), preferred_element_type=F32) for h in range(GDN_HEADS)]
        for h in range(GDN_HEADS):
            cols = slice(h * dk, (h + 1) * dk)
            _, c_add, o_add, egl = terms[c, h]
            o = r[h][dk:] + o_add
            s[h] = egl * s[h] - r[h][:dk] + c_add
            o_ref[rows, cols] = (_rms(o, og) * jax.nn.silu(z_ref[rows, cols])).astype(o_ref.dtype)
    for h in range(GDN_HEADS):
        st_ref[h] = s[h]


def _gdn(proj, cw, alog, dtb, og, bsz, seq, ts):
    nt = seq // ts
    row = lambda b, s: b * nt + s
    col = lambda c: pl.BlockSpec((ts, D_GDN), lambda b, s: (row(b, s), c))
    c0 = 2 * D_LRU // D_GDN
    return pl.pallas_call(
        functools.partial(_gdn_kernel, group_chunks=4),
        grid=(bsz, nt),
        in_specs=[col(c0), col(c0 + 1), col(c0 + 2), col(c0 + 3),
                  pl.BlockSpec((ts, LANES), lambda b, s: (row(b, s), O_BA // LANES)),
                  pl.BlockSpec((3, CONV_W, D_GDN), lambda b, s: (0, 0, 0)),
                  pl.BlockSpec((1, LANES), lambda b, s: (0, 0)),
                  pl.BlockSpec((1, LANES), lambda b, s: (0, 0)),
                  pl.BlockSpec((1, GDN_HEAD_DIM), lambda b, s: (0, 0))],
        out_specs=pl.BlockSpec((ts, D_GDN), lambda b, s: (row(b, s), 0)),
        out_shape=jax.ShapeDtypeStruct((bsz * seq, D_GDN), BF16),
        scratch_shapes=[pltpu.VMEM((SUBLANES + ts, D_GDN), F32)] * 3
        + [pltpu.VMEM((ts, D_GDN), F32)] * 3
        + [pltpu.VMEM((ts, LANES), F32)] * 3
        + [pltpu.VMEM((LANES, ts), F32)]
        + [pltpu.VMEM((GDN_HEADS, GDN_HEAD_DIM, GDN_HEAD_DIM), F32)],
        compiler_params=_cparams("parallel", "arbitrary"),
        name="gdn",
    )(proj, proj, proj, proj, proj, cw, alog, dtb, og)


def _out_router_kernel(x_ref, yl_ref, yg_ref, wo_ref, g_ref, wr_ref, br_ref, h_ref, xn_ref, rt_ref):
    h = x_ref[...] + jnp.dot(yl_ref[...], wo_ref[:D_LRU, :], preferred_element_type=F32) \
        + jnp.dot(yg_ref[...], wo_ref[D_LRU:, :], preferred_element_type=F32)
    h_ref[...] = h
    xn = _rms(h, g_ref[...])
    _store_token_rows(xn_ref, xn)
    xh = xn.astype(BF16)
    xl = (xn - xh.astype(F32)).astype(BF16)
    logits = jnp.dot(jnp.concatenate([xh, xl, xh], axis=1), wr_ref[...],
                     preferred_element_type=F32) + br_ref[...]
    lane = lax.broadcasted_iota(jnp.int32, logits.shape, 1).astype(F32)
    big = jnp.float32(2 * LANES)
    ninf = jnp.float32(-jnp.inf)

    def top1(vals):
        m = jnp.max(vals, axis=-1, keepdims=True)
        return m, jnp.min(jnp.where(vals == m, lane, big), axis=-1, keepdims=True)

    gl = jnp.where(lane < N_GROUPS, logits, ninf)
    gmax, gsel = top1(gl)
    p_group = 1.0 / jnp.sum(jnp.exp(gl - gmax), axis=-1, keepdims=True)
    lo = N_GROUPS + EXPERTS_PER_GROUP * gsel
    el = jnp.where((lane >= lo) & (lane < lo + EXPERTS_PER_GROUP), logits, ninf)
    m1, i1 = top1(el)
    m2, i2 = top1(jnp.where(lane == i1, ninf, el))
    r = jnp.exp(m2 - m1)
    g1 = p_group / (1.0 + r)
    g2 = p_group * r / (1.0 + r)
    rt = jnp.where(lane == 0, i1 - N_GROUPS,
                   jnp.where(lane == 1, i2 - N_GROUPS,
                             jnp.where(lane == 2, g1, jnp.where(lane == 3, g2, 0.0))))
    rt_ref[...] = rt


def _out_router(x2, yl, yg, wo, g, wr, br, tm):
    t = x2.shape[0]
    tile = lambda n: pl.BlockSpec((tm, n), lambda i: (i, 0))
    full = lambda a, b: pl.BlockSpec((a, b), lambda i: (0, 0))
    return pl.pallas_call(
        _out_router_kernel,
        grid=(t // tm,),
        in_specs=[tile(D_MODEL), tile(D_LRU), tile(D_GDN), full(D_MODEL, D_MODEL), full(1, D_MODEL),
                  full(3 * D_MODEL, LANES), full(1, LANES)],
        out_specs=[tile(D_MODEL), pl.BlockSpec((tm * ROW_TILES, LANES), lambda i: (i, 0)), tile(LANES)],
        out_shape=[jax.ShapeDtypeStruct((t, D_MODEL), F32),
                   jax.ShapeDtypeStruct((t * ROW_TILES, LANES), jnp.uint32),
                   jax.ShapeDtypeStruct((t, LANES), F32)],
        compiler_params=_cparams("parallel"),
        name="out_router",
    )(x2, yl, yg, wo, g, wr, br)


def _slots_kernel(rt_ref, dest_ref, pend_ref, cnt_ref, before_ref):
    phase = pl.program_id(0)
    i = pl.program_id(1)
    tm = rt_ref.shape[0]
    rt = rt_ref[...]
    lane = lax.broadcasted_iota(jnp.int32, rt.shape, 1)
    e0 = rt[:, 0:1].astype(jnp.int32)
    e1 = rt[:, 1:2].astype(jnp.int32)
    member = ((lane == e0) | (lane == e1)).astype(BF16)
    tile_counts = jnp.dot(jnp.ones((SUBLANES, tm), BF16), member, preferred_element_type=F32)

    @pl.when((phase == 0) & (i == 0))
    def _():
        cnt_ref[...] = jnp.zeros_like(cnt_ref)
        ri = lax.broadcasted_iota(jnp.int32, (tm, tm), 0)
        ci = lax.broadcasted_iota(jnp.int32, (tm, tm), 1)
        before_ref[...] = (ri > ci).astype(BF16)

    @pl.when(phase == 0)
    def _():
        cnt_ref[...] += tile_counts

    @pl.when((phase == 1) & (i == 0))
    def _():
        cnt = cnt_ref[...]
        padded = jnp.ceil(cnt / MOE_BLOCK) * MOE_BLOCK
        l8 = lax.broadcasted_iota(jnp.int32, cnt.shape, 1)
        incl = padded
        d = 1
        while d < LANES:
            incl = incl + jnp.where(l8 >= d, pltpu.roll(incl, d, axis=1), 0.0)
            d *= 2
        sub = lax.broadcasted_iota(jnp.int32, cnt.shape, 0)
        pend_ref[...] = jnp.where(sub == 0, incl, cnt)
        cnt_ref[...] = incl - padded

    @pl.when(phase == 1)
    def _():
        start = cnt_ref[...]
        pos = start[0:1, :] + jnp.dot(before_ref[...], member, preferred_element_type=F32)
        d0 = jnp.sum(jnp.where(lane == e0, pos, 0.0), axis=-1, keepdims=True)
        d1 = jnp.sum(jnp.where(lane == e1, pos, 0.0), axis=-1, keepdims=True)
        dest = jnp.where(lane == 0, d0, jnp.where(lane == 1, d1, 0.0))
        dest_ref[...] = dest.T[:SUBLANES].astype(jnp.int32)
        cnt_ref[...] = start + tile_counts


def _slots(rt, tm):
    t = rt.shape[0]
    return pl.pallas_call(
        _slots_kernel,
        grid=(2, t // tm),
        in_specs=[pl.BlockSpec((tm, LANES), lambda p, i: (i, 0))],
        out_specs=[pl.BlockSpec((SUBLANES, tm), lambda p, i: (0, i * p)),
                   pl.BlockSpec((SUBLANES, LANES), lambda p, i: (0, 0))],
        out_shape=[jax.ShapeDtypeStruct((SUBLANES, t), jnp.int32),
                   jax.ShapeDtypeStruct((SUBLANES, LANES), F32)],
        scratch_shapes=[pltpu.VMEM((SUBLANES, LANES), F32), pltpu.VMEM((tm, tm), BF16)],
        compiler_params=_cparams("arbitrary", "arbitrary"),
        name="moe_slots",
    )(rt)


def _sc_mesh():
    return plsc.VectorSubcoreMesh(core_axis_name="c", subcore_axis_name="s")


def _sc_worker():
    return lax.axis_index("s") * SC_CORES + lax.axis_index("c")


def _sc_scatter_rows(src, idx, n_rows):
    t = src.shape[0]
    per_w = t // SC_WORKERS
    n_win = per_w // SC_WINDOW
    idx = idx.reshape(TOP_K, SC_WORKERS, n_win, SC_WINDOW).transpose(1, 0, 2, 3)

    @functools.partial(
        pl.kernel, mesh=_sc_mesh(),
        out_type=jax.ShapeDtypeStruct((n_rows,) + src.shape[1:], src.dtype),
        scratch_types=[pltpu.VMEM((TOP_K, n_win, SC_WINDOW), jnp.int32),
                       pltpu.VMEM((2, SC_WINDOW) + src.shape[1:], src.dtype),
                       pltpu.SemaphoreType.DMA((2,)), pltpu.SemaphoreType.DMA((2,))],
        compiler_params=pltpu.CompilerParams(use_tc_tiling_on_sc=True),
        name="sc_dispatch")
    def scatter(src_hbm, idx_hbm, out_hbm, idx_v, rows_v, lsem, ssem):
        wid = _sc_worker()
        base = wid * per_w
        pltpu.sync_copy(idx_hbm.at[wid], idx_v)

        def load(w, slot):
            return pltpu.make_async_copy(src_hbm.at[pl.ds(base + w * SC_WINDOW, SC_WINDOW)], rows_v.at[slot],
                                         lsem.at[slot])

        def put(w, slot, k):
            return pltpu.make_async_copy(rows_v.at[slot], out_hbm.at[idx_v.at[k, w]], ssem.at[slot])

        load(0, 0).start()

        @pl.loop(0, n_win, step=2)
        def _(w0):
            for s in range(2):
                w = w0 + s

                @pl.when(w + 1 < n_win)
                def _():
                    @pl.when(w >= 1)
                    def _():
                        for k in range(TOP_K):
                            put(w - 1, 1 - s, k).wait()

                    load(w + 1, 1 - s).start()

                load(w, s).wait()
                for k in range(TOP_K):
                    put(w, s, k).start()

        for s in range(2):
            for k in range(TOP_K):
                put(n_win - 2 + s, s, k).wait()

    return scatter(src, idx)


def _sc_gather_rows(table, idx):
    b = idx.shape[0]
    per_w = b // SC_WORKERS
    n_win = per_w // SC_WINDOW
    idx = idx.reshape(SC_WORKERS, n_win, SC_WINDOW)

    @functools.partial(
        pl.kernel, mesh=_sc_mesh(),
        out_type=jax.ShapeDtypeStruct((b,) + table.shape[1:], table.dtype),
        scratch_types=[pltpu.VMEM((n_win, SC_WINDOW), jnp.int32),
                       pltpu.VMEM((2, SC_WINDOW) + table.shape[1:], table.dtype),
                       pltpu.SemaphoreType.DMA((2,)), pltpu.SemaphoreType.DMA((2,))],
        compiler_params=pltpu.CompilerParams(use_tc_tiling_on_sc=True),
        name="sc_combine_gather")
    def gather(table_hbm, idx_hbm, out_hbm, idx_v, rows_v, gsem, psem):
        wid = _sc_worker()
        base = wid * per_w
        pltpu.sync_copy(idx_hbm.at[wid], idx_v)

        def get(w, slot):
            return pltpu.make_async_copy(table_hbm.at[idx_v.at[w]], rows_v.at[slot], gsem.at[slot])

        def put(w, slot):
            return pltpu.make_async_copy(rows_v.at[slot], out_hbm.at[pl.ds(base + w * SC_WINDOW, SC_WINDOW)],
                                         psem.at[slot])

        get(0, 0).start()

        @pl.loop(0, n_win, step=2)
        def _(w0):
            for s in range(2):
                w = w0 + s

                @pl.when(w + 1 < n_win)
                def _():
                    @pl.when(w >= 1)
                    def _():
                        put(w - 1, 1 - s).wait()

                    get(w + 1, 1 - s).start()

                get(w, s).wait()
                put(w, s).start()

        for s in range(2):
            put(n_win - 2 + s, s).wait()

    return gather(table, idx)


def _experts_kernel(be_ref, nv_ref, xs_ref, wg_ref, wu_ref, wd_ref, ys_ref, wgb_ref, wub_ref, wdb_ref):
    b = pl.program_id(0)
    prev = be_ref[jnp.maximum(b - 1, 0)]

    @pl.when((b == 0) | (be_ref[b] != prev))
    def _():
        wgb_ref[...] = wg_ref[0].astype(BF16)
        wub_ref[...] = wu_ref[0].astype(BF16)
        wdb_ref[...] = wd_ref[0].astype(BF16)

    @pl.when(nv_ref[b] > 0)
    def _():
        row = lax.broadcasted_iota(jnp.int32, (MOE_BLOCK, D_MODEL), 0)
        xb = jnp.where(row < nv_ref[b], _load_token_rows(xs_ref, MOE_BLOCK, BF16), 0.0)
        hg = jnp.dot(xb, wgb_ref[...], preferred_element_type=F32)
        hu = jnp.dot(xb, wub_ref[...], preferred_element_type=F32)
        hb = (jax.nn.silu(hg) * hu).astype(BF16)
        _store_token_rows(ys_ref, jnp.dot(hb, wdb_ref[...], preferred_element_type=F32))

    @pl.when(nv_ref[b] == 0)
    def _():
        ys_ref[...] = jnp.zeros_like(ys_ref)


def _experts(block_expert, n_valid, xs, wg, wu, wd):
    nb = xs.shape[0] // (MOE_BLOCK * ROW_TILES)
    blk = pl.BlockSpec((MOE_BLOCK * ROW_TILES, LANES), lambda b, be, nv: (b, 0))
    wspec = lambda a, c: pl.BlockSpec((1, a, c), lambda b, be, nv: (be[b], 0, 0))
    return pl.pallas_call(
        _experts_kernel,
        grid_spec=pltpu.PrefetchScalarGridSpec(
            num_scalar_prefetch=2,
            grid=(nb,),
            in_specs=[blk, wspec(D_MODEL, D_EXPERT), wspec(D_MODEL, D_EXPERT), wspec(D_EXPERT, D_MODEL)],
            out_specs=blk,
            scratch_shapes=[pltpu.VMEM((D_MODEL, D_EXPERT), BF16), pltpu.VMEM((D_MODEL, D_EXPERT), BF16),
                            pltpu.VMEM((D_EXPERT, D_MODEL), BF16)]),
        out_shape=jax.ShapeDtypeStruct(xs.shape, xs.dtype),
        compiler_params=_cparams("arbitrary"),
        name="moe_experts",
    )(block_expert, n_valid, xs, wg, wu, wd)


def _combine_kernel(h_ref, rt_ref, p_ref, y0_ref, y1_ref, gp_ref, wpg_ref, wp_ref, gf_ref, o_ref):
    tm = h_ref.shape[0]
    rt = rt_ref[...]
    h = h_ref[...] + (_load_token_rows(y0_ref, tm, F32) * rt[:, 2:3] + _load_token_rows(y1_ref, tm, F32) * rt[:, 3:4])
    gate = jax.nn.sigmoid(_dot(_rms(h, gp_ref[...]), wpg_ref[...]))
    h = h + _dot(p_ref[...], wp_ref[...]) * gate
    o_ref[...] = _rms(h, gf_ref[...])


def _combine(h1, rt, p2, y, gp, wpg, wp, gf, tm):
    t = h1.shape[0]
    nt = t // tm
    tile = lambda n: pl.BlockSpec((tm, n), lambda i: (i, 0))
    full = lambda a, b: pl.BlockSpec((a, b), lambda i: (0, 0))
    ytile = lambda k: pl.BlockSpec((tm * ROW_TILES, LANES), lambda i: (i + k * nt, 0))
    return pl.pallas_call(
        _combine_kernel,
        grid=(nt,),
        in_specs=[tile(D_MODEL), tile(LANES), tile(D_PLE), ytile(0), ytile(1),
                  full(1, D_MODEL), full(D_MODEL, D_MODEL), full(D_PLE, D_MODEL), full(1, D_MODEL)],
        out_specs=tile(D_MODEL),
        out_shape=jax.ShapeDtypeStruct((t, D_MODEL), F32),
        compiler_params=_cparams("parallel"),
        name="moe_combine_ple",
    )(h1, rt, p2, y, y, gp, wpg, wp, gf)


def _block_diag(w):
    n, c, d = w.shape
    eye = jnp.eye(n, dtype=w.dtype)
    return (w[:, :, None, :] * eye[:, None, :, None]).reshape(n * c, n * d)


def _lane_row(vals, offset):
    return jnp.zeros((1, LANES), F32).at[0, offset:offset + vals.shape[0]].set(vals)


def kernel(x, p, norm_mix, w_in, lru_conv_w, lru_conv_b, lru_wa, lru_ba, lru_wi, lru_bi, lru_lambda,
           lru_out_norm, gdn_conv_w, gdn_a_log, gdn_dt_bias, gdn_out_norm, w_out, norm_ffn,
           w_router_group, b_router_group, w_router_expert, b_router_expert, w_exp_gate, w_exp_up,
           w_exp_down, norm_ple, w_ple_gate, w_ple, norm_final):
    bsz, seq, d = x.shape
    t = bsz * seq
    depth = w_in.shape[0]
    assert depth == 1, "the final norm is fused into the last layer's combine kernel"
    n_blocks = -(-t * TOP_K // MOE_BLOCK) + N_EXPERTS
    row = lambda v: v.reshape(1, -1).astype(F32)
    h = x.reshape(t, d).astype(F32)
    for l in range(depth):
        w_in_p = jnp.pad(w_in[l], ((0, 0), (0, PROJ_COLS - w_in.shape[2]))).astype(BF16)
        proj = _in_proj(h, row(norm_mix[l]), w_in_p, 512)
        w_gates = jnp.concatenate([_block_diag(lru_wa[l]), _block_diag(lru_wi[l])], axis=1).astype(BF16)
        b_gates = jnp.concatenate([lru_ba[l], lru_bi[l]]).reshape(1, -1)
        y_lru = _lru(proj, lru_conv_w[l], row(lru_conv_b[l]), w_gates, b_gates, row(lru_lambda[l]),
                     row(lru_out_norm[l]), bsz, seq, 512)
        cw = gdn_conv_w[l].reshape(CONV_W, 3, D_GDN).transpose(1, 0, 2)
        y_gdn = _gdn(proj, cw, _lane_row(gdn_a_log[l], GDN_HEADS), _lane_row(gdn_dt_bias[l], GDN_HEADS),
                     row(gdn_out_norm[l]), bsz, seq, 512)
        w_r = jnp.pad(jnp.concatenate([w_router_group[l], w_router_expert[l]], axis=1),
                      ((0, 0), (0, LANES - N_GROUPS - N_EXPERTS)))
        w_r_hi = w_r.astype(BF16)
        w_r_lo = (w_r - w_r_hi.astype(F32)).astype(BF16)
        w_r = jnp.concatenate([w_r_hi, w_r_hi, w_r_lo], axis=0)
        b_r = _lane_row(jnp.concatenate([b_router_group[l], b_router_expert[l]]), 0)
        h1, xn2, rt = _out_router(h, y_lru, y_gdn, w_out[l].astype(BF16), row(norm_ffn[l]), w_r, b_r, 512)
        dest8, seg = _slots(rt, 1024)
        dest = dest8[:TOP_K]
        pad_end = seg[0, :N_EXPERTS].astype(jnp.int32)
        counts = seg[1, :N_EXPERTS].astype(jnp.int32)
        block_start = jnp.arange(n_blocks, dtype=jnp.int32) * MOE_BLOCK
        block_expert = jnp.minimum(jnp.sum(pad_end[None, :] <= block_start[:, None], axis=1),
                                   N_EXPERTS - 1).astype(jnp.int32)
        seg_end = (pad_end - (-counts % MOE_BLOCK))[block_expert]
        n_valid = jnp.where(block_start < pad_end[N_EXPERTS - 1],
                            jnp.clip(seg_end - block_start, 0, MOE_BLOCK), 0).astype(jnp.int32)
        tiles = lambda a: a.reshape(-1, ROW_TILES, LANES)
        xs = _sc_scatter_rows(tiles(xn2), dest, n_blocks * MOE_BLOCK)
        ys = _experts(block_expert, n_valid, xs.reshape(-1, LANES), w_exp_gate[l], w_exp_up[l], w_exp_down[l])
        y = _sc_gather_rows(tiles(ys), dest.reshape(-1))
        h = _combine(h1, rt, p[l].reshape(t, -1).astype(F32), y.reshape(-1, LANES), row(norm_ple[l]),
                     w_ple_gate[l].astype(BF16), w_ple[l].astype(BF16), row(norm_final), 512)
    return h.reshape(bsz, seq, d).astype(x.dtype)
```

```python
import functools

import jax
import jax.numpy as jnp
import numpy as np
from jax import lax
from jax.experimental import pallas as pl
from jax.experimental.pallas import tpu as pltpu
from jax.experimental.pallas import tpu_sc as plsc

D_MODEL = 1024
D_LRU = 512
LRU_BLOCKS = 8
LRU_BLOCK_W = D_LRU // LRU_BLOCKS
LRU_C = 8.0
D_GDN = 512
GDN_HEADS = 4
GDN_HEAD_DIM = D_GDN // GDN_HEADS
CONV_W = 4
CHUNK = 64
N_GROUPS = 4
EXPERTS_PER_GROUP = 8
N_EXPERTS = N_GROUPS * EXPERTS_PER_GROUP
TOP_K = 2
D_EXPERT = 512
MOE_BLOCK = 512
D_PLE = 256
EPS = 1e-6

LANES = 128
SUBLANES = 8
ROW_TILES = D_MODEL // (2 * LANES)
SC_CORES = 2
SC_SUBCORES = 16
SC_WORKERS = SC_CORES * SC_SUBCORES
SC_WINDOW = 64
PROJ_COLS = 2 * D_LRU + 4 * D_GDN + LANES
O_BA = 2 * D_LRU + 4 * D_GDN
VMEM_LIMIT = 56 * 1024 * 1024

BF16 = jnp.bfloat16
F32 = jnp.float32


def _cparams(*sem):
    return pltpu.CompilerParams(dimension_semantics=sem, vmem_limit_bytes=VMEM_LIMIT)


def _rms(x, g):
    return x * lax.rsqrt(jnp.mean(x * x, axis=-1, keepdims=True) + EPS) * g


def _dot(a, b):
    return jnp.dot(a.astype(BF16), b.astype(BF16), preferred_element_type=F32)


def _dot_nt(a, b):
    return lax.dot_general(a.astype(BF16), b.astype(BF16), (((1,), (1,)), ((), ())),
                           preferred_element_type=F32)


def _dot_tn(a, b):
    return lax.dot_general(a.astype(BF16), b.astype(BF16), (((0,), (0,)), ((), ())),
                           preferred_element_type=F32)


def _conv_scratch(rows, channels):
    return pltpu.VMEM((channels // LANES, 2 * (SUBLANES + rows), LANES), F32)


def _conv_reset(xp_ref):
    for k in range(xp_ref.shape[0]):
        xp_ref.at[k][pl.ds(0, SUBLANES, stride=2), :] = jnp.zeros((SUBLANES, LANES), F32)


def _causal_conv(xp_ref, x, w):
    rows = x.shape[0]
    out = []
    for k in range(xp_ref.shape[0]):
        lanes = slice(k * LANES, (k + 1) * LANES)
        slab = xp_ref.at[k]
        xk = x[:, lanes]
        slab[pl.ds(2 * SUBLANES, rows, stride=2), :] = xk
        acc = xk * w[CONV_W - 1:CONV_W, lanes]
        for j in range(1, CONV_W):
            acc = acc + slab[pl.ds(2 * (SUBLANES - j), rows, stride=2), :] * w[CONV_W - 1 - j:CONV_W - j, lanes]
        slab[pl.ds(0, SUBLANES, stride=2), :] = xk[rows - SUBLANES:]
        out.append(acc)
    return jnp.concatenate(out, axis=1)


def _store_token_rows(ref, x, row0=0):
    rows = x.shape[0]
    bits = lax.bitcast_convert_type(x.astype(BF16).astype(F32), jnp.uint32)
    words = (bits[:, :D_MODEL // 2] >> 16) | bits[:, D_MODEL // 2:]
    for j in range(ROW_TILES):
        ref[pl.ds(row0 * ROW_TILES + j, rows, stride=ROW_TILES), :] = words[:, j * LANES:(j + 1) * LANES]


def _load_token_rows(ref, rows, dtype, row0=0):
    words = [ref[pl.ds(row0 * ROW_TILES + j, rows, stride=ROW_TILES), :] for j in range(ROW_TILES)]
    lo = [lax.bitcast_convert_type(w << 16, F32).astype(dtype) for w in words]
    hi = [lax.bitcast_convert_type(w & jnp.uint32(0xFFFF0000), F32).astype(dtype) for w in words]
    return jnp.concatenate(lo + hi, axis=1)


def _in_proj_kernel(x_ref, g_ref, w_ref, o_ref, *, col_tile, sub):
    n_sub = x_ref.shape[0] // sub

    def norm(r):
        return _rms(x_ref[r * sub:(r + 1) * sub, :], g_ref[...]).astype(BF16)

    xn = norm(0)
    for r in range(n_sub):
        for j in range(PROJ_COLS // col_tile):
            cols = slice(j * col_tile, (j + 1) * col_tile)
            o_ref[r * sub:(r + 1) * sub, cols] = jnp.dot(xn, w_ref[:, cols], preferred_element_type=F32)
            if j == 0 and r + 1 < n_sub:
                xn_next = norm(r + 1)
        if r + 1 < n_sub:
            xn = xn_next


def _in_proj(x2, g, w, tm):
    t = x2.shape[0]
    return pl.pallas_call(
        functools.partial(_in_proj_kernel, col_tile=PROJ_COLS, sub=256),
        grid=(t // tm,),
        in_specs=[pl.BlockSpec((tm, D_MODEL), lambda i: (i, 0)),
                  pl.BlockSpec((1, D_MODEL), lambda i: (0, 0)),
                  pl.BlockSpec((D_MODEL, PROJ_COLS), lambda i: (0, 0))],
        out_specs=pl.BlockSpec((tm, PROJ_COLS), lambda i: (i, 0)),
        out_shape=jax.ShapeDtypeStruct((t, PROJ_COLS), F32),
        compiler_params=_cparams("parallel"),
        name="in_proj",
    )(x2, g, w)


def _lru_kernel(x_ref, gate_ref, cw_ref, cb_ref, wg_ref, bg_ref, lam_ref, og_ref, o_ref,
                tail_ref, h_ref):
    ts = x_ref.shape[0]

    @pl.when(pl.program_id(1) == 0)
    def _():
        _conv_reset(tail_ref)
        h_ref[...] = jnp.zeros_like(h_ref)

    xc = _causal_conv(tail_ref, x_ref[...], cw_ref[...]) + cb_ref[...]
    gates = jax.nn.sigmoid(_dot(xc, wg_ref[...]) + bg_ref[...])
    r = gates[:, :D_LRU]
    i = gates[:, D_LRU:]
    log_a = LRU_C * r * jax.nn.log_sigmoid(lam_ref[...])
    a = jnp.exp(log_a)
    th = jnp.tanh(log_a)
    u = jnp.sqrt(-2.0 * th / (1.0 - th)) * (i * xc)
    row = lax.broadcasted_iota(jnp.int32, a.shape, 0)
    d = 1
    while d < ts:
        keep = row >= d
        a_prev = jnp.where(keep, pltpu.roll(a, d, axis=0), 1.0)
        u_prev = jnp.where(keep, pltpu.roll(u, d, axis=0), 0.0)
        u = a * u_prev + u
        a = a * a_prev
        d *= 2
    h = a * h_ref[...] + u
    h_ref[...] = h[ts - 1:]
    y = h * jax.nn.gelu(gate_ref[...])
    o_ref[...] = _rms(y, og_ref[...]).astype(o_ref.dtype)


def _lru(proj, cw, cb, wg, bg, lam, og, bsz, seq, ts):
    nt = seq // ts
    row = lambda b, s: b * nt + s
    vec = lambda n: pl.BlockSpec((1, n), lambda b, s: (0, 0))
    return pl.pallas_call(
        _lru_kernel,
        grid=(bsz, nt),
        in_specs=[pl.BlockSpec((ts, D_LRU), lambda b, s: (row(b, s), 0)),
                  pl.BlockSpec((ts, D_LRU), lambda b, s: (row(b, s), 1)),
                  pl.BlockSpec((CONV_W, D_LRU), lambda b, s: (0, 0)),
                  vec(D_LRU),
                  pl.BlockSpec((D_LRU, 2 * D_LRU), lambda b, s: (0, 0)),
                  vec(2 * D_LRU), vec(D_LRU), vec(D_LRU)],
        out_specs=pl.BlockSpec((ts, D_LRU), lambda b, s: (row(b, s), 0)),
        out_shape=jax.ShapeDtypeStruct((bsz * seq, D_LRU), BF16),
        scratch_shapes=[_conv_scratch(ts, D_LRU), pltpu.VMEM((1, D_LRU), F32)],
        compiler_params=_cparams("parallel", "arbitrary"),
        name="rglru",
    )(proj, proj, cw, cb, wg, bg, lam, og)


def _gdn_kernel(q_ref, k_ref, v_ref, z_ref, ba_ref, cw_ref, alog_ref, dtb_ref, og_ref, o_ref,
                qt_ref, kt_ref, vt_ref, qs_ref, ks_ref, vs_ref, bs_ref, gs_ref, gc_ref, gct_ref, st_ref,
                *, group_chunks):
    ts = q_ref.shape[0]
    dk = GDN_HEAD_DIM

    @pl.when(pl.program_id(1) == 0)
    def _():
        for tail_ref in (qt_ref, kt_ref, vt_ref):
            _conv_reset(tail_ref)
        st_ref[...] = jnp.zeros_like(st_ref)

    def conv_silu(x_ref, tail_ref, part):
        return jax.nn.silu(_causal_conv(tail_ref, x_ref[...], cw_ref[part]))

    def l2n(x, scale):
        parts = []
        for h in range(GDN_HEADS):
            xh = x[:, h * dk:(h + 1) * dk]
            parts.append(xh * (lax.rsqrt(jnp.sum(xh * xh, axis=-1, keepdims=True) + EPS) * scale))
        return jnp.concatenate(parts, axis=1)

    qs_ref[...] = l2n(conv_silu(q_ref, qt_ref, 0), dk ** -0.5)
    ks_ref[...] = l2n(conv_silu(k_ref, kt_ref, 1), 1.0)
    vs_ref[...] = conv_silu(v_ref, vt_ref, 2)
    ba = ba_ref[...]
    bs_ref[...] = jax.nn.sigmoid(ba)
    gs_ref[...] = -jnp.exp(alog_ref[...]) * jax.nn.softplus(ba + dtb_ref[...])

    ri = lax.broadcasted_iota(jnp.int32, (CHUNK, CHUNK), 0)
    ci = lax.broadcasted_iota(jnp.int32, (CHUNK, CHUNK), 1)
    causal = ri >= ci
    strict = ri > ci
    tril = causal.astype(F32)
    eye = (ri == ci).astype(F32)
    og = og_ref[...]

    def chunk_terms(pairs):
        n = range(len(pairs))
        rows = [slice(c * CHUNK, (c + 1) * CHUNK) for c, _ in pairs]
        cols = [slice(h * dk, (h + 1) * dk) for _, h in pairs]
        gl = [GDN_HEADS + h for _, h in pairs]
        kh = [ks_ref[rows[i], cols[i]] for i in n]
        kb = [kh[i] * bs_ref[rows[i], pairs[i][1]:pairs[i][1] + 1] for i in n]
        r = [_dot_nt(jnp.concatenate([kb[i], qs_ref[rows[i], cols[i]]], axis=0), kh[i]) for i in n]
        gcol = [gc_ref[rows[i], gl[i]:gl[i] + 1] for i in n]
        decay = []
        for i in n:
            diff = gcol[i] - gct_ref[gl[i]:gl[i] + 1, rows[i]]
            decay.append(jnp.where(causal, jnp.exp(jnp.where(causal, diff, 0.0)), 0.0))
        a = [jnp.where(strict, r[i][:CHUNK] * decay[i], 0.0) for i in n]
        qk = [(r[i][CHUNK:] * decay[i]).astype(BF16) for i in n]
        tinv = [eye - a[i] for i in n]
        p = 2
        while p < CHUNK:
            a = [_dot(a[i], a[i]) for i in n]
            tinv = [tinv[i] + _dot(tinv[i], a[i]) for i in n]
            p *= 2
        eg = [jnp.exp(gcol[i]) for i in n]
        rhs = [jnp.concatenate([vs_ref[rows[i], cols[i]] * bs_ref[rows[i], pairs[i][1]:pairs[i][1] + 1],
                                kb[i] * eg[i]], axis=1) for i in n]
        uw = [_dot(tinv[i], rhs[i]).astype(BF16) for i in n]
        qk_uw = [jnp.dot(qk[i], uw[i], preferred_element_type=F32) for i in n]
        glast = [gc_ref[(c + 1) * CHUNK - 1:(c + 1) * CHUNK, gl[i]:gl[i] + 1] for i, (c, _) in enumerate(pairs)]
        kd_uw = [_dot_tn(kh[i] * jnp.exp(glast[i] - gcol[i]), uw[i]) for i in n]
        out = {}
        for i in n:
            lhs = jnp.concatenate([kd_uw[i][:, dk:], qs_ref[rows[i], cols[i]] * eg[i] - qk_uw[i][:, dk:]],
                                  axis=0).astype(BF16)
            out[pairs[i]] = (lhs, kd_uw[i][:, :dk], qk_uw[i][:, :dk], jnp.exp(glast[i]))
        return out

    nc = ts // CHUNK
    for c in range(nc):
        rows = slice(c * CHUNK, (c + 1) * CHUNK)
        gc_ref[rows, :] = jnp.dot(tril, gs_ref[rows, :], precision=lax.Precision.HIGHEST,
                                  preferred_element_type=F32)
    gct_ref[...] = gc_ref[...].T
    terms = {}
    for c0 in range(0, nc, group_chunks):
        terms.update(chunk_terms([(c, h) for c in range(c0, c0 + group_chunks) for h in range(GDN_HEADS)]))
    s = [st_ref[h] for h in range(GDN_HEADS)]
    for c in range(nc):
        rows = slice(c * CHUNK, (c + 1) * CHUNK)
        r = [jnp.dot(terms[c, h][0], s[h].astype(BF16), preferred_element_type=F32) for h in range(GDN_HEADS)]
        for h in range(GDN_HEADS):
            cols = slice(h * dk, (h + 1) * dk)
            _, c_add, o_add, egl = terms[c, h]
            o = r[h][dk:] + o_add
            s[h] = egl * s[h] - r[h][:dk] + c_add
            o_ref[rows, cols] = (_rms(o, og) * jax.nn.silu(z_ref[rows, cols])).astype(o_ref.dtype)
    for h in range(GDN_HEADS):
        st_ref[h] = s[h]


def _gdn(proj, cw, alog, dtb, og, bsz, seq, ts):
    nt = seq // ts
    row = lambda b, s: b * nt + s
    col = lambda c: pl.BlockSpec((ts, D_GDN), lambda b, s: (row(b, s), c))
    c0 = 2 * D_LRU // D_GDN
    return pl.pallas_call(
        functools.partial(_gdn_kernel, group_chunks=4),
        grid=(bsz, nt),
        in_specs=[col(c0), col(c0 + 1), col(c0 + 2), col(c0 + 3),
                  pl.BlockSpec((ts, LANES), lambda b, s: (row(b, s), O_BA // LANES)),
                  pl.BlockSpec((3, CONV_W, D_GDN), lambda b, s: (0, 0, 0)),
                  pl.BlockSpec((1, LANES), lambda b, s: (0, 0)),
                  pl.BlockSpec((1, LANES), lambda b, s: (0, 0)),
                  pl.BlockSpec((1, GDN_HEAD_DIM), lambda b, s: (0, 0))],
        out_specs=pl.BlockSpec((ts, D_GDN), lambda b, s: (row(b, s), 0)),
        out_shape=jax.ShapeDtypeStruct((bsz * seq, D_GDN), BF16),
        scratch_shapes=[_conv_scratch(ts, D_GDN)] * 3
        + [pltpu.VMEM((ts, D_GDN), F32)] * 3
        + [pltpu.VMEM((ts, LANES), F32)] * 3
        + [pltpu.VMEM((LANES, ts), F32)]
        + [pltpu.VMEM((GDN_HEADS, GDN_HEAD_DIM, GDN_HEAD_DIM), F32)],
        compiler_params=_cparams("parallel", "arbitrary"),
        name="gdn",
    )(proj, proj, proj, proj, proj, cw, alog, dtb, og)


def _out_router_kernel(x_ref, yl_ref, yg_ref, wo_ref, g_ref, wr_ref, br_ref, h_ref, xn_ref, rt_ref, *, sub):
    n_sub = x_ref.shape[0] // sub

    def project(r):
        rows = slice(r * sub, (r + 1) * sub)
        h = x_ref[rows, :] + jnp.dot(yl_ref[rows, :], wo_ref[:D_LRU, :], preferred_element_type=F32) \
            + jnp.dot(yg_ref[rows, :], wo_ref[D_LRU:, :], preferred_element_type=F32)
        h_ref[rows, :] = h
        return h

    def route(r, h):
        xn = _rms(h, g_ref[...])
        _store_token_rows(xn_ref, xn, r * sub)
        xh = xn.astype(BF16)
        xl = (xn - xh.astype(F32)).astype(BF16)
        logits = jnp.dot(jnp.concatenate([xh, xl, xh], axis=1), wr_ref[...],
                         preferred_element_type=F32) + br_ref[...]
        lane = lax.broadcasted_iota(jnp.int32, logits.shape, 1).astype(F32)
        big = jnp.float32(2 * LANES)
        ninf = jnp.float32(-jnp.inf)

        def top1(vals):
            m = jnp.max(vals, axis=-1, keepdims=True)
            return m, jnp.min(jnp.where(vals == m, lane, big), axis=-1, keepdims=True)

        gl = jnp.where(lane < N_GROUPS, logits, ninf)
        gmax, gsel = top1(gl)
        p_group = 1.0 / jnp.sum(jnp.exp(gl - gmax), axis=-1, keepdims=True)
        lo = N_GROUPS + EXPERTS_PER_GROUP * gsel
        el = jnp.where((lane >= lo) & (lane < lo + EXPERTS_PER_GROUP), logits, ninf)
        m1, i1 = top1(el)
        m2, i2 = top1(jnp.where(lane == i1, ninf, el))
        rr = jnp.exp(m2 - m1)
        g1 = p_group / (1.0 + rr)
        g2 = p_group * rr / (1.0 + rr)
        rt_ref[r * sub:(r + 1) * sub, :] = jnp.where(
            lane == 0, i1 - N_GROUPS,
            jnp.where(lane == 1, i2 - N_GROUPS, jnp.where(lane == 2, g1, jnp.where(lane == 3, g2, 0.0))))

    h = project(0)
    for r in range(1, n_sub):
        h_next = project(r)
        route(r - 1, h)
        h = h_next
    route(n_sub - 1, h)


def _out_router(x2, yl, yg, wo, g, wr, br, tm):
    t = x2.shape[0]
    tile = lambda n: pl.BlockSpec((tm, n), lambda i: (i, 0))
    full = lambda a, b: pl.BlockSpec((a, b), lambda i: (0, 0))
    return pl.pallas_call(
        functools.partial(_out_router_kernel, sub=128),
        grid=(t // tm,),
        in_specs=[tile(D_MODEL), tile(D_LRU), tile(D_GDN), full(D_MODEL, D_MODEL), full(1, D_MODEL),
                  full(3 * D_MODEL, LANES), full(1, LANES)],
        out_specs=[tile(D_MODEL), pl.BlockSpec((tm * ROW_TILES, LANES), lambda i: (i, 0)), tile(LANES)],
        out_shape=[jax.ShapeDtypeStruct((t, D_MODEL), F32),
                   jax.ShapeDtypeStruct((t * ROW_TILES, LANES), jnp.uint32),
                   jax.ShapeDtypeStruct((t, LANES), F32)],
        compiler_params=_cparams("parallel"),
        name="out_router",
    )(x2, yl, yg, wo, g, wr, br)


def _slots_kernel(rt_ref, dest_ref, pend_ref, cnt_ref, before_ref):
    phase = pl.program_id(0)
    i = pl.program_id(1)
    tm = rt_ref.shape[0]
    rt = rt_ref[...]
    lane = lax.broadcasted_iota(jnp.int32, rt.shape, 1)
    e0 = rt[:, 0:1].astype(jnp.int32)
    e1 = rt[:, 1:2].astype(jnp.int32)
    member = ((lane == e0) | (lane == e1)).astype(BF16)
    tile_counts = jnp.dot(jnp.ones((SUBLANES, tm), BF16), member, preferred_element_type=F32)

    @pl.when((phase == 0) & (i == 0))
    def _():
        cnt_ref[...] = jnp.zeros_like(cnt_ref)
        ri = lax.broadcasted_iota(jnp.int32, (tm, tm), 0)
        ci = lax.broadcasted_iota(jnp.int32, (tm, tm), 1)
        before_ref[...] = (ri > ci).astype(BF16)

    @pl.when(phase == 0)
    def _():
        cnt_ref[...] += tile_counts

    @pl.when((phase == 1) & (i == 0))
    def _():
        cnt = cnt_ref[...]
        padded = jnp.ceil(cnt / MOE_BLOCK) * MOE_BLOCK
        l8 = lax.broadcasted_iota(jnp.int32, cnt.shape, 1)
        incl = padded
        d = 1
        while d < LANES:
            incl = incl + jnp.where(l8 >= d, pltpu.roll(incl, d, axis=1), 0.0)
            d *= 2
        sub = lax.broadcasted_iota(jnp.int32, cnt.shape, 0)
        pend_ref[...] = jnp.where(sub == 0, incl, cnt)
        cnt_ref[...] = incl - padded

    @pl.when(phase == 1)
    def _():
        start = cnt_ref[...]
        pos = start[0:1, :] + jnp.dot(before_ref[...], member, preferred_element_type=F32)
        d0 = jnp.sum(jnp.where(lane == e0, pos, 0.0), axis=-1, keepdims=True)
        d1 = jnp.sum(jnp.where(lane == e1, pos, 0.0), axis=-1, keepdims=True)
        dest = jnp.where(lane == 0, d0, jnp.where(lane == 1, d1, 0.0))
        dest_ref[...] = dest.T[:SUBLANES].astype(jnp.int32)
        cnt_ref[...] = start + tile_counts


def _slots(rt, tm):
    t = rt.shape[0]
    return pl.pallas_call(
        _slots_kernel,
        grid=(2, t // tm),
        in_specs=[pl.BlockSpec((tm, LANES), lambda p, i: (i, 0))],
        out_specs=[pl.BlockSpec((SUBLANES, tm), lambda p, i: (0, i * p)),
                   pl.BlockSpec((SUBLANES, LANES), lambda p, i: (0, 0))],
        out_shape=[jax.ShapeDtypeStruct((SUBLANES, t), jnp.int32),
                   jax.ShapeDtypeStruct((SUBLANES, LANES), F32)],
        scratch_shapes=[pltpu.VMEM((SUBLANES, LANES), F32), pltpu.VMEM((tm, tm), BF16)],
        compiler_params=_cparams("arbitrary", "arbitrary"),
        name="moe_slots",
    )(rt)


def _sc_mesh():
    return plsc.VectorSubcoreMesh(core_axis_name="c", subcore_axis_name="s")


def _sc_worker():
    return lax.axis_index("s") * SC_CORES + lax.axis_index("c")


def _sc_scatter_rows(src, idx, n_rows):
    t = src.shape[0]
    per_w = t // SC_WORKERS
    n_win = per_w // SC_WINDOW
    idx = idx.reshape(TOP_K, SC_WORKERS, n_win, SC_WINDOW).transpose(1, 0, 2, 3)

    @functools.partial(
        pl.kernel, mesh=_sc_mesh(),
        out_type=jax.ShapeDtypeStruct((n_rows,) + src.shape[1:], src.dtype),
        scratch_types=[pltpu.VMEM((TOP_K, n_win, SC_WINDOW), jnp.int32),
                       pltpu.VMEM((2, SC_WINDOW) + src.shape[1:], src.dtype),
                       pltpu.SemaphoreType.DMA((2,)), pltpu.SemaphoreType.DMA((2,))],
        compiler_params=pltpu.CompilerParams(use_tc_tiling_on_sc=True),
        name="sc_dispatch")
    def scatter(src_hbm, idx_hbm, out_hbm, idx_v, rows_v, lsem, ssem):
        wid = _sc_worker()
        base = wid * per_w
        pltpu.sync_copy(idx_hbm.at[wid], idx_v)

        def load(w, slot):
            return pltpu.make_async_copy(src_hbm.at[pl.ds(base + w * SC_WINDOW, SC_WINDOW)], rows_v.at[slot],
                                         lsem.at[slot])

        def put(w, slot, k):
            return pltpu.make_async_copy(rows_v.at[slot], out_hbm.at[idx_v.at[k, w]], ssem.at[slot])

        load(0, 0).start()

        @pl.loop(0, n_win, step=2)
        def _(w0):
            for s in range(2):
                w = w0 + s

                @pl.when(w + 1 < n_win)
                def _():
                    @pl.when(w >= 1)
                    def _():
                        for k in range(TOP_K):
                            put(w - 1, 1 - s, k).wait()

                    load(w + 1, 1 - s).start()

                load(w, s).wait()
                for k in range(TOP_K):
                    put(w, s, k).start()

        for s in range(2):
            for k in range(TOP_K):
                put(n_win - 2 + s, s, k).wait()

    return scatter(src, idx)


def _sc_gather_rows(table, idx):
    b = idx.shape[0]
    per_w = b // SC_WORKERS
    n_win = per_w // SC_WINDOW
    idx = idx.reshape(SC_WORKERS, n_win, SC_WINDOW)

    @functools.partial(
        pl.kernel, mesh=_sc_mesh(),
        out_type=jax.ShapeDtypeStruct((b,) + table.shape[1:], table.dtype),
        scratch_types=[pltpu.VMEM((n_win, SC_WINDOW), jnp.int32),
                       pltpu.VMEM((2, SC_WINDOW) + table.shape[1:], table.dtype),
                       pltpu.SemaphoreType.DMA((2,)), pltpu.SemaphoreType.DMA((2,))],
        compiler_params=pltpu.CompilerParams(use_tc_tiling_on_sc=True),
        name="sc_combine_gather")
    def gather(table_hbm, idx_hbm, out_hbm, idx_v, rows_v, gsem, psem):
        wid = _sc_worker()
        base = wid * per_w
        pltpu.sync_copy(idx_hbm.at[wid], idx_v)

        def get(w, slot):
            return pltpu.make_async_copy(table_hbm.at[idx_v.at[w]], rows_v.at[slot], gsem.at[slot])

        def put(w, slot):
            return pltpu.make_async_copy(rows_v.at[slot], out_hbm.at[pl.ds(base + w * SC_WINDOW, SC_WINDOW)],
                                         psem.at[slot])

        get(0, 0).start()

        @pl.loop(0, n_win, step=2)
        def _(w0):
            for s in range(2):
                w = w0 + s

                @pl.when(w + 1 < n_win)
                def _():
                    @pl.when(w >= 1)
                    def _():
                        put(w - 1, 1 - s).wait()

                    get(w + 1, 1 - s).start()

                get(w, s).wait()
                put(w, s).start()

        for s in range(2):
            put(n_win - 2 + s, s).wait()

    return gather(table, idx)


def _experts_kernel(be_ref, nv_ref, xs_ref, wg_ref, wu_ref, wd_ref, ys_ref, wgb_ref, wub_ref, wdb_ref):
    b = pl.program_id(0)
    prev = be_ref[jnp.maximum(b - 1, 0)]

    @pl.when((b == 0) | (be_ref[b] != prev))
    def _():
        wgb_ref[...] = wg_ref[0].astype(BF16)
        wub_ref[...] = wu_ref[0].astype(BF16)
        wdb_ref[...] = wd_ref[0].astype(BF16)

    @pl.when(nv_ref[b] > 0)
    def _():
        row = lax.broadcasted_iota(jnp.int32, (MOE_BLOCK, D_MODEL), 0)
        xb = jnp.where(row < nv_ref[b], _load_token_rows(xs_ref, MOE_BLOCK, BF16), 0.0)
        hg = jnp.dot(xb, wgb_ref[...], preferred_element_type=F32)
        hu = jnp.dot(xb, wub_ref[...], preferred_element_type=F32)
        hb = (jax.nn.silu(hg) * hu).astype(BF16)
        _store_token_rows(ys_ref, jnp.dot(hb, wdb_ref[...], preferred_element_type=F32))

    @pl.when(nv_ref[b] == 0)
    def _():
        ys_ref[...] = jnp.zeros_like(ys_ref)


def _experts(block_expert, n_valid, xs, wg, wu, wd):
    nb = xs.shape[0] // (MOE_BLOCK * ROW_TILES)
    blk = pl.BlockSpec((MOE_BLOCK * ROW_TILES, LANES), lambda b, be, nv: (b, 0))
    wspec = lambda a, c: pl.BlockSpec((1, a, c), lambda b, be, nv: (be[b], 0, 0))
    return pl.pallas_call(
        _experts_kernel,
        grid_spec=pltpu.PrefetchScalarGridSpec(
            num_scalar_prefetch=2,
            grid=(nb,),
            in_specs=[blk, wspec(D_MODEL, D_EXPERT), wspec(D_MODEL, D_EXPERT), wspec(D_EXPERT, D_MODEL)],
            out_specs=blk,
            scratch_shapes=[pltpu.VMEM((D_MODEL, D_EXPERT), BF16), pltpu.VMEM((D_MODEL, D_EXPERT), BF16),
                            pltpu.VMEM((D_EXPERT, D_MODEL), BF16)]),
        out_shape=jax.ShapeDtypeStruct(xs.shape, xs.dtype),
        compiler_params=_cparams("arbitrary"),
        name="moe_experts",
    )(block_expert, n_valid, xs, wg, wu, wd)


def _combine_kernel(h_ref, rt_ref, p_ref, y0_ref, y1_ref, gp_ref, wpg_ref, wp_ref, gf_ref, o_ref, *, sub):
    n_sub = h_ref.shape[0] // sub

    def residual(r):
        rows = slice(r * sub, (r + 1) * sub)
        rt = rt_ref[rows, :]
        h = h_ref[rows, :] + (_load_token_rows(y0_ref, sub, F32, r * sub) * rt[:, 2:3]
                              + _load_token_rows(y1_ref, sub, F32, r * sub) * rt[:, 3:4])
        return h, _rms(h, gp_ref[...]).astype(BF16)

    def products(r, xn):
        return (jnp.dot(xn, wpg_ref[...], preferred_element_type=F32),
                _dot(p_ref[r * sub:(r + 1) * sub, :], wp_ref[...]))

    def finish(r, h, gate_lin, ple):
        h = h + ple * jax.nn.sigmoid(gate_lin)
        o_ref[r * sub:(r + 1) * sub, :] = _rms(h, gf_ref[...])

    h, xn = residual(0)
    for r in range(n_sub):
        gate_lin, ple = products(r, xn)
        if r + 1 < n_sub:
            h_next, xn = residual(r + 1)
        finish(r, h, gate_lin, ple)
        if r + 1 < n_sub:
            h = h_next


def _combine(h1, rt, p2, y, gp, wpg, wp, gf, tm):
    t = h1.shape[0]
    nt = t // tm
    tile = lambda n: pl.BlockSpec((tm, n), lambda i: (i, 0))
    full = lambda a, b: pl.BlockSpec((a, b), lambda i: (0, 0))
    ytile = lambda k: pl.BlockSpec((tm * ROW_TILES, LANES), lambda i: (i + k * nt, 0))
    return pl.pallas_call(
        functools.partial(_combine_kernel, sub=128),
        grid=(nt,),
        in_specs=[tile(D_MODEL), tile(LANES), tile(D_PLE), ytile(0), ytile(1),
                  full(1, D_MODEL), full(D_MODEL, D_MODEL), full(D_PLE, D_MODEL), full(1, D_MODEL)],
        out_specs=tile(D_MODEL),
        out_shape=jax.ShapeDtypeStruct((t, D_MODEL), F32),
        compiler_params=_cparams("parallel"),
        name="moe_combine_ple",
    )(h1, rt, p2, y, y, gp, wpg, wp, gf)


def _block_diag(w):
    n, c, d = w.shape
    eye = jnp.eye(n, dtype=w.dtype)
    return (w[:, :, None, :] * eye[:, None, :, None]).reshape(n * c, n * d)


def _lane_row(vals, offset):
    return jnp.zeros((1, LANES), F32).at[0, offset:offset + vals.shape[0]].set(vals)


def kernel(x, p, norm_mix, w_in, lru_conv_w, lru_conv_b, lru_wa, lru_ba, lru_wi, lru_bi, lru_lambda,
           lru_out_norm, gdn_conv_w, gdn_a_log, gdn_dt_bias, gdn_out_norm, w_out, norm_ffn,
           w_router_group, b_router_group, w_router_expert, b_router_expert, w_exp_gate, w_exp_up,
           w_exp_down, norm_ple, w_ple_gate, w_ple, norm_final):
    bsz, seq, d = x.shape
    t = bsz * seq
    depth = w_in.shape[0]
    assert depth == 1, "the final norm is fused into the last layer's combine kernel"
    n_blocks = -(-t * TOP_K // MOE_BLOCK) + N_EXPERTS
    row = lambda v: v.reshape(1, -1).astype(F32)
    h = x.reshape(t, d).astype(F32)
    for l in range(depth):
        w_in_p = jnp.pad(w_in[l], ((0, 0), (0, PROJ_COLS - w_in.shape[2]))).astype(BF16)
        proj = _in_proj(h, row(norm_mix[l]), w_in_p, 512)
        w_gates = jnp.concatenate([_block_diag(lru_wa[l]), _block_diag(lru_wi[l])], axis=1).astype(BF16)
        b_gates = jnp.concatenate([lru_ba[l], lru_bi[l]]).reshape(1, -1)
        y_lru = _lru(proj, lru_conv_w[l], row(lru_conv_b[l]), w_gates, b_gates, row(lru_lambda[l]),
                     row(lru_out_norm[l]), bsz, seq, 512)
        cw = gdn_conv_w[l].reshape(CONV_W, 3, D_GDN).transpose(1, 0, 2)
        y_gdn = _gdn(proj, cw, _lane_row(gdn_a_log[l], GDN_HEADS), _lane_row(gdn_dt_bias[l], GDN_HEADS),
                     row(gdn_out_norm[l]), bsz, seq, 512)
        w_r = jnp.pad(jnp.concatenate([w_router_group[l], w_router_expert[l]], axis=1),
                      ((0, 0), (0, LANES - N_GROUPS - N_EXPERTS)))
        w_r_hi = w_r.astype(BF16)
        w_r_lo = (w_r - w_r_hi.astype(F32)).astype(BF16)
        w_r = jnp.concatenate([w_r_hi, w_r_hi, w_r_lo], axis=0)
        b_r = _lane_row(jnp.concatenate([b_router_group[l], b_router_expert[l]]), 0)
        h1, xn2, rt = _out_router(h, y_lru, y_gdn, w_out[l].astype(BF16), row(norm_ffn[l]), w_r, b_r, 512)
        dest8, seg = _slots(rt, 1024)
        dest = dest8[:TOP_K]
        pad_end = seg[0, :N_EXPERTS].astype(jnp.int32)
        counts = seg[1, :N_EXPERTS].astype(jnp.int32)
        block_start = jnp.arange(n_blocks, dtype=jnp.int32) * MOE_BLOCK
        block_expert = jnp.minimum(jnp.sum(pad_end[None, :] <= block_start[:, None], axis=1),
                                   N_EXPERTS - 1).astype(jnp.int32)
        seg_end = (pad_end - (-counts % MOE_BLOCK))[block_expert]
        n_valid = jnp.where(block_start < pad_end[N_EXPERTS - 1],
                            jnp.clip(seg_end - block_start, 0, MOE_BLOCK), 0).astype(jnp.int32)
        tiles = lambda a: a.reshape(-1, ROW_TILES, LANES)
        xs = _sc_scatter_rows(tiles(xn2), dest, n_blocks * MOE_BLOCK)
        ys = _experts(block_expert, n_valid, xs.reshape(-1, LANES), w_exp_gate[l], w_exp_up[l], w_exp_down[l])
        y = _sc_gather_rows(tiles(ys), dest.reshape(-1))
        h = _combine(h1, rt, p[l].reshape(t, -1).astype(F32), y.reshape(-1, LANES), row(norm_ple[l]),
                     w_ple_gate[l].astype(BF16), w_ple[l].astype(BF16), row(norm_final), 512)
    return h.reshape(bsz, seq, d).astype(x.dtype)
```

```python
import functools

import jax
import jax.numpy as jnp
import numpy as np
from jax import lax
from jax.experimental import pallas as pl
from jax.experimental.pallas import tpu as pltpu
from jax.experimental.pallas import tpu_sc as plsc

D_MODEL = 1024
D_LRU = 512
LRU_BLOCKS = 8
LRU_BLOCK_W = D_LRU // LRU_BLOCKS
LRU_C = 8.0
D_GDN = 512
GDN_HEADS = 4
GDN_HEAD_DIM = D_GDN // GDN_HEADS
CONV_W = 4
CHUNK = 64
N_GROUPS = 4
EXPERTS_PER_GROUP = 8
N_EXPERTS = N_GROUPS * EXPERTS_PER_GROUP
TOP_K = 2
D_EXPERT = 512
MOE_BLOCK = 512
D_PLE = 256
EPS = 1e-6

LANES = 128
SUBLANES = 8
ROW_TILES = D_MODEL // (2 * LANES)
SC_CORES = 2
SC_SUBCORES = 16
SC_WORKERS = SC_CORES * SC_SUBCORES
SC_WINDOW = 64
PROJ_COLS = 2 * D_LRU + 4 * D_GDN + LANES
O_BA = 2 * D_LRU + 4 * D_GDN
VMEM_LIMIT = 56 * 1024 * 1024

BF16 = jnp.bfloat16
F32 = jnp.float32


def _cparams(*sem):
    return pltpu.CompilerParams(dimension_semantics=sem, vmem_limit_bytes=VMEM_LIMIT)


def _rms(x, g):
    return x * lax.rsqrt(jnp.mean(x * x, axis=-1, keepdims=True) + EPS) * g


def _sigmoid(x):
    return 0.5 * jnp.tanh(0.5 * x) + 0.5


def _silu(x):
    return x * _sigmoid(x)


def _dot(a, b):
    return jnp.dot(a.astype(BF16), b.astype(BF16), preferred_element_type=F32)


def _dot_nt(a, b):
    return lax.dot_general(a.astype(BF16), b.astype(BF16), (((1,), (1,)), ((), ())),
                           preferred_element_type=F32)


def _dot_tn(a, b):
    return lax.dot_general(a.astype(BF16), b.astype(BF16), (((0,), (0,)), ((), ())),
                           preferred_element_type=F32)


def _conv_scratch(rows, channels):
    return pltpu.VMEM((channels // LANES, 2 * (SUBLANES + rows), LANES), F32)


def _conv_reset(xp_ref):
    for k in range(xp_ref.shape[0]):
        xp_ref.at[k][pl.ds(0, SUBLANES, stride=2), :] = jnp.zeros((SUBLANES, LANES), F32)


def _causal_conv(xp_ref, x_ref, w, r0, rows):
    out = []
    for k in range(xp_ref.shape[0]):
        lanes = slice(k * LANES, (k + 1) * LANES)
        slab = xp_ref.at[k]
        xk = x_ref[r0:r0 + rows, lanes]
        slab[pl.ds(2 * (SUBLANES + r0), rows, stride=2), :] = xk
        acc = xk * w[CONV_W - 1:CONV_W, lanes]
        for j in range(1, CONV_W):
            acc = acc + slab[pl.ds(2 * (SUBLANES + r0 - j), rows, stride=2), :] * w[CONV_W - 1 - j:CONV_W - j, lanes]
        out.append(acc)
    return jnp.concatenate(out, axis=1)


def _conv_carry(xp_ref, x_ref):
    ts = x_ref.shape[0]
    for k in range(xp_ref.shape[0]):
        xp_ref.at[k][pl.ds(0, SUBLANES, stride=2), :] = x_ref[ts - SUBLANES:, k * LANES:(k + 1) * LANES]


def _interleave(*stages):
    live = list(stages)
    while live:
        for g in list(live):
            try:
                next(g)
            except StopIteration:
                live.remove(g)


def _store_token_rows(ref, x, row0=0):
    rows = x.shape[0]
    bits = lax.bitcast_convert_type(x.astype(BF16).astype(F32), jnp.uint32)
    words = (bits[:, :D_MODEL // 2] >> 16) | bits[:, D_MODEL // 2:]
    for j in range(ROW_TILES):
        ref[pl.ds(row0 * ROW_TILES + j, rows, stride=ROW_TILES), :] = words[:, j * LANES:(j + 1) * LANES]


def _load_token_rows(ref, rows, dtype, row0=0):
    words = [ref[pl.ds(row0 * ROW_TILES + j, rows, stride=ROW_TILES), :] for j in range(ROW_TILES)]
    lo = [lax.bitcast_convert_type(w << 16, F32).astype(dtype) for w in words]
    hi = [lax.bitcast_convert_type(w & jnp.uint32(0xFFFF0000), F32).astype(dtype) for w in words]
    return jnp.concatenate(lo + hi, axis=1)


def _in_proj_kernel(x_ref, g_ref, w_ref, o_ref, *, col_tile, sub):
    n_sub = x_ref.shape[0] // sub

    def norm(r):
        return _rms(x_ref[r * sub:(r + 1) * sub, :], g_ref[...]).astype(BF16)

    xn = norm(0)
    for r in range(n_sub):
        for j in range(PROJ_COLS // col_tile):
            cols = slice(j * col_tile, (j + 1) * col_tile)
            o_ref[r * sub:(r + 1) * sub, cols] = jnp.dot(xn, w_ref[:, cols], preferred_element_type=F32)
            if j == 0 and r + 1 < n_sub:
                xn_next = norm(r + 1)
        if r + 1 < n_sub:
            xn = xn_next


def _in_proj(x2, g, w, tm):
    t = x2.shape[0]
    return pl.pallas_call(
        functools.partial(_in_proj_kernel, col_tile=PROJ_COLS, sub=256),
        grid=(t // tm,),
        in_specs=[pl.BlockSpec((tm, D_MODEL), lambda i: (i, 0)),
                  pl.BlockSpec((1, D_MODEL), lambda i: (0, 0)),
                  pl.BlockSpec((D_MODEL, PROJ_COLS), lambda i: (0, 0))],
        out_specs=pl.BlockSpec((tm, PROJ_COLS), lambda i: (i, 0)),
        out_shape=jax.ShapeDtypeStruct((t, PROJ_COLS), F32),
        compiler_params=_cparams("parallel"),
        name="in_proj",
    )(x2, g, w)


def _lru_kernel(x_ref, gate_ref, cw_ref, cb_ref, wg_ref, bg_ref, lam_ref, og_ref, o_ref,
                tail_ref, h_ref):
    ts = x_ref.shape[0]

    @pl.when(pl.program_id(1) == 0)
    def _():
        _conv_reset(tail_ref)
        h_ref[...] = jnp.zeros_like(h_ref)

    xc = _causal_conv(tail_ref, x_ref, cw_ref[...], 0, ts) + cb_ref[...]
    _conv_carry(tail_ref, x_ref)
    gates = _sigmoid(_dot(xc, wg_ref[...]) + bg_ref[...])
    r = gates[:, :D_LRU]
    i = gates[:, D_LRU:]
    log_a = LRU_C * r * jax.nn.log_sigmoid(lam_ref[...])
    a = jnp.exp(log_a)
    th = jnp.tanh(log_a)
    u = jnp.sqrt(-2.0 * th) * lax.rsqrt(1.0 - th) * (i * xc)
    a = a.reshape(ts // SUBLANES, SUBLANES, D_LRU)
    u = u.reshape(ts // SUBLANES, SUBLANES, D_LRU)
    row = lax.broadcasted_iota(jnp.int32, a.shape, 1)
    d = 1
    while d < SUBLANES:
        keep = row >= d
        a_prev = jnp.where(keep, pltpu.roll(a, d, axis=1), 1.0)
        u_prev = jnp.where(keep, pltpu.roll(u, d, axis=1), 0.0)
        u = a * u_prev + u
        a = a * a_prev
        d *= 2
    carry = h_ref[...]
    groups = []
    for g in range(ts // SUBLANES):
        groups.append(a[g] * carry + u[g])
        carry = groups[-1][SUBLANES - 1:]
    h = jnp.concatenate(groups, axis=0)
    h_ref[...] = carry
    y = h * jax.nn.gelu(gate_ref[...])
    o_ref[...] = _rms(y, og_ref[...]).astype(o_ref.dtype)


def _lru(proj, cw, cb, wg, bg, lam, og, bsz, seq, ts):
    nt = seq // ts
    row = lambda b, s: b * nt + s
    vec = lambda n: pl.BlockSpec((1, n), lambda b, s: (0, 0))
    return pl.pallas_call(
        _lru_kernel,
        grid=(bsz, nt),
        in_specs=[pl.BlockSpec((ts, D_LRU), lambda b, s: (row(b, s), 0)),
                  pl.BlockSpec((ts, D_LRU), lambda b, s: (row(b, s), 1)),
                  pl.BlockSpec((CONV_W, D_LRU), lambda b, s: (0, 0)),
                  vec(D_LRU),
                  pl.BlockSpec((D_LRU, 2 * D_LRU), lambda b, s: (0, 0)),
                  vec(2 * D_LRU), vec(D_LRU), vec(D_LRU)],
        out_specs=pl.BlockSpec((ts, D_LRU), lambda b, s: (row(b, s), 0)),
        out_shape=jax.ShapeDtypeStruct((bsz * seq, D_LRU), BF16),
        scratch_shapes=[_conv_scratch(ts, D_LRU), pltpu.VMEM((1, D_LRU), F32)],
        compiler_params=_cparams("parallel", "arbitrary"),
        name="rglru",
    )(proj, proj, cw, cb, wg, bg, lam, og)


def _gdn_kernel(q_ref, k_ref, v_ref, z_ref, ba_ref, cw_ref, alog_ref, dtb_ref, og_ref, o_ref,
                qt_ref, kt_ref, vt_ref, qs_ref, ks_ref, vs_ref, bs_ref, gs_ref, gc_ref, gct_ref, st_ref,
                *, group_chunks, prep_rows):
    ts = q_ref.shape[0]
    dk = GDN_HEAD_DIM
    nc = ts // CHUNK

    @pl.when(pl.program_id(1) == 0)
    def _():
        for tail_ref in (qt_ref, kt_ref, vt_ref):
            _conv_reset(tail_ref)
        st_ref[...] = jnp.zeros_like(st_ref)

    ri = lax.broadcasted_iota(jnp.int32, (CHUNK, CHUNK), 0)
    ci = lax.broadcasted_iota(jnp.int32, (CHUNK, CHUNK), 1)
    causal = ri >= ci
    strict = ri > ci
    tril = causal.astype(F32)
    eye = (ri == ci).astype(F32)
    og = og_ref[...]

    def l2n(x, scale):
        parts = []
        for h in range(GDN_HEADS):
            xh = x[:, h * dk:(h + 1) * dk]
            parts.append(xh * (lax.rsqrt(jnp.sum(xh * xh, axis=-1, keepdims=True) + EPS) * scale))
        return jnp.concatenate(parts, axis=1)

    def prepare(r0, r1):
        for p0 in range(r0, r1, prep_rows):
            rows = slice(p0, p0 + prep_rows)
            conv = lambda x_ref, tail_ref, part: _silu(
                _causal_conv(tail_ref, x_ref, cw_ref[part], p0, prep_rows))
            qs_ref[rows, :] = l2n(conv(q_ref, qt_ref, 0), dk ** -0.5)
            yield
            ks_ref[rows, :] = l2n(conv(k_ref, kt_ref, 1), 1.0)
            yield
            vs_ref[rows, :] = conv(v_ref, vt_ref, 2)
            ba = ba_ref[rows, :]
            bs_ref[rows, :] = _sigmoid(ba)
            g = -jnp.exp(alog_ref[...]) * jax.nn.softplus(ba + dtb_ref[...])
            for c0 in range(0, prep_rows, CHUNK):
                gc_ref[p0 + c0:p0 + c0 + CHUNK, :] = jnp.dot(tril, g[c0:c0 + CHUNK], precision=lax.Precision.HIGHEST,
                                                             preferred_element_type=F32)
            gct_ref[:, rows] = gc_ref[rows, :].T
            yield

    terms = {}

    def chunk_terms(pairs):
        n = range(len(pairs))
        rows = [slice(c * CHUNK, (c + 1) * CHUNK) for c, _ in pairs]
        cols = [slice(h * dk, (h + 1) * dk) for _, h in pairs]
        gl = [GDN_HEADS + h for _, h in pairs]
        kh = [ks_ref[rows[i], cols[i]] for i in n]
        kb = [kh[i] * bs_ref[rows[i], pairs[i][1]:pairs[i][1] + 1] for i in n]
        r = [_dot_nt(jnp.concatenate([kb[i], qs_ref[rows[i], cols[i]]], axis=0), kh[i]) for i in n]
        yield
        gcol = [gc_ref[rows[i], gl[i]:gl[i] + 1] for i in n]
        decay = []
        for i in n:
            diff = gcol[i] - gct_ref[gl[i]:gl[i] + 1, rows[i]]
            decay.append(jnp.where(causal, jnp.exp(jnp.where(causal, diff, 0.0)), 0.0))
        a = [jnp.where(strict, r[i][:CHUNK] * decay[i], 0.0) for i in n]
        qk = [(r[i][CHUNK:] * decay[i]).astype(BF16) for i in n]
        tinv = [eye - a[i] for i in n]
        p = 2
        while p < CHUNK:
            a = [_dot(a[i], a[i]) for i in n]
            yield
            tinv = [tinv[i] + _dot(tinv[i], a[i]) for i in n]
            yield
            p *= 2
        eg = [jnp.exp(gcol[i]) for i in n]
        rhs = [jnp.concatenate([vs_ref[rows[i], cols[i]] * bs_ref[rows[i], pairs[i][1]:pairs[i][1] + 1],
                                kb[i] * eg[i]], axis=1) for i in n]
        uw = [_dot(tinv[i], rhs[i]).astype(BF16) for i in n]
        yield
        qk_uw = [jnp.dot(qk[i], uw[i], preferred_element_type=F32) for i in n]
        glast = [gc_ref[(c + 1) * CHUNK - 1:(c + 1) * CHUNK, gl[i]:gl[i] + 1] for i, (c, _) in enumerate(pairs)]
        kd_uw = [_dot_tn(kh[i] * jnp.exp(glast[i] - gcol[i]), uw[i]) for i in n]
        yield
        for i in n:
            lhs = jnp.concatenate([kd_uw[i][:, dk:], qs_ref[rows[i], cols[i]] * eg[i] - qk_uw[i][:, dk:]],
                                  axis=0).astype(BF16)
            terms[pairs[i]] = (lhs, kd_uw[i][:, :dk], qk_uw[i][:, :dk], jnp.exp(glast[i]))

    state = [st_ref[h] for h in range(GDN_HEADS)]

    def advance(chunks):
        for c in chunks:
            rows = slice(c * CHUNK, (c + 1) * CHUNK)
            r = [jnp.dot(terms[c, h][0], state[h].astype(BF16), preferred_element_type=F32)
                 for h in range(GDN_HEADS)]
            for h in range(GDN_HEADS):
                cols = slice(h * dk, (h + 1) * dk)
                _, c_add, o_add, egl = terms[c, h]
                o = r[h][dk:] + o_add
                state[h] = egl * state[h] - r[h][:dk] + c_add
                o_ref[rows, cols] = (_rms(o, og) * _silu(z_ref[rows, cols])).astype(o_ref.dtype)
            yield

    groups = [range(c0, c0 + group_chunks) for c0 in range(0, nc, group_chunks)]
    pairs_of = lambda chunks: [(c, h) for c in chunks for h in range(GDN_HEADS)]
    span = lambda chunks: (chunks[0] * CHUNK, (chunks[-1] + 1) * CHUNK)
    _interleave(prepare(*span(groups[0])))
    for i, chunks in enumerate(groups):
        side = []
        if i + 1 < len(groups):
            side.append(prepare(*span(groups[i + 1])))
        if i > 0:
            side.append(advance(groups[i - 1]))
        _interleave(chunk_terms(pairs_of(chunks)), *side)
    _interleave(advance(groups[-1]))
    for h in range(GDN_HEADS):
        st_ref[h] = state[h]
    for x_ref, tail_ref in ((q_ref, qt_ref), (k_ref, kt_ref), (v_ref, vt_ref)):
        _conv_carry(tail_ref, x_ref)


def _gdn(proj, cw, alog, dtb, og, bsz, seq, ts):
    nt = seq // ts
    row = lambda b, s: b * nt + s
    col = lambda c: pl.BlockSpec((ts, D_GDN), lambda b, s: (row(b, s), c))
    c0 = 2 * D_LRU // D_GDN
    return pl.pallas_call(
        functools.partial(_gdn_kernel, group_chunks=4, prep_rows=128),
        grid=(bsz, nt),
        in_specs=[col(c0), col(c0 + 1), col(c0 + 2), col(c0 + 3),
                  pl.BlockSpec((ts, LANES), lambda b, s: (row(b, s), O_BA // LANES)),
                  pl.BlockSpec((3, CONV_W, D_GDN), lambda b, s: (0, 0, 0)),
                  pl.BlockSpec((1, LANES), lambda b, s: (0, 0)),
                  pl.BlockSpec((1, LANES), lambda b, s: (0, 0)),
                  pl.BlockSpec((1, GDN_HEAD_DIM), lambda b, s: (0, 0))],
        out_specs=pl.BlockSpec((ts, D_GDN), lambda b, s: (row(b, s), 0)),
        out_shape=jax.ShapeDtypeStruct((bsz * seq, D_GDN), BF16),
        scratch_shapes=[_conv_scratch(ts, D_GDN)] * 3
        + [pltpu.VMEM((ts, D_GDN), F32)] * 3
        + [pltpu.VMEM((ts, LANES), F32)] * 3
        + [pltpu.VMEM((LANES, ts), F32)]
        + [pltpu.VMEM((GDN_HEADS, GDN_HEAD_DIM, GDN_HEAD_DIM), F32)],
        compiler_params=_cparams("parallel", "arbitrary"),
        name="gdn",
    )(proj, proj, proj, proj, proj, cw, alog, dtb, og)


def _out_router_kernel(x_ref, yl_ref, yg_ref, wo_ref, g_ref, wr_ref, br_ref, h_ref, xn_ref, rt_ref, *, sub):
    n_sub = x_ref.shape[0] // sub

    def project(r):
        rows = slice(r * sub, (r + 1) * sub)
        h = x_ref[rows, :] + jnp.dot(yl_ref[rows, :], wo_ref[:D_LRU, :], preferred_element_type=F32) \
            + jnp.dot(yg_ref[rows, :], wo_ref[D_LRU:, :], preferred_element_type=F32)
        h_ref[rows, :] = h
        return h

    def route(r, h):
        xn = _rms(h, g_ref[...])
        _store_token_rows(xn_ref, xn, r * sub)
        xh = xn.astype(BF16)
        xl = (xn - xh.astype(F32)).astype(BF16)
        logits = jnp.dot(jnp.concatenate([xh, xl, xh], axis=1), wr_ref[...],
                         preferred_element_type=F32) + br_ref[...]
        lane = lax.broadcasted_iota(jnp.int32, logits.shape, 1).astype(F32)
        big = jnp.float32(2 * LANES)
        ninf = jnp.float32(-jnp.inf)

        def top1(vals):
            m = jnp.max(vals, axis=-1, keepdims=True)
            return m, jnp.min(jnp.where(vals == m, lane, big), axis=-1, keepdims=True)

        gl = jnp.where(lane < N_GROUPS, logits, ninf)
        gmax, gsel = top1(gl)
        p_group = 1.0 / jnp.sum(jnp.exp(gl - gmax), axis=-1, keepdims=True)
        lo = N_GROUPS + EXPERTS_PER_GROUP * gsel
        el = jnp.where((lane >= lo) & (lane < lo + EXPERTS_PER_GROUP), logits, ninf)
        m1, i1 = top1(el)
        m2, i2 = top1(jnp.where(lane == i1, ninf, el))
        rr = jnp.exp(m2 - m1)
        g1 = p_group / (1.0 + rr)
        g2 = p_group * rr / (1.0 + rr)
        rt_ref[r * sub:(r + 1) * sub, :] = jnp.where(
            lane == 0, i1 - N_GROUPS,
            jnp.where(lane == 1, i2 - N_GROUPS, jnp.where(lane == 2, g1, jnp.where(lane == 3, g2, 0.0))))

    h = project(0)
    for r in range(1, n_sub):
        h_next = project(r)
        route(r - 1, h)
        h = h_next
    route(n_sub - 1, h)


def _out_router(x2, yl, yg, wo, g, wr, br, tm):
    t = x2.shape[0]
    tile = lambda n: pl.BlockSpec((tm, n), lambda i: (i, 0))
    full = lambda a, b: pl.BlockSpec((a, b), lambda i: (0, 0))
    return pl.pallas_call(
        functools.partial(_out_router_kernel, sub=128),
        grid=(t // tm,),
        in_specs=[tile(D_MODEL), tile(D_LRU), tile(D_GDN), full(D_MODEL, D_MODEL), full(1, D_MODEL),
                  full(3 * D_MODEL, LANES), full(1, LANES)],
        out_specs=[tile(D_MODEL), pl.BlockSpec((tm * ROW_TILES, LANES), lambda i: (i, 0)), tile(LANES)],
        out_shape=[jax.ShapeDtypeStruct((t, D_MODEL), F32),
                   jax.ShapeDtypeStruct((t * ROW_TILES, LANES), jnp.uint32),
                   jax.ShapeDtypeStruct((t, LANES), F32)],
        compiler_params=_cparams("parallel"),
        name="out_router",
    )(x2, yl, yg, wo, g, wr, br)


def _slots_kernel(rt_ref, dest_ref, pend_ref, cnt_ref, before_ref):
    phase = pl.program_id(0)
    i = pl.program_id(1)
    tm = rt_ref.shape[0]
    rt = rt_ref[...]
    lane = lax.broadcasted_iota(jnp.int32, rt.shape, 1)
    e0 = rt[:, 0:1].astype(jnp.int32)
    e1 = rt[:, 1:2].astype(jnp.int32)
    member = ((lane == e0) | (lane == e1)).astype(BF16)
    tile_counts = jnp.dot(jnp.ones((SUBLANES, tm), BF16), member, preferred_element_type=F32)

    @pl.when((phase == 0) & (i == 0))
    def _():
        cnt_ref[...] = jnp.zeros_like(cnt_ref)
        ri = lax.broadcasted_iota(jnp.int32, (tm, tm), 0)
        ci = lax.broadcasted_iota(jnp.int32, (tm, tm), 1)
        before_ref[...] = (ri > ci).astype(BF16)

    @pl.when(phase == 0)
    def _():
        cnt_ref[...] += tile_counts

    @pl.when((phase == 1) & (i == 0))
    def _():
        cnt = cnt_ref[...]
        padded = jnp.ceil(cnt / MOE_BLOCK) * MOE_BLOCK
        l8 = lax.broadcasted_iota(jnp.int32, cnt.shape, 1)
        incl = padded
        d = 1
        while d < LANES:
            incl = incl + jnp.where(l8 >= d, pltpu.roll(incl, d, axis=1), 0.0)
            d *= 2
        sub = lax.broadcasted_iota(jnp.int32, cnt.shape, 0)
        pend_ref[...] = jnp.where(sub == 0, incl, cnt)
        cnt_ref[...] = incl - padded

    @pl.when(phase == 1)
    def _():
        start = cnt_ref[...]
        pos = start[0:1, :] + jnp.dot(before_ref[...], member, preferred_element_type=F32)
        d0 = jnp.sum(jnp.where(lane == e0, pos, 0.0), axis=-1, keepdims=True)
        d1 = jnp.sum(jnp.where(lane == e1, pos, 0.0), axis=-1, keepdims=True)
        dest = jnp.where(lane == 0, d0, jnp.where(lane == 1, d1, 0.0))
        dest_ref[...] = dest.T[:SUBLANES].astype(jnp.int32)
        cnt_ref[...] = start + tile_counts


def _slots(rt, tm):
    t = rt.shape[0]
    return pl.pallas_call(
        _slots_kernel,
        grid=(2, t // tm),
        in_specs=[pl.BlockSpec((tm, LANES), lambda p, i: (i, 0))],
        out_specs=[pl.BlockSpec((SUBLANES, tm), lambda p, i: (0, i * p)),
                   pl.BlockSpec((SUBLANES, LANES), lambda p, i: (0, 0))],
        out_shape=[jax.ShapeDtypeStruct((SUBLANES, t), jnp.int32),
                   jax.ShapeDtypeStruct((SUBLANES, LANES), F32)],
        scratch_shapes=[pltpu.VMEM((SUBLANES, LANES), F32), pltpu.VMEM((tm, tm), BF16)],
        compiler_params=_cparams("arbitrary", "arbitrary"),
        name="moe_slots",
    )(rt)


def _sc_mesh():
    return plsc.VectorSubcoreMesh(core_axis_name="c", subcore_axis_name="s")


def _sc_worker():
    return lax.axis_index("s") * SC_CORES + lax.axis_index("c")


def _sc_scatter_rows(src, idx, n_rows):
    t = src.shape[0]
    per_w = t // SC_WORKERS
    n_win = per_w // SC_WINDOW
    idx = idx.reshape(TOP_K, SC_WORKERS, n_win, SC_WINDOW).transpose(1, 0, 2, 3)

    @functools.partial(
        pl.kernel, mesh=_sc_mesh(),
        out_type=jax.ShapeDtypeStruct((n_rows,) + src.shape[1:], src.dtype),
        scratch_types=[pltpu.VMEM((TOP_K, n_win, SC_WINDOW), jnp.int32),
                       pltpu.VMEM((2, SC_WINDOW) + src.shape[1:], src.dtype),
                       pltpu.SemaphoreType.DMA((2,)), pltpu.SemaphoreType.DMA((2,))],
        compiler_params=pltpu.CompilerParams(use_tc_tiling_on_sc=True),
        name="sc_dispatch")
    def scatter(src_hbm, idx_hbm, out_hbm, idx_v, rows_v, lsem, ssem):
        wid = _sc_worker()
        base = wid * per_w
        pltpu.sync_copy(idx_hbm.at[wid], idx_v)

        def load(w, slot):
            return pltpu.make_async_copy(src_hbm.at[pl.ds(base + w * SC_WINDOW, SC_WINDOW)], rows_v.at[slot],
                                         lsem.at[slot])

        def put(w, slot, k):
            return pltpu.make_async_copy(rows_v.at[slot], out_hbm.at[idx_v.at[k, w]], ssem.at[slot])

        load(0, 0).start()

        @pl.loop(0, n_win, step=2)
        def _(w0):
            for s in range(2):
                w = w0 + s

                @pl.when(w + 1 < n_win)
                def _():
                    @pl.when(w >= 1)
                    def _():
                        for k in range(TOP_K):
                            put(w - 1, 1 - s, k).wait()

                    load(w + 1, 1 - s).start()

                load(w, s).wait()
                for k in range(TOP_K):
                    put(w, s, k).start()

        for s in range(2):
            for k in range(TOP_K):
                put(n_win - 2 + s, s, k).wait()

    return scatter(src, idx)


def _sc_gather_rows(table, idx):
    b = idx.shape[0]
    per_w = b // SC_WORKERS
    n_win = per_w // SC_WINDOW
    idx = idx.reshape(SC_WORKERS, n_win, SC_WINDOW)

    @functools.partial(
        pl.kernel, mesh=_sc_mesh(),
        out_type=jax.ShapeDtypeStruct((b,) + table.shape[1:], table.dtype),
        scratch_types=[pltpu.VMEM((n_win, SC_WINDOW), jnp.int32),
                       pltpu.VMEM((2, SC_WINDOW) + table.shape[1:], table.dtype),
                       pltpu.SemaphoreType.DMA((2,)), pltpu.SemaphoreType.DMA((2,))],
        compiler_params=pltpu.CompilerParams(use_tc_tiling_on_sc=True),
        name="sc_combine_gather")
    def gather(table_hbm, idx_hbm, out_hbm, idx_v, rows_v, gsem, psem):
        wid = _sc_worker()
        base = wid * per_w
        pltpu.sync_copy(idx_hbm.at[wid], idx_v)

        def get(w, slot):
            return pltpu.make_async_copy(table_hbm.at[idx_v.at[w]], rows_v.at[slot], gsem.at[slot])

        def put(w, slot):
            return pltpu.make_async_copy(rows_v.at[slot], out_hbm.at[pl.ds(base + w * SC_WINDOW, SC_WINDOW)],
                                         psem.at[slot])

        get(0, 0).start()

        @pl.loop(0, n_win, step=2)
        def _(w0):
            for s in range(2):
                w = w0 + s

                @pl.when(w + 1 < n_win)
                def _():
                    @pl.when(w >= 1)
                    def _():
                        put(w - 1, 1 - s).wait()

                    get(w + 1, 1 - s).start()

                get(w, s).wait()
                put(w, s).start()

        for s in range(2):
            put(n_win - 2 + s, s).wait()

    return gather(table, idx)


def _experts_kernel(be_ref, nv_ref, xs_ref, wg_ref, wu_ref, wd_ref, ys_ref, wgb_ref, wub_ref, wdb_ref):
    b = pl.program_id(0)
    prev = be_ref[jnp.maximum(b - 1, 0)]

    @pl.when((b == 0) | (be_ref[b] != prev))
    def _():
        wgb_ref[...] = wg_ref[0].astype(BF16)
        wub_ref[...] = wu_ref[0].astype(BF16)
        wdb_ref[...] = wd_ref[0].astype(BF16)

    @pl.when(nv_ref[b] > 0)
    def _():
        row = lax.broadcasted_iota(jnp.int32, (MOE_BLOCK, D_MODEL), 0)
        xb = jnp.where(row < nv_ref[b], _load_token_rows(xs_ref, MOE_BLOCK, BF16), 0.0)
        hg = jnp.dot(xb, wgb_ref[...], preferred_element_type=F32)
        hu = jnp.dot(xb, wub_ref[...], preferred_element_type=F32)
        hb = (_silu(hg) * hu).astype(BF16)
        _store_token_rows(ys_ref, jnp.dot(hb, wdb_ref[...], preferred_element_type=F32))

    @pl.when(nv_ref[b] == 0)
    def _():
        ys_ref[...] = jnp.zeros_like(ys_ref)


def _experts(block_expert, n_valid, xs, wg, wu, wd):
    nb = xs.shape[0] // (MOE_BLOCK * ROW_TILES)
    blk = pl.BlockSpec((MOE_BLOCK * ROW_TILES, LANES), lambda b, be, nv: (b, 0))
    wspec = lambda a, c: pl.BlockSpec((1, a, c), lambda b, be, nv: (be[b], 0, 0))
    return pl.pallas_call(
        _experts_kernel,
        grid_spec=pltpu.PrefetchScalarGridSpec(
            num_scalar_prefetch=2,
            grid=(nb,),
            in_specs=[blk, wspec(D_MODEL, D_EXPERT), wspec(D_MODEL, D_EXPERT), wspec(D_EXPERT, D_MODEL)],
            out_specs=blk,
            scratch_shapes=[pltpu.VMEM((D_MODEL, D_EXPERT), BF16), pltpu.VMEM((D_MODEL, D_EXPERT), BF16),
                            pltpu.VMEM((D_EXPERT, D_MODEL), BF16)]),
        out_shape=jax.ShapeDtypeStruct(xs.shape, xs.dtype),
        compiler_params=_cparams("arbitrary"),
        name="moe_experts",
    )(block_expert, n_valid, xs, wg, wu, wd)


def _combine_kernel(h_ref, rt_ref, p_ref, y0_ref, y1_ref, gp_ref, wpg_ref, wp_ref, gf_ref, o_ref, *, sub):
    n_sub = h_ref.shape[0] // sub

    def residual(r):
        rows = slice(r * sub, (r + 1) * sub)
        rt = rt_ref[rows, :]
        h = h_ref[rows, :] + (_load_token_rows(y0_ref, sub, F32, r * sub) * rt[:, 2:3]
                              + _load_token_rows(y1_ref, sub, F32, r * sub) * rt[:, 3:4])
        return h, _rms(h, gp_ref[...]).astype(BF16)

    def products(r, xn):
        return (jnp.dot(xn, wpg_ref[...], preferred_element_type=F32),
                _dot(p_ref[r * sub:(r + 1) * sub, :], wp_ref[...]))

    def finish(r, h, gate_lin, ple):
        h = h + ple * _sigmoid(gate_lin)
        o_ref[r * sub:(r + 1) * sub, :] = _rms(h, gf_ref[...])

    h, xn = residual(0)
    for r in range(n_sub):
        gate_lin, ple = products(r, xn)
        if r + 1 < n_sub:
            h_next, xn = residual(r + 1)
        finish(r, h, gate_lin, ple)
        if r + 1 < n_sub:
            h = h_next


def _combine(h1, rt, p2, y, gp, wpg, wp, gf, tm):
    t = h1.shape[0]
    nt = t // tm
    tile = lambda n: pl.BlockSpec((tm, n), lambda i: (i, 0))
    full = lambda a, b: pl.BlockSpec((a, b), lambda i: (0, 0))
    ytile = lambda k: pl.BlockSpec((tm * ROW_TILES, LANES), lambda i: (i + k * nt, 0))
    return pl.pallas_call(
        functools.partial(_combine_kernel, sub=128),
        grid=(nt,),
        in_specs=[tile(D_MODEL), tile(LANES), tile(D_PLE), ytile(0), ytile(1),
                  full(1, D_MODEL), full(D_MODEL, D_MODEL), full(D_PLE, D_MODEL), full(1, D_MODEL)],
        out_specs=tile(D_MODEL),
        out_shape=jax.ShapeDtypeStruct((t, D_MODEL), F32),
        compiler_params=_cparams("parallel"),
        name="moe_combine_ple",
    )(h1, rt, p2, y, y, gp, wpg, wp, gf)


def _block_diag(w):
    n, c, d = w.shape
    eye = jnp.eye(n, dtype=w.dtype)
    return (w[:, :, None, :] * eye[:, None, :, None]).reshape(n * c, n * d)


def _lane_row(vals, offset):
    return jnp.zeros((1, LANES), F32).at[0, offset:offset + vals.shape[0]].set(vals)


def kernel(x, p, norm_mix, w_in, lru_conv_w, lru_conv_b, lru_wa, lru_ba, lru_wi, lru_bi, lru_lambda,
           lru_out_norm, gdn_conv_w, gdn_a_log, gdn_dt_bias, gdn_out_norm, w_out, norm_ffn,
           w_router_group, b_router_group, w_router_expert, b_router_expert, w_exp_gate, w_exp_up,
           w_exp_down, norm_ple, w_ple_gate, w_ple, norm_final):
    bsz, seq, d = x.shape
    t = bsz * seq
    depth = w_in.shape[0]
    assert depth == 1, "the final norm is fused into the last layer's combine kernel"
    n_blocks = -(-t * TOP_K // MOE_BLOCK) + N_EXPERTS
    row = lambda v: v.reshape(1, -1).astype(F32)
    h = x.reshape(t, d).astype(F32)
    for l in range(depth):
        w_in_p = jnp.pad(w_in[l], ((0, 0), (0, PROJ_COLS - w_in.shape[2]))).astype(BF16)
        proj = _in_proj(h, row(norm_mix[l]), w_in_p, 512)
        w_gates = jnp.concatenate([_block_diag(lru_wa[l]), _block_diag(lru_wi[l])], axis=1).astype(BF16)
        b_gates = jnp.concatenate([lru_ba[l], lru_bi[l]]).reshape(1, -1)
        y_lru = _lru(proj, lru_conv_w[l], row(lru_conv_b[l]), w_gates, b_gates, row(lru_lambda[l]),
                     row(lru_out_norm[l]), bsz, seq, 512)
        cw = gdn_conv_w[l].reshape(CONV_W, 3, D_GDN).transpose(1, 0, 2)
        y_gdn = _gdn(proj, cw, _lane_row(gdn_a_log[l], GDN_HEADS), _lane_row(gdn_dt_bias[l], GDN_HEADS),
                     row(gdn_out_norm[l]), bsz, seq, 512)
        w_r = jnp.pad(jnp.concatenate([w_router_group[l], w_router_expert[l]], axis=1),
                      ((0, 0), (0, LANES - N_GROUPS - N_EXPERTS)))
        w_r_hi = w_r.astype(BF16)
        w_r_lo = (w_r - w_r_hi.astype(F32)).astype(BF16)
        w_r = jnp.concatenate([w_r_hi, w_r_hi, w_r_lo], axis=0)
        b_r = _lane_row(jnp.concatenate([b_router_group[l], b_router_expert[l]]), 0)
        h1, xn2, rt = _out_router(h, y_lru, y_gdn, w_out[l].astype(BF16), row(norm_ffn[l]), w_r, b_r, 512)
        dest8, seg = _slots(rt, 1024)
        dest = dest8[:TOP_K]
        pad_end = seg[0, :N_EXPERTS].astype(jnp.int32)
        counts = seg[1, :N_EXPERTS].astype(jnp.int32)
        block_start = jnp.arange(n_blocks, dtype=jnp.int32) * MOE_BLOCK
        block_expert = jnp.minimum(jnp.sum(pad_end[None, :] <= block_start[:, None], axis=1),
                                   N_EXPERTS - 1).astype(jnp.int32)
        seg_end = (pad_end - (-counts % MOE_BLOCK))[block_expert]
        n_valid = jnp.where(block_start < pad_end[N_EXPERTS - 1],
                            jnp.clip(seg_end - block_start, 0, MOE_BLOCK), 0).astype(jnp.int32)
        tiles = lambda a: a.reshape(-1, ROW_TILES, LANES)
        xs = _sc_scatter_rows(tiles(xn2), dest, n_blocks * MOE_BLOCK)
        ys = _experts(block_expert, n_valid, xs.reshape(-1, LANES), w_exp_gate[l], w_exp_up[l], w_exp_down[l])
        y = _sc_gather_rows(tiles(ys), dest.reshape(-1))
        h = _combine(h1, rt, p[l].reshape(t, -1).astype(F32), y.reshape(-1, LANES), row(norm_ple[l]),
                     w_ple_gate[l].astype(BF16), w_ple[l].astype(BF16), row(norm_final), 512)
    return h.reshape(bsz, seq, d).astype(x.dtype)
```

```python
import functools

import jax
import jax.numpy as jnp
import numpy as np
from jax import lax
from jax.experimental import pallas as pl
from jax.experimental.pallas import tpu as pltpu
from jax.experimental.pallas import tpu_sc as plsc

D_MODEL = 1024
D_LRU = 512
LRU_BLOCKS = 8
LRU_BLOCK_W = D_LRU // LRU_BLOCKS
LRU_C = 8.0
D_GDN = 512
GDN_HEADS = 4
GDN_HEAD_DIM = D_GDN // GDN_HEADS
CONV_W = 4
CHUNK = 64
N_GROUPS = 4
EXPERTS_PER_GROUP = 8
N_EXPERTS = N_GROUPS * EXPERTS_PER_GROUP
TOP_K = 2
D_EXPERT = 512
MOE_BLOCK = 512
D_PLE = 256
EPS = 1e-6

LANES = 128
SUBLANES = 8
ROW_TILES = D_MODEL // (2 * LANES)
SC_CORES = 2
SC_SUBCORES = 16
SC_WORKERS = SC_CORES * SC_SUBCORES
SC_WINDOW = 64
PROJ_COLS = 2 * D_LRU + 4 * D_GDN + LANES
VMEM_LIMIT = 56 * 1024 * 1024

BF16 = jnp.bfloat16
F32 = jnp.float32


def _cparams(*sem):
    return pltpu.CompilerParams(dimension_semantics=sem, vmem_limit_bytes=VMEM_LIMIT)


def _rms(x, g):
    return x * lax.rsqrt(jnp.mean(x * x, axis=-1, keepdims=True) + EPS) * g


def _sigmoid(x):
    return 0.5 * jnp.tanh(0.5 * x) + 0.5


def _silu(x):
    return x * _sigmoid(x)


def _dot(a, b):
    return jnp.dot(a.astype(BF16), b.astype(BF16), preferred_element_type=F32)


def _dot_nt(a, b):
    return lax.dot_general(a.astype(BF16), b.astype(BF16), (((1,), (1,)), ((), ())),
                           preferred_element_type=F32)


def _dot_tn(a, b):
    return lax.dot_general(a.astype(BF16), b.astype(BF16), (((0,), (0,)), ((), ())),
                           preferred_element_type=F32)


def _conv_scratch(rows, channels):
    return pltpu.VMEM((channels // LANES, 2 * (SUBLANES + rows), LANES), F32)


def _conv_reset(xp_ref):
    for k in range(xp_ref.shape[0]):
        xp_ref.at[k][pl.ds(0, SUBLANES, stride=2), :] = jnp.zeros((SUBLANES, LANES), F32)


def _causal_conv(xp_ref, x_ref, w, r0, rows, col0=0):
    out = []
    for k in range(xp_ref.shape[0]):
        lanes = slice(k * LANES, (k + 1) * LANES)
        slab = xp_ref.at[k]
        xk = x_ref[r0:r0 + rows, col0 + k * LANES:col0 + (k + 1) * LANES]
        slab[pl.ds(2 * (SUBLANES + r0), rows, stride=2), :] = xk
        acc = xk * w[CONV_W - 1:CONV_W, lanes]
        for j in range(1, CONV_W):
            acc = acc + slab[pl.ds(2 * (SUBLANES + r0 - j), rows, stride=2), :] * w[CONV_W - 1 - j:CONV_W - j, lanes]
        out.append(acc)
    return jnp.concatenate(out, axis=1)


def _conv_carry(xp_ref, x_ref, col0=0):
    ts = x_ref.shape[0]
    for k in range(xp_ref.shape[0]):
        xp_ref.at[k][pl.ds(0, SUBLANES, stride=2), :] = x_ref[ts - SUBLANES:, col0 + k * LANES:col0 + (k + 1) * LANES]


def _interleave(*stages):
    live = list(stages)
    while live:
        for g in list(live):
            try:
                next(g)
            except StopIteration:
                live.remove(g)


def _store_token_rows(ref, x, row0=0):
    rows = x.shape[0]
    bits = lax.bitcast_convert_type(x.astype(BF16).astype(F32), jnp.uint32)
    words = (bits[:, :D_MODEL // 2] >> 16) | bits[:, D_MODEL // 2:]
    for j in range(ROW_TILES):
        ref[pl.ds(row0 * ROW_TILES + j, rows, stride=ROW_TILES), :] = words[:, j * LANES:(j + 1) * LANES]


def _load_token_rows(ref, rows, dtype, row0=0):
    words = [ref[pl.ds(row0 * ROW_TILES + j, rows, stride=ROW_TILES), :] for j in range(ROW_TILES)]
    lo = [lax.bitcast_convert_type(w << 16, F32).astype(dtype) for w in words]
    hi = [lax.bitcast_convert_type(w & jnp.uint32(0xFFFF0000), F32).astype(dtype) for w in words]
    return jnp.concatenate(lo + hi, axis=1)


def _proj_lru_kernel(x_ref, gin_ref, win_ref, cw_ref, cb_ref, wa_ref, wi_ref, bg_ref, lam_ref, og_ref,
                     gdn_ref, o_ref, pj_ref, tail_ref, h_ref, wg_ref, *, tiles_per_seq, col_tile):
    g = pl.program_id(0)
    ts = x_ref.shape[0]
    pj_new = pj_ref.at[g % 2]
    pj = pj_ref.at[(g + 1) % 2]

    @pl.when(g == 0)
    def _():
        pj_ref[...] = jnp.zeros_like(pj_ref)
        wg_ref[...] = jnp.zeros_like(wg_ref)
        for n in range(LRU_BLOCKS):
            blk = slice(n * LRU_BLOCK_W, (n + 1) * LRU_BLOCK_W)
            wg_ref[blk, blk] = wa_ref[n].astype(BF16)
            wg_ref[blk, D_LRU + n * LRU_BLOCK_W:D_LRU + (n + 1) * LRU_BLOCK_W] = wi_ref[n].astype(BF16)

    @pl.when((g == 0) | (lax.rem(g - 1, tiles_per_seq) == 0))
    def _():
        _conv_reset(tail_ref)
        h_ref[...] = jnp.zeros_like(h_ref)

    def project():
        xn = _rms(x_ref[...], gin_ref[...]).astype(BF16)
        yield
        for c0 in range(0, PROJ_COLS, col_tile):
            c1 = min(c0 + col_tile, PROJ_COLS)
            y = jnp.dot(xn, win_ref[:, c0:c1], preferred_element_type=F32)
            if c1 <= 2 * D_LRU:
                pj_new[:, c0:c1] = y
            else:
                gdn_ref[:, c0 - 2 * D_LRU:c1 - 2 * D_LRU] = y
            yield

    def recur():
        xc = _causal_conv(tail_ref, pj, cw_ref[...], 0, ts) + cb_ref[...]
        _conv_carry(tail_ref, pj)
        yield
        gates = _sigmoid(_dot(xc, wg_ref[...]) + bg_ref[...])
        r = gates[:, :D_LRU]
        i = gates[:, D_LRU:]
        yield
        log_a = LRU_C * r * jax.nn.log_sigmoid(lam_ref[...])
        a = jnp.exp(log_a)
        th = jnp.tanh(log_a)
        u = jnp.sqrt(-2.0 * th) * lax.rsqrt(1.0 - th) * (i * xc)
        yield
        a = a.reshape(ts // SUBLANES, SUBLANES, D_LRU)
        u = u.reshape(ts // SUBLANES, SUBLANES, D_LRU)
        row = lax.broadcasted_iota(jnp.int32, a.shape, 1)
        d = 1
        while d < SUBLANES:
            keep = row >= d
            a_prev = jnp.where(keep, pltpu.roll(a, d, axis=1), 1.0)
            u_prev = jnp.where(keep, pltpu.roll(u, d, axis=1), 0.0)
            u = a * u_prev + u
            a = a * a_prev
            d *= 2
            yield
        carry = h_ref[...]
        groups = []
        for n in range(ts // SUBLANES):
            groups.append(a[n] * carry + u[n])
            carry = groups[-1][SUBLANES - 1:]
        h = jnp.concatenate(groups, axis=0)
        h_ref[...] = carry
        yield
        y = h * jax.nn.gelu(pj[:, D_LRU:])
        o_ref[...] = _rms(y, og_ref[...]).astype(o_ref.dtype)

    _interleave(project(), recur())


def _proj_lru(x2, gin, win, cw, cb, wa, wi, bg, lam, og, seq, ts):
    t = x2.shape[0]
    n_tiles = t // ts
    this = lambda g: (jnp.minimum(g, n_tiles - 1), 0)
    last = lambda g: (jnp.maximum(g - 1, 0), 0)
    const = lambda *shape: pl.BlockSpec(shape, lambda g: (0,) * len(shape))
    gdn_cols = PROJ_COLS - 2 * D_LRU
    return pl.pallas_call(
        functools.partial(_proj_lru_kernel, tiles_per_seq=seq // ts, col_tile=D_LRU),
        grid=(n_tiles + 1,),
        in_specs=[pl.BlockSpec((ts, D_MODEL), this), const(1, D_MODEL), const(D_MODEL, PROJ_COLS),
                  const(CONV_W, D_LRU), const(1, D_LRU),
                  const(LRU_BLOCKS, LRU_BLOCK_W, LRU_BLOCK_W), const(LRU_BLOCKS, LRU_BLOCK_W, LRU_BLOCK_W),
                  const(1, 2 * D_LRU), const(1, D_LRU), const(1, D_LRU)],
        out_specs=[pl.BlockSpec((ts, gdn_cols), this), pl.BlockSpec((ts, D_LRU), last)],
        out_shape=[jax.ShapeDtypeStruct((t, gdn_cols), F32), jax.ShapeDtypeStruct((t, D_LRU), BF16)],
        scratch_shapes=[pltpu.VMEM((2, ts, 2 * D_LRU), F32), _conv_scratch(ts, D_LRU), pltpu.VMEM((1, D_LRU), F32),
                        pltpu.VMEM((D_LRU, 2 * D_LRU), BF16)],
        compiler_params=_cparams("arbitrary"),
        name="in_proj_rglru",
    )(x2, gin, win, cw, cb, wa, wi, bg, lam, og)


def _gdn_kernel(q_ref, k_ref, v_ref, z_ref, ba_ref, cw_ref, alog_ref, dtb_ref, og_ref, o_ref,
                qt_ref, kt_ref, vt_ref, qs_ref, ks_ref, vs_ref, bs_ref, gc_ref, gct_ref, st_ref,
                *, group_chunks, prep_rows):
    ts = q_ref.shape[0]
    dk = GDN_HEAD_DIM
    nc = ts // CHUNK

    @pl.when(pl.program_id(1) == 0)
    def _():
        for tail_ref in (qt_ref, kt_ref, vt_ref):
            _conv_reset(tail_ref)
        st_ref[...] = jnp.zeros_like(st_ref)

    ri = lax.broadcasted_iota(jnp.int32, (CHUNK, CHUNK), 0)
    ci = lax.broadcasted_iota(jnp.int32, (CHUNK, CHUNK), 1)
    causal = ri >= ci
    strict = ri > ci
    tril = causal.astype(F32)
    eye = (ri == ci).astype(F32)
    og = og_ref[...]

    def l2n(x, scale):
        parts = []
        for h in range(GDN_HEADS):
            xh = x[:, h * dk:(h + 1) * dk]
            parts.append(xh * (lax.rsqrt(jnp.sum(xh * xh, axis=-1, keepdims=True) + EPS) * scale))
        return jnp.concatenate(parts, axis=1)

    def prepare(r0, r1):
        for p0 in range(r0, r1, prep_rows):
            rows = slice(p0, p0 + prep_rows)
            conv = lambda x_ref, tail_ref, part: _silu(
                _causal_conv(tail_ref, x_ref, cw_ref[part], p0, prep_rows))
            qs_ref[rows, :] = l2n(conv(q_ref, qt_ref, 0), dk ** -0.5)
            yield
            ks_ref[rows, :] = l2n(conv(k_ref, kt_ref, 1), 1.0)
            yield
            vs_ref[rows, :] = conv(v_ref, vt_ref, 2)
            ba = ba_ref[rows, :]
            bs_ref[rows, :] = _sigmoid(ba)
            g = -jnp.exp(alog_ref[...]) * jax.nn.softplus(ba + dtb_ref[...])
            for c0 in range(0, prep_rows, CHUNK):
                gc_ref[p0 + c0:p0 + c0 + CHUNK, :] = jnp.dot(tril, g[c0:c0 + CHUNK], precision=lax.Precision.HIGHEST,
                                                             preferred_element_type=F32)
            gct_ref[:, rows] = gc_ref[rows, :].T
            yield

    terms = {}

    def chunk_terms(pairs):
        n = range(len(pairs))
        rows = [slice(c * CHUNK, (c + 1) * CHUNK) for c, _ in pairs]
        cols = [slice(h * dk, (h + 1) * dk) for _, h in pairs]
        gl = [GDN_HEADS + h for _, h in pairs]
        kh = [ks_ref[rows[i], cols[i]] for i in n]
        kb = [kh[i] * bs_ref[rows[i], pairs[i][1]:pairs[i][1] + 1] for i in n]
        r = [_dot_nt(jnp.concatenate([kb[i], qs_ref[rows[i], cols[i]]], axis=0), kh[i]) for i in n]
        yield
        gcol = [gc_ref[rows[i], gl[i]:gl[i] + 1] for i in n]
        decay = []
        for i in n:
            diff = gcol[i] - gct_ref[gl[i]:gl[i] + 1, rows[i]]
            decay.append(jnp.where(causal, jnp.exp(jnp.where(causal, diff, 0.0)), 0.0))
        a = [jnp.where(strict, r[i][:CHUNK] * decay[i], 0.0) for i in n]
        qk = [(r[i][CHUNK:] * decay[i]).astype(BF16) for i in n]
        tinv = [eye - a[i] for i in n]
        p = 2
        while p < CHUNK:
            a = [_dot(a[i], a[i]) for i in n]
            yield
            tinv = [tinv[i] + _dot(tinv[i], a[i]) for i in n]
            yield
            p *= 2
        eg = [jnp.exp(gcol[i]) for i in n]
        rhs = [jnp.concatenate([vs_ref[rows[i], cols[i]] * bs_ref[rows[i], pairs[i][1]:pairs[i][1] + 1],
                                kb[i] * eg[i]], axis=1) for i in n]
        uw = [_dot(tinv[i], rhs[i]).astype(BF16) for i in n]
        yield
        qk_uw = [jnp.dot(qk[i], uw[i], preferred_element_type=F32) for i in n]
        glast = [gc_ref[(c + 1) * CHUNK - 1:(c + 1) * CHUNK, gl[i]:gl[i] + 1] for i, (c, _) in enumerate(pairs)]
        kd_uw = [_dot_tn(kh[i] * jnp.exp(glast[i] - gcol[i]), uw[i]) for i in n]
        yield
        for i in n:
            lhs = jnp.concatenate([kd_uw[i][:, dk:], qs_ref[rows[i], cols[i]] * eg[i] - qk_uw[i][:, dk:]],
                                  axis=0).astype(BF16)
            terms[pairs[i]] = (lhs, kd_uw[i][:, :dk], qk_uw[i][:, :dk], jnp.exp(glast[i]))

    state = [st_ref[h] for h in range(GDN_HEADS)]

    def advance(chunks):
        for c in chunks:
            rows = slice(c * CHUNK, (c + 1) * CHUNK)
            r = [jnp.dot(terms[c, h][0], state[h].astype(BF16), preferred_element_type=F32)
                 for h in range(GDN_HEADS)]
            for h in range(GDN_HEADS):
                cols = slice(h * dk, (h + 1) * dk)
                _, c_add, o_add, egl = terms[c, h]
                o = r[h][dk:] + o_add
                state[h] = egl * state[h] - r[h][:dk] + c_add
                o_ref[rows, cols] = (_rms(o, og) * _silu(z_ref[rows, cols])).astype(o_ref.dtype)
            yield

    groups = [range(c0, c0 + group_chunks) for c0 in range(0, nc, group_chunks)]
    pairs_of = lambda chunks: [(c, h) for c in chunks for h in range(GDN_HEADS)]
    span = lambda chunks: (chunks[0] * CHUNK, (chunks[-1] + 1) * CHUNK)
    _interleave(prepare(*span(groups[0])))
    for i, chunks in enumerate(groups):
        side = []
        if i + 1 < len(groups):
            side.append(prepare(*span(groups[i + 1])))
        if i > 0:
            side.append(advance(groups[i - 1]))
        _interleave(chunk_terms(pairs_of(chunks)), *side)
    _interleave(advance(groups[-1]))
    for h in range(GDN_HEADS):
        st_ref[h] = state[h]
    for x_ref, tail_ref in ((q_ref, qt_ref), (k_ref, kt_ref), (v_ref, vt_ref)):
        _conv_carry(tail_ref, x_ref)


def _gdn(proj, cw, alog, dtb, og, bsz, seq, ts):
    nt = seq // ts
    row = lambda b, s: b * nt + s
    col = lambda c: pl.BlockSpec((ts, D_GDN), lambda b, s: (row(b, s), c))
    return pl.pallas_call(
        functools.partial(_gdn_kernel, group_chunks=4, prep_rows=128),
        grid=(bsz, nt),
        in_specs=[col(0), col(1), col(2), col(3),
                  pl.BlockSpec((ts, LANES), lambda b, s: (row(b, s), 4 * D_GDN // LANES)),
                  pl.BlockSpec((3, CONV_W, D_GDN), lambda b, s: (0, 0, 0)),
                  pl.BlockSpec((1, LANES), lambda b, s: (0, 0)),
                  pl.BlockSpec((1, LANES), lambda b, s: (0, 0)),
                  pl.BlockSpec((1, GDN_HEAD_DIM), lambda b, s: (0, 0))],
        out_specs=pl.BlockSpec((ts, D_GDN), lambda b, s: (row(b, s), 0)),
        out_shape=jax.ShapeDtypeStruct((bsz * seq, D_GDN), BF16),
        scratch_shapes=[_conv_scratch(ts, D_GDN)] * 3
        + [pltpu.VMEM((ts, D_GDN), F32)] * 3
        + [pltpu.VMEM((ts, LANES), F32)] * 2
        + [pltpu.VMEM((LANES, ts), F32)]
        + [pltpu.VMEM((GDN_HEADS, GDN_HEAD_DIM, GDN_HEAD_DIM), F32)],
        compiler_params=_cparams("parallel", "arbitrary"),
        name="gdn",
    )(proj, proj, proj, proj, proj, cw, alog, dtb, og)


def _out_router_kernel(x_ref, yl_ref, yg_ref, wo_ref, g_ref, wr_ref, br_ref, h_ref, xn_ref, rt_ref, cnt_ref,
                       *, sub):
    n_sub = x_ref.shape[0] // sub
    counts = []

    def project(r):
        rows = slice(r * sub, (r + 1) * sub)
        h = x_ref[rows, :] + jnp.dot(yl_ref[rows, :], wo_ref[:D_LRU, :], preferred_element_type=F32) \
            + jnp.dot(yg_ref[rows, :], wo_ref[D_LRU:, :], preferred_element_type=F32)
        h_ref[rows, :] = h
        return h

    def route(r, h):
        xn = _rms(h, g_ref[...])
        _store_token_rows(xn_ref, xn, r * sub)
        xh = xn.astype(BF16)
        xl = (xn - xh.astype(F32)).astype(BF16)
        logits = jnp.dot(jnp.concatenate([xh, xl, xh], axis=1), wr_ref[...],
                         preferred_element_type=F32) + br_ref[...]
        lane = lax.broadcasted_iota(jnp.int32, logits.shape, 1).astype(F32)
        big = jnp.float32(2 * LANES)
        ninf = jnp.float32(-jnp.inf)

        def top1(vals):
            m = jnp.max(vals, axis=-1, keepdims=True)
            return m, jnp.min(jnp.where(vals == m, lane, big), axis=-1, keepdims=True)

        gl = jnp.where(lane < N_GROUPS, logits, ninf)
        gmax, gsel = top1(gl)
        p_group = 1.0 / jnp.sum(jnp.exp(gl - gmax), axis=-1, keepdims=True)
        lo = N_GROUPS + EXPERTS_PER_GROUP * gsel
        el = jnp.where((lane >= lo) & (lane < lo + EXPERTS_PER_GROUP), logits, ninf)
        m1, i1 = top1(el)
        m2, i2 = top1(jnp.where(lane == i1, ninf, el))
        rr = jnp.exp(m2 - m1)
        g1 = p_group / (1.0 + rr)
        g2 = p_group * rr / (1.0 + rr)
        rt_ref[r * sub:(r + 1) * sub, :] = jnp.where(
            lane == 0, i1 - N_GROUPS,
            jnp.where(lane == 1, i2 - N_GROUPS, jnp.where(lane == 2, g1, jnp.where(lane == 3, g2, 0.0))))
        member = ((lane == i1 - N_GROUPS) | (lane == i2 - N_GROUPS)).astype(F32)
        counts.append(jnp.sum(member, axis=0, keepdims=True))

    h = project(0)
    for r in range(1, n_sub):
        h_next = project(r)
        route(r - 1, h)
        h = h_next
    route(n_sub - 1, h)
    cnt_ref[...] = jnp.broadcast_to(sum(counts), cnt_ref.shape)


def _out_router(x2, yl, yg, wo, g, wr, br, tm):
    t = x2.shape[0]
    tile = lambda n: pl.BlockSpec((tm, n), lambda i: (i, 0))
    full = lambda a, b: pl.BlockSpec((a, b), lambda i: (0, 0))
    return pl.pallas_call(
        functools.partial(_out_router_kernel, sub=128),
        grid=(t // tm,),
        in_specs=[tile(D_MODEL), tile(D_LRU), tile(D_GDN), full(D_MODEL, D_MODEL), full(1, D_MODEL),
                  full(3 * D_MODEL, LANES), full(1, LANES)],
        out_specs=[tile(D_MODEL), pl.BlockSpec((tm * ROW_TILES, LANES), lambda i: (i, 0)), tile(LANES),
                   pl.BlockSpec((SUBLANES, LANES), lambda i: (i, 0))],
        out_shape=[jax.ShapeDtypeStruct((t, D_MODEL), F32),
                   jax.ShapeDtypeStruct((t * ROW_TILES, LANES), jnp.uint32),
                   jax.ShapeDtypeStruct((t, LANES), F32),
                   jax.ShapeDtypeStruct((t // tm * SUBLANES, LANES), F32)],
        compiler_params=_cparams("parallel"),
        name="out_router",
    )(x2, yl, yg, wo, g, wr, br)


def _slots_kernel(rt_ref, tc_ref, dest_ref, blocks_ref, cnt_ref, before_ref):
    i = pl.program_id(0)
    tm = rt_ref.shape[0]
    rt = rt_ref[...]
    lane = lax.broadcasted_iota(jnp.int32, rt.shape, 1)
    e0 = rt[:, 0:1].astype(jnp.int32)
    e1 = rt[:, 1:2].astype(jnp.int32)
    member = ((lane == e0) | (lane == e1)).astype(BF16)

    @pl.when(i == 0)
    def _():
        ri = lax.broadcasted_iota(jnp.int32, (tm, tm), 0)
        ci = lax.broadcasted_iota(jnp.int32, (tm, tm), 1)
        before_ref[...] = (ri > ci).astype(BF16)
        cnt = jnp.broadcast_to(jnp.sum(tc_ref[...], axis=0, keepdims=True) / SUBLANES, cnt_ref.shape)
        padded = jnp.ceil(cnt / MOE_BLOCK) * MOE_BLOCK
        l8 = lax.broadcasted_iota(jnp.int32, cnt.shape, 1)
        incl = padded
        d = 1
        while d < LANES:
            incl = incl + jnp.where(l8 >= d, pltpu.roll(incl, d, axis=1), 0.0)
            d *= 2
        cnt_ref[...] = incl - padded
        seg_start = (incl - padded)[0:1, :]
        seg_end = incl[0:1, :]
        real_end = seg_start + cnt[0:1, :]
        bl = lax.broadcasted_iota(jnp.int32, blocks_ref.shape, 1)
        b0 = (lax.broadcasted_iota(jnp.int32, blocks_ref.shape, 0) * MOE_BLOCK).astype(F32)
        is_expert = bl < N_EXPERTS
        expert = jnp.sum(jnp.where(is_expert & (seg_end <= b0), 1.0, 0.0), axis=-1, keepdims=True)
        real = jnp.maximum(jnp.minimum(real_end, b0 + MOE_BLOCK) - jnp.maximum(seg_start, b0), 0.0)
        n_real = jnp.sum(jnp.where(is_expert, real, 0.0), axis=-1, keepdims=True)
        blocks_ref[...] = jnp.where(bl == 0, jnp.minimum(expert, N_EXPERTS - 1.0),
                                    jnp.where(bl == 1, n_real, 0.0)).astype(jnp.int32)

    start = cnt_ref[...]
    pos = start[0:1, :] + jnp.dot(before_ref[...], member, preferred_element_type=F32)
    d0 = jnp.sum(jnp.where(lane == e0, pos, 0.0), axis=-1, keepdims=True)
    d1 = jnp.sum(jnp.where(lane == e1, pos, 0.0), axis=-1, keepdims=True)
    dest = jnp.where(lane == 0, d0, jnp.where(lane == 1, d1, 0.0))
    dest_ref[...] = dest.T[:SUBLANES].astype(jnp.int32)
    cnt_ref[...] = start + jnp.dot(jnp.ones((SUBLANES, tm), BF16), member, preferred_element_type=F32)


def _slots(rt, tile_counts, n_blocks, tm):
    t = rt.shape[0]
    return pl.pallas_call(
        _slots_kernel,
        grid=(t // tm,),
        in_specs=[pl.BlockSpec((tm, LANES), lambda i: (i, 0)),
                  pl.BlockSpec(tile_counts.shape, lambda i: (0, 0))],
        out_specs=[pl.BlockSpec((SUBLANES, tm), lambda i: (0, i)),
                   pl.BlockSpec((n_blocks, LANES), lambda i: (0, 0))],
        out_shape=[jax.ShapeDtypeStruct((SUBLANES, t), jnp.int32),
                   jax.ShapeDtypeStruct((n_blocks, LANES), jnp.int32)],
        scratch_shapes=[pltpu.VMEM((SUBLANES, LANES), F32), pltpu.VMEM((tm, tm), BF16)],
        compiler_params=_cparams("arbitrary"),
        name="moe_slots",
    )(rt, tile_counts)


def _sc_mesh():
    return plsc.VectorSubcoreMesh(core_axis_name="c", subcore_axis_name="s")


def _sc_worker():
    return lax.axis_index("s") * SC_CORES + lax.axis_index("c")


def _sc_scatter_rows(src, idx, n_rows):
    t = src.shape[0]
    per_w = t // SC_WORKERS
    n_win = per_w // SC_WINDOW
    idx = idx.reshape(TOP_K, SC_WORKERS, n_win, SC_WINDOW)

    @functools.partial(
        pl.kernel, mesh=_sc_mesh(),
        out_type=jax.ShapeDtypeStruct((n_rows,) + src.shape[1:], src.dtype),
        scratch_types=[pltpu.VMEM((TOP_K, n_win, SC_WINDOW), jnp.int32),
                       pltpu.VMEM((2, SC_WINDOW) + src.shape[1:], src.dtype),
                       pltpu.SemaphoreType.DMA((2,)), pltpu.SemaphoreType.DMA((2,))],
        compiler_params=pltpu.CompilerParams(use_tc_tiling_on_sc=True),
        name="sc_dispatch")
    def scatter(src_hbm, idx_hbm, out_hbm, idx_v, rows_v, lsem, ssem):
        wid = _sc_worker()
        base = wid * per_w
        for k in range(TOP_K):
            pltpu.sync_copy(idx_hbm.at[k, wid], idx_v.at[k])

        def load(w, slot):
            return pltpu.make_async_copy(src_hbm.at[pl.ds(base + w * SC_WINDOW, SC_WINDOW)], rows_v.at[slot],
                                         lsem.at[slot])

        def put(w, slot, k):
            return pltpu.make_async_copy(rows_v.at[slot], out_hbm.at[idx_v.at[k, w]], ssem.at[slot])

        load(0, 0).start()

        @pl.loop(0, n_win, step=2)
        def _(w0):
            for s in range(2):
                w = w0 + s

                @pl.when(w + 1 < n_win)
                def _():
                    @pl.when(w >= 1)
                    def _():
                        for k in range(TOP_K):
                            put(w - 1, 1 - s, k).wait()

                    load(w + 1, 1 - s).start()

                load(w, s).wait()
                for k in range(TOP_K):
                    put(w, s, k).start()

        for s in range(2):
            for k in range(TOP_K):
                put(n_win - 2 + s, s, k).wait()

    return scatter(src, idx)


def _sc_gather_rows(table, idx):
    b = idx.shape[0]
    per_w = b // SC_WORKERS
    n_win = per_w // SC_WINDOW
    idx = idx.reshape(SC_WORKERS, n_win, SC_WINDOW)

    @functools.partial(
        pl.kernel, mesh=_sc_mesh(),
        out_type=jax.ShapeDtypeStruct((b,) + table.shape[1:], table.dtype),
        scratch_types=[pltpu.VMEM((n_win, SC_WINDOW), jnp.int32),
                       pltpu.VMEM((2, SC_WINDOW) + table.shape[1:], table.dtype),
                       pltpu.SemaphoreType.DMA((2,)), pltpu.SemaphoreType.DMA((2,))],
        compiler_params=pltpu.CompilerParams(use_tc_tiling_on_sc=True),
        name="sc_combine_gather")
    def gather(table_hbm, idx_hbm, out_hbm, idx_v, rows_v, gsem, psem):
        wid = _sc_worker()
        base = wid * per_w
        pltpu.sync_copy(idx_hbm.at[wid], idx_v)

        def get(w, slot):
            return pltpu.make_async_copy(table_hbm.at[idx_v.at[w]], rows_v.at[slot], gsem.at[slot])

        def put(w, slot):
            return pltpu.make_async_copy(rows_v.at[slot], out_hbm.at[pl.ds(base + w * SC_WINDOW, SC_WINDOW)],
                                         psem.at[slot])

        get(0, 0).start()

        @pl.loop(0, n_win, step=2)
        def _(w0):
            for s in range(2):
                w = w0 + s

                @pl.when(w + 1 < n_win)
                def _():
                    @pl.when(w >= 1)
                    def _():
                        put(w - 1, 1 - s).wait()

                    get(w + 1, 1 - s).start()

                get(w, s).wait()
                put(w, s).start()

        for s in range(2):
            put(n_win - 2 + s, s).wait()

    return gather(table, idx)


def _experts_kernel(be_ref, nv_ref, xs_ref, wg_ref, wu_ref, wd_ref, ys_ref, wgb_ref, wub_ref, wdb_ref):
    b = pl.program_id(0)
    prev = be_ref[jnp.maximum(b - 1, 0)]

    @pl.when((b == 0) | (be_ref[b] != prev))
    def _():
        wgb_ref[...] = wg_ref[0].astype(BF16)
        wub_ref[...] = wu_ref[0].astype(BF16)
        wdb_ref[...] = wd_ref[0].astype(BF16)

    @pl.when(nv_ref[b] > 0)
    def _():
        row = lax.broadcasted_iota(jnp.int32, (MOE_BLOCK, D_MODEL), 0)
        xb = jnp.where(row < nv_ref[b], _load_token_rows(xs_ref, MOE_BLOCK, BF16), 0.0)
        hg = jnp.dot(xb, wgb_ref[...], preferred_element_type=F32)
        hu = jnp.dot(xb, wub_ref[...], preferred_element_type=F32)
        hb = (_silu(hg) * hu).astype(BF16)
        _store_token_rows(ys_ref, jnp.dot(hb, wdb_ref[...], preferred_element_type=F32))

    @pl.when(nv_ref[b] == 0)
    def _():
        ys_ref[...] = jnp.zeros_like(ys_ref)


def _experts(block_expert, n_valid, xs, wg, wu, wd):
    nb = xs.shape[0] // (MOE_BLOCK * ROW_TILES)
    blk = pl.BlockSpec((MOE_BLOCK * ROW_TILES, LANES), lambda b, be, nv: (b, 0))
    wspec = lambda a, c: pl.BlockSpec((1, a, c), lambda b, be, nv: (be[b], 0, 0))
    return pl.pallas_call(
        _experts_kernel,
        grid_spec=pltpu.PrefetchScalarGridSpec(
            num_scalar_prefetch=2,
            grid=(nb,),
            in_specs=[blk, wspec(D_MODEL, D_EXPERT), wspec(D_MODEL, D_EXPERT), wspec(D_EXPERT, D_MODEL)],
            out_specs=blk,
            scratch_shapes=[pltpu.VMEM((D_MODEL, D_EXPERT), BF16), pltpu.VMEM((D_MODEL, D_EXPERT), BF16),
                            pltpu.VMEM((D_EXPERT, D_MODEL), BF16)]),
        out_shape=jax.ShapeDtypeStruct(xs.shape, xs.dtype),
        compiler_params=_cparams("arbitrary"),
        name="moe_experts",
    )(block_expert, n_valid, xs, wg, wu, wd)


def _combine_kernel(h_ref, rt_ref, p_ref, y0_ref, y1_ref, gp_ref, wpg_ref, wp_ref, gf_ref, o_ref, *, sub):
    n_sub = h_ref.shape[0] // sub

    def residual(r):
        rows = slice(r * sub, (r + 1) * sub)
        rt = rt_ref[rows, :]
        h = h_ref[rows, :] + (_load_token_rows(y0_ref, sub, F32, r * sub) * rt[:, 2:3]
                              + _load_token_rows(y1_ref, sub, F32, r * sub) * rt[:, 3:4])
        return h, _rms(h, gp_ref[...]).astype(BF16)

    def products(r, xn):
        return (jnp.dot(xn, wpg_ref[...], preferred_element_type=F32),
                _dot(p_ref[r * sub:(r + 1) * sub, :], wp_ref[...]))

    def finish(r, h, gate_lin, ple):
        h = h + ple * _sigmoid(gate_lin)
        o_ref[r * sub:(r + 1) * sub, :] = _rms(h, gf_ref[...])

    h, xn = residual(0)
    for r in range(n_sub):
        gate_lin, ple = products(r, xn)
        if r + 1 < n_sub:
            h_next, xn = residual(r + 1)
        finish(r, h, gate_lin, ple)
        if r + 1 < n_sub:
            h = h_next


def _combine(h1, rt, p2, y, gp, wpg, wp, gf, tm):
    t = h1.shape[0]
    nt = t // tm
    tile = lambda n: pl.BlockSpec((tm, n), lambda i: (i, 0))
    full = lambda a, b: pl.BlockSpec((a, b), lambda i: (0, 0))
    ytile = lambda k: pl.BlockSpec((tm * ROW_TILES, LANES), lambda i: (i + k * nt, 0))
    return pl.pallas_call(
        functools.partial(_combine_kernel, sub=128),
        grid=(nt,),
        in_specs=[tile(D_MODEL), tile(LANES), tile(D_PLE), ytile(0), ytile(1),
                  full(1, D_MODEL), full(D_MODEL, D_MODEL), full(D_PLE, D_MODEL), full(1, D_MODEL)],
        out_specs=tile(D_MODEL),
        out_shape=jax.ShapeDtypeStruct((t, D_MODEL), F32),
        compiler_params=_cparams("parallel"),
        name="moe_combine_ple",
    )(h1, rt, p2, y, y, gp, wpg, wp, gf)


def _lane_row(vals, offset):
    return jnp.zeros((1, LANES), F32).at[0, offset:offset + vals.shape[0]].set(vals)


def kernel(x, p, norm_mix, w_in, lru_conv_w, lru_conv_b, lru_wa, lru_ba, lru_wi, lru_bi, lru_lambda,
           lru_out_norm, gdn_conv_w, gdn_a_log, gdn_dt_bias, gdn_out_norm, w_out, norm_ffn,
           w_router_group, b_router_group, w_router_expert, b_router_expert, w_exp_gate, w_exp_up,
           w_exp_down, norm_ple, w_ple_gate, w_ple, norm_final):
    bsz, seq, d = x.shape
    t = bsz * seq
    depth = w_in.shape[0]
    assert depth == 1, "the final norm is fused into the last layer's combine kernel"
    n_blocks = -(-t * TOP_K // MOE_BLOCK) + N_EXPERTS
    row = lambda v: v.reshape(1, -1).astype(F32)
    h = x.reshape(t, d).astype(F32)
    for l in range(depth):
        w_in_p = jnp.pad(w_in[l], ((0, 0), (0, PROJ_COLS - w_in.shape[2]))).astype(BF16)
        b_gates = jnp.concatenate([lru_ba[l], lru_bi[l]]).reshape(1, -1)
        proj_gdn, y_lru = _proj_lru(h, row(norm_mix[l]), w_in_p, lru_conv_w[l], row(lru_conv_b[l]), lru_wa[l],
                                    lru_wi[l], b_gates, row(lru_lambda[l]), row(lru_out_norm[l]), seq, 512)
        cw = gdn_conv_w[l].reshape(CONV_W, 3, D_GDN).transpose(1, 0, 2)
        y_gdn = _gdn(proj_gdn, cw, _lane_row(gdn_a_log[l], GDN_HEADS), _lane_row(gdn_dt_bias[l], GDN_HEADS),
                     row(gdn_out_norm[l]), bsz, seq, 512)
        w_r = jnp.pad(jnp.concatenate([w_router_group[l], w_router_expert[l]], axis=1),
                      ((0, 0), (0, LANES - N_GROUPS - N_EXPERTS)))
        w_r_hi = w_r.astype(BF16)
        w_r_lo = (w_r - w_r_hi.astype(F32)).astype(BF16)
        w_r = jnp.concatenate([w_r_hi, w_r_hi, w_r_lo], axis=0)
        b_r = _lane_row(jnp.concatenate([b_router_group[l], b_router_expert[l]]), 0)
        h1, xn2, rt, tile_counts = _out_router(h, y_lru, y_gdn, w_out[l].astype(BF16), row(norm_ffn[l]), w_r, b_r, 512)
        dest8, blocks = _slots(rt, tile_counts, n_blocks, 1024)
        dest = dest8[:TOP_K]
        block_expert, n_valid = blocks[:, 0], blocks[:, 1]
        tiles = lambda a: a.reshape(-1, ROW_TILES, LANES)
        xs = _sc_scatter_rows(tiles(xn2), dest, n_blocks * MOE_BLOCK)
        ys = _experts(block_expert, n_valid, xs.reshape(-1, LANES), w_exp_gate[l], w_exp_up[l], w_exp_down[l])
        y = _sc_gather_rows(tiles(ys), dest.reshape(-1))
        h = _combine(h1, rt, p[l].reshape(t, -1).astype(F32), y.reshape(-1, LANES), row(norm_ple[l]),
                     w_ple_gate[l].astype(BF16), w_ple[l].astype(BF16), row(norm_final), 512)
    return h.reshape(bsz, seq, d).astype(x.dtype)
```

```python
import functools

import jax
import jax.numpy as jnp
import numpy as np
from jax import lax
from jax.experimental import pallas as pl
from jax.experimental.pallas import tpu as pltpu
from jax.experimental.pallas import tpu_sc as plsc

D_MODEL = 1024
D_LRU = 512
LRU_BLOCKS = 8
LRU_BLOCK_W = D_LRU // LRU_BLOCKS
LRU_C = 8.0
D_GDN = 512
GDN_HEADS = 4
GDN_HEAD_DIM = D_GDN // GDN_HEADS
CONV_W = 4
CHUNK = 64
N_GROUPS = 4
EXPERTS_PER_GROUP = 8
N_EXPERTS = N_GROUPS * EXPERTS_PER_GROUP
TOP_K = 2
D_EXPERT = 512
MOE_BLOCK = 512
D_PLE = 256
EPS = 1e-6

LANES = 128
SUBLANES = 8
ROW_TILES = D_MODEL // (2 * LANES)
SC_CORES = 2
SC_SUBCORES = 16
SC_WORKERS = SC_CORES * SC_SUBCORES
SC_WINDOW = 64
PROJ_COLS = 2 * D_LRU + 4 * D_GDN + LANES
VMEM_LIMIT = 56 * 1024 * 1024

BF16 = jnp.bfloat16
F32 = jnp.float32


def _cparams(*sem):
    return pltpu.CompilerParams(dimension_semantics=sem, vmem_limit_bytes=VMEM_LIMIT)


def _rms(x, g):
    return x * lax.rsqrt(jnp.mean(x * x, axis=-1, keepdims=True) + EPS) * g


def _sigmoid(x):
    return 0.5 * jnp.tanh(0.5 * x) + 0.5


def _silu(x):
    return x * _sigmoid(x)


def _dot(a, b):
    return jnp.dot(a.astype(BF16), b.astype(BF16), preferred_element_type=F32)


def _dot_nt(a, b):
    return lax.dot_general(a.astype(BF16), b.astype(BF16), (((1,), (1,)), ((), ())),
                           preferred_element_type=F32)


def _dot_tn(a, b):
    return lax.dot_general(a.astype(BF16), b.astype(BF16), (((0,), (0,)), ((), ())),
                           preferred_element_type=F32)


def _conv_scratch(rows, channels):
    return pltpu.VMEM((channels // LANES, 2 * (SUBLANES + rows), LANES), F32)


def _conv_reset(xp_ref):
    for k in range(xp_ref.shape[0]):
        xp_ref.at[k][pl.ds(0, SUBLANES, stride=2), :] = jnp.zeros((SUBLANES, LANES), F32)


def _causal_conv(xp_ref, x_ref, w, r0, rows, col0=0):
    out = []
    for k in range(xp_ref.shape[0]):
        lanes = slice(k * LANES, (k + 1) * LANES)
        slab = xp_ref.at[k]
        xk = x_ref[r0:r0 + rows, col0 + k * LANES:col0 + (k + 1) * LANES]
        slab[pl.ds(2 * (SUBLANES + r0), rows, stride=2), :] = xk
        acc = xk * w[CONV_W - 1:CONV_W, lanes]
        for j in range(1, CONV_W):
            acc = acc + slab[pl.ds(2 * (SUBLANES + r0 - j), rows, stride=2), :] * w[CONV_W - 1 - j:CONV_W - j, lanes]
        out.append(acc)
    return jnp.concatenate(out, axis=1)


def _conv_carry(xp_ref, x_ref, col0=0):
    ts = x_ref.shape[0]
    for k in range(xp_ref.shape[0]):
        xp_ref.at[k][pl.ds(0, SUBLANES, stride=2), :] = x_ref[ts - SUBLANES:, col0 + k * LANES:col0 + (k + 1) * LANES]


def _interleave(*stages):
    live = list(stages)
    while live:
        for g in list(live):
            try:
                next(g)
            except StopIteration:
                live.remove(g)


def _store_token_rows(ref, x, row0=0):
    rows = x.shape[0]
    bits = lax.bitcast_convert_type(x.astype(BF16).astype(F32), jnp.uint32)
    words = (bits[:, :D_MODEL // 2] >> 16) | bits[:, D_MODEL // 2:]
    for j in range(ROW_TILES):
        ref[pl.ds(row0 * ROW_TILES + j, rows, stride=ROW_TILES), :] = words[:, j * LANES:(j + 1) * LANES]


def _load_token_rows(ref, rows, dtype, row0=0):
    words = [ref[pl.ds(row0 * ROW_TILES + j, rows, stride=ROW_TILES), :] for j in range(ROW_TILES)]
    lo = [lax.bitcast_convert_type(w << 16, F32).astype(dtype) for w in words]
    hi = [lax.bitcast_convert_type(w & jnp.uint32(0xFFFF0000), F32).astype(dtype) for w in words]
    return jnp.concatenate(lo + hi, axis=1)


def _proj_lru_kernel(x_ref, gin_ref, win_ref, cw_ref, cb_ref, wa_ref, wi_ref, bg_ref, lam_ref, og_ref,
                     gdn_ref, o_ref, pj_ref, tail_ref, h_ref, wg_ref, *, tiles_per_seq, col_tile):
    g = pl.program_id(0)
    ts = x_ref.shape[0]
    pj_new = pj_ref.at[g % 2]
    pj = pj_ref.at[(g + 1) % 2]

    @pl.when(g == 0)
    def _():
        pj_ref[...] = jnp.zeros_like(pj_ref)
        wg_ref[...] = jnp.zeros_like(wg_ref)
        for n in range(LRU_BLOCKS):
            blk = slice(n * LRU_BLOCK_W, (n + 1) * LRU_BLOCK_W)
            wg_ref[blk, blk] = wa_ref[n].astype(BF16)
            wg_ref[blk, D_LRU + n * LRU_BLOCK_W:D_LRU + (n + 1) * LRU_BLOCK_W] = wi_ref[n].astype(BF16)

    @pl.when((g == 0) | (lax.rem(g - 1, tiles_per_seq) == 0))
    def _():
        _conv_reset(tail_ref)
        h_ref[...] = jnp.zeros_like(h_ref)

    done = []

    def after_projection(v):
        return v + jnp.concatenate([done[max(len(done) - 2, 0)]] * (v.shape[-1] // LANES), axis=-1)

    def project():
        xn = _rms(x_ref[...], gin_ref[...]).astype(BF16)
        yield
        for c0 in range(0, PROJ_COLS, col_tile):
            c1 = min(c0 + col_tile, PROJ_COLS)
            y = jnp.dot(xn, win_ref[:, c0:c1], preferred_element_type=F32)
            if c1 <= 2 * D_LRU:
                pj_new[:, c0:c1] = y
            else:
                gdn_ref[:, c0 - 2 * D_LRU:c1 - 2 * D_LRU] = y
            bits = lax.bitcast_convert_type(y[ts - 1:, c1 - c0 - LANES:], jnp.uint32)
            done.append(lax.bitcast_convert_type((bits >> 16) >> 16, F32))
            yield

    def recur():
        xc = _causal_conv(tail_ref, pj, cw_ref[...], 0, ts) + cb_ref[...]
        _conv_carry(tail_ref, pj)
        yield
        gates = _sigmoid(_dot(xc, wg_ref[...]) + after_projection(bg_ref[...]))
        r = gates[:, :D_LRU]
        i = gates[:, D_LRU:]
        yield
        log_a = LRU_C * r * jax.nn.log_sigmoid(after_projection(lam_ref[...]))
        a = jnp.exp(log_a)
        th = jnp.tanh(log_a)
        u = jnp.sqrt(-2.0 * th) * lax.rsqrt(1.0 - th) * (i * xc)
        yield
        a = a.reshape(ts // SUBLANES, SUBLANES, D_LRU)
        u = u.reshape(ts // SUBLANES, SUBLANES, D_LRU)
        row = lax.broadcasted_iota(jnp.int32, a.shape, 1)
        d = 1
        while d < SUBLANES:
            keep = row >= d
            a_prev = jnp.where(keep, pltpu.roll(a, d, axis=1), 1.0)
            u_prev = jnp.where(keep, pltpu.roll(u, d, axis=1), 0.0)
            u = a * u_prev + u
            a = a * a_prev
            d *= 2
            yield
        carry = after_projection(h_ref[...])
        groups = []
        for n in range(ts // SUBLANES):
            groups.append(a[n] * carry + u[n])
            carry = groups[-1][SUBLANES - 1:]
        h = jnp.concatenate(groups, axis=0)
        h_ref[...] = carry
        yield
        y = h * jax.nn.gelu(pj[:, D_LRU:])
        o_ref[...] = _rms(y, after_projection(og_ref[...])).astype(o_ref.dtype)

    _interleave(project(), recur())


def _proj_lru(x2, gin, win, cw, cb, wa, wi, bg, lam, og, seq, ts):
    t = x2.shape[0]
    n_tiles = t // ts
    this = lambda g: (jnp.minimum(g, n_tiles - 1), 0)
    last = lambda g: (jnp.maximum(g - 1, 0), 0)
    const = lambda *shape: pl.BlockSpec(shape, lambda g: (0,) * len(shape))
    gdn_cols = PROJ_COLS - 2 * D_LRU
    return pl.pallas_call(
        functools.partial(_proj_lru_kernel, tiles_per_seq=seq // ts, col_tile=D_LRU),
        grid=(n_tiles + 1,),
        in_specs=[pl.BlockSpec((ts, D_MODEL), this), const(1, D_MODEL), const(D_MODEL, PROJ_COLS),
                  const(CONV_W, D_LRU), const(1, D_LRU),
                  const(LRU_BLOCKS, LRU_BLOCK_W, LRU_BLOCK_W), const(LRU_BLOCKS, LRU_BLOCK_W, LRU_BLOCK_W),
                  const(1, 2 * D_LRU), const(1, D_LRU), const(1, D_LRU)],
        out_specs=[pl.BlockSpec((ts, gdn_cols), this), pl.BlockSpec((ts, D_LRU), last)],
        out_shape=[jax.ShapeDtypeStruct((t, gdn_cols), F32), jax.ShapeDtypeStruct((t, D_LRU), BF16)],
        scratch_shapes=[pltpu.VMEM((2, ts, 2 * D_LRU), F32), _conv_scratch(ts, D_LRU), pltpu.VMEM((1, D_LRU), F32),
                        pltpu.VMEM((D_LRU, 2 * D_LRU), BF16)],
        compiler_params=_cparams("arbitrary"),
        name="in_proj_rglru",
    )(x2, gin, win, cw, cb, wa, wi, bg, lam, og)


def _gdn_kernel(q_ref, k_ref, v_ref, z_ref, ba_ref, cw_ref, alog_ref, dtb_ref, og_ref, o_ref,
                qt_ref, kt_ref, vt_ref, qs_ref, ks_ref, vs_ref, bs_ref, gc_ref, gct_ref, st_ref,
                *, group_chunks, prep_rows):
    ts = q_ref.shape[0]
    dk = GDN_HEAD_DIM
    nc = ts // CHUNK

    @pl.when(pl.program_id(1) == 0)
    def _():
        for tail_ref in (qt_ref, kt_ref, vt_ref):
            _conv_reset(tail_ref)
        st_ref[...] = jnp.zeros_like(st_ref)

    ri = lax.broadcasted_iota(jnp.int32, (CHUNK, CHUNK), 0)
    ci = lax.broadcasted_iota(jnp.int32, (CHUNK, CHUNK), 1)
    causal = ri >= ci
    strict = ri > ci
    tril = causal.astype(F32)
    eye = (ri == ci).astype(F32)
    og = og_ref[...]

    def l2n(x, scale):
        parts = []
        for h in range(GDN_HEADS):
            xh = x[:, h * dk:(h + 1) * dk]
            parts.append(xh * (lax.rsqrt(jnp.sum(xh * xh, axis=-1, keepdims=True) + EPS) * scale))
        return jnp.concatenate(parts, axis=1)

    def prepare(r0, r1):
        for p0 in range(r0, r1, prep_rows):
            rows = slice(p0, p0 + prep_rows)
            conv = lambda x_ref, tail_ref, part: _silu(
                _causal_conv(tail_ref, x_ref, cw_ref[part], p0, prep_rows))
            qs_ref[rows, :] = l2n(conv(q_ref, qt_ref, 0), dk ** -0.5)
            yield
            ks_ref[rows, :] = l2n(conv(k_ref, kt_ref, 1), 1.0)
            yield
            vs_ref[rows, :] = conv(v_ref, vt_ref, 2)
            ba = ba_ref[rows, :]
            bs_ref[rows, :] = _sigmoid(ba)
            g = -jnp.exp(alog_ref[...]) * jax.nn.softplus(ba + dtb_ref[...])
            for c0 in range(0, prep_rows, CHUNK):
                gc_ref[p0 + c0:p0 + c0 + CHUNK, :] = jnp.dot(tril, g[c0:c0 + CHUNK], precision=lax.Precision.HIGHEST,
                                                             preferred_element_type=F32)
            gct_ref[:, rows] = gc_ref[rows, :].T
            yield

    terms = {}

    def chunk_terms(pairs):
        n = range(len(pairs))
        rows = [slice(c * CHUNK, (c + 1) * CHUNK) for c, _ in pairs]
        cols = [slice(h * dk, (h + 1) * dk) for _, h in pairs]
        gl = [GDN_HEADS + h for _, h in pairs]
        kh = [ks_ref[rows[i], cols[i]] for i in n]
        kb = [kh[i] * bs_ref[rows[i], pairs[i][1]:pairs[i][1] + 1] for i in n]
        r = [_dot_nt(jnp.concatenate([kb[i], qs_ref[rows[i], cols[i]]], axis=0), kh[i]) for i in n]
        yield
        gcol = [gc_ref[rows[i], gl[i]:gl[i] + 1] for i in n]
        decay = []
        for i in n:
            diff = gcol[i] - gct_ref[gl[i]:gl[i] + 1, rows[i]]
            decay.append(jnp.where(causal, jnp.exp(jnp.where(causal, diff, 0.0)), 0.0))
        a = [jnp.where(strict, r[i][:CHUNK] * decay[i], 0.0) for i in n]
        qk = [(r[i][CHUNK:] * decay[i]).astype(BF16) for i in n]
        tinv = [eye - a[i] for i in n]
        p = 2
        while p < CHUNK:
            a = [_dot(a[i], a[i]) for i in n]
            yield
            tinv = [tinv[i] + _dot(tinv[i], a[i]) for i in n]
            yield
            p *= 2
        eg = [jnp.exp(gcol[i]) for i in n]
        rhs = [jnp.concatenate([vs_ref[rows[i], cols[i]] * bs_ref[rows[i], pairs[i][1]:pairs[i][1] + 1],
                                kb[i] * eg[i]], axis=1) for i in n]
        uw = [_dot(tinv[i], rhs[i]).astype(BF16) for i in n]
        yield
        qk_uw = [jnp.dot(qk[i], uw[i], preferred_element_type=F32) for i in n]
        glast = [gc_ref[(c + 1) * CHUNK - 1:(c + 1) * CHUNK, gl[i]:gl[i] + 1] for i, (c, _) in enumerate(pairs)]
        kd_uw = [_dot_tn(kh[i] * jnp.exp(glast[i] - gcol[i]), uw[i]) for i in n]
        yield
        for i in n:
            lhs = jnp.concatenate([kd_uw[i][:, dk:], qs_ref[rows[i], cols[i]] * eg[i] - qk_uw[i][:, dk:]],
                                  axis=0).astype(BF16)
            terms[pairs[i]] = (lhs, kd_uw[i][:, :dk], qk_uw[i][:, :dk], jnp.exp(glast[i]))

    state = [st_ref[h] for h in range(GDN_HEADS)]

    def advance(chunks):
        for c in chunks:
            rows = slice(c * CHUNK, (c + 1) * CHUNK)
            r = [jnp.dot(terms[c, h][0], state[h].astype(BF16), preferred_element_type=F32)
                 for h in range(GDN_HEADS)]
            for h in range(GDN_HEADS):
                cols = slice(h * dk, (h + 1) * dk)
                _, c_add, o_add, egl = terms[c, h]
                o = r[h][dk:] + o_add
                state[h] = egl * state[h] - r[h][:dk] + c_add
                o_ref[rows, cols] = (_rms(o, og) * _silu(z_ref[rows, cols])).astype(o_ref.dtype)
            yield

    groups = [range(c0, c0 + group_chunks) for c0 in range(0, nc, group_chunks)]
    pairs_of = lambda chunks: [(c, h) for c in chunks for h in range(GDN_HEADS)]
    span = lambda chunks: (chunks[0] * CHUNK, (chunks[-1] + 1) * CHUNK)
    _interleave(prepare(*span(groups[0])))
    for i, chunks in enumerate(groups):
        side = []
        if i + 1 < len(groups):
            side.append(prepare(*span(groups[i + 1])))
        if i > 0:
            side.append(advance(groups[i - 1]))
        _interleave(chunk_terms(pairs_of(chunks)), *side)
    _interleave(advance(groups[-1]))
    for h in range(GDN_HEADS):
        st_ref[h] = state[h]
    for x_ref, tail_ref in ((q_ref, qt_ref), (k_ref, kt_ref), (v_ref, vt_ref)):
        _conv_carry(tail_ref, x_ref)


def _gdn(proj, cw, alog, dtb, og, bsz, seq, ts):
    nt = seq // ts
    row = lambda b, s: b * nt + s
    col = lambda c: pl.BlockSpec((ts, D_GDN), lambda b, s: (row(b, s), c))
    return pl.pallas_call(
        functools.partial(_gdn_kernel, group_chunks=4, prep_rows=128),
        grid=(bsz, nt),
        in_specs=[col(0), col(1), col(2), col(3),
                  pl.BlockSpec((ts, LANES), lambda b, s: (row(b, s), 4 * D_GDN // LANES)),
                  pl.BlockSpec((3, CONV_W, D_GDN), lambda b, s: (0, 0, 0)),
                  pl.BlockSpec((1, LANES), lambda b, s: (0, 0)),
                  pl.BlockSpec((1, LANES), lambda b, s: (0, 0)),
                  pl.BlockSpec((1, GDN_HEAD_DIM), lambda b, s: (0, 0))],
        out_specs=pl.BlockSpec((ts, D_GDN), lambda b, s: (row(b, s), 0)),
        out_shape=jax.ShapeDtypeStruct((bsz * seq, D_GDN), BF16),
        scratch_shapes=[_conv_scratch(ts, D_GDN)] * 3
        + [pltpu.VMEM((ts, D_GDN), F32)] * 3
        + [pltpu.VMEM((ts, LANES), F32)] * 2
        + [pltpu.VMEM((LANES, ts), F32)]
        + [pltpu.VMEM((GDN_HEADS, GDN_HEAD_DIM, GDN_HEAD_DIM), F32)],
        compiler_params=_cparams("parallel", "arbitrary"),
        name="gdn",
    )(proj, proj, proj, proj, proj, cw, alog, dtb, og)


def _out_router_kernel(x_ref, yl_ref, yg_ref, wo_ref, g_ref, wr_ref, br_ref, h_ref, xn_ref, rt_ref, cnt_ref,
                       *, sub):
    n_sub = x_ref.shape[0] // sub
    counts = []

    def project(r):
        rows = slice(r * sub, (r + 1) * sub)
        h = x_ref[rows, :] + jnp.dot(yl_ref[rows, :], wo_ref[:D_LRU, :], preferred_element_type=F32) \
            + jnp.dot(yg_ref[rows, :], wo_ref[D_LRU:, :], preferred_element_type=F32)
        h_ref[rows, :] = h
        return h

    def route(r, h):
        xn = _rms(h, g_ref[...])
        _store_token_rows(xn_ref, xn, r * sub)
        xh = xn.astype(BF16)
        xl = (xn - xh.astype(F32)).astype(BF16)
        logits = jnp.dot(jnp.concatenate([xh, xl, xh], axis=1), wr_ref[...],
                         preferred_element_type=F32) + br_ref[...]
        lane = lax.broadcasted_iota(jnp.int32, logits.shape, 1).astype(F32)
        big = jnp.float32(2 * LANES)
        ninf = jnp.float32(-jnp.inf)

        def top1(vals):
            m = jnp.max(vals, axis=-1, keepdims=True)
            return m, jnp.min(jnp.where(vals == m, lane, big), axis=-1, keepdims=True)

        gl = jnp.where(lane < N_GROUPS, logits, ninf)
        gmax, gsel = top1(gl)
        p_group = 1.0 / jnp.sum(jnp.exp(gl - gmax), axis=-1, keepdims=True)
        lo = N_GROUPS + EXPERTS_PER_GROUP * gsel
        el = jnp.where((lane >= lo) & (lane < lo + EXPERTS_PER_GROUP), logits, ninf)
        m1, i1 = top1(el)
        m2, i2 = top1(jnp.where(lane == i1, ninf, el))
        rr = jnp.exp(m2 - m1)
        g1 = p_group / (1.0 + rr)
        g2 = p_group * rr / (1.0 + rr)
        rt_ref[r * sub:(r + 1) * sub, :] = jnp.where(
            lane == 0, i1 - N_GROUPS,
            jnp.where(lane == 1, i2 - N_GROUPS, jnp.where(lane == 2, g1, jnp.where(lane == 3, g2, 0.0))))
        member = ((lane == i1 - N_GROUPS) | (lane == i2 - N_GROUPS)).astype(F32)
        counts.append(jnp.sum(member, axis=0, keepdims=True))

    h = project(0)
    for r in range(1, n_sub):
        h_next = project(r)
        route(r - 1, h)
        h = h_next
    route(n_sub - 1, h)
    cnt_ref[...] = jnp.broadcast_to(sum(counts), cnt_ref.shape)


def _out_router(x2, yl, yg, wo, g, wr, br, tm):
    t = x2.shape[0]
    tile = lambda n: pl.BlockSpec((tm, n), lambda i: (i, 0))
    full = lambda a, b: pl.BlockSpec((a, b), lambda i: (0, 0))
    return pl.pallas_call(
        functools.partial(_out_router_kernel, sub=128),
        grid=(t // tm,),
        in_specs=[tile(D_MODEL), tile(D_LRU), tile(D_GDN), full(D_MODEL, D_MODEL), full(1, D_MODEL),
                  full(3 * D_MODEL, LANES), full(1, LANES)],
        out_specs=[tile(D_MODEL), pl.BlockSpec((tm * ROW_TILES, LANES), lambda i: (i, 0)), tile(LANES),
                   pl.BlockSpec((SUBLANES, LANES), lambda i: (i, 0))],
        out_shape=[jax.ShapeDtypeStruct((t, D_MODEL), F32),
                   jax.ShapeDtypeStruct((t * ROW_TILES, LANES), jnp.uint32),
                   jax.ShapeDtypeStruct((t, LANES), F32),
                   jax.ShapeDtypeStruct((t // tm * SUBLANES, LANES), F32)],
        compiler_params=_cparams("parallel"),
        name="out_router",
    )(x2, yl, yg, wo, g, wr, br)


def _slots_kernel(rt_ref, tc_ref, dest_ref, blocks_ref, cnt_ref, before_ref):
    i = pl.program_id(0)
    tm = rt_ref.shape[0]
    rt = rt_ref[...]
    lane = lax.broadcasted_iota(jnp.int32, rt.shape, 1)
    e0 = rt[:, 0:1].astype(jnp.int32)
    e1 = rt[:, 1:2].astype(jnp.int32)
    member = ((lane == e0) | (lane == e1)).astype(BF16)

    @pl.when(i == 0)
    def _():
        ri = lax.broadcasted_iota(jnp.int32, (tm, tm), 0)
        ci = lax.broadcasted_iota(jnp.int32, (tm, tm), 1)
        before_ref[...] = (ri > ci).astype(BF16)
        cnt = jnp.broadcast_to(jnp.sum(tc_ref[...], axis=0, keepdims=True) / SUBLANES, cnt_ref.shape)
        padded = jnp.ceil(cnt / MOE_BLOCK) * MOE_BLOCK
        l8 = lax.broadcasted_iota(jnp.int32, cnt.shape, 1)
        incl = padded
        d = 1
        while d < LANES:
            incl = incl + jnp.where(l8 >= d, pltpu.roll(incl, d, axis=1), 0.0)
            d *= 2
        cnt_ref[...] = incl - padded
        seg_start = (incl - padded)[0:1, :]
        seg_end = incl[0:1, :]
        real_end = seg_start + cnt[0:1, :]
        bl = lax.broadcasted_iota(jnp.int32, blocks_ref.shape, 1)
        b0 = (lax.broadcasted_iota(jnp.int32, blocks_ref.shape, 0) * MOE_BLOCK).astype(F32)
        is_expert = bl < N_EXPERTS
        expert = jnp.sum(jnp.where(is_expert & (seg_end <= b0), 1.0, 0.0), axis=-1, keepdims=True)
        real = jnp.maximum(jnp.minimum(real_end, b0 + MOE_BLOCK) - jnp.maximum(seg_start, b0), 0.0)
        n_real = jnp.sum(jnp.where(is_expert, real, 0.0), axis=-1, keepdims=True)
        blocks_ref[...] = jnp.where(bl == 0, jnp.minimum(expert, N_EXPERTS - 1.0),
                                    jnp.where(bl == 1, n_real, 0.0)).astype(jnp.int32)

    start = cnt_ref[...]
    pos = start[0:1, :] + jnp.dot(before_ref[...], member, preferred_element_type=F32)
    d0 = jnp.sum(jnp.where(lane == e0, pos, 0.0), axis=-1, keepdims=True)
    d1 = jnp.sum(jnp.where(lane == e1, pos, 0.0), axis=-1, keepdims=True)
    dest = jnp.where(lane == 0, d0, jnp.where(lane == 1, d1, 0.0))
    dest_ref[...] = dest.T[:SUBLANES].astype(jnp.int32)
    cnt_ref[...] = start + jnp.dot(jnp.ones((SUBLANES, tm), BF16), member, preferred_element_type=F32)


def _slots(rt, tile_counts, n_blocks, tm):
    t = rt.shape[0]
    return pl.pallas_call(
        _slots_kernel,
        grid=(t // tm,),
        in_specs=[pl.BlockSpec((tm, LANES), lambda i: (i, 0)),
                  pl.BlockSpec(tile_counts.shape, lambda i: (0, 0))],
        out_specs=[pl.BlockSpec((SUBLANES, tm), lambda i: (0, i)),
                   pl.BlockSpec((n_blocks, LANES), lambda i: (0, 0))],
        out_shape=[jax.ShapeDtypeStruct((SUBLANES, t), jnp.int32),
                   jax.ShapeDtypeStruct((n_blocks, LANES), jnp.int32)],
        scratch_shapes=[pltpu.VMEM((SUBLANES, LANES), F32), pltpu.VMEM((tm, tm), BF16)],
        compiler_params=_cparams("arbitrary"),
        name="moe_slots",
    )(rt, tile_counts)


def _sc_mesh():
    return plsc.VectorSubcoreMesh(core_axis_name="c", subcore_axis_name="s")


def _sc_worker():
    return lax.axis_index("s") * SC_CORES + lax.axis_index("c")


def _sc_scatter_rows(src, idx, n_rows):
    t = src.shape[0]
    per_w = t // SC_WORKERS
    n_win = per_w // SC_WINDOW
    idx = idx.reshape(TOP_K, SC_WORKERS, n_win, SC_WINDOW)

    @functools.partial(
        pl.kernel, mesh=_sc_mesh(),
        out_type=jax.ShapeDtypeStruct((n_rows,) + src.shape[1:], src.dtype),
        scratch_types=[pltpu.VMEM((TOP_K, n_win, SC_WINDOW), jnp.int32),
                       pltpu.VMEM((2, SC_WINDOW) + src.shape[1:], src.dtype),
                       pltpu.SemaphoreType.DMA((2,)), pltpu.SemaphoreType.DMA((2,))],
        compiler_params=pltpu.CompilerParams(use_tc_tiling_on_sc=True),
        name="sc_dispatch")
    def scatter(src_hbm, idx_hbm, out_hbm, idx_v, rows_v, lsem, ssem):
        wid = _sc_worker()
        base = wid * per_w
        for k in range(TOP_K):
            pltpu.sync_copy(idx_hbm.at[k, wid], idx_v.at[k])

        def load(w, slot):
            return pltpu.make_async_copy(src_hbm.at[pl.ds(base + w * SC_WINDOW, SC_WINDOW)], rows_v.at[slot],
                                         lsem.at[slot])

        def put(w, slot, k):
            return pltpu.make_async_copy(rows_v.at[slot], out_hbm.at[idx_v.at[k, w]], ssem.at[slot])

        load(0, 0).start()

        @pl.loop(0, n_win, step=2)
        def _(w0):
            for s in range(2):
                w = w0 + s

                @pl.when(w + 1 < n_win)
                def _():
                    @pl.when(w >= 1)
                    def _():
                        for k in range(TOP_K):
                            put(w - 1, 1 - s, k).wait()

                    load(w + 1, 1 - s).start()

                load(w, s).wait()
                for k in range(TOP_K):
                    put(w, s, k).start()

        for s in range(2):
            for k in range(TOP_K):
                put(n_win - 2 + s, s, k).wait()

    return scatter(src, idx)


def _sc_gather_rows(table, idx):
    b = idx.shape[0]
    per_w = b // SC_WORKERS
    n_win = per_w // SC_WINDOW
    idx = idx.reshape(SC_WORKERS, n_win, SC_WINDOW)

    @functools.partial(
        pl.kernel, mesh=_sc_mesh(),
        out_type=jax.ShapeDtypeStruct((b,) + table.shape[1:], table.dtype),
        scratch_types=[pltpu.VMEM((n_win, SC_WINDOW), jnp.int32),
                       pltpu.VMEM((2, SC_WINDOW) + table.shape[1:], table.dtype),
                       pltpu.SemaphoreType.DMA((2,)), pltpu.SemaphoreType.DMA((2,))],
        compiler_params=pltpu.CompilerParams(use_tc_tiling_on_sc=True),
        name="sc_combine_gather")
    def gather(table_hbm, idx_hbm, out_hbm, idx_v, rows_v, gsem, psem):
        wid = _sc_worker()
        base = wid * per_w
        pltpu.sync_copy(idx_hbm.at[wid], idx_v)

        def get(w, slot):
            return pltpu.make_async_copy(table_hbm.at[idx_v.at[w]], rows_v.at[slot], gsem.at[slot])

        def put(w, slot):
            return pltpu.make_async_copy(rows_v.at[slot], out_hbm.at[pl.ds(base + w * SC_WINDOW, SC_WINDOW)],
                                         psem.at[slot])

        get(0, 0).start()

        @pl.loop(0, n_win, step=2)
        def _(w0):
            for s in range(2):
                w = w0 + s

                @pl.when(w + 1 < n_win)
                def _():
                    @pl.when(w >= 1)
                    def _():
                        put(w - 1, 1 - s).wait()

                    get(w + 1, 1 - s).start()

                get(w, s).wait()
                put(w, s).start()

        for s in range(2):
            put(n_win - 2 + s, s).wait()

    return gather(table, idx)


def _experts_kernel(be_ref, nv_ref, xs_ref, wg_ref, wu_ref, wd_ref, ys_ref, wgb_ref, wub_ref, wdb_ref):
    b = pl.program_id(0)
    prev = be_ref[jnp.maximum(b - 1, 0)]

    @pl.when((b == 0) | (be_ref[b] != prev))
    def _():
        wgb_ref[...] = wg_ref[0].astype(BF16)
        wub_ref[...] = wu_ref[0].astype(BF16)
        wdb_ref[...] = wd_ref[0].astype(BF16)

    @pl.when(nv_ref[b] > 0)
    def _():
        row = lax.broadcasted_iota(jnp.int32, (MOE_BLOCK, D_MODEL), 0)
        xb = jnp.where(row < nv_ref[b], _load_token_rows(xs_ref, MOE_BLOCK, BF16), 0.0)
        hg = jnp.dot(xb, wgb_ref[...], preferred_element_type=F32)
        hu = jnp.dot(xb, wub_ref[...], preferred_element_type=F32)
        hb = (_silu(hg) * hu).astype(BF16)
        _store_token_rows(ys_ref, jnp.dot(hb, wdb_ref[...], preferred_element_type=F32))

    @pl.when(nv_ref[b] == 0)
    def _():
        ys_ref[...] = jnp.zeros_like(ys_ref)


def _experts(block_expert, n_valid, xs, wg, wu, wd):
    nb = xs.shape[0] // (MOE_BLOCK * ROW_TILES)
    blk = pl.BlockSpec((MOE_BLOCK * ROW_TILES, LANES), lambda b, be, nv: (b, 0))
    wspec = lambda a, c: pl.BlockSpec((1, a, c), lambda b, be, nv: (be[b], 0, 0))
    return pl.pallas_call(
        _experts_kernel,
        grid_spec=pltpu.PrefetchScalarGridSpec(
            num_scalar_prefetch=2,
            grid=(nb,),
            in_specs=[blk, wspec(D_MODEL, D_EXPERT), wspec(D_MODEL, D_EXPERT), wspec(D_EXPERT, D_MODEL)],
            out_specs=blk,
            scratch_shapes=[pltpu.VMEM((D_MODEL, D_EXPERT), BF16), pltpu.VMEM((D_MODEL, D_EXPERT), BF16),
                            pltpu.VMEM((D_EXPERT, D_MODEL), BF16)]),
        out_shape=jax.ShapeDtypeStruct(xs.shape, xs.dtype),
        compiler_params=_cparams("arbitrary"),
        name="moe_experts",
    )(block_expert, n_valid, xs, wg, wu, wd)


def _combine_kernel(h_ref, rt_ref, p_ref, y0_ref, y1_ref, gp_ref, wpg_ref, wp_ref, gf_ref, o_ref, *, sub):
    n_sub = h_ref.shape[0] // sub

    def residual(r):
        rows = slice(r * sub, (r + 1) * sub)
        rt = rt_ref[rows, :]
        h = h_ref[rows, :] + (_load_token_rows(y0_ref, sub, F32, r * sub) * rt[:, 2:3]
                              + _load_token_rows(y1_ref, sub, F32, r * sub) * rt[:, 3:4])
        return h, _rms(h, gp_ref[...]).astype(BF16)

    def products(r, xn):
        return (jnp.dot(xn, wpg_ref[...], preferred_element_type=F32),
                _dot(p_ref[r * sub:(r + 1) * sub, :], wp_ref[...]))

    def finish(r, h, gate_lin, ple):
        h = h + ple * _sigmoid(gate_lin)
        o_ref[r * sub:(r + 1) * sub, :] = _rms(h, gf_ref[...])

    h, xn = residual(0)
    for r in range(n_sub):
        gate_lin, ple = products(r, xn)
        if r + 1 < n_sub:
            h_next, xn = residual(r + 1)
        finish(r, h, gate_lin, ple)
        if r + 1 < n_sub:
            h = h_next


def _combine(h1, rt, p2, y, gp, wpg, wp, gf, tm):
    t = h1.shape[0]
    nt = t // tm
    tile = lambda n: pl.BlockSpec((tm, n), lambda i: (i, 0))
    full = lambda a, b: pl.BlockSpec((a, b), lambda i: (0, 0))
    ytile = lambda k: pl.BlockSpec((tm * ROW_TILES, LANES), lambda i: (i + k * nt, 0))
    return pl.pallas_call(
        functools.partial(_combine_kernel, sub=128),
        grid=(nt,),
        in_specs=[tile(D_MODEL), tile(LANES), tile(D_PLE), ytile(0), ytile(1),
                  full(1, D_MODEL), full(D_MODEL, D_MODEL), full(D_PLE, D_MODEL), full(1, D_MODEL)],
        out_specs=tile(D_MODEL),
        out_shape=jax.ShapeDtypeStruct((t, D_MODEL), F32),
        compiler_params=_cparams("parallel"),
        name="moe_combine_ple",
    )(h1, rt, p2, y, y, gp, wpg, wp, gf)


def _lane_row(vals, offset):
    return jnp.zeros((1, LANES), F32).at[0, offset:offset + vals.shape[0]].set(vals)


def kernel(x, p, norm_mix, w_in, lru_conv_w, lru_conv_b, lru_wa, lru_ba, lru_wi, lru_bi, lru_lambda,
           lru_out_norm, gdn_conv_w, gdn_a_log, gdn_dt_bias, gdn_out_norm, w_out, norm_ffn,
           w_router_group, b_router_group, w_router_expert, b_router_expert, w_exp_gate, w_exp_up,
           w_exp_down, norm_ple, w_ple_gate, w_ple, norm_final):
    bsz, seq, d = x.shape
    t = bsz * seq
    depth = w_in.shape[0]
    assert depth == 1, "the final norm is fused into the last layer's combine kernel"
    n_blocks = -(-t * TOP_K // MOE_BLOCK) + N_EXPERTS
    row = lambda v: v.reshape(1, -1).astype(F32)
    h = x.reshape(t, d).astype(F32)
    for l in range(depth):
        w_in_p = jnp.pad(w_in[l], ((0, 0), (0, PROJ_COLS - w_in.shape[2]))).astype(BF16)
        b_gates = jnp.concatenate([lru_ba[l], lru_bi[l]]).reshape(1, -1)
        proj_gdn, y_lru = _proj_lru(h, row(norm_mix[l]), w_in_p, lru_conv_w[l], row(lru_conv_b[l]), lru_wa[l],
                                    lru_wi[l], b_gates, row(lru_lambda[l]), row(lru_out_norm[l]), seq, 512)
        cw = gdn_conv_w[l].reshape(CONV_W, 3, D_GDN).transpose(1, 0, 2)
        y_gdn = _gdn(proj_gdn, cw, _lane_row(gdn_a_log[l], GDN_HEADS), _lane_row(gdn_dt_bias[l], GDN_HEADS),
                     row(gdn_out_norm[l]), bsz, seq, 512)
        w_r = jnp.pad(jnp.concatenate([w_router_group[l], w_router_expert[l]], axis=1),
                      ((0, 0), (0, LANES - N_GROUPS - N_EXPERTS)))
        w_r_hi = w_r.astype(BF16)
        w_r_lo = (w_r - w_r_hi.astype(F32)).astype(BF16)
        w_r = jnp.concatenate([w_r_hi, w_r_hi, w_r_lo], axis=0)
        b_r = _lane_row(jnp.concatenate([b_router_group[l], b_router_expert[l]]), 0)
        h1, xn2, rt, tile_counts = _out_router(h, y_lru, y_gdn, w_out[l].astype(BF16), row(norm_ffn[l]), w_r, b_r, 512)
        dest8, blocks = _slots(rt, tile_counts, n_blocks, 1024)
        dest = dest8[:TOP_K]
        block_expert, n_valid = blocks[:, 0], blocks[:, 1]
        tiles = lambda a: a.reshape(-1, ROW_TILES, LANES)
        xs = _sc_scatter_rows(tiles(xn2), dest, n_blocks * MOE_BLOCK)
        ys = _experts(block_expert, n_valid, xs.reshape(-1, LANES), w_exp_gate[l], w_exp_up[l], w_exp_down[l])
        y = _sc_gather_rows(tiles(ys), dest.reshape(-1))
        h = _combine(h1, rt, p[l].reshape(t, -1).astype(F32), y.reshape(-1, LANES), row(norm_ple[l]),
                     w_ple_gate[l].astype(BF16), w_ple[l].astype(BF16), row(norm_final), 512)
    return h.reshape(bsz, seq, d).astype(x.dtype)
```

```python
import functools

import jax
import jax.numpy as jnp
import numpy as np
from jax import lax
from jax.experimental import pallas as pl
from jax.experimental.pallas import tpu as pltpu
from jax.experimental.pallas import tpu_sc as plsc

D_MODEL = 1024
D_LRU = 512
LRU_BLOCKS = 8
LRU_BLOCK_W = D_LRU // LRU_BLOCKS
LRU_C = 8.0
D_GDN = 512
GDN_HEADS = 4
GDN_HEAD_DIM = D_GDN // GDN_HEADS
CONV_W = 4
CHUNK = 64
N_GROUPS = 4
EXPERTS_PER_GROUP = 8
N_EXPERTS = N_GROUPS * EXPERTS_PER_GROUP
TOP_K = 2
D_EXPERT = 512
MOE_BLOCK = 512
D_PLE = 256
EPS = 1e-6

LANES = 128
SUBLANES = 8
ROW_TILES = D_MODEL // (2 * LANES)
SC_CORES = 2
SC_SUBCORES = 16
SC_WORKERS = SC_CORES * SC_SUBCORES
SC_WINDOW = 64
PROJ_COLS = 2 * D_LRU + 4 * D_GDN + LANES
VMEM_LIMIT = 56 * 1024 * 1024

BF16 = jnp.bfloat16
F32 = jnp.float32


def _cparams(*sem):
    return pltpu.CompilerParams(dimension_semantics=sem, vmem_limit_bytes=VMEM_LIMIT)


def _rms(x, g):
    return x * lax.rsqrt(jnp.mean(x * x, axis=-1, keepdims=True) + EPS) * g


def _sigmoid(x):
    return 0.5 * jnp.tanh(0.5 * x) + 0.5


def _silu(x):
    return x * _sigmoid(x)


def _dot(a, b):
    return jnp.dot(a.astype(BF16), b.astype(BF16), preferred_element_type=F32)


def _dot_nt(a, b):
    return lax.dot_general(a.astype(BF16), b.astype(BF16), (((1,), (1,)), ((), ())),
                           preferred_element_type=F32)


def _dot_tn(a, b):
    return lax.dot_general(a.astype(BF16), b.astype(BF16), (((0,), (0,)), ((), ())),
                           preferred_element_type=F32)


def _conv_scratch(rows, channels):
    return pltpu.VMEM((channels // LANES, 2 * (SUBLANES + rows), LANES), F32)


def _conv_reset(xp_ref):
    for k in range(xp_ref.shape[0]):
        xp_ref.at[k][pl.ds(0, SUBLANES, stride=2), :] = jnp.zeros((SUBLANES, LANES), F32)


def _causal_conv(xp_ref, x_ref, w, r0, rows, col0=0):
    out = []
    for k in range(xp_ref.shape[0]):
        lanes = slice(k * LANES, (k + 1) * LANES)
        slab = xp_ref.at[k]
        xk = x_ref[r0:r0 + rows, col0 + k * LANES:col0 + (k + 1) * LANES]
        slab[pl.ds(2 * (SUBLANES + r0), rows, stride=2), :] = xk
        acc = xk * w[CONV_W - 1:CONV_W, lanes]
        for j in range(1, CONV_W):
            acc = acc + slab[pl.ds(2 * (SUBLANES + r0 - j), rows, stride=2), :] * w[CONV_W - 1 - j:CONV_W - j, lanes]
        out.append(acc)
    return jnp.concatenate(out, axis=1)


def _conv_carry(xp_ref, x_ref, col0=0):
    ts = x_ref.shape[0]
    for k in range(xp_ref.shape[0]):
        xp_ref.at[k][pl.ds(0, SUBLANES, stride=2), :] = x_ref[ts - SUBLANES:, col0 + k * LANES:col0 + (k + 1) * LANES]


def _interleave(*stages):
    live = list(stages)
    while live:
        for g in list(live):
            try:
                next(g)
            except StopIteration:
                live.remove(g)


def _store_token_rows(ref, x, row0=0):
    rows = x.shape[0]
    bits = lax.bitcast_convert_type(x.astype(BF16).astype(F32), jnp.uint32)
    words = (bits[:, :D_MODEL // 2] >> 16) | bits[:, D_MODEL // 2:]
    for j in range(ROW_TILES):
        ref[pl.ds(row0 * ROW_TILES + j, rows, stride=ROW_TILES), :] = words[:, j * LANES:(j + 1) * LANES]


def _load_token_rows(ref, rows, dtype, row0=0):
    words = [ref[pl.ds(row0 * ROW_TILES + j, rows, stride=ROW_TILES), :] for j in range(ROW_TILES)]
    lo = [lax.bitcast_convert_type(w << 16, F32).astype(dtype) for w in words]
    hi = [lax.bitcast_convert_type(w & jnp.uint32(0xFFFF0000), F32).astype(dtype) for w in words]
    return jnp.concatenate(lo + hi, axis=1)


def _proj_lru_kernel(x_ref, gin_ref, win_ref, cw_ref, cb_ref, wa_ref, wi_ref, bg_ref, lam_ref, og_ref,
                     gdn_ref, o_ref, pj_ref, tail_ref, h_ref, wg_ref, *, tiles_per_seq, col_tile):
    g = pl.program_id(0)
    ts = x_ref.shape[0]
    pj_new = pj_ref.at[g % 2]
    pj = pj_ref.at[(g + 1) % 2]

    @pl.when(g == 0)
    def _():
        pj_ref[...] = jnp.zeros_like(pj_ref)
        wg_ref[...] = jnp.zeros_like(wg_ref)
        for n in range(LRU_BLOCKS):
            blk = slice(n * LRU_BLOCK_W, (n + 1) * LRU_BLOCK_W)
            wg_ref[blk, blk] = wa_ref[n].astype(BF16)
            wg_ref[blk, D_LRU + n * LRU_BLOCK_W:D_LRU + (n + 1) * LRU_BLOCK_W] = wi_ref[n].astype(BF16)

    @pl.when((g == 0) | (lax.rem(g - 1, tiles_per_seq) == 0))
    def _():
        _conv_reset(tail_ref)
        h_ref[...] = jnp.zeros_like(h_ref)

    done = []

    def after_projection(v):
        return v + jnp.concatenate([done[max(len(done) - 2, 0)]] * (v.shape[-1] // LANES), axis=-1)

    def project():
        xn = _rms(x_ref[...], gin_ref[...]).astype(BF16)
        yield
        for c0 in range(0, PROJ_COLS, col_tile):
            c1 = min(c0 + col_tile, PROJ_COLS)
            y = jnp.dot(xn, win_ref[:, c0:c1], preferred_element_type=F32)
            if c1 <= 2 * D_LRU:
                pj_new[:, c0:c1] = y
            else:
                gdn_ref[:, c0 - 2 * D_LRU:c1 - 2 * D_LRU] = y
            bits = lax.bitcast_convert_type(y[ts - 1:, c1 - c0 - LANES:], jnp.uint32)
            done.append(lax.bitcast_convert_type((bits >> 16) >> 16, F32))
            yield

    def recur():
        xc = _causal_conv(tail_ref, pj, cw_ref[...], 0, ts) + cb_ref[...]
        _conv_carry(tail_ref, pj)
        yield
        gates = _sigmoid(_dot(xc, wg_ref[...]) + after_projection(bg_ref[...]))
        r = gates[:, :D_LRU]
        i = gates[:, D_LRU:]
        yield
        log_a = LRU_C * r * jax.nn.log_sigmoid(after_projection(lam_ref[...]))
        a = jnp.exp(log_a)
        th = jnp.tanh(log_a)
        u = jnp.sqrt(-2.0 * th) * lax.rsqrt(1.0 - th) * (i * xc)
        yield
        a = a.reshape(ts // SUBLANES, SUBLANES, D_LRU)
        u = u.reshape(ts // SUBLANES, SUBLANES, D_LRU)
        row = lax.broadcasted_iota(jnp.int32, a.shape, 1)
        d = 1
        while d < SUBLANES:
            keep = row >= d
            a_prev = jnp.where(keep, pltpu.roll(a, d, axis=1), 1.0)
            u_prev = jnp.where(keep, pltpu.roll(u, d, axis=1), 0.0)
            u = a * u_prev + u
            a = a * a_prev
            d *= 2
            yield
        carry = after_projection(h_ref[...])
        groups = []
        for n in range(ts // SUBLANES):
            groups.append(a[n] * carry + u[n])
            carry = groups[-1][SUBLANES - 1:]
        h = jnp.concatenate(groups, axis=0)
        h_ref[...] = carry
        yield
        y = h * jax.nn.gelu(pj[:, D_LRU:])
        o_ref[...] = _rms(y, after_projection(og_ref[...])).astype(o_ref.dtype)

    _interleave(project(), recur())


def _proj_lru(x2, gin, win, cw, cb, wa, wi, bg, lam, og, seq, ts):
    t = x2.shape[0]
    n_tiles = t // ts
    this = lambda g: (jnp.minimum(g, n_tiles - 1), 0)
    last = lambda g: (jnp.maximum(g - 1, 0), 0)
    const = lambda *shape: pl.BlockSpec(shape, lambda g: (0,) * len(shape))
    gdn_cols = PROJ_COLS - 2 * D_LRU
    return pl.pallas_call(
        functools.partial(_proj_lru_kernel, tiles_per_seq=seq // ts, col_tile=D_LRU),
        grid=(n_tiles + 1,),
        in_specs=[pl.BlockSpec((ts, D_MODEL), this), const(1, D_MODEL), const(D_MODEL, PROJ_COLS),
                  const(CONV_W, D_LRU), const(1, D_LRU),
                  const(LRU_BLOCKS, LRU_BLOCK_W, LRU_BLOCK_W), const(LRU_BLOCKS, LRU_BLOCK_W, LRU_BLOCK_W),
                  const(1, 2 * D_LRU), const(1, D_LRU), const(1, D_LRU)],
        out_specs=[pl.BlockSpec((ts, gdn_cols), this), pl.BlockSpec((ts, D_LRU), last)],
        out_shape=[jax.ShapeDtypeStruct((t, gdn_cols), F32), jax.ShapeDtypeStruct((t, D_LRU), BF16)],
        scratch_shapes=[pltpu.VMEM((2, ts, 2 * D_LRU), F32), _conv_scratch(ts, D_LRU), pltpu.VMEM((1, D_LRU), F32),
                        pltpu.VMEM((D_LRU, 2 * D_LRU), BF16)],
        compiler_params=_cparams("arbitrary"),
        name="in_proj_rglru",
    )(x2, gin, win, cw, cb, wa, wi, bg, lam, og)


def _gdn_router_kernel(q_ref, k_ref, v_ref, z_ref, ba_ref, cw_ref, alog_ref, dtb_ref, og_ref,
                       x_ref, yl_ref, wo_ref, gf_ref, wr_ref, br_ref, h_ref, xn_ref, rt_ref, cnt_ref,
                       qt_ref, kt_ref, vt_ref, qs_ref, ks_ref, vs_ref, bs_ref, gc_ref, gct_ref, st_ref, yg_ref,
                       *, tiles_per_seq, group_chunks, prep_rows, sub):
    g = pl.program_id(0)
    ts = q_ref.shape[0]
    dk = GDN_HEAD_DIM
    nc = ts // CHUNK
    n_sub = ts // sub
    yg_new = yg_ref.at[g % 2]
    yg = yg_ref.at[(g + 1) % 2]

    @pl.when(g == 0)
    def _():
        yg_ref[...] = jnp.zeros_like(yg_ref)

    @pl.when(lax.rem(g, tiles_per_seq) == 0)
    def _():
        for tail_ref in (qt_ref, kt_ref, vt_ref):
            _conv_reset(tail_ref)
        st_ref[...] = jnp.zeros_like(st_ref)

    done = []
    hs = {}
    counts = []

    def after_projection(v):
        if not done:
            return v
        return v + jnp.concatenate([done[-1]] * (v.shape[-1] // LANES), axis=-1)

    def project_out():
        for r in range(n_sub):
            rows = slice(r * sub, (r + 1) * sub)
            h = x_ref[rows, :] + jnp.dot(yl_ref[rows, :], wo_ref[:D_LRU, :], preferred_element_type=F32) \
                + jnp.dot(yg[rows, :], wo_ref[D_LRU:, :], preferred_element_type=F32)
            h_ref[rows, :] = h
            hs[r] = h
            bits = lax.bitcast_convert_type(h[sub - 1:, D_MODEL - LANES:], jnp.uint32)
            done.append(lax.bitcast_convert_type((bits >> 16) >> 16, F32))
            yield

    def route(r, h):
        xn = _rms(h, gf_ref[...])
        _store_token_rows(xn_ref, xn, r * sub)
        xh = xn.astype(BF16)
        xl = (xn - xh.astype(F32)).astype(BF16)
        logits = jnp.dot(jnp.concatenate([xh, xl, xh], axis=1), wr_ref[...],
                         preferred_element_type=F32) + br_ref[...]
        lane = lax.broadcasted_iota(jnp.int32, logits.shape, 1).astype(F32)
        big = jnp.float32(2 * LANES)
        ninf = jnp.float32(-jnp.inf)

        def top1(vals):
            m = jnp.max(vals, axis=-1, keepdims=True)
            return m, jnp.min(jnp.where(vals == m, lane, big), axis=-1, keepdims=True)

        gl = jnp.where(lane < N_GROUPS, logits, ninf)
        gmax, gsel = top1(gl)
        p_group = 1.0 / jnp.sum(jnp.exp(gl - gmax), axis=-1, keepdims=True)
        lo = N_GROUPS + EXPERTS_PER_GROUP * gsel
        el = jnp.where((lane >= lo) & (lane < lo + EXPERTS_PER_GROUP), logits, ninf)
        m1, i1 = top1(el)
        m2, i2 = top1(jnp.where(lane == i1, ninf, el))
        rr = jnp.exp(m2 - m1)
        g1 = p_group / (1.0 + rr)
        g2 = p_group * rr / (1.0 + rr)
        rt_ref[r * sub:(r + 1) * sub, :] = jnp.where(
            lane == 0, i1 - N_GROUPS,
            jnp.where(lane == 1, i2 - N_GROUPS, jnp.where(lane == 2, g1, jnp.where(lane == 3, g2, 0.0))))
        member = ((lane == i1 - N_GROUPS) | (lane == i2 - N_GROUPS)).astype(F32)
        counts.append(jnp.sum(member, axis=0, keepdims=True))

    def route_subtiles(subtiles):
        for r in subtiles:
            route(r, hs[r])
            yield

    ri = lax.broadcasted_iota(jnp.int32, (CHUNK, CHUNK), 0)
    ci = lax.broadcasted_iota(jnp.int32, (CHUNK, CHUNK), 1)
    causal = ri >= ci
    strict = ri > ci
    tril = causal.astype(F32)
    eye = (ri == ci).astype(F32)
    og = og_ref[...]

    def l2n(x, scale):
        parts = []
        for h in range(GDN_HEADS):
            xh = x[:, h * dk:(h + 1) * dk]
            parts.append(xh * (lax.rsqrt(jnp.sum(xh * xh, axis=-1, keepdims=True) + EPS) * scale))
        return jnp.concatenate(parts, axis=1)

    def prepare(r0, r1):
        for p0 in range(r0, r1, prep_rows):
            rows = slice(p0, p0 + prep_rows)
            conv = lambda x_ref, tail_ref, part: _silu(
                _causal_conv(tail_ref, x_ref, after_projection(cw_ref[part]), p0, prep_rows))
            qs_ref[rows, :] = l2n(conv(q_ref, qt_ref, 0), dk ** -0.5)
            yield
            ks_ref[rows, :] = l2n(conv(k_ref, kt_ref, 1), 1.0)
            yield
            vs_ref[rows, :] = conv(v_ref, vt_ref, 2)
            ba = ba_ref[rows, :]
            bs_ref[rows, :] = _sigmoid(ba)
            g = -jnp.exp(alog_ref[...]) * jax.nn.softplus(ba + dtb_ref[...])
            for c0 in range(0, prep_rows, CHUNK):
                gc_ref[p0 + c0:p0 + c0 + CHUNK, :] = jnp.dot(tril, g[c0:c0 + CHUNK], precision=lax.Precision.HIGHEST,
                                                             preferred_element_type=F32)
            gct_ref[:, rows] = gc_ref[rows, :].T
            yield

    terms = {}

    def chunk_terms(pairs):
        n = range(len(pairs))
        rows = [slice(c * CHUNK, (c + 1) * CHUNK) for c, _ in pairs]
        cols = [slice(h * dk, (h + 1) * dk) for _, h in pairs]
        gl = [GDN_HEADS + h for _, h in pairs]
        kh = [ks_ref[rows[i], cols[i]] for i in n]
        kb = [kh[i] * bs_ref[rows[i], pairs[i][1]:pairs[i][1] + 1] for i in n]
        r = [_dot_nt(jnp.concatenate([kb[i], qs_ref[rows[i], cols[i]]], axis=0), kh[i]) for i in n]
        yield
        gcol = [gc_ref[rows[i], gl[i]:gl[i] + 1] for i in n]
        decay = []
        for i in n:
            diff = gcol[i] - gct_ref[gl[i]:gl[i] + 1, rows[i]]
            decay.append(jnp.where(causal, jnp.exp(jnp.where(causal, diff, 0.0)), 0.0))
        a = [jnp.where(strict, r[i][:CHUNK] * decay[i], 0.0) for i in n]
        qk = [(r[i][CHUNK:] * decay[i]).astype(BF16) for i in n]
        tinv = [eye - a[i] for i in n]
        p = 2
        while p < CHUNK:
            a = [_dot(a[i], a[i]) for i in n]
            yield
            tinv = [tinv[i] + _dot(tinv[i], a[i]) for i in n]
            yield
            p *= 2
        eg = [jnp.exp(gcol[i]) for i in n]
        rhs = [jnp.concatenate([vs_ref[rows[i], cols[i]] * bs_ref[rows[i], pairs[i][1]:pairs[i][1] + 1],
                                kb[i] * eg[i]], axis=1) for i in n]
        uw = [_dot(tinv[i], rhs[i]).astype(BF16) for i in n]
        yield
        qk_uw = [jnp.dot(qk[i], uw[i], preferred_element_type=F32) for i in n]
        glast = [gc_ref[(c + 1) * CHUNK - 1:(c + 1) * CHUNK, gl[i]:gl[i] + 1] for i, (c, _) in enumerate(pairs)]
        kd_uw = [_dot_tn(kh[i] * jnp.exp(glast[i] - gcol[i]), uw[i]) for i in n]
        yield
        for i in n:
            lhs = jnp.concatenate([kd_uw[i][:, dk:], qs_ref[rows[i], cols[i]] * eg[i] - qk_uw[i][:, dk:]],
                                  axis=0).astype(BF16)
            terms[pairs[i]] = (lhs, kd_uw[i][:, :dk], qk_uw[i][:, :dk], jnp.exp(glast[i]))

    state = [st_ref[h] for h in range(GDN_HEADS)]

    def advance(chunks):
        for c in chunks:
            rows = slice(c * CHUNK, (c + 1) * CHUNK)
            r = [jnp.dot(terms[c, h][0], state[h].astype(BF16), preferred_element_type=F32)
                 for h in range(GDN_HEADS)]
            for h in range(GDN_HEADS):
                cols = slice(h * dk, (h + 1) * dk)
                _, c_add, o_add, egl = terms[c, h]
                o = r[h][dk:] + o_add
                state[h] = egl * state[h] - r[h][:dk] + c_add
                yg_new[rows, cols] = (_rms(o, og) * _silu(z_ref[rows, cols])).astype(yg_new.dtype)
            yield

    groups = [range(c0, c0 + group_chunks) for c0 in range(0, nc, group_chunks)]
    pairs_of = lambda chunks: [(c, h) for c in chunks for h in range(GDN_HEADS)]
    span = lambda chunks: (chunks[0] * CHUNK, (chunks[-1] + 1) * CHUNK)
    per_group = -(-n_sub // len(groups))
    _interleave(prepare(*span(groups[0])), project_out())
    for i, chunks in enumerate(groups):
        side = [route_subtiles(range(i * per_group, min((i + 1) * per_group, n_sub)))]
        if i + 1 < len(groups):
            side.append(prepare(*span(groups[i + 1])))
        if i > 0:
            side.append(advance(groups[i - 1]))
        _interleave(chunk_terms(pairs_of(chunks)), *side)
    _interleave(advance(groups[-1]))
    for h in range(GDN_HEADS):
        st_ref[h] = state[h]
    for x_in, tail_ref in ((q_ref, qt_ref), (k_ref, kt_ref), (v_ref, vt_ref)):
        _conv_carry(tail_ref, x_in)
    cnt_ref[...] = jnp.broadcast_to(sum(counts), cnt_ref.shape)


def _gdn_router(proj, cw, alog, dtb, og, x2, yl, wo, gf, wr, br, seq, ts):
    t = x2.shape[0]
    n_tiles = t // ts
    this = lambda c: (lambda g: (jnp.minimum(g, n_tiles - 1), c))
    last = lambda g: (jnp.maximum(g - 1, 0), 0)
    const = lambda *shape: pl.BlockSpec(shape, lambda g: (0,) * len(shape))
    col = lambda c: pl.BlockSpec((ts, D_GDN), this(c))
    return pl.pallas_call(
        functools.partial(_gdn_router_kernel, tiles_per_seq=seq // ts, group_chunks=4, prep_rows=128, sub=128),
        grid=(n_tiles + 1,),
        in_specs=[col(0), col(1), col(2), col(3), pl.BlockSpec((ts, LANES), this(4 * D_GDN // LANES)),
                  const(3, CONV_W, D_GDN), const(1, LANES), const(1, LANES), const(1, GDN_HEAD_DIM),
                  pl.BlockSpec((ts, D_MODEL), last), pl.BlockSpec((ts, D_LRU), last), const(D_MODEL, D_MODEL),
                  const(1, D_MODEL), const(3 * D_MODEL, LANES), const(1, LANES)],
        out_specs=[pl.BlockSpec((ts, D_MODEL), last), pl.BlockSpec((ts * ROW_TILES, LANES), last),
                   pl.BlockSpec((ts, LANES), last), pl.BlockSpec((SUBLANES, LANES), last)],
        out_shape=[jax.ShapeDtypeStruct((t, D_MODEL), F32),
                   jax.ShapeDtypeStruct((t * ROW_TILES, LANES), jnp.uint32),
                   jax.ShapeDtypeStruct((t, LANES), F32),
                   jax.ShapeDtypeStruct((n_tiles * SUBLANES, LANES), F32)],
        scratch_shapes=[_conv_scratch(ts, D_GDN)] * 3
        + [pltpu.VMEM((ts, D_GDN), F32)] * 3
        + [pltpu.VMEM((ts, LANES), F32)] * 2
        + [pltpu.VMEM((LANES, ts), F32)]
        + [pltpu.VMEM((GDN_HEADS, GDN_HEAD_DIM, GDN_HEAD_DIM), F32)]
        + [pltpu.VMEM((2, ts, D_GDN), BF16)],
        compiler_params=_cparams("arbitrary"),
        name="gdn_out_router",
    )(proj, proj, proj, proj, proj, cw, alog, dtb, og, x2, yl, wo, gf, wr, br)


def _slots_kernel(rt_ref, tc_ref, dest_ref, blocks_ref, cnt_ref, before_ref):
    i = pl.program_id(0)
    tm = rt_ref.shape[0]
    rt = rt_ref[...]
    lane = lax.broadcasted_iota(jnp.int32, rt.shape, 1)
    e0 = rt[:, 0:1].astype(jnp.int32)
    e1 = rt[:, 1:2].astype(jnp.int32)
    member = ((lane == e0) | (lane == e1)).astype(BF16)

    @pl.when(i == 0)
    def _():
        ri = lax.broadcasted_iota(jnp.int32, (tm, tm), 0)
        ci = lax.broadcasted_iota(jnp.int32, (tm, tm), 1)
        before_ref[...] = (ri > ci).astype(BF16)
        cnt = jnp.broadcast_to(jnp.sum(tc_ref[...], axis=0, keepdims=True) / SUBLANES, cnt_ref.shape)
        padded = jnp.ceil(cnt / MOE_BLOCK) * MOE_BLOCK
        l8 = lax.broadcasted_iota(jnp.int32, cnt.shape, 1)
        incl = padded
        d = 1
        while d < LANES:
            incl = incl + jnp.where(l8 >= d, pltpu.roll(incl, d, axis=1), 0.0)
            d *= 2
        cnt_ref[...] = incl - padded
        seg_start = (incl - padded)[0:1, :]
        seg_end = incl[0:1, :]
        real_end = seg_start + cnt[0:1, :]
        bl = lax.broadcasted_iota(jnp.int32, blocks_ref.shape, 1)
        b0 = (lax.broadcasted_iota(jnp.int32, blocks_ref.shape, 0) * MOE_BLOCK).astype(F32)
        is_expert = bl < N_EXPERTS
        expert = jnp.sum(jnp.where(is_expert & (seg_end <= b0), 1.0, 0.0), axis=-1, keepdims=True)
        real = jnp.maximum(jnp.minimum(real_end, b0 + MOE_BLOCK) - jnp.maximum(seg_start, b0), 0.0)
        n_real = jnp.sum(jnp.where(is_expert, real, 0.0), axis=-1, keepdims=True)
        blocks_ref[...] = jnp.where(bl == 0, jnp.minimum(expert, N_EXPERTS - 1.0),
                                    jnp.where(bl == 1, n_real, 0.0)).astype(jnp.int32)

    start = cnt_ref[...]
    pos = start[0:1, :] + jnp.dot(before_ref[...], member, preferred_element_type=F32)
    d0 = jnp.sum(jnp.where(lane == e0, pos, 0.0), axis=-1, keepdims=True)
    d1 = jnp.sum(jnp.where(lane == e1, pos, 0.0), axis=-1, keepdims=True)
    dest = jnp.where(lane == 0, d0, jnp.where(lane == 1, d1, 0.0))
    dest_ref[...] = dest.T[:SUBLANES].astype(jnp.int32)
    cnt_ref[...] = start + jnp.dot(jnp.ones((SUBLANES, tm), BF16), member, preferred_element_type=F32)


def _slots(rt, tile_counts, n_blocks, tm):
    t = rt.shape[0]
    return pl.pallas_call(
        _slots_kernel,
        grid=(t // tm,),
        in_specs=[pl.BlockSpec((tm, LANES), lambda i: (i, 0)),
                  pl.BlockSpec(tile_counts.shape, lambda i: (0, 0))],
        out_specs=[pl.BlockSpec((SUBLANES, tm), lambda i: (0, i)),
                   pl.BlockSpec((n_blocks, LANES), lambda i: (0, 0))],
        out_shape=[jax.ShapeDtypeStruct((SUBLANES, t), jnp.int32),
                   jax.ShapeDtypeStruct((n_blocks, LANES), jnp.int32)],
        scratch_shapes=[pltpu.VMEM((SUBLANES, LANES), F32), pltpu.VMEM((tm, tm), BF16)],
        compiler_params=_cparams("arbitrary"),
        name="moe_slots",
    )(rt, tile_counts)


def _sc_mesh():
    return plsc.VectorSubcoreMesh(core_axis_name="c", subcore_axis_name="s")


def _sc_worker():
    return lax.axis_index("s") * SC_CORES + lax.axis_index("c")


def _sc_scatter_rows(src, idx, n_rows):
    t = src.shape[0]
    per_w = t // SC_WORKERS
    n_win = per_w // SC_WINDOW
    idx = idx.reshape(TOP_K, SC_WORKERS, n_win, SC_WINDOW)

    @functools.partial(
        pl.kernel, mesh=_sc_mesh(),
        out_type=jax.ShapeDtypeStruct((n_rows,) + src.shape[1:], src.dtype),
        scratch_types=[pltpu.VMEM((TOP_K, n_win, SC_WINDOW), jnp.int32),
                       pltpu.VMEM((2, SC_WINDOW) + src.shape[1:], src.dtype),
                       pltpu.SemaphoreType.DMA((2,)), pltpu.SemaphoreType.DMA((2,))],
        compiler_params=pltpu.CompilerParams(use_tc_tiling_on_sc=True),
        name="sc_dispatch")
    def scatter(src_hbm, idx_hbm, out_hbm, idx_v, rows_v, lsem, ssem):
        wid = _sc_worker()
        base = wid * per_w
        for k in range(TOP_K):
            pltpu.sync_copy(idx_hbm.at[k, wid], idx_v.at[k])

        def load(w, slot):
            return pltpu.make_async_copy(src_hbm.at[pl.ds(base + w * SC_WINDOW, SC_WINDOW)], rows_v.at[slot],
                                         lsem.at[slot])

        def put(w, slot, k):
            return pltpu.make_async_copy(rows_v.at[slot], out_hbm.at[idx_v.at[k, w]], ssem.at[slot])

        load(0, 0).start()

        @pl.loop(0, n_win, step=2)
        def _(w0):
            for s in range(2):
                w = w0 + s

                @pl.when(w + 1 < n_win)
                def _():
                    @pl.when(w >= 1)
                    def _():
                        for k in range(TOP_K):
                            put(w - 1, 1 - s, k).wait()

                    load(w + 1, 1 - s).start()

                load(w, s).wait()
                for k in range(TOP_K):
                    put(w, s, k).start()

        for s in range(2):
            for k in range(TOP_K):
                put(n_win - 2 + s, s, k).wait()

    return scatter(src, idx)


def _sc_gather_rows(table, idx):
    b = idx.shape[0]
    per_w = b // SC_WORKERS
    n_win = per_w // SC_WINDOW
    idx = idx.reshape(SC_WORKERS, n_win, SC_WINDOW)

    @functools.partial(
        pl.kernel, mesh=_sc_mesh(),
        out_type=jax.ShapeDtypeStruct((b,) + table.shape[1:], table.dtype),
        scratch_types=[pltpu.VMEM((n_win, SC_WINDOW), jnp.int32),
                       pltpu.VMEM((2, SC_WINDOW) + table.shape[1:], table.dtype),
                       pltpu.SemaphoreType.DMA((2,)), pltpu.SemaphoreType.DMA((2,))],
        compiler_params=pltpu.CompilerParams(use_tc_tiling_on_sc=True),
        name="sc_combine_gather")
    def gather(table_hbm, idx_hbm, out_hbm, idx_v, rows_v, gsem, psem):
        wid = _sc_worker()
        base = wid * per_w
        pltpu.sync_copy(idx_hbm.at[wid], idx_v)

        def get(w, slot):
            return pltpu.make_async_copy(table_hbm.at[idx_v.at[w]], rows_v.at[slot], gsem.at[slot])

        def put(w, slot):
            return pltpu.make_async_copy(rows_v.at[slot], out_hbm.at[pl.ds(base + w * SC_WINDOW, SC_WINDOW)],
                                         psem.at[slot])

        get(0, 0).start()

        @pl.loop(0, n_win, step=2)
        def _(w0):
            for s in range(2):
                w = w0 + s

                @pl.when(w + 1 < n_win)
                def _():
                    @pl.when(w >= 1)
                    def _():
                        put(w - 1, 1 - s).wait()

                    get(w + 1, 1 - s).start()

                get(w, s).wait()
                put(w, s).start()

        for s in range(2):
            put(n_win - 2 + s, s).wait()

    return gather(table, idx)


def _experts_kernel(be_ref, nv_ref, xs_ref, wg_ref, wu_ref, wd_ref, ys_ref, wgb_ref, wub_ref, wdb_ref):
    b = pl.program_id(0)
    prev = be_ref[jnp.maximum(b - 1, 0)]

    @pl.when((b == 0) | (be_ref[b] != prev))
    def _():
        wgb_ref[...] = wg_ref[0].astype(BF16)
        wub_ref[...] = wu_ref[0].astype(BF16)
        wdb_ref[...] = wd_ref[0].astype(BF16)

    @pl.when(nv_ref[b] > 0)
    def _():
        row = lax.broadcasted_iota(jnp.int32, (MOE_BLOCK, D_MODEL), 0)
        xb = jnp.where(row < nv_ref[b], _load_token_rows(xs_ref, MOE_BLOCK, BF16), 0.0)
        hg = jnp.dot(xb, wgb_ref[...], preferred_element_type=F32)
        hu = jnp.dot(xb, wub_ref[...], preferred_element_type=F32)
        hb = (_silu(hg) * hu).astype(BF16)
        _store_token_rows(ys_ref, jnp.dot(hb, wdb_ref[...], preferred_element_type=F32))

    @pl.when(nv_ref[b] == 0)
    def _():
        ys_ref[...] = jnp.zeros_like(ys_ref)


def _experts(block_expert, n_valid, xs, wg, wu, wd):
    nb = xs.shape[0] // (MOE_BLOCK * ROW_TILES)
    blk = pl.BlockSpec((MOE_BLOCK * ROW_TILES, LANES), lambda b, be, nv: (b, 0))
    wspec = lambda a, c: pl.BlockSpec((1, a, c), lambda b, be, nv: (be[b], 0, 0))
    return pl.pallas_call(
        _experts_kernel,
        grid_spec=pltpu.PrefetchScalarGridSpec(
            num_scalar_prefetch=2,
            grid=(nb,),
            in_specs=[blk, wspec(D_MODEL, D_EXPERT), wspec(D_MODEL, D_EXPERT), wspec(D_EXPERT, D_MODEL)],
            out_specs=blk,
            scratch_shapes=[pltpu.VMEM((D_MODEL, D_EXPERT), BF16), pltpu.VMEM((D_MODEL, D_EXPERT), BF16),
                            pltpu.VMEM((D_EXPERT, D_MODEL), BF16)]),
        out_shape=jax.ShapeDtypeStruct(xs.shape, xs.dtype),
        compiler_params=_cparams("arbitrary"),
        name="moe_experts",
    )(block_expert, n_valid, xs, wg, wu, wd)


def _combine_kernel(h_ref, rt_ref, p_ref, y0_ref, y1_ref, gp_ref, wpg_ref, wp_ref, gf_ref, o_ref, *, sub):
    n_sub = h_ref.shape[0] // sub

    def residual(r):
        rows = slice(r * sub, (r + 1) * sub)
        rt = rt_ref[rows, :]
        h = h_ref[rows, :] + (_load_token_rows(y0_ref, sub, F32, r * sub) * rt[:, 2:3]
                              + _load_token_rows(y1_ref, sub, F32, r * sub) * rt[:, 3:4])
        return h, _rms(h, gp_ref[...]).astype(BF16)

    def products(r, xn):
        return (jnp.dot(xn, wpg_ref[...], preferred_element_type=F32),
                _dot(p_ref[r * sub:(r + 1) * sub, :], wp_ref[...]))

    def finish(r, h, gate_lin, ple):
        h = h + ple * _sigmoid(gate_lin)
        o_ref[r * sub:(r + 1) * sub, :] = _rms(h, gf_ref[...])

    h, xn = residual(0)
    for r in range(n_sub):
        gate_lin, ple = products(r, xn)
        if r + 1 < n_sub:
            h_next, xn = residual(r + 1)
        finish(r, h, gate_lin, ple)
        if r + 1 < n_sub:
            h = h_next


def _combine(h1, rt, p2, y, gp, wpg, wp, gf, tm):
    t = h1.shape[0]
    nt = t // tm
    tile = lambda n: pl.BlockSpec((tm, n), lambda i: (i, 0))
    full = lambda a, b: pl.BlockSpec((a, b), lambda i: (0, 0))
    ytile = lambda k: pl.BlockSpec((tm * ROW_TILES, LANES), lambda i: (i + k * nt, 0))
    return pl.pallas_call(
        functools.partial(_combine_kernel, sub=128),
        grid=(nt,),
        in_specs=[tile(D_MODEL), tile(LANES), tile(D_PLE), ytile(0), ytile(1),
                  full(1, D_MODEL), full(D_MODEL, D_MODEL), full(D_PLE, D_MODEL), full(1, D_MODEL)],
        out_specs=tile(D_MODEL),
        out_shape=jax.ShapeDtypeStruct((t, D_MODEL), F32),
        compiler_params=_cparams("parallel"),
        name="moe_combine_ple",
    )(h1, rt, p2, y, y, gp, wpg, wp, gf)


def _lane_row(vals, offset):
    return jnp.zeros((1, LANES), F32).at[0, offset:offset + vals.shape[0]].set(vals)


def kernel(x, p, norm_mix, w_in, lru_conv_w, lru_conv_b, lru_wa, lru_ba, lru_wi, lru_bi, lru_lambda,
           lru_out_norm, gdn_conv_w, gdn_a_log, gdn_dt_bias, gdn_out_norm, w_out, norm_ffn,
           w_router_group, b_router_group, w_router_expert, b_router_expert, w_exp_gate, w_exp_up,
           w_exp_down, norm_ple, w_ple_gate, w_ple, norm_final):
    bsz, seq, d = x.shape
    t = bsz * seq
    depth = w_in.shape[0]
    assert depth == 1, "the final norm is fused into the last layer's combine kernel"
    n_blocks = -(-t * TOP_K // MOE_BLOCK) + N_EXPERTS
    row = lambda v: v.reshape(1, -1).astype(F32)
    h = x.reshape(t, d).astype(F32)
    for l in range(depth):
        w_in_p = jnp.pad(w_in[l], ((0, 0), (0, PROJ_COLS - w_in.shape[2]))).astype(BF16)
        b_gates = jnp.concatenate([lru_ba[l], lru_bi[l]]).reshape(1, -1)
        proj_gdn, y_lru = _proj_lru(h, row(norm_mix[l]), w_in_p, lru_conv_w[l], row(lru_conv_b[l]), lru_wa[l],
                                    lru_wi[l], b_gates, row(lru_lambda[l]), row(lru_out_norm[l]), seq, 512)
        cw = gdn_conv_w[l].reshape(CONV_W, 3, D_GDN).transpose(1, 0, 2)
        w_r = jnp.pad(jnp.concatenate([w_router_group[l], w_router_expert[l]], axis=1),
                      ((0, 0), (0, LANES - N_GROUPS - N_EXPERTS)))
        w_r_hi = w_r.astype(BF16)
        w_r_lo = (w_r - w_r_hi.astype(F32)).astype(BF16)
        w_r = jnp.concatenate([w_r_hi, w_r_hi, w_r_lo], axis=0)
        b_r = _lane_row(jnp.concatenate([b_router_group[l], b_router_expert[l]]), 0)
        h1, xn2, rt, tile_counts = _gdn_router(
            proj_gdn, cw, _lane_row(gdn_a_log[l], GDN_HEADS), _lane_row(gdn_dt_bias[l], GDN_HEADS),
            row(gdn_out_norm[l]), h, y_lru, w_out[l].astype(BF16), row(norm_ffn[l]), w_r, b_r, seq, 512)
        dest8, blocks = _slots(rt, tile_counts, n_blocks, 1024)
        dest = dest8[:TOP_K]
        block_expert, n_valid = blocks[:, 0], blocks[:, 1]
        tiles = lambda a: a.reshape(-1, ROW_TILES, LANES)
        xs = _sc_scatter_rows(tiles(xn2), dest, n_blocks * MOE_BLOCK)
        ys = _experts(block_expert, n_valid, xs.reshape(-1, LANES), w_exp_gate[l], w_exp_up[l], w_exp_down[l])
        y = _sc_gather_rows(tiles(ys), dest.reshape(-1))
        h = _combine(h1, rt, p[l].reshape(t, -1).astype(F32), y.reshape(-1, LANES), row(norm_ple[l]),
                     w_ple_gate[l].astype(BF16), w_ple[l].astype(BF16), row(norm_final), 512)
    return h.reshape(bsz, seq, d).astype(x.dtype)
```

```python
import functools

import jax
import jax.numpy as jnp
from jax import lax
from jax.experimental import pallas as pl
from jax.experimental.pallas import tpu as pltpu
from jax.experimental.pallas import tpu_sc as plsc

D_MODEL = 1024
D_LRU = 512
LRU_BLOCKS = 8
LRU_BLOCK_W = D_LRU // LRU_BLOCKS
LRU_C = 8.0
D_GDN = 512
GDN_HEADS = 4
GDN_HEAD_DIM = D_GDN // GDN_HEADS
CONV_W = 4
CHUNK = 64
N_GROUPS = 4
EXPERTS_PER_GROUP = 8
N_EXPERTS = N_GROUPS * EXPERTS_PER_GROUP
TOP_K = 2
D_EXPERT = 512
MOE_BLOCK = 512
D_PLE = 256
EPS = 1e-6

LANES = 128
SUBLANES = 8
ROW_TILES = D_MODEL // (2 * LANES)
SC_CORES = 2
SC_SUBCORES = 16
SC_WORKERS = SC_CORES * SC_SUBCORES
SC_WINDOW = 64
PROJ_COLS = 2 * D_LRU + 4 * D_GDN + LANES
VMEM_LIMIT = 56 * 1024 * 1024

BF16 = jnp.bfloat16
F32 = jnp.float32


def _cparams(*sem):
    return pltpu.CompilerParams(dimension_semantics=sem, vmem_limit_bytes=VMEM_LIMIT)


def _rms(x, g):
    return x * lax.rsqrt(jnp.mean(x * x, axis=-1, keepdims=True) + EPS) * g


def _sigmoid(x):
    return 0.5 * jnp.tanh(0.5 * x) + 0.5


def _silu(x):
    return x * _sigmoid(x)


def _dot(a, b):
    return jnp.dot(a.astype(BF16), b.astype(BF16), preferred_element_type=F32)


def _dot_nt(a, b):
    return lax.dot_general(a.astype(BF16), b.astype(BF16), (((1,), (1,)), ((), ())),
                           preferred_element_type=F32)


def _dot_tn(a, b):
    return lax.dot_general(a.astype(BF16), b.astype(BF16), (((0,), (0,)), ((), ())),
                           preferred_element_type=F32)


def _conv_scratch(rows, channels):
    return pltpu.VMEM((channels // LANES, 2 * (SUBLANES + rows), LANES), F32)


def _conv_reset(xp_ref):
    for k in range(xp_ref.shape[0]):
        xp_ref.at[k][pl.ds(0, SUBLANES, stride=2), :] = jnp.zeros((SUBLANES, LANES), F32)


def _causal_conv(xp_ref, x_ref, w, r0, rows, col0=0):
    out = []
    for k in range(xp_ref.shape[0]):
        lanes = slice(k * LANES, (k + 1) * LANES)
        slab = xp_ref.at[k]
        xk = x_ref[r0:r0 + rows, col0 + k * LANES:col0 + (k + 1) * LANES]
        slab[pl.ds(2 * (SUBLANES + r0), rows, stride=2), :] = xk
        acc = xk * w[CONV_W - 1:CONV_W, lanes]
        for j in range(1, CONV_W):
            acc = acc + slab[pl.ds(2 * (SUBLANES + r0 - j), rows, stride=2), :] * w[CONV_W - 1 - j:CONV_W - j, lanes]
        out.append(acc)
    return jnp.concatenate(out, axis=1)


def _conv_carry(xp_ref, x_ref, col0=0):
    ts = x_ref.shape[0]
    for k in range(xp_ref.shape[0]):
        xp_ref.at[k][pl.ds(0, SUBLANES, stride=2), :] = x_ref[ts - SUBLANES:, col0 + k * LANES:col0 + (k + 1) * LANES]


def _interleave(*stages):
    live = list(stages)
    while live:
        for g in list(live):
            try:
                next(g)
            except StopIteration:
                live.remove(g)


def _store_token_rows(ref, x, row0=0):
    rows = x.shape[0]
    bits = lax.bitcast_convert_type(x.astype(BF16).astype(F32), jnp.uint32)
    words = (bits[:, :D_MODEL // 2] >> 16) | bits[:, D_MODEL // 2:]
    for j in range(ROW_TILES):
        ref[pl.ds(row0 * ROW_TILES + j, rows, stride=ROW_TILES), :] = words[:, j * LANES:(j + 1) * LANES]


def _load_token_rows(ref, rows, dtype, row0=0):
    words = [ref[pl.ds(row0 * ROW_TILES + j, rows, stride=ROW_TILES), :] for j in range(ROW_TILES)]
    lo = [lax.bitcast_convert_type(w << 16, F32).astype(dtype) for w in words]
    hi = [lax.bitcast_convert_type(w & jnp.uint32(0xFFFF0000), F32).astype(dtype) for w in words]
    return jnp.concatenate(lo + hi, axis=1)


def _proj_lru_kernel(x_ref, gin_ref, win_ref, cw_ref, cb_ref, wa_ref, wi_ref, bg_ref, lam_ref, og_ref,
                     gdn_ref, o_ref, pj_ref, tail_ref, h_ref, wg_ref, *, tiles_per_seq, col_tile):
    g = pl.program_id(0)
    ts = x_ref.shape[0]
    pj_new = pj_ref.at[g % 2]
    pj = pj_ref.at[(g + 1) % 2]

    @pl.when(g == 0)
    def _():
        pj_ref[...] = jnp.zeros_like(pj_ref)
        wg_ref[...] = jnp.zeros_like(wg_ref)
        for n in range(LRU_BLOCKS):
            blk = slice(n * LRU_BLOCK_W, (n + 1) * LRU_BLOCK_W)
            wg_ref[blk, blk] = wa_ref[n].astype(BF16)
            wg_ref[blk, D_LRU + n * LRU_BLOCK_W:D_LRU + (n + 1) * LRU_BLOCK_W] = wi_ref[n].astype(BF16)

    @pl.when((g == 0) | (lax.rem(g - 1, tiles_per_seq) == 0))
    def _():
        _conv_reset(tail_ref)
        h_ref[...] = jnp.zeros_like(h_ref)

    done = []

    def after_projection(v):
        return v + jnp.concatenate([done[max(len(done) - 2, 0)]] * (v.shape[-1] // LANES), axis=-1)

    def project():
        xn = _rms(x_ref[...], gin_ref[...]).astype(BF16)
        yield
        for c0 in range(0, PROJ_COLS, col_tile):
            c1 = min(c0 + col_tile, PROJ_COLS)
            y = jnp.dot(xn, win_ref[:, c0:c1], preferred_element_type=F32)
            if c1 <= 2 * D_LRU:
                pj_new[:, c0:c1] = y
            else:
                gdn_ref[:, c0 - 2 * D_LRU:c1 - 2 * D_LRU] = y
            bits = lax.bitcast_convert_type(y[ts - 1:, c1 - c0 - LANES:], jnp.uint32)
            done.append(lax.bitcast_convert_type((bits >> 16) >> 16, F32))
            yield

    def recur():
        xc = _causal_conv(tail_ref, pj, cw_ref[...], 0, ts) + cb_ref[...]
        _conv_carry(tail_ref, pj)
        yield
        gates = _sigmoid(_dot(xc, wg_ref[...]) + after_projection(bg_ref[...]))
        r = gates[:, :D_LRU]
        i = gates[:, D_LRU:]
        yield
        log_a = LRU_C * r * jax.nn.log_sigmoid(after_projection(lam_ref[...]))
        a = jnp.exp(log_a)
        th = jnp.tanh(log_a)
        u = jnp.sqrt(-2.0 * th) * lax.rsqrt(1.0 - th) * (i * xc)
        yield
        a = a.reshape(ts // SUBLANES, SUBLANES, D_LRU)
        u = u.reshape(ts // SUBLANES, SUBLANES, D_LRU)
        row = lax.broadcasted_iota(jnp.int32, a.shape, 1)
        d = 1
        while d < SUBLANES:
            keep = row >= d
            a_prev = jnp.where(keep, pltpu.roll(a, d, axis=1), 1.0)
            u_prev = jnp.where(keep, pltpu.roll(u, d, axis=1), 0.0)
            u = a * u_prev + u
            a = a * a_prev
            d *= 2
            yield
        carry = after_projection(h_ref[...])
        groups = []
        for n in range(ts // SUBLANES):
            groups.append(a[n] * carry + u[n])
            carry = groups[-1][SUBLANES - 1:]
        h = jnp.concatenate(groups, axis=0)
        h_ref[...] = carry
        yield
        y = h * jax.nn.gelu(pj[:, D_LRU:])
        o_ref[...] = _rms(y, after_projection(og_ref[...])).astype(o_ref.dtype)

    _interleave(project(), recur())


def _proj_lru(x2, gin, win, cw, cb, wa, wi, bg, lam, og, seq, ts):
    t = x2.shape[0]
    n_tiles = t // ts
    this = lambda g: (jnp.minimum(g, n_tiles - 1), 0)
    last = lambda g: (jnp.maximum(g - 1, 0), 0)
    const = lambda *shape: pl.BlockSpec(shape, lambda g: (0,) * len(shape))
    gdn_cols = PROJ_COLS - 2 * D_LRU
    return pl.pallas_call(
        functools.partial(_proj_lru_kernel, tiles_per_seq=seq // ts, col_tile=D_LRU),
        grid=(n_tiles + 1,),
        in_specs=[pl.BlockSpec((ts, D_MODEL), this), const(1, D_MODEL), const(D_MODEL, PROJ_COLS),
                  const(CONV_W, D_LRU), const(1, D_LRU),
                  const(LRU_BLOCKS, LRU_BLOCK_W, LRU_BLOCK_W), const(LRU_BLOCKS, LRU_BLOCK_W, LRU_BLOCK_W),
                  const(1, 2 * D_LRU), const(1, D_LRU), const(1, D_LRU)],
        out_specs=[pl.BlockSpec((ts, gdn_cols), this), pl.BlockSpec((ts, D_LRU), last)],
        out_shape=[jax.ShapeDtypeStruct((t, gdn_cols), F32), jax.ShapeDtypeStruct((t, D_LRU), BF16)],
        scratch_shapes=[pltpu.VMEM((2, ts, 2 * D_LRU), F32), _conv_scratch(ts, D_LRU), pltpu.VMEM((1, D_LRU), F32),
                        pltpu.VMEM((D_LRU, 2 * D_LRU), BF16)],
        compiler_params=_cparams("arbitrary"),
        name="in_proj_rglru",
    )(x2, gin, win, cw, cb, wa, wi, bg, lam, og)


def _gdn_router_kernel(q_ref, k_ref, v_ref, z_ref, ba_ref, cw_ref, alog_ref, dtb_ref, og_ref,
                       x_ref, yl_ref, wo_ref, gf_ref, wr_ref, br_ref, h_ref, xn_ref, rt_ref, cnt_ref,
                       qt_ref, kt_ref, vt_ref, qs_ref, ks_ref, vs_ref, bs_ref, gc_ref, gct_ref, st_ref, yg_ref,
                       *, tiles_per_seq, group_chunks, prep_rows, sub):
    g = pl.program_id(0)
    ts = q_ref.shape[0]
    dk = GDN_HEAD_DIM
    nc = ts // CHUNK
    n_sub = ts // sub
    yg_new = yg_ref.at[g % 2]
    yg = yg_ref.at[(g + 1) % 2]

    @pl.when(g == 0)
    def _():
        yg_ref[...] = jnp.zeros_like(yg_ref)

    @pl.when(lax.rem(g, tiles_per_seq) == 0)
    def _():
        for tail_ref in (qt_ref, kt_ref, vt_ref):
            _conv_reset(tail_ref)
        st_ref[...] = jnp.zeros_like(st_ref)

    done = []
    hs = {}
    counts = []

    def after_projection(v):
        if not done:
            return v
        return v + jnp.concatenate([done[-1]] * (v.shape[-1] // LANES), axis=-1)

    def project_out():
        for r in range(n_sub):
            rows = slice(r * sub, (r + 1) * sub)
            h = x_ref[rows, :] + jnp.dot(yl_ref[rows, :], wo_ref[:D_LRU, :], preferred_element_type=F32) \
                + jnp.dot(yg[rows, :], wo_ref[D_LRU:, :], preferred_element_type=F32)
            h_ref[rows, :] = h
            hs[r] = h
            bits = lax.bitcast_convert_type(h[sub - 1:, D_MODEL - LANES:], jnp.uint32)
            done.append(lax.bitcast_convert_type((bits >> 16) >> 16, F32))
            yield

    def route(r, h):
        xn = _rms(h, gf_ref[...])
        _store_token_rows(xn_ref, xn, r * sub)
        xh = xn.astype(BF16)
        xl = (xn - xh.astype(F32)).astype(BF16)
        logits = jnp.dot(jnp.concatenate([xh, xl, xh], axis=1), wr_ref[...],
                         preferred_element_type=F32) + br_ref[...]
        lane = lax.broadcasted_iota(jnp.int32, logits.shape, 1).astype(F32)
        big = jnp.float32(2 * LANES)
        ninf = jnp.float32(-jnp.inf)

        def top1(vals):
            m = jnp.max(vals, axis=-1, keepdims=True)
            return m, jnp.min(jnp.where(vals == m, lane, big), axis=-1, keepdims=True)

        gl = jnp.where(lane < N_GROUPS, logits, ninf)
        gmax, gsel = top1(gl)
        p_group = 1.0 / jnp.sum(jnp.exp(gl - gmax), axis=-1, keepdims=True)
        lo = N_GROUPS + EXPERTS_PER_GROUP * gsel
        el = jnp.where((lane >= lo) & (lane < lo + EXPERTS_PER_GROUP), logits, ninf)
        m1, i1 = top1(el)
        m2, i2 = top1(jnp.where(lane == i1, ninf, el))
        rr = jnp.exp(m2 - m1)
        g1 = p_group / (1.0 + rr)
        g2 = p_group * rr / (1.0 + rr)
        rt_ref[r * sub:(r + 1) * sub, :] = jnp.where(
            lane == 0, i1 - N_GROUPS,
            jnp.where(lane == 1, i2 - N_GROUPS, jnp.where(lane == 2, g1, jnp.where(lane == 3, g2, 0.0))))
        member = ((lane == i1 - N_GROUPS) | (lane == i2 - N_GROUPS)).astype(F32)
        counts.append(jnp.sum(member, axis=0, keepdims=True))

    def route_subtiles(subtiles):
        for r in subtiles:
            route(r, hs[r])
            yield

    ri = lax.broadcasted_iota(jnp.int32, (CHUNK, CHUNK), 0)
    ci = lax.broadcasted_iota(jnp.int32, (CHUNK, CHUNK), 1)
    causal = ri >= ci
    strict = ri > ci
    tril = causal.astype(F32)
    eye = (ri == ci).astype(F32)
    og = og_ref[...]

    def l2n(x, scale):
        parts = []
        for h in range(GDN_HEADS):
            xh = x[:, h * dk:(h + 1) * dk]
            parts.append(xh * (lax.rsqrt(jnp.sum(xh * xh, axis=-1, keepdims=True) + EPS) * scale))
        return jnp.concatenate(parts, axis=1)

    def prepare(r0, r1):
        for p0 in range(r0, r1, prep_rows):
            rows = slice(p0, p0 + prep_rows)
            conv = lambda x_ref, tail_ref, part: _silu(
                _causal_conv(tail_ref, x_ref, after_projection(cw_ref[part]), p0, prep_rows))
            qs_ref[rows, :] = l2n(conv(q_ref, qt_ref, 0), dk ** -0.5)
            yield
            ks_ref[rows, :] = l2n(conv(k_ref, kt_ref, 1), 1.0)
            yield
            vs_ref[rows, :] = conv(v_ref, vt_ref, 2)
            ba = ba_ref[rows, :]
            bs_ref[rows, :] = _sigmoid(ba)
            g = -jnp.exp(alog_ref[...]) * jax.nn.softplus(ba + dtb_ref[...])
            for c0 in range(0, prep_rows, CHUNK):
                gc_ref[p0 + c0:p0 + c0 + CHUNK, :] = jnp.dot(tril, g[c0:c0 + CHUNK], precision=lax.Precision.HIGHEST,
                                                             preferred_element_type=F32)
            gct_ref[:, rows] = gc_ref[rows, :].T
            yield

    terms = {}

    def chunk_terms(pairs):
        n = range(len(pairs))
        rows = [slice(c * CHUNK, (c + 1) * CHUNK) for c, _ in pairs]
        cols = [slice(h * dk, (h + 1) * dk) for _, h in pairs]
        gl = [GDN_HEADS + h for _, h in pairs]
        kh = [ks_ref[rows[i], cols[i]] for i in n]
        kb = [kh[i] * bs_ref[rows[i], pairs[i][1]:pairs[i][1] + 1] for i in n]
        r = [_dot_nt(jnp.concatenate([kb[i], qs_ref[rows[i], cols[i]]], axis=0), kh[i]) for i in n]
        yield
        gcol = [gc_ref[rows[i], gl[i]:gl[i] + 1] for i in n]
        decay = []
        for i in n:
            diff = gcol[i] - gct_ref[gl[i]:gl[i] + 1, rows[i]]
            decay.append(jnp.where(causal, jnp.exp(jnp.where(causal, diff, 0.0)), 0.0))
        a = [jnp.where(strict, r[i][:CHUNK] * decay[i], 0.0) for i in n]
        qk = [(r[i][CHUNK:] * decay[i]).astype(BF16) for i in n]
        tinv = [eye - a[i] for i in n]
        p = 2
        while p < CHUNK:
            a = [_dot(a[i], a[i]) for i in n]
            yield
            tinv = [tinv[i] + _dot(tinv[i], a[i]) for i in n]
            yield
            p *= 2
        eg = [jnp.exp(gcol[i]) for i in n]
        rhs = [jnp.concatenate([vs_ref[rows[i], cols[i]] * bs_ref[rows[i], pairs[i][1]:pairs[i][1] + 1],
                                kb[i] * eg[i]], axis=1) for i in n]
        uw = [_dot(tinv[i], rhs[i]).astype(BF16) for i in n]
        yield
        qk_uw = [jnp.dot(qk[i], uw[i], preferred_element_type=F32) for i in n]
        glast = [gc_ref[(c + 1) * CHUNK - 1:(c + 1) * CHUNK, gl[i]:gl[i] + 1] for i, (c, _) in enumerate(pairs)]
        kd_uw = [_dot_tn(kh[i] * jnp.exp(glast[i] - gcol[i]), uw[i]) for i in n]
        yield
        for i in n:
            lhs = jnp.concatenate([kd_uw[i][:, dk:], qs_ref[rows[i], cols[i]] * eg[i] - qk_uw[i][:, dk:]],
                                  axis=0).astype(BF16)
            terms[pairs[i]] = (lhs, kd_uw[i][:, :dk], qk_uw[i][:, :dk], jnp.exp(glast[i]))

    state = [st_ref[h] for h in range(GDN_HEADS)]

    def advance(chunks):
        for c in chunks:
            rows = slice(c * CHUNK, (c + 1) * CHUNK)
            r = [jnp.dot(terms[c, h][0], state[h].astype(BF16), preferred_element_type=F32)
                 for h in range(GDN_HEADS)]
            for h in range(GDN_HEADS):
                cols = slice(h * dk, (h + 1) * dk)
                _, c_add, o_add, egl = terms[c, h]
                o = r[h][dk:] + o_add
                state[h] = egl * state[h] - r[h][:dk] + c_add
                yg_new[rows, cols] = (_rms(o, og) * _silu(z_ref[rows, cols])).astype(yg_new.dtype)
            yield

    groups = [range(c0, c0 + group_chunks) for c0 in range(0, nc, group_chunks)]
    pairs_of = lambda chunks: [(c, h) for c in chunks for h in range(GDN_HEADS)]
    span = lambda chunks: (chunks[0] * CHUNK, (chunks[-1] + 1) * CHUNK)
    per_group = -(-n_sub // len(groups))
    _interleave(prepare(*span(groups[0])), project_out())
    for i, chunks in enumerate(groups):
        side = [route_subtiles(range(i * per_group, min((i + 1) * per_group, n_sub)))]
        if i + 1 < len(groups):
            side.append(prepare(*span(groups[i + 1])))
        if i > 0:
            side.append(advance(groups[i - 1]))
        _interleave(chunk_terms(pairs_of(chunks)), *side)
    _interleave(advance(groups[-1]))
    for h in range(GDN_HEADS):
        st_ref[h] = state[h]
    for x_in, tail_ref in ((q_ref, qt_ref), (k_ref, kt_ref), (v_ref, vt_ref)):
        _conv_carry(tail_ref, x_in)
    cnt_ref[...] = jnp.broadcast_to(sum(counts), cnt_ref.shape)


def _gdn_router(proj, cw, alog, dtb, og, x2, yl, wo, gf, wr, br, seq, ts):
    t = x2.shape[0]
    n_tiles = t // ts
    this = lambda c: (lambda g: (jnp.minimum(g, n_tiles - 1), c))
    last = lambda g: (jnp.maximum(g - 1, 0), 0)
    const = lambda *shape: pl.BlockSpec(shape, lambda g: (0,) * len(shape))
    col = lambda c: pl.BlockSpec((ts, D_GDN), this(c))
    return pl.pallas_call(
        functools.partial(_gdn_router_kernel, tiles_per_seq=seq // ts, group_chunks=4, prep_rows=128, sub=128),
        grid=(n_tiles + 1,),
        in_specs=[col(0), col(1), col(2), col(3), pl.BlockSpec((ts, LANES), this(4 * D_GDN // LANES)),
                  const(3, CONV_W, D_GDN), const(1, LANES), const(1, LANES), const(1, GDN_HEAD_DIM),
                  pl.BlockSpec((ts, D_MODEL), last), pl.BlockSpec((ts, D_LRU), last), const(D_MODEL, D_MODEL),
                  const(1, D_MODEL), const(3 * D_MODEL, LANES), const(1, LANES)],
        out_specs=[pl.BlockSpec((ts, D_MODEL), last), pl.BlockSpec((ts * ROW_TILES, LANES), last),
                   pl.BlockSpec((ts, LANES), last), pl.BlockSpec((SUBLANES, LANES), last)],
        out_shape=[jax.ShapeDtypeStruct((t, D_MODEL), F32),
                   jax.ShapeDtypeStruct((t * ROW_TILES, LANES), jnp.uint32),
                   jax.ShapeDtypeStruct((t, LANES), F32),
                   jax.ShapeDtypeStruct((n_tiles * SUBLANES, LANES), F32)],
        scratch_shapes=[_conv_scratch(ts, D_GDN)] * 3
        + [pltpu.VMEM((ts, D_GDN), F32)] * 3
        + [pltpu.VMEM((ts, LANES), F32)] * 2
        + [pltpu.VMEM((LANES, ts), F32)]
        + [pltpu.VMEM((GDN_HEADS, GDN_HEAD_DIM, GDN_HEAD_DIM), F32)]
        + [pltpu.VMEM((2, ts, D_GDN), BF16)],
        compiler_params=_cparams("arbitrary"),
        name="gdn_out_router",
    )(proj, proj, proj, proj, proj, cw, alog, dtb, og, x2, yl, wo, gf, wr, br)


def _slots_kernel(rt_ref, tc_ref, dest_ref, blocks_ref, cnt_ref, before_ref):
    i = pl.program_id(0)
    tm = rt_ref.shape[0]
    rt = rt_ref[...]
    lane = lax.broadcasted_iota(jnp.int32, rt.shape, 1)
    e0 = rt[:, 0:1].astype(jnp.int32)
    e1 = rt[:, 1:2].astype(jnp.int32)
    member = ((lane == e0) | (lane == e1)).astype(BF16)

    @pl.when(i == 0)
    def _():
        ri = lax.broadcasted_iota(jnp.int32, (tm, tm), 0)
        ci = lax.broadcasted_iota(jnp.int32, (tm, tm), 1)
        before_ref[...] = (ri > ci).astype(BF16)
        cnt = jnp.broadcast_to(jnp.sum(tc_ref[...], axis=0, keepdims=True) / SUBLANES, cnt_ref.shape)
        padded = jnp.ceil(cnt / MOE_BLOCK) * MOE_BLOCK
        l8 = lax.broadcasted_iota(jnp.int32, cnt.shape, 1)
        incl = padded
        d = 1
        while d < LANES:
            incl = incl + jnp.where(l8 >= d, pltpu.roll(incl, d, axis=1), 0.0)
            d *= 2
        cnt_ref[...] = incl - padded
        seg_start = (incl - padded)[0:1, :]
        seg_end = incl[0:1, :]
        real_end = seg_start + cnt[0:1, :]
        bl = lax.broadcasted_iota(jnp.int32, blocks_ref.shape, 1)
        b0 = (lax.broadcasted_iota(jnp.int32, blocks_ref.shape, 0) * MOE_BLOCK).astype(F32)
        is_expert = bl < N_EXPERTS
        expert = jnp.sum(jnp.where(is_expert & (seg_end <= b0), 1.0, 0.0), axis=-1, keepdims=True)
        real = jnp.maximum(jnp.minimum(real_end, b0 + MOE_BLOCK) - jnp.maximum(seg_start, b0), 0.0)
        n_real = jnp.sum(jnp.where(is_expert, real, 0.0), axis=-1, keepdims=True)
        blocks_ref[...] = jnp.where(bl == 0, jnp.minimum(expert, N_EXPERTS - 1.0),
                                    jnp.where(bl == 1, n_real, 0.0)).astype(jnp.int32)

    start = cnt_ref[...]
    pos = start[0:1, :] + jnp.dot(before_ref[...], member, preferred_element_type=F32)
    d0 = jnp.sum(jnp.where(lane == e0, pos, 0.0), axis=-1, keepdims=True)
    d1 = jnp.sum(jnp.where(lane == e1, pos, 0.0), axis=-1, keepdims=True)
    dest = jnp.where(lane == 0, d0, jnp.where(lane == 1, d1, 0.0))
    dest_ref[...] = dest.T[:SUBLANES].astype(jnp.int32)
    cnt_ref[...] = start + jnp.dot(jnp.ones((SUBLANES, tm), BF16), member, preferred_element_type=F32)


def _slots(rt, tile_counts, n_blocks, tm):
    t = rt.shape[0]
    return pl.pallas_call(
        _slots_kernel,
        grid=(t // tm,),
        in_specs=[pl.BlockSpec((tm, LANES), lambda i: (i, 0)),
                  pl.BlockSpec(tile_counts.shape, lambda i: (0, 0))],
        out_specs=[pl.BlockSpec((SUBLANES, tm), lambda i: (0, i)),
                   pl.BlockSpec((n_blocks, LANES), lambda i: (0, 0))],
        out_shape=[jax.ShapeDtypeStruct((SUBLANES, t), jnp.int32),
                   jax.ShapeDtypeStruct((n_blocks, LANES), jnp.int32)],
        scratch_shapes=[pltpu.VMEM((SUBLANES, LANES), F32), pltpu.VMEM((tm, tm), BF16)],
        compiler_params=_cparams("arbitrary"),
        name="moe_slots",
    )(rt, tile_counts)


def _sc_mesh():
    return plsc.VectorSubcoreMesh(core_axis_name="c", subcore_axis_name="s")


def _sc_worker():
    return lax.axis_index("s") * SC_CORES + lax.axis_index("c")


def _sc_scatter_rows(src, idx, n_rows):
    t = src.shape[0]
    per_w = t // SC_WORKERS
    n_win = per_w // SC_WINDOW
    idx = idx.reshape(TOP_K, SC_WORKERS, n_win, SC_WINDOW)

    @functools.partial(
        pl.kernel, mesh=_sc_mesh(),
        out_type=jax.ShapeDtypeStruct((n_rows,) + src.shape[1:], src.dtype),
        scratch_types=[pltpu.VMEM((TOP_K, n_win, SC_WINDOW), jnp.int32),
                       pltpu.VMEM((2, SC_WINDOW) + src.shape[1:], src.dtype),
                       pltpu.SemaphoreType.DMA((2,)), pltpu.SemaphoreType.DMA((2,))],
        compiler_params=pltpu.CompilerParams(use_tc_tiling_on_sc=True),
        name="sc_dispatch")
    def scatter(src_hbm, idx_hbm, out_hbm, idx_v, rows_v, lsem, ssem):
        wid = _sc_worker()
        base = wid * per_w
        for k in range(TOP_K):
            pltpu.sync_copy(idx_hbm.at[k, wid], idx_v.at[k])

        def load(w, slot):
            return pltpu.make_async_copy(src_hbm.at[pl.ds(base + w * SC_WINDOW, SC_WINDOW)], rows_v.at[slot],
                                         lsem.at[slot])

        def put(w, slot, k):
            return pltpu.make_async_copy(rows_v.at[slot], out_hbm.at[idx_v.at[k, w]], ssem.at[slot])

        load(0, 0).start()

        @pl.loop(0, n_win, step=2)
        def _(w0):
            for s in range(2):
                w = w0 + s

                @pl.when(w + 1 < n_win)
                def _():
                    @pl.when(w >= 1)
                    def _():
                        for k in range(TOP_K):
                            put(w - 1, 1 - s, k).wait()

                    load(w + 1, 1 - s).start()

                load(w, s).wait()
                for k in range(TOP_K):
                    put(w, s, k).start()

        for s in range(2):
            for k in range(TOP_K):
                put(n_win - 2 + s, s, k).wait()

    return scatter(src, idx)


def _sc_gather_rows(table, idx):
    b = idx.shape[0]
    per_w = b // SC_WORKERS
    n_win = per_w // SC_WINDOW
    idx = idx.reshape(SC_WORKERS, n_win, SC_WINDOW)

    @functools.partial(
        pl.kernel, mesh=_sc_mesh(),
        out_type=jax.ShapeDtypeStruct((b,) + table.shape[1:], table.dtype),
        scratch_types=[pltpu.VMEM((n_win, SC_WINDOW), jnp.int32),
                       pltpu.VMEM((2, SC_WINDOW) + table.shape[1:], table.dtype),
                       pltpu.SemaphoreType.DMA((2,)), pltpu.SemaphoreType.DMA((2,))],
        compiler_params=pltpu.CompilerParams(use_tc_tiling_on_sc=True),
        name="sc_combine_gather")
    def gather(table_hbm, idx_hbm, out_hbm, idx_v, rows_v, gsem, psem):
        wid = _sc_worker()
        base = wid * per_w
        pltpu.sync_copy(idx_hbm.at[wid], idx_v)

        def get(w, slot):
            return pltpu.make_async_copy(table_hbm.at[idx_v.at[w]], rows_v.at[slot], gsem.at[slot])

        def put(w, slot):
            return pltpu.make_async_copy(rows_v.at[slot], out_hbm.at[pl.ds(base + w * SC_WINDOW, SC_WINDOW)],
                                         psem.at[slot])

        get(0, 0).start()

        @pl.loop(0, n_win, step=2)
        def _(w0):
            for s in range(2):
                w = w0 + s

                @pl.when(w + 1 < n_win)
                def _():
                    @pl.when(w >= 1)
                    def _():
                        put(w - 1, 1 - s).wait()

                    get(w + 1, 1 - s).start()

                get(w, s).wait()
                put(w, s).start()

        for s in range(2):
            put(n_win - 2 + s, s).wait()

    return gather(table, idx)


def _experts_kernel(be_ref, nv_ref, xs_ref, wg_ref, wu_ref, wd_ref, ys_ref, wgb_ref, wub_ref, wdb_ref):
    b = pl.program_id(0)
    prev = be_ref[jnp.maximum(b - 1, 0)]

    @pl.when((b == 0) | (be_ref[b] != prev))
    def _():
        wgb_ref[...] = wg_ref[0].astype(BF16)
        wub_ref[...] = wu_ref[0].astype(BF16)
        wdb_ref[...] = wd_ref[0].astype(BF16)

    @pl.when(nv_ref[b] > 0)
    def _():
        row = lax.broadcasted_iota(jnp.int32, (MOE_BLOCK, D_MODEL), 0)
        xb = jnp.where(row < nv_ref[b], _load_token_rows(xs_ref, MOE_BLOCK, BF16), 0.0)
        hg = jnp.dot(xb, wgb_ref[...], preferred_element_type=F32)
        hu = jnp.dot(xb, wub_ref[...], preferred_element_type=F32)
        hb = (_silu(hg) * hu).astype(BF16)
        _store_token_rows(ys_ref, jnp.dot(hb, wdb_ref[...], preferred_element_type=F32))

    @pl.when(nv_ref[b] == 0)
    def _():
        ys_ref[...] = jnp.zeros_like(ys_ref)


def _experts(block_expert, n_valid, xs, wg, wu, wd):
    nb = xs.shape[0] // (MOE_BLOCK * ROW_TILES)
    blk = pl.BlockSpec((MOE_BLOCK * ROW_TILES, LANES), lambda b, be, nv: (b, 0))
    wspec = lambda a, c: pl.BlockSpec((1, a, c), lambda b, be, nv: (be[b], 0, 0))
    return pl.pallas_call(
        _experts_kernel,
        grid_spec=pltpu.PrefetchScalarGridSpec(
            num_scalar_prefetch=2,
            grid=(nb,),
            in_specs=[blk, wspec(D_MODEL, D_EXPERT), wspec(D_MODEL, D_EXPERT), wspec(D_EXPERT, D_MODEL)],
            out_specs=blk,
            scratch_shapes=[pltpu.VMEM((D_MODEL, D_EXPERT), BF16), pltpu.VMEM((D_MODEL, D_EXPERT), BF16),
                            pltpu.VMEM((D_EXPERT, D_MODEL), BF16)]),
        out_shape=jax.ShapeDtypeStruct(xs.shape, xs.dtype),
        compiler_params=_cparams("arbitrary"),
        name="moe_experts",
    )(block_expert, n_valid, xs, wg, wu, wd)


def _combine_kernel(h_ref, rt_ref, p_ref, y0_ref, y1_ref, gp_ref, wpg_ref, wp_ref, gf_ref, o_ref, *, sub):
    n_sub = h_ref.shape[0] // sub

    def residual(r):
        rows = slice(r * sub, (r + 1) * sub)
        rt = rt_ref[rows, :]
        h = h_ref[rows, :] + (_load_token_rows(y0_ref, sub, F32, r * sub) * rt[:, 2:3]
                              + _load_token_rows(y1_ref, sub, F32, r * sub) * rt[:, 3:4])
        return h, _rms(h, gp_ref[...]).astype(BF16)

    def products(r, xn):
        return (jnp.dot(xn, wpg_ref[...], preferred_element_type=F32),
                _dot(p_ref[r * sub:(r + 1) * sub, :], wp_ref[...]))

    def finish(r, h, gate_lin, ple):
        h = h + ple * _sigmoid(gate_lin)
        o_ref[r * sub:(r + 1) * sub, :] = _rms(h, gf_ref[...])

    h, xn = residual(0)
    for r in range(n_sub):
        gate_lin, ple = products(r, xn)
        if r + 1 < n_sub:
            h_next, xn = residual(r + 1)
        finish(r, h, gate_lin, ple)
        if r + 1 < n_sub:
            h = h_next


def _combine(h1, rt, p2, y, gp, wpg, wp, gf, tm):
    t = h1.shape[0]
    nt = t // tm
    tile = lambda n: pl.BlockSpec((tm, n), lambda i: (i, 0))
    full = lambda a, b: pl.BlockSpec((a, b), lambda i: (0, 0))
    ytile = lambda k: pl.BlockSpec((tm * ROW_TILES, LANES), lambda i: (i + k * nt, 0))
    return pl.pallas_call(
        functools.partial(_combine_kernel, sub=256),
        grid=(nt,),
        in_specs=[tile(D_MODEL), tile(LANES), tile(D_PLE), ytile(0), ytile(1),
                  full(1, D_MODEL), full(D_MODEL, D_MODEL), full(D_PLE, D_MODEL), full(1, D_MODEL)],
        out_specs=tile(D_MODEL),
        out_shape=jax.ShapeDtypeStruct((t, D_MODEL), F32),
        compiler_params=_cparams("parallel"),
        name="moe_combine_ple",
    )(h1, rt, p2, y, y, gp, wpg, wp, gf)


def _lane_row(vals, offset):
    return jnp.zeros((1, LANES), F32).at[0, offset:offset + vals.shape[0]].set(vals)


def kernel(x, p, norm_mix, w_in, lru_conv_w, lru_conv_b, lru_wa, lru_ba, lru_wi, lru_bi, lru_lambda,
           lru_out_norm, gdn_conv_w, gdn_a_log, gdn_dt_bias, gdn_out_norm, w_out, norm_ffn,
           w_router_group, b_router_group, w_router_expert, b_router_expert, w_exp_gate, w_exp_up,
           w_exp_down, norm_ple, w_ple_gate, w_ple, norm_final):
    bsz, seq, d = x.shape
    t = bsz * seq
    depth = w_in.shape[0]
    assert depth == 1, "the final norm is fused into the last layer's combine kernel"
    n_blocks = -(-t * TOP_K // MOE_BLOCK) + N_EXPERTS
    row = lambda v: v.reshape(1, -1).astype(F32)
    h = x.reshape(t, d).astype(F32)
    for l in range(depth):
        w_in_p = jnp.pad(w_in[l], ((0, 0), (0, PROJ_COLS - w_in.shape[2]))).astype(BF16)
        b_gates = jnp.concatenate([lru_ba[l], lru_bi[l]]).reshape(1, -1)
        proj_gdn, y_lru = _proj_lru(h, row(norm_mix[l]), w_in_p, lru_conv_w[l], row(lru_conv_b[l]), lru_wa[l],
                                    lru_wi[l], b_gates, row(lru_lambda[l]), row(lru_out_norm[l]), seq, 512)
        cw = gdn_conv_w[l].reshape(CONV_W, 3, D_GDN).transpose(1, 0, 2)
        w_r = jnp.pad(jnp.concatenate([w_router_group[l], w_router_expert[l]], axis=1),
                      ((0, 0), (0, LANES - N_GROUPS - N_EXPERTS)))
        w_r_hi = w_r.astype(BF16)
        w_r_lo = (w_r - w_r_hi.astype(F32)).astype(BF16)
        w_r = jnp.concatenate([w_r_hi, w_r_hi, w_r_lo], axis=0)
        b_r = _lane_row(jnp.concatenate([b_router_group[l], b_router_expert[l]]), 0)
        h1, xn2, rt, tile_counts = _gdn_router(
            proj_gdn, cw, _lane_row(gdn_a_log[l], GDN_HEADS), _lane_row(gdn_dt_bias[l], GDN_HEADS),
            row(gdn_out_norm[l]), h, y_lru, w_out[l].astype(BF16), row(norm_ffn[l]), w_r, b_r, seq, 512)
        dest8, blocks = _slots(rt, tile_counts, n_blocks, 1024)
        dest = dest8[:TOP_K]
        block_expert, n_valid = blocks[:, 0], blocks[:, 1]
        tiles = lambda a: a.reshape(-1, ROW_TILES, LANES)
        xs = _sc_scatter_rows(tiles(xn2), dest, n_blocks * MOE_BLOCK)
        ys = _experts(block_expert, n_valid, xs.reshape(-1, LANES), w_exp_gate[l], w_exp_up[l], w_exp_down[l])
        y = _sc_gather_rows(tiles(ys), dest.reshape(-1))
        h = _combine(h1, rt, p[l].reshape(t, -1).astype(F32), y.reshape(-1, LANES), row(norm_ple[l]),
                     w_ple_gate[l].astype(BF16), w_ple[l].astype(BF16), row(norm_final), 512)
    return h.reshape(bsz, seq, d).astype(x.dtype)
```

```python
import functools

import jax
import jax.numpy as jnp
from jax import lax
from jax.experimental import pallas as pl
from jax.experimental.pallas import tpu as pltpu
from jax.experimental.pallas import tpu_sc as plsc

D_MODEL = 1024
D_LRU = 512
LRU_BLOCKS = 8
LRU_BLOCK_W = D_LRU // LRU_BLOCKS
LRU_C = 8.0
D_GDN = 512
GDN_HEADS = 4
GDN_HEAD_DIM = D_GDN // GDN_HEADS
CONV_W = 4
CHUNK = 64
N_GROUPS = 4
EXPERTS_PER_GROUP = 8
N_EXPERTS = N_GROUPS * EXPERTS_PER_GROUP
TOP_K = 2
D_EXPERT = 512
MOE_BLOCK = 512
D_PLE = 256
EPS = 1e-6

LANES = 128
SUBLANES = 8
ROW_TILES = D_MODEL // (2 * LANES)
SC_CORES = 2
SC_SUBCORES = 16
SC_WORKERS = SC_CORES * SC_SUBCORES
SC_WINDOW = 64
PROJ_COLS = 2 * D_LRU + 4 * D_GDN + LANES
VMEM_LIMIT = 56 * 1024 * 1024

BF16 = jnp.bfloat16
F32 = jnp.float32


def _cparams(*sem):
    return pltpu.CompilerParams(dimension_semantics=sem, vmem_limit_bytes=VMEM_LIMIT)


def _rms(x, g):
    return x * lax.rsqrt(jnp.mean(x * x, axis=-1, keepdims=True) + EPS) * g


def _sigmoid(x):
    return 0.5 * jnp.tanh(0.5 * x) + 0.5


def _silu(x):
    return x * _sigmoid(x)


def _dot(a, b):
    return jnp.dot(a.astype(BF16), b.astype(BF16), preferred_element_type=F32)


def _dot_nt(a, b):
    return lax.dot_general(a.astype(BF16), b.astype(BF16), (((1,), (1,)), ((), ())),
                           preferred_element_type=F32)


def _dot_tn(a, b):
    return lax.dot_general(a.astype(BF16), b.astype(BF16), (((0,), (0,)), ((), ())),
                           preferred_element_type=F32)


def _conv_scratch(rows, channels):
    return pltpu.VMEM((channels // LANES, 2 * (SUBLANES + rows), LANES), F32)


def _conv_reset(xp_ref):
    for k in range(xp_ref.shape[0]):
        xp_ref.at[k][pl.ds(0, SUBLANES, stride=2), :] = jnp.zeros((SUBLANES, LANES), F32)


def _causal_conv(xp_ref, x_ref, w, r0, rows, col0=0):
    out = []
    for k in range(xp_ref.shape[0]):
        lanes = slice(k * LANES, (k + 1) * LANES)
        slab = xp_ref.at[k]
        xk = x_ref[r0:r0 + rows, col0 + k * LANES:col0 + (k + 1) * LANES]
        slab[pl.ds(2 * (SUBLANES + r0), rows, stride=2), :] = xk
        acc = xk * w[CONV_W - 1:CONV_W, lanes]
        for j in range(1, CONV_W):
            acc = acc + slab[pl.ds(2 * (SUBLANES + r0 - j), rows, stride=2), :] * w[CONV_W - 1 - j:CONV_W - j, lanes]
        out.append(acc)
    return jnp.concatenate(out, axis=1)


def _conv_carry(xp_ref, x_ref, col0=0):
    ts = x_ref.shape[0]
    for k in range(xp_ref.shape[0]):
        xp_ref.at[k][pl.ds(0, SUBLANES, stride=2), :] = x_ref[ts - SUBLANES:, col0 + k * LANES:col0 + (k + 1) * LANES]


def _interleave(*stages):
    live = list(stages)
    while live:
        for g in list(live):
            try:
                next(g)
            except StopIteration:
                live.remove(g)


def _store_token_rows(ref, x, row0=0):
    rows = x.shape[0]
    bits = lax.bitcast_convert_type(x.astype(BF16).astype(F32), jnp.uint32)
    words = (bits[:, :D_MODEL // 2] >> 16) | bits[:, D_MODEL // 2:]
    for j in range(ROW_TILES):
        ref[pl.ds(row0 * ROW_TILES + j, rows, stride=ROW_TILES), :] = words[:, j * LANES:(j + 1) * LANES]


def _load_token_rows(ref, rows, dtype, row0=0):
    words = [ref[pl.ds(row0 * ROW_TILES + j, rows, stride=ROW_TILES), :] for j in range(ROW_TILES)]
    lo = [lax.bitcast_convert_type(w << 16, F32).astype(dtype) for w in words]
    hi = [lax.bitcast_convert_type(w & jnp.uint32(0xFFFF0000), F32).astype(dtype) for w in words]
    return jnp.concatenate(lo + hi, axis=1)


def _proj_lru_kernel(x_ref, gin_ref, win_ref, cw_ref, cb_ref, wa_ref, wi_ref, bg_ref, lam_ref, og_ref,
                     gdn_ref, o_ref, pj_ref, tail_ref, h_ref, wg_ref, wb_ref, *, tiles_per_seq, col_tile):
    g = pl.program_id(0)
    ts = x_ref.shape[0]
    pj_new = pj_ref.at[g % 2]
    pj = pj_ref.at[(g + 1) % 2]

    @pl.when(g == 0)
    def _():
        pj_ref[...] = jnp.zeros_like(pj_ref)
        in_cols = win_ref.shape[1]
        whole = in_cols // LANES * LANES
        for c0 in range(0, whole, col_tile):
            wb_ref[:, c0:c0 + col_tile] = win_ref[:, c0:c0 + col_tile].astype(BF16)
        wb_ref[:, whole:] = jnp.zeros((D_MODEL, PROJ_COLS - whole), BF16)
        wb_ref[:, whole:in_cols] = win_ref[:, whole:].astype(BF16)
        wg_ref[...] = jnp.zeros_like(wg_ref)
        for n in range(LRU_BLOCKS):
            blk = slice(n * LRU_BLOCK_W, (n + 1) * LRU_BLOCK_W)
            wg_ref[blk, blk] = wa_ref[n].astype(BF16)
            wg_ref[blk, D_LRU + n * LRU_BLOCK_W:D_LRU + (n + 1) * LRU_BLOCK_W] = wi_ref[n].astype(BF16)

    @pl.when((g == 0) | (lax.rem(g - 1, tiles_per_seq) == 0))
    def _():
        _conv_reset(tail_ref)
        h_ref[...] = jnp.zeros_like(h_ref)

    done = []

    def after_projection(v):
        return v + jnp.concatenate([done[max(len(done) - 2, 0)]] * (v.shape[-1] // LANES), axis=-1)

    def project():
        xn = _rms(x_ref[...], gin_ref[...]).astype(BF16)
        yield
        for c0 in range(0, PROJ_COLS, col_tile):
            c1 = min(c0 + col_tile, PROJ_COLS)
            y = jnp.dot(xn, wb_ref[:, c0:c1], preferred_element_type=F32)
            if c1 <= 2 * D_LRU:
                pj_new[:, c0:c1] = y
            else:
                gdn_ref[:, c0 - 2 * D_LRU:c1 - 2 * D_LRU] = y
            bits = lax.bitcast_convert_type(y[ts - 1:, c1 - c0 - LANES:], jnp.uint32)
            done.append(lax.bitcast_convert_type((bits >> 16) >> 16, F32))
            yield

    def recur():
        xc = _causal_conv(tail_ref, pj, cw_ref[...], 0, ts) + cb_ref[...]
        _conv_carry(tail_ref, pj)
        yield
        gates = _sigmoid(_dot(xc, wg_ref[...]) + after_projection(bg_ref[...]))
        r = gates[:, :D_LRU]
        i = gates[:, D_LRU:]
        yield
        log_a = LRU_C * r * jax.nn.log_sigmoid(after_projection(lam_ref[...]))
        a = jnp.exp(log_a)
        th = jnp.tanh(log_a)
        u = jnp.sqrt(-2.0 * th) * lax.rsqrt(1.0 - th) * (i * xc)
        yield
        a = a.reshape(ts // SUBLANES, SUBLANES, D_LRU)
        u = u.reshape(ts // SUBLANES, SUBLANES, D_LRU)
        row = lax.broadcasted_iota(jnp.int32, a.shape, 1)
        d = 1
        while d < SUBLANES:
            keep = row >= d
            a_prev = jnp.where(keep, pltpu.roll(a, d, axis=1), 1.0)
            u_prev = jnp.where(keep, pltpu.roll(u, d, axis=1), 0.0)
            u = a * u_prev + u
            a = a * a_prev
            d *= 2
            yield
        carry = after_projection(h_ref[...])
        groups = []
        for n in range(ts // SUBLANES):
            groups.append(a[n] * carry + u[n])
            carry = groups[-1][SUBLANES - 1:]
        h = jnp.concatenate(groups, axis=0)
        h_ref[...] = carry
        yield
        y = h * jax.nn.gelu(pj[:, D_LRU:])
        o_ref[...] = _rms(y, after_projection(og_ref[...])).astype(o_ref.dtype)

    _interleave(project(), recur())


def _proj_lru(x2, gin, win, cw, cb, wa, wi, bg, lam, og, seq, ts):
    t = x2.shape[0]
    n_tiles = t // ts
    this = lambda g: (jnp.minimum(g, n_tiles - 1), 0)
    last = lambda g: (jnp.maximum(g - 1, 0), 0)
    const = lambda *shape: pl.BlockSpec(shape, lambda g: (0,) * len(shape))
    gdn_cols = PROJ_COLS - 2 * D_LRU
    return pl.pallas_call(
        functools.partial(_proj_lru_kernel, tiles_per_seq=seq // ts, col_tile=D_LRU),
        grid=(n_tiles + 1,),
        in_specs=[pl.BlockSpec((ts, D_MODEL), this), const(1, D_MODEL),
                  pl.BlockSpec(win.shape, lambda g: (0, 0), pipeline_mode=pl.Buffered(1)),
                  const(CONV_W, D_LRU), const(1, D_LRU),
                  const(LRU_BLOCKS, LRU_BLOCK_W, LRU_BLOCK_W), const(LRU_BLOCKS, LRU_BLOCK_W, LRU_BLOCK_W),
                  const(1, 2 * D_LRU), const(1, D_LRU), const(1, D_LRU)],
        out_specs=[pl.BlockSpec((ts, gdn_cols), this), pl.BlockSpec((ts, D_LRU), last)],
        out_shape=[jax.ShapeDtypeStruct((t, gdn_cols), F32), jax.ShapeDtypeStruct((t, D_LRU), BF16)],
        scratch_shapes=[pltpu.VMEM((2, ts, 2 * D_LRU), F32), _conv_scratch(ts, D_LRU), pltpu.VMEM((1, D_LRU), F32),
                        pltpu.VMEM((D_LRU, 2 * D_LRU), BF16), pltpu.VMEM((D_MODEL, PROJ_COLS), BF16)],
        compiler_params=_cparams("arbitrary"),
        name="in_proj_rglru",
    )(x2, gin, win, cw, cb, wa, wi, bg, lam, og)


def _gdn_router_kernel(q_ref, k_ref, v_ref, z_ref, ba_ref, cw_ref, alog_ref, dtb_ref, og_ref,
                       x_ref, yl_ref, wo_ref, gf_ref, wr_ref, br_ref, h_ref, xn_ref, rt_ref, cnt_ref,
                       qt_ref, kt_ref, vt_ref, qs_ref, ks_ref, vs_ref, bs_ref, gc_ref, gct_ref, st_ref, yg_ref,
                       *, tiles_per_seq, group_chunks, prep_rows, sub):
    g = pl.program_id(0)
    ts = q_ref.shape[0]
    dk = GDN_HEAD_DIM
    nc = ts // CHUNK
    n_sub = ts // sub
    yg_new = yg_ref.at[g % 2]
    yg = yg_ref.at[(g + 1) % 2]

    @pl.when(g == 0)
    def _():
        yg_ref[...] = jnp.zeros_like(yg_ref)

    @pl.when(lax.rem(g, tiles_per_seq) == 0)
    def _():
        for tail_ref in (qt_ref, kt_ref, vt_ref):
            _conv_reset(tail_ref)
        st_ref[...] = jnp.zeros_like(st_ref)

    done = []
    hs = {}
    counts = []

    def after_projection(v):
        if not done:
            return v
        return v + jnp.concatenate([done[-1]] * (v.shape[-1] // LANES), axis=-1)

    def project_out():
        for r in range(n_sub):
            rows = slice(r * sub, (r + 1) * sub)
            h = x_ref[rows, :] + jnp.dot(yl_ref[rows, :], wo_ref[:D_LRU, :], preferred_element_type=F32) \
                + jnp.dot(yg[rows, :], wo_ref[D_LRU:, :], preferred_element_type=F32)
            h_ref[rows, :] = h
            hs[r] = h
            bits = lax.bitcast_convert_type(h[sub - 1:, D_MODEL - LANES:], jnp.uint32)
            done.append(lax.bitcast_convert_type((bits >> 16) >> 16, F32))
            yield

    def route(r, h):
        xn = _rms(h, gf_ref[...])
        _store_token_rows(xn_ref, xn, r * sub)
        xh = xn.astype(BF16)
        xl = (xn - xh.astype(F32)).astype(BF16)
        logits = jnp.dot(jnp.concatenate([xh, xl, xh], axis=1), wr_ref[...],
                         preferred_element_type=F32) + br_ref[...]
        lane = lax.broadcasted_iota(jnp.int32, logits.shape, 1).astype(F32)
        big = jnp.float32(2 * LANES)
        ninf = jnp.float32(-jnp.inf)

        def top1(vals):
            m = jnp.max(vals, axis=-1, keepdims=True)
            return m, jnp.min(jnp.where(vals == m, lane, big), axis=-1, keepdims=True)

        gl = jnp.where(lane < N_GROUPS, logits, ninf)
        gmax, gsel = top1(gl)
        p_group = 1.0 / jnp.sum(jnp.exp(gl - gmax), axis=-1, keepdims=True)
        lo = N_GROUPS + EXPERTS_PER_GROUP * gsel
        el = jnp.where((lane >= lo) & (lane < lo + EXPERTS_PER_GROUP), logits, ninf)
        m1, i1 = top1(el)
        m2, i2 = top1(jnp.where(lane == i1, ninf, el))
        rr = jnp.exp(m2 - m1)
        g1 = p_group / (1.0 + rr)
        g2 = p_group * rr / (1.0 + rr)
        rt_ref[r * sub:(r + 1) * sub, :] = jnp.where(
            lane == 0, i1 - N_GROUPS,
            jnp.where(lane == 1, i2 - N_GROUPS, jnp.where(lane == 2, g1, jnp.where(lane == 3, g2, 0.0))))
        member = ((lane == i1 - N_GROUPS) | (lane == i2 - N_GROUPS)).astype(F32)
        counts.append(jnp.sum(member, axis=0, keepdims=True))

    def route_subtiles(subtiles):
        for r in subtiles:
            route(r, hs[r])
            yield

    ri = lax.broadcasted_iota(jnp.int32, (CHUNK, CHUNK), 0)
    ci = lax.broadcasted_iota(jnp.int32, (CHUNK, CHUNK), 1)
    causal = ri >= ci
    strict = ri > ci
    tril = causal.astype(F32)
    eye = (ri == ci).astype(F32)
    og = og_ref[...]

    def l2n(x, scale):
        parts = []
        for h in range(GDN_HEADS):
            xh = x[:, h * dk:(h + 1) * dk]
            parts.append(xh * (lax.rsqrt(jnp.sum(xh * xh, axis=-1, keepdims=True) + EPS) * scale))
        return jnp.concatenate(parts, axis=1)

    def prepare(r0, r1):
        for p0 in range(r0, r1, prep_rows):
            rows = slice(p0, p0 + prep_rows)
            conv = lambda x_ref, tail_ref, part: _silu(
                _causal_conv(tail_ref, x_ref, after_projection(cw_ref[part]), p0, prep_rows))
            qs_ref[rows, :] = l2n(conv(q_ref, qt_ref, 0), dk ** -0.5)
            yield
            ks_ref[rows, :] = l2n(conv(k_ref, kt_ref, 1), 1.0)
            yield
            vs_ref[rows, :] = conv(v_ref, vt_ref, 2)
            ba = ba_ref[rows, :]
            bs_ref[rows, :] = _sigmoid(ba)
            g = -jnp.exp(alog_ref[...]) * jax.nn.softplus(ba + dtb_ref[...])
            for c0 in range(0, prep_rows, CHUNK):
                gc_ref[p0 + c0:p0 + c0 + CHUNK, :] = jnp.dot(tril, g[c0:c0 + CHUNK], precision=lax.Precision.HIGHEST,
                                                             preferred_element_type=F32)
            gct_ref[:, rows] = gc_ref[rows, :].T
            yield

    terms = {}

    def chunk_terms(pairs):
        n = range(len(pairs))
        rows = [slice(c * CHUNK, (c + 1) * CHUNK) for c, _ in pairs]
        cols = [slice(h * dk, (h + 1) * dk) for _, h in pairs]
        gl = [GDN_HEADS + h for _, h in pairs]
        kh = [ks_ref[rows[i], cols[i]] for i in n]
        kb = [kh[i] * bs_ref[rows[i], pairs[i][1]:pairs[i][1] + 1] for i in n]
        r = [_dot_nt(jnp.concatenate([kb[i], qs_ref[rows[i], cols[i]]], axis=0), kh[i]) for i in n]
        yield
        gcol = [gc_ref[rows[i], gl[i]:gl[i] + 1] for i in n]
        decay = []
        for i in n:
            diff = gcol[i] - gct_ref[gl[i]:gl[i] + 1, rows[i]]
            decay.append(jnp.where(causal, jnp.exp(jnp.where(causal, diff, 0.0)), 0.0))
        a = [jnp.where(strict, r[i][:CHUNK] * decay[i], 0.0) for i in n]
        qk = [(r[i][CHUNK:] * decay[i]).astype(BF16) for i in n]
        tinv = [eye - a[i] for i in n]
        p = 2
        while p < CHUNK:
            a = [_dot(a[i], a[i]) for i in n]
            yield
            tinv = [tinv[i] + _dot(tinv[i], a[i]) for i in n]
            yield
            p *= 2
        eg = [jnp.exp(gcol[i]) for i in n]
        rhs = [jnp.concatenate([vs_ref[rows[i], cols[i]] * bs_ref[rows[i], pairs[i][1]:pairs[i][1] + 1],
                                kb[i] * eg[i]], axis=1) for i in n]
        uw = [_dot(tinv[i], rhs[i]).astype(BF16) for i in n]
        yield
        qk_uw = [jnp.dot(qk[i], uw[i], preferred_element_type=F32) for i in n]
        glast = [gc_ref[(c + 1) * CHUNK - 1:(c + 1) * CHUNK, gl[i]:gl[i] + 1] for i, (c, _) in enumerate(pairs)]
        kd_uw = [_dot_tn(kh[i] * jnp.exp(glast[i] - gcol[i]), uw[i]) for i in n]
        yield
        for i in n:
            lhs = jnp.concatenate([kd_uw[i][:, dk:], qs_ref[rows[i], cols[i]] * eg[i] - qk_uw[i][:, dk:]],
                                  axis=0).astype(BF16)
            terms[pairs[i]] = (lhs, kd_uw[i][:, :dk], qk_uw[i][:, :dk], jnp.exp(glast[i]))

    state = [st_ref[h] for h in range(GDN_HEADS)]

    def advance(chunks):
        for c in chunks:
            rows = slice(c * CHUNK, (c + 1) * CHUNK)
            r = [jnp.dot(terms[c, h][0], state[h].astype(BF16), preferred_element_type=F32)
                 for h in range(GDN_HEADS)]
            for h in range(GDN_HEADS):
                cols = slice(h * dk, (h + 1) * dk)
                _, c_add, o_add, egl = terms[c, h]
                o = r[h][dk:] + o_add
                state[h] = egl * state[h] - r[h][:dk] + c_add
                yg_new[rows, cols] = (_rms(o, og) * _silu(z_ref[rows, cols])).astype(yg_new.dtype)
            yield

    groups = [range(c0, c0 + group_chunks) for c0 in range(0, nc, group_chunks)]
    pairs_of = lambda chunks: [(c, h) for c in chunks for h in range(GDN_HEADS)]
    span = lambda chunks: (chunks[0] * CHUNK, (chunks[-1] + 1) * CHUNK)
    per_group = -(-n_sub // len(groups))
    _interleave(prepare(*span(groups[0])), project_out())
    for i, chunks in enumerate(groups):
        side = [route_subtiles(range(i * per_group, min((i + 1) * per_group, n_sub)))]
        if i + 1 < len(groups):
            side.append(prepare(*span(groups[i + 1])))
        if i > 0:
            side.append(advance(groups[i - 1]))
        _interleave(chunk_terms(pairs_of(chunks)), *side)
    _interleave(advance(groups[-1]))
    for h in range(GDN_HEADS):
        st_ref[h] = state[h]
    for x_in, tail_ref in ((q_ref, qt_ref), (k_ref, kt_ref), (v_ref, vt_ref)):
        _conv_carry(tail_ref, x_in)
    cnt_ref[...] = jnp.broadcast_to(sum(counts), cnt_ref.shape)


def _gdn_router(proj, cw, alog, dtb, og, x2, yl, wo, gf, wr, br, seq, ts):
    t = x2.shape[0]
    n_tiles = t // ts
    this = lambda c: (lambda g: (jnp.minimum(g, n_tiles - 1), c))
    last = lambda g: (jnp.maximum(g - 1, 0), 0)
    const = lambda *shape: pl.BlockSpec(shape, lambda g: (0,) * len(shape))
    col = lambda c: pl.BlockSpec((ts, D_GDN), this(c))
    return pl.pallas_call(
        functools.partial(_gdn_router_kernel, tiles_per_seq=seq // ts, group_chunks=4, prep_rows=128, sub=128),
        grid=(n_tiles + 1,),
        in_specs=[col(0), col(1), col(2), col(3), pl.BlockSpec((ts, LANES), this(4 * D_GDN // LANES)),
                  const(3, CONV_W, D_GDN), const(1, LANES), const(1, LANES), const(1, GDN_HEAD_DIM),
                  pl.BlockSpec((ts, D_MODEL), last), pl.BlockSpec((ts, D_LRU), last), const(D_MODEL, D_MODEL),
                  const(1, D_MODEL), const(3 * D_MODEL, LANES), const(1, LANES)],
        out_specs=[pl.BlockSpec((ts, D_MODEL), last), pl.BlockSpec((ts * ROW_TILES, LANES), last),
                   pl.BlockSpec((ts, LANES), last), pl.BlockSpec((SUBLANES, LANES), last)],
        out_shape=[jax.ShapeDtypeStruct((t, D_MODEL), F32),
                   jax.ShapeDtypeStruct((t * ROW_TILES, LANES), jnp.uint32),
                   jax.ShapeDtypeStruct((t, LANES), F32),
                   jax.ShapeDtypeStruct((n_tiles * SUBLANES, LANES), F32)],
        scratch_shapes=[_conv_scratch(ts, D_GDN)] * 3
        + [pltpu.VMEM((ts, D_GDN), F32)] * 3
        + [pltpu.VMEM((ts, LANES), F32)] * 2
        + [pltpu.VMEM((LANES, ts), F32)]
        + [pltpu.VMEM((GDN_HEADS, GDN_HEAD_DIM, GDN_HEAD_DIM), F32)]
        + [pltpu.VMEM((2, ts, D_GDN), BF16)],
        compiler_params=_cparams("arbitrary"),
        name="gdn_out_router",
    )(proj, proj, proj, proj, proj, cw, alog, dtb, og, x2, yl, wo, gf, wr, br)


def _slots_kernel(rt_ref, tc_ref, dest_ref, blocks_ref, cnt_ref, before_ref):
    i = pl.program_id(0)
    tm = rt_ref.shape[0]
    rt = rt_ref[...]
    lane = lax.broadcasted_iota(jnp.int32, rt.shape, 1)
    e0 = rt[:, 0:1].astype(jnp.int32)
    e1 = rt[:, 1:2].astype(jnp.int32)
    member = ((lane == e0) | (lane == e1)).astype(BF16)

    @pl.when(i == 0)
    def _():
        ri = lax.broadcasted_iota(jnp.int32, (tm, tm), 0)
        ci = lax.broadcasted_iota(jnp.int32, (tm, tm), 1)
        before_ref[...] = (ri > ci).astype(BF16)
        cnt = jnp.broadcast_to(jnp.sum(tc_ref[...], axis=0, keepdims=True) / SUBLANES, cnt_ref.shape)
        padded = jnp.ceil(cnt / MOE_BLOCK) * MOE_BLOCK
        l8 = lax.broadcasted_iota(jnp.int32, cnt.shape, 1)
        incl = padded
        d = 1
        while d < LANES:
            incl = incl + jnp.where(l8 >= d, pltpu.roll(incl, d, axis=1), 0.0)
            d *= 2
        cnt_ref[...] = incl - padded
        seg_start = (incl - padded)[0:1, :]
        seg_end = incl[0:1, :]
        real_end = seg_start + cnt[0:1, :]
        bl = lax.broadcasted_iota(jnp.int32, blocks_ref.shape, 1)
        b0 = (lax.broadcasted_iota(jnp.int32, blocks_ref.shape, 0) * MOE_BLOCK).astype(F32)
        is_expert = bl < N_EXPERTS
        expert = jnp.sum(jnp.where(is_expert & (seg_end <= b0), 1.0, 0.0), axis=-1, keepdims=True)
        real = jnp.maximum(jnp.minimum(real_end, b0 + MOE_BLOCK) - jnp.maximum(seg_start, b0), 0.0)
        n_real = jnp.sum(jnp.where(is_expert, real, 0.0), axis=-1, keepdims=True)
        blocks_ref[...] = jnp.where(bl == 0, jnp.minimum(expert, N_EXPERTS - 1.0),
                                    jnp.where(bl == 1, n_real, 0.0)).astype(jnp.int32)

    start = cnt_ref[...]
    pos = start[0:1, :] + jnp.dot(before_ref[...], member, preferred_element_type=F32)
    d0 = jnp.sum(jnp.where(lane == e0, pos, 0.0), axis=-1, keepdims=True)
    d1 = jnp.sum(jnp.where(lane == e1, pos, 0.0), axis=-1, keepdims=True)
    dest = jnp.where(lane == 0, d0, jnp.where(lane == 1, d1, 0.0))
    dest_ref[...] = dest.T[:SUBLANES].astype(jnp.int32)
    cnt_ref[...] = start + jnp.dot(jnp.ones((SUBLANES, tm), BF16), member, preferred_element_type=F32)


def _slots(rt, tile_counts, n_blocks, tm):
    t = rt.shape[0]
    return pl.pallas_call(
        _slots_kernel,
        grid=(t // tm,),
        in_specs=[pl.BlockSpec((tm, LANES), lambda i: (i, 0)),
                  pl.BlockSpec(tile_counts.shape, lambda i: (0, 0))],
        out_specs=[pl.BlockSpec((SUBLANES, tm), lambda i: (0, i)),
                   pl.BlockSpec((n_blocks, LANES), lambda i: (0, 0))],
        out_shape=[jax.ShapeDtypeStruct((SUBLANES, t), jnp.int32),
                   jax.ShapeDtypeStruct((n_blocks, LANES), jnp.int32)],
        scratch_shapes=[pltpu.VMEM((SUBLANES, LANES), F32), pltpu.VMEM((tm, tm), BF16)],
        compiler_params=_cparams("arbitrary"),
        name="moe_slots",
    )(rt, tile_counts)


def _sc_mesh():
    return plsc.VectorSubcoreMesh(core_axis_name="c", subcore_axis_name="s")


def _sc_worker():
    return lax.axis_index("s") * SC_CORES + lax.axis_index("c")


def _sc_scatter_rows(src, idx, n_rows):
    t = src.shape[0]
    per_w = t // SC_WORKERS
    n_win = per_w // SC_WINDOW
    idx = idx.reshape(TOP_K, SC_WORKERS, n_win, SC_WINDOW)

    @functools.partial(
        pl.kernel, mesh=_sc_mesh(),
        out_type=jax.ShapeDtypeStruct((n_rows,) + src.shape[1:], src.dtype),
        scratch_types=[pltpu.VMEM((TOP_K, n_win, SC_WINDOW), jnp.int32),
                       pltpu.VMEM((2, SC_WINDOW) + src.shape[1:], src.dtype),
                       pltpu.SemaphoreType.DMA((2,)), pltpu.SemaphoreType.DMA((2,))],
        compiler_params=pltpu.CompilerParams(use_tc_tiling_on_sc=True),
        name="sc_dispatch")
    def scatter(src_hbm, idx_hbm, out_hbm, idx_v, rows_v, lsem, ssem):
        wid = _sc_worker()
        base = wid * per_w
        for k in range(TOP_K):
            pltpu.sync_copy(idx_hbm.at[k, wid], idx_v.at[k])

        def load(w, slot):
            return pltpu.make_async_copy(src_hbm.at[pl.ds(base + w * SC_WINDOW, SC_WINDOW)], rows_v.at[slot],
                                         lsem.at[slot])

        def put(w, slot, k):
            return pltpu.make_async_copy(rows_v.at[slot], out_hbm.at[idx_v.at[k, w]], ssem.at[slot])

        load(0, 0).start()

        @pl.loop(0, n_win, step=2)
        def _(w0):
            for s in range(2):
                w = w0 + s

                @pl.when(w + 1 < n_win)
                def _():
                    @pl.when(w >= 1)
                    def _():
                        for k in range(TOP_K):
                            put(w - 1, 1 - s, k).wait()

                    load(w + 1, 1 - s).start()

                load(w, s).wait()
                for k in range(TOP_K):
                    put(w, s, k).start()

        for s in range(2):
            for k in range(TOP_K):
                put(n_win - 2 + s, s, k).wait()

    return scatter(src, idx)


def _sc_gather_rows(table, idx):
    b = idx.shape[0]
    per_w = b // SC_WORKERS
    n_win = per_w // SC_WINDOW
    idx = idx.reshape(SC_WORKERS, n_win, SC_WINDOW)

    @functools.partial(
        pl.kernel, mesh=_sc_mesh(),
        out_type=jax.ShapeDtypeStruct((b,) + table.shape[1:], table.dtype),
        scratch_types=[pltpu.VMEM((n_win, SC_WINDOW), jnp.int32),
                       pltpu.VMEM((2, SC_WINDOW) + table.shape[1:], table.dtype),
                       pltpu.SemaphoreType.DMA((2,)), pltpu.SemaphoreType.DMA((2,))],
        compiler_params=pltpu.CompilerParams(use_tc_tiling_on_sc=True),
        name="sc_combine_gather")
    def gather(table_hbm, idx_hbm, out_hbm, idx_v, rows_v, gsem, psem):
        wid = _sc_worker()
        base = wid * per_w
        pltpu.sync_copy(idx_hbm.at[wid], idx_v)

        def get(w, slot):
            return pltpu.make_async_copy(table_hbm.at[idx_v.at[w]], rows_v.at[slot], gsem.at[slot])

        def put(w, slot):
            return pltpu.make_async_copy(rows_v.at[slot], out_hbm.at[pl.ds(base + w * SC_WINDOW, SC_WINDOW)],
                                         psem.at[slot])

        get(0, 0).start()

        @pl.loop(0, n_win, step=2)
        def _(w0):
            for s in range(2):
                w = w0 + s

                @pl.when(w + 1 < n_win)
                def _():
                    @pl.when(w >= 1)
                    def _():
                        put(w - 1, 1 - s).wait()

                    get(w + 1, 1 - s).start()

                get(w, s).wait()
                put(w, s).start()

        for s in range(2):
            put(n_win - 2 + s, s).wait()

    return gather(table, idx)


def _experts_kernel(be_ref, nv_ref, xs_ref, wg_ref, wu_ref, wd_ref, ys_ref, wgb_ref, wub_ref, wdb_ref):
    b = pl.program_id(0)
    prev = be_ref[jnp.maximum(b - 1, 0)]

    @pl.when((b == 0) | (be_ref[b] != prev))
    def _():
        wgb_ref[...] = wg_ref[0].astype(BF16)
        wub_ref[...] = wu_ref[0].astype(BF16)
        wdb_ref[...] = wd_ref[0].astype(BF16)

    @pl.when(nv_ref[b] > 0)
    def _():
        row = lax.broadcasted_iota(jnp.int32, (MOE_BLOCK, D_MODEL), 0)
        xb = jnp.where(row < nv_ref[b], _load_token_rows(xs_ref, MOE_BLOCK, BF16), 0.0)
        hg = jnp.dot(xb, wgb_ref[...], preferred_element_type=F32)
        hu = jnp.dot(xb, wub_ref[...], preferred_element_type=F32)
        hb = (_silu(hg) * hu).astype(BF16)
        _store_token_rows(ys_ref, jnp.dot(hb, wdb_ref[...], preferred_element_type=F32))

    @pl.when(nv_ref[b] == 0)
    def _():
        ys_ref[...] = jnp.zeros_like(ys_ref)


def _experts(block_expert, n_valid, xs, wg, wu, wd):
    nb = xs.shape[0] // (MOE_BLOCK * ROW_TILES)
    blk = pl.BlockSpec((MOE_BLOCK * ROW_TILES, LANES), lambda b, be, nv: (b, 0))
    wspec = lambda a, c: pl.BlockSpec((1, a, c), lambda b, be, nv: (be[b], 0, 0))
    return pl.pallas_call(
        _experts_kernel,
        grid_spec=pltpu.PrefetchScalarGridSpec(
            num_scalar_prefetch=2,
            grid=(nb,),
            in_specs=[blk, wspec(D_MODEL, D_EXPERT), wspec(D_MODEL, D_EXPERT), wspec(D_EXPERT, D_MODEL)],
            out_specs=blk,
            scratch_shapes=[pltpu.VMEM((D_MODEL, D_EXPERT), BF16), pltpu.VMEM((D_MODEL, D_EXPERT), BF16),
                            pltpu.VMEM((D_EXPERT, D_MODEL), BF16)]),
        out_shape=jax.ShapeDtypeStruct(xs.shape, xs.dtype),
        compiler_params=_cparams("arbitrary"),
        name="moe_experts",
    )(block_expert, n_valid, xs, wg, wu, wd)


def _combine_kernel(h_ref, rt_ref, p_ref, y0_ref, y1_ref, gp_ref, wpg_ref, wp_ref, gf_ref, o_ref, *, sub):
    n_sub = h_ref.shape[0] // sub

    def residual(r):
        rows = slice(r * sub, (r + 1) * sub)
        rt = rt_ref[rows, :]
        h = h_ref[rows, :] + (_load_token_rows(y0_ref, sub, F32, r * sub) * rt[:, 2:3]
                              + _load_token_rows(y1_ref, sub, F32, r * sub) * rt[:, 3:4])
        return h, _rms(h, gp_ref[...]).astype(BF16)

    def products(r, xn):
        return (jnp.dot(xn, wpg_ref[...], preferred_element_type=F32),
                _dot(p_ref[r * sub:(r + 1) * sub, :], wp_ref[...]))

    def finish(r, h, gate_lin, ple):
        h = h + ple * _sigmoid(gate_lin)
        o_ref[r * sub:(r + 1) * sub, :] = _rms(h, gf_ref[...])

    h, xn = residual(0)
    for r in range(n_sub):
        gate_lin, ple = products(r, xn)
        if r + 1 < n_sub:
            h_next, xn = residual(r + 1)
        finish(r, h, gate_lin, ple)
        if r + 1 < n_sub:
            h = h_next


def _combine(h1, rt, p2, y, gp, wpg, wp, gf, tm):
    t = h1.shape[0]
    nt = t // tm
    tile = lambda n: pl.BlockSpec((tm, n), lambda i: (i, 0))
    full = lambda a, b: pl.BlockSpec((a, b), lambda i: (0, 0))
    ytile = lambda k: pl.BlockSpec((tm * ROW_TILES, LANES), lambda i: (i + k * nt, 0))
    return pl.pallas_call(
        functools.partial(_combine_kernel, sub=256),
        grid=(nt,),
        in_specs=[tile(D_MODEL), tile(LANES), tile(D_PLE), ytile(0), ytile(1),
                  full(1, D_MODEL), full(D_MODEL, D_MODEL), full(D_PLE, D_MODEL), full(1, D_MODEL)],
        out_specs=tile(D_MODEL),
        out_shape=jax.ShapeDtypeStruct((t, D_MODEL), F32),
        compiler_params=_cparams("parallel"),
        name="moe_combine_ple",
    )(h1, rt, p2, y, y, gp, wpg, wp, gf)


def _lane_row(vals, offset):
    return jnp.zeros((1, LANES), F32).at[0, offset:offset + vals.shape[0]].set(vals)


def kernel(x, p, norm_mix, w_in, lru_conv_w, lru_conv_b, lru_wa, lru_ba, lru_wi, lru_bi, lru_lambda,
           lru_out_norm, gdn_conv_w, gdn_a_log, gdn_dt_bias, gdn_out_norm, w_out, norm_ffn,
           w_router_group, b_router_group, w_router_expert, b_router_expert, w_exp_gate, w_exp_up,
           w_exp_down, norm_ple, w_ple_gate, w_ple, norm_final):
    bsz, seq, d = x.shape
    t = bsz * seq
    depth = w_in.shape[0]
    assert depth == 1, "the final norm is fused into the last layer's combine kernel"
    n_blocks = -(-t * TOP_K // MOE_BLOCK) + N_EXPERTS
    row = lambda v: v.reshape(1, -1).astype(F32)
    h = x.reshape(t, d).astype(F32)
    for l in range(depth):
        b_gates = jnp.concatenate([lru_ba[l], lru_bi[l]]).reshape(1, -1)
        proj_gdn, y_lru = _proj_lru(h, row(norm_mix[l]), w_in[l], lru_conv_w[l], row(lru_conv_b[l]), lru_wa[l],
                                    lru_wi[l], b_gates, row(lru_lambda[l]), row(lru_out_norm[l]), seq, 512)
        cw = gdn_conv_w[l].reshape(CONV_W, 3, D_GDN).transpose(1, 0, 2)
        w_r = jnp.pad(jnp.concatenate([w_router_group[l], w_router_expert[l]], axis=1),
                      ((0, 0), (0, LANES - N_GROUPS - N_EXPERTS)))
        w_r_hi = w_r.astype(BF16)
        w_r_lo = (w_r - w_r_hi.astype(F32)).astype(BF16)
        w_r = jnp.concatenate([w_r_hi, w_r_hi, w_r_lo], axis=0)
        b_r = _lane_row(jnp.concatenate([b_router_group[l], b_router_expert[l]]), 0)
        h1, xn2, rt, tile_counts = _gdn_router(
            proj_gdn, cw, _lane_row(gdn_a_log[l], GDN_HEADS), _lane_row(gdn_dt_bias[l], GDN_HEADS),
            row(gdn_out_norm[l]), h, y_lru, w_out[l].astype(BF16), row(norm_ffn[l]), w_r, b_r, seq, 512)
        dest8, blocks = _slots(rt, tile_counts, n_blocks, 1024)
        dest = dest8[:TOP_K]
        block_expert, n_valid = blocks[:, 0], blocks[:, 1]
        tiles = lambda a: a.reshape(-1, ROW_TILES, LANES)
        xs = _sc_scatter_rows(tiles(xn2), dest, n_blocks * MOE_BLOCK)
        ys = _experts(block_expert, n_valid, xs.reshape(-1, LANES), w_exp_gate[l], w_exp_up[l], w_exp_down[l])
        y = _sc_gather_rows(tiles(ys), dest.reshape(-1))
        h = _combine(h1, rt, p[l].reshape(t, -1).astype(F32), y.reshape(-1, LANES), row(norm_ple[l]),
                     w_ple_gate[l].astype(BF16), w_ple[l].astype(BF16), row(norm_final), 512)
    return h.reshape(bsz, seq, d).astype(x.dtype)
```

```python
import functools

import jax
import jax.numpy as jnp
from jax import lax
from jax.experimental import pallas as pl
from jax.experimental.pallas import tpu as pltpu
from jax.experimental.pallas import tpu_sc as plsc

D_MODEL = 1024
D_LRU = 512
LRU_BLOCKS = 8
LRU_BLOCK_W = D_LRU // LRU_BLOCKS
LRU_C = 8.0
D_GDN = 512
GDN_HEADS = 4
GDN_HEAD_DIM = D_GDN // GDN_HEADS
CONV_W = 4
CHUNK = 64
N_GROUPS = 4
EXPERTS_PER_GROUP = 8
N_EXPERTS = N_GROUPS * EXPERTS_PER_GROUP
TOP_K = 2
D_EXPERT = 512
MOE_BLOCK = 512
D_PLE = 256
EPS = 1e-6

LANES = 128
SUBLANES = 8
ROW_TILES = D_MODEL // (2 * LANES)
SC_CORES = 2
SC_SUBCORES = 16
SC_WORKERS = SC_CORES * SC_SUBCORES
SC_WINDOW = 64
PROJ_COLS = 2 * D_LRU + 4 * D_GDN + LANES
VMEM_LIMIT = 56 * 1024 * 1024

BF16 = jnp.bfloat16
F32 = jnp.float32


def _cparams(*sem):
    return pltpu.CompilerParams(dimension_semantics=sem, vmem_limit_bytes=VMEM_LIMIT)


def _rms(x, g):
    return x * lax.rsqrt(jnp.mean(x * x, axis=-1, keepdims=True) + EPS) * g


def _sigmoid(x):
    return 0.5 * jnp.tanh(0.5 * x) + 0.5


def _silu(x):
    return x * _sigmoid(x)


def _dot(a, b):
    return jnp.dot(a.astype(BF16), b.astype(BF16), preferred_element_type=F32)


def _dot_nt(a, b):
    return lax.dot_general(a.astype(BF16), b.astype(BF16), (((1,), (1,)), ((), ())),
                           preferred_element_type=F32)


def _dot_tn(a, b):
    return lax.dot_general(a.astype(BF16), b.astype(BF16), (((0,), (0,)), ((), ())),
                           preferred_element_type=F32)


def _conv_scratch(rows, channels):
    return pltpu.VMEM((channels // LANES, 2 * (SUBLANES + rows), LANES), F32)


def _conv_reset(xp_ref):
    for k in range(xp_ref.shape[0]):
        xp_ref.at[k][pl.ds(0, SUBLANES, stride=2), :] = jnp.zeros((SUBLANES, LANES), F32)


def _causal_conv(xp_ref, x_ref, w, r0, rows, col0=0):
    out = []
    for k in range(xp_ref.shape[0]):
        lanes = slice(k * LANES, (k + 1) * LANES)
        slab = xp_ref.at[k]
        xk = x_ref[r0:r0 + rows, col0 + k * LANES:col0 + (k + 1) * LANES]
        slab[pl.ds(2 * (SUBLANES + r0), rows, stride=2), :] = xk
        acc = xk * w[CONV_W - 1:CONV_W, lanes]
        for j in range(1, CONV_W):
            acc = acc + slab[pl.ds(2 * (SUBLANES + r0 - j), rows, stride=2), :] * w[CONV_W - 1 - j:CONV_W - j, lanes]
        out.append(acc)
    return jnp.concatenate(out, axis=1)


def _conv_carry(xp_ref, x_ref, col0=0):
    ts = x_ref.shape[0]
    for k in range(xp_ref.shape[0]):
        xp_ref.at[k][pl.ds(0, SUBLANES, stride=2), :] = x_ref[ts - SUBLANES:, col0 + k * LANES:col0 + (k + 1) * LANES]


def _interleave(*stages):
    live = list(stages)
    while live:
        for g in list(live):
            try:
                next(g)
            except StopIteration:
                live.remove(g)


def _store_token_rows(ref, x, row0=0):
    rows = x.shape[0]
    bits = lax.bitcast_convert_type(x.astype(BF16).astype(F32), jnp.uint32)
    words = (bits[:, :D_MODEL // 2] >> 16) | bits[:, D_MODEL // 2:]
    for j in range(ROW_TILES):
        ref[pl.ds(row0 * ROW_TILES + j, rows, stride=ROW_TILES), :] = words[:, j * LANES:(j + 1) * LANES]


def _load_token_rows(ref, rows, dtype, row0=0):
    words = [ref[pl.ds(row0 * ROW_TILES + j, rows, stride=ROW_TILES), :] for j in range(ROW_TILES)]
    lo = [lax.bitcast_convert_type(w << 16, F32).astype(dtype) for w in words]
    hi = [lax.bitcast_convert_type(w & jnp.uint32(0xFFFF0000), F32).astype(dtype) for w in words]
    return jnp.concatenate(lo + hi, axis=1)


def _proj_lru_kernel(x_ref, gin_ref, win_ref, cw_ref, cb_ref, wa_ref, wi_ref, bg_ref, lam_ref, og_ref,
                     gdn_ref, o_ref, pj_ref, tail_ref, h_ref, wg_ref, wb_ref, *, tiles_per_seq, col_tile):
    g = pl.program_id(0)
    ts = x_ref.shape[0]
    pj_new = pj_ref.at[g % 2]
    pj = pj_ref.at[(g + 1) % 2]

    @pl.when(g == 0)
    def _():
        pj_ref[...] = jnp.zeros_like(pj_ref)
        in_cols = win_ref.shape[1]
        whole = in_cols // LANES * LANES
        for c0 in range(0, whole, col_tile):
            wb_ref[:, c0:c0 + col_tile] = win_ref[:, c0:c0 + col_tile].astype(BF16)
        wb_ref[:, whole:] = jnp.zeros((D_MODEL, PROJ_COLS - whole), BF16)
        wb_ref[:, whole:in_cols] = win_ref[:, whole:].astype(BF16)
        wg_ref[...] = jnp.zeros_like(wg_ref)
        for n in range(LRU_BLOCKS):
            blk = slice(n * LRU_BLOCK_W, (n + 1) * LRU_BLOCK_W)
            wg_ref[blk, blk] = wa_ref[n].astype(BF16)
            wg_ref[blk, D_LRU + n * LRU_BLOCK_W:D_LRU + (n + 1) * LRU_BLOCK_W] = wi_ref[n].astype(BF16)

    @pl.when((g == 0) | (lax.rem(g - 1, tiles_per_seq) == 0))
    def _():
        _conv_reset(tail_ref)
        h_ref[...] = jnp.zeros_like(h_ref)

    done = []

    def after_projection(v):
        return v + jnp.concatenate([done[max(len(done) - 2, 0)]] * (v.shape[-1] // LANES), axis=-1)

    def project():
        xn = _rms(x_ref[...], gin_ref[...]).astype(BF16)
        yield
        for c0 in range(0, PROJ_COLS, col_tile):
            c1 = min(c0 + col_tile, PROJ_COLS)
            y = jnp.dot(xn, wb_ref[:, c0:c1], preferred_element_type=F32)
            if c1 <= 2 * D_LRU:
                pj_new[:, c0:c1] = y
            else:
                gdn_ref[:, c0 - 2 * D_LRU:c1 - 2 * D_LRU] = y
            bits = lax.bitcast_convert_type(y[ts - 1:, c1 - c0 - LANES:], jnp.uint32)
            done.append(lax.bitcast_convert_type((bits >> 16) >> 16, F32))
            yield

    def recur():
        xc = _causal_conv(tail_ref, pj, cw_ref[...], 0, ts) + cb_ref[...]
        _conv_carry(tail_ref, pj)
        yield
        gates = _sigmoid(_dot(xc, wg_ref[...]) + after_projection(bg_ref[...]))
        r = gates[:, :D_LRU]
        i = gates[:, D_LRU:]
        yield
        log_a = LRU_C * r * jax.nn.log_sigmoid(after_projection(lam_ref[...]))
        a = jnp.exp(log_a)
        th = jnp.tanh(log_a)
        u = jnp.sqrt(-2.0 * th) * lax.rsqrt(1.0 - th) * (i * xc)
        yield
        a = a.reshape(ts // SUBLANES, SUBLANES, D_LRU)
        u = u.reshape(ts // SUBLANES, SUBLANES, D_LRU)
        row = lax.broadcasted_iota(jnp.int32, a.shape, 1)
        d = 1
        while d < SUBLANES:
            keep = row >= d
            a_prev = jnp.where(keep, pltpu.roll(a, d, axis=1), 1.0)
            u_prev = jnp.where(keep, pltpu.roll(u, d, axis=1), 0.0)
            u = a * u_prev + u
            a = a * a_prev
            d *= 2
            yield
        carry = after_projection(h_ref[...])
        groups = []
        for n in range(ts // SUBLANES):
            groups.append(a[n] * carry + u[n])
            carry = groups[-1][SUBLANES - 1:]
        h = jnp.concatenate(groups, axis=0)
        h_ref[...] = carry
        yield
        y = h * jax.nn.gelu(pj[:, D_LRU:])
        o_ref[...] = _rms(y, after_projection(og_ref[...])).astype(o_ref.dtype)

    _interleave(project(), recur())


def _proj_lru(x2, gin, win, cw, cb, wa, wi, bg, lam, og, seq, ts):
    t = x2.shape[0]
    n_tiles = t // ts
    this = lambda g: (jnp.minimum(g, n_tiles - 1), 0)
    last = lambda g: (jnp.maximum(g - 1, 0), 0)
    const = lambda *shape: pl.BlockSpec(shape, lambda g: (0,) * len(shape))
    gdn_cols = PROJ_COLS - 2 * D_LRU
    return pl.pallas_call(
        functools.partial(_proj_lru_kernel, tiles_per_seq=seq // ts, col_tile=D_LRU),
        grid=(n_tiles + 1,),
        in_specs=[pl.BlockSpec((ts, D_MODEL), this), const(1, D_MODEL),
                  pl.BlockSpec(win.shape, lambda g: (0, 0), pipeline_mode=pl.Buffered(1)),
                  const(CONV_W, D_LRU), const(1, D_LRU),
                  const(LRU_BLOCKS, LRU_BLOCK_W, LRU_BLOCK_W), const(LRU_BLOCKS, LRU_BLOCK_W, LRU_BLOCK_W),
                  const(1, 2 * D_LRU), const(1, D_LRU), const(1, D_LRU)],
        out_specs=[pl.BlockSpec((ts, gdn_cols), this), pl.BlockSpec((ts, D_LRU), last)],
        out_shape=[jax.ShapeDtypeStruct((t, gdn_cols), F32), jax.ShapeDtypeStruct((t, D_LRU), BF16)],
        scratch_shapes=[pltpu.VMEM((2, ts, 2 * D_LRU), F32), _conv_scratch(ts, D_LRU), pltpu.VMEM((1, D_LRU), F32),
                        pltpu.VMEM((D_LRU, 2 * D_LRU), BF16), pltpu.VMEM((D_MODEL, PROJ_COLS), BF16)],
        compiler_params=_cparams("arbitrary"),
        name="in_proj_rglru",
    )(x2, gin, win, cw, cb, wa, wi, bg, lam, og)


def _gdn_router_kernel(q_ref, k_ref, v_ref, z_ref, ba_ref, cw_ref, alog_ref, dtb_ref, og_ref,
                       x_ref, yl_ref, wo_ref, gf_ref, wr_ref, br_ref, h_ref, xn_ref, rt_ref, cnt_ref,
                       qt_ref, kt_ref, vt_ref, qs_ref, ks_ref, vs_ref, bs_ref, gc_ref, gct_ref, st_ref, yg_ref,
                       *, tiles_per_seq, group_chunks, prep_rows, sub):
    g = pl.program_id(0)
    ts = q_ref.shape[0]
    dk = GDN_HEAD_DIM
    nc = ts // CHUNK
    n_sub = ts // sub
    yg_new = yg_ref.at[g % 2]
    yg = yg_ref.at[(g + 1) % 2]

    @pl.when(g == 0)
    def _():
        yg_ref[...] = jnp.zeros_like(yg_ref)

    @pl.when(lax.rem(g, tiles_per_seq) == 0)
    def _():
        for tail_ref in (qt_ref, kt_ref, vt_ref):
            _conv_reset(tail_ref)
        st_ref[...] = jnp.zeros_like(st_ref)

    done = []
    hs = {}
    counts = []

    def after_projection(v):
        if not done:
            return v
        return v + jnp.concatenate([done[-1]] * (v.shape[-1] // LANES), axis=-1)

    def project_out():
        for r in range(n_sub):
            rows = slice(r * sub, (r + 1) * sub)
            h = x_ref[rows, :] + jnp.dot(yl_ref[rows, :], wo_ref[:D_LRU, :], preferred_element_type=F32) \
                + jnp.dot(yg[rows, :], wo_ref[D_LRU:, :], preferred_element_type=F32)
            h_ref[rows, :] = h
            hs[r] = h
            bits = lax.bitcast_convert_type(h[sub - 1:, D_MODEL - LANES:], jnp.uint32)
            done.append(lax.bitcast_convert_type((bits >> 16) >> 16, F32))
            yield

    def route(r, h):
        xn = _rms(h, gf_ref[...])
        _store_token_rows(xn_ref, xn, r * sub)
        xh = xn.astype(BF16)
        xl = (xn - xh.astype(F32)).astype(BF16)
        logits = jnp.dot(jnp.concatenate([xh, xl, xh], axis=1), wr_ref[...],
                         preferred_element_type=F32) + br_ref[...]
        lane = lax.broadcasted_iota(jnp.int32, logits.shape, 1).astype(F32)
        big = jnp.float32(2 * LANES)
        ninf = jnp.float32(-jnp.inf)

        def top1(vals):
            m = jnp.max(vals, axis=-1, keepdims=True)
            return m, jnp.min(jnp.where(vals == m, lane, big), axis=-1, keepdims=True)

        gl = jnp.where(lane < N_GROUPS, logits, ninf)
        gmax, gsel = top1(gl)
        p_group = 1.0 / jnp.sum(jnp.exp(gl - gmax), axis=-1, keepdims=True)
        lo = N_GROUPS + EXPERTS_PER_GROUP * gsel
        el = jnp.where((lane >= lo) & (lane < lo + EXPERTS_PER_GROUP), logits, ninf)
        m1, i1 = top1(el)
        m2, i2 = top1(jnp.where(lane == i1, ninf, el))
        rr = jnp.exp(m2 - m1)
        g1 = p_group / (1.0 + rr)
        g2 = p_group * rr / (1.0 + rr)
        rt_ref[r * sub:(r + 1) * sub, :] = jnp.where(
            lane == 0, i1 - N_GROUPS,
            jnp.where(lane == 1, i2 - N_GROUPS, jnp.where(lane == 2, g1, jnp.where(lane == 3, g2, 0.0))))
        member = ((lane == i1 - N_GROUPS) | (lane == i2 - N_GROUPS)).astype(F32)
        counts.append(jnp.sum(member, axis=0, keepdims=True))

    def route_subtiles(subtiles):
        for r in subtiles:
            route(r, hs[r])
            yield

    ri = lax.broadcasted_iota(jnp.int32, (CHUNK, CHUNK), 0)
    ci = lax.broadcasted_iota(jnp.int32, (CHUNK, CHUNK), 1)
    causal = ri >= ci
    strict = ri > ci
    tril = causal.astype(F32)
    eye = (ri == ci).astype(F32)
    og = og_ref[...]

    def l2n(x, scale):
        parts = []
        for h in range(GDN_HEADS):
            xh = x[:, h * dk:(h + 1) * dk]
            parts.append(xh * (lax.rsqrt(jnp.sum(xh * xh, axis=-1, keepdims=True) + EPS) * scale))
        return jnp.concatenate(parts, axis=1)

    def prepare(r0, r1):
        for p0 in range(r0, r1, prep_rows):
            rows = slice(p0, p0 + prep_rows)
            conv = lambda x_ref, tail_ref, part: _silu(
                _causal_conv(tail_ref, x_ref, after_projection(cw_ref[part]), p0, prep_rows))
            qs_ref[rows, :] = l2n(conv(q_ref, qt_ref, 0), dk ** -0.5)
            yield
            ks_ref[rows, :] = l2n(conv(k_ref, kt_ref, 1), 1.0)
            yield
            vs_ref[rows, :] = conv(v_ref, vt_ref, 2)
            ba = ba_ref[rows, :]
            bs_ref[rows, :] = _sigmoid(ba)
            g = -jnp.exp(alog_ref[...]) * jax.nn.softplus(ba + dtb_ref[...])
            for c0 in range(0, prep_rows, CHUNK):
                gc_ref[p0 + c0:p0 + c0 + CHUNK, :] = jnp.dot(tril, g[c0:c0 + CHUNK], precision=lax.Precision.HIGHEST,
                                                             preferred_element_type=F32)
            gct_ref[:, rows] = gc_ref[rows, :].T
            yield

    terms = {}

    def chunk_terms(pairs):
        n = range(len(pairs))
        rows = [slice(c * CHUNK, (c + 1) * CHUNK) for c, _ in pairs]
        cols = [slice(h * dk, (h + 1) * dk) for _, h in pairs]
        gl = [GDN_HEADS + h for _, h in pairs]
        kh = [ks_ref[rows[i], cols[i]] for i in n]
        kb = [kh[i] * bs_ref[rows[i], pairs[i][1]:pairs[i][1] + 1] for i in n]
        r = [_dot_nt(jnp.concatenate([kb[i], qs_ref[rows[i], cols[i]]], axis=0), kh[i]) for i in n]
        yield
        gcol = [gc_ref[rows[i], gl[i]:gl[i] + 1] for i in n]
        decay = []
        for i in n:
            diff = gcol[i] - gct_ref[gl[i]:gl[i] + 1, rows[i]]
            decay.append(jnp.where(causal, jnp.exp(jnp.where(causal, diff, 0.0)), 0.0))
        a = [jnp.where(strict, r[i][:CHUNK] * decay[i], 0.0) for i in n]
        qk = [(r[i][CHUNK:] * decay[i]).astype(BF16) for i in n]
        tinv = [eye - a[i] for i in n]
        p = 2
        while p < CHUNK:
            a = [_dot(a[i], a[i]) for i in n]
            yield
            tinv = [tinv[i] + _dot(tinv[i], a[i]) for i in n]
            yield
            p *= 2
        eg = [jnp.exp(gcol[i]) for i in n]
        rhs = [jnp.concatenate([vs_ref[rows[i], cols[i]] * bs_ref[rows[i], pairs[i][1]:pairs[i][1] + 1],
                                kb[i] * eg[i]], axis=1) for i in n]
        uw = [_dot(tinv[i], rhs[i]).astype(BF16) for i in n]
        yield
        qk_uw = [jnp.dot(qk[i], uw[i], preferred_element_type=F32) for i in n]
        glast = [gc_ref[(c + 1) * CHUNK - 1:(c + 1) * CHUNK, gl[i]:gl[i] + 1] for i, (c, _) in enumerate(pairs)]
        kd_uw = [_dot_tn(kh[i] * jnp.exp(glast[i] - gcol[i]), uw[i]) for i in n]
        yield
        for i in n:
            lhs = jnp.concatenate([kd_uw[i][:, dk:], qs_ref[rows[i], cols[i]] * eg[i] - qk_uw[i][:, dk:]],
                                  axis=0).astype(BF16)
            terms[pairs[i]] = (lhs, kd_uw[i][:, :dk], qk_uw[i][:, :dk], jnp.exp(glast[i]))

    state = [st_ref[h] for h in range(GDN_HEADS)]

    def advance(chunks):
        for c in chunks:
            rows = slice(c * CHUNK, (c + 1) * CHUNK)
            r = [jnp.dot(terms[c, h][0], state[h].astype(BF16), preferred_element_type=F32)
                 for h in range(GDN_HEADS)]
            for h in range(GDN_HEADS):
                cols = slice(h * dk, (h + 1) * dk)
                _, c_add, o_add, egl = terms[c, h]
                o = r[h][dk:] + o_add
                state[h] = egl * state[h] - r[h][:dk] + c_add
                yg_new[rows, cols] = (_rms(o, og) * _silu(z_ref[rows, cols])).astype(yg_new.dtype)
            yield

    groups = [range(c0, c0 + group_chunks) for c0 in range(0, nc, group_chunks)]
    pairs_of = lambda chunks: [(c, h) for c in chunks for h in range(GDN_HEADS)]
    span = lambda chunks: (chunks[0] * CHUNK, (chunks[-1] + 1) * CHUNK)
    per_phase = n_sub // (len(groups) + 1)
    _interleave(prepare(*span(groups[0])), project_out())
    for i, chunks in enumerate(groups):
        side = [route_subtiles(range(i * per_phase, (i + 1) * per_phase))]
        if i + 1 < len(groups):
            side.append(prepare(*span(groups[i + 1])))
        if i > 0:
            side.append(advance(groups[i - 1]))
        _interleave(chunk_terms(pairs_of(chunks)), *side)
    _interleave(advance(groups[-1]), route_subtiles(range(len(groups) * per_phase, n_sub)))
    for h in range(GDN_HEADS):
        st_ref[h] = state[h]
    for x_in, tail_ref in ((q_ref, qt_ref), (k_ref, kt_ref), (v_ref, vt_ref)):
        _conv_carry(tail_ref, x_in)
    cnt_ref[...] = jnp.broadcast_to(sum(counts), cnt_ref.shape)


def _gdn_router(proj, cw, alog, dtb, og, x2, yl, wo, gf, wr, br, seq, ts):
    t = x2.shape[0]
    n_tiles = t // ts
    this = lambda c: (lambda g: (jnp.minimum(g, n_tiles - 1), c))
    last = lambda g: (jnp.maximum(g - 1, 0), 0)
    const = lambda *shape: pl.BlockSpec(shape, lambda g: (0,) * len(shape))
    col = lambda c: pl.BlockSpec((ts, D_GDN), this(c))
    return pl.pallas_call(
        functools.partial(_gdn_router_kernel, tiles_per_seq=seq // ts, group_chunks=4, prep_rows=128, sub=128),
        grid=(n_tiles + 1,),
        in_specs=[col(0), col(1), col(2), col(3), pl.BlockSpec((ts, LANES), this(4 * D_GDN // LANES)),
                  const(3, CONV_W, D_GDN), const(1, LANES), const(1, LANES), const(1, GDN_HEAD_DIM),
                  pl.BlockSpec((ts, D_MODEL), last), pl.BlockSpec((ts, D_LRU), last), const(D_MODEL, D_MODEL),
                  const(1, D_MODEL), const(3 * D_MODEL, LANES), const(1, LANES)],
        out_specs=[pl.BlockSpec((ts, D_MODEL), last), pl.BlockSpec((ts * ROW_TILES, LANES), last),
                   pl.BlockSpec((ts, LANES), last), pl.BlockSpec((SUBLANES, LANES), last)],
        out_shape=[jax.ShapeDtypeStruct((t, D_MODEL), F32),
                   jax.ShapeDtypeStruct((t * ROW_TILES, LANES), jnp.uint32),
                   jax.ShapeDtypeStruct((t, LANES), F32),
                   jax.ShapeDtypeStruct((n_tiles * SUBLANES, LANES), F32)],
        scratch_shapes=[_conv_scratch(ts, D_GDN)] * 3
        + [pltpu.VMEM((ts, D_GDN), F32)] * 3
        + [pltpu.VMEM((ts, LANES), F32)] * 2
        + [pltpu.VMEM((LANES, ts), F32)]
        + [pltpu.VMEM((GDN_HEADS, GDN_HEAD_DIM, GDN_HEAD_DIM), F32)]
        + [pltpu.VMEM((2, ts, D_GDN), BF16)],
        compiler_params=_cparams("arbitrary"),
        name="gdn_out_router",
    )(proj, proj, proj, proj, proj, cw, alog, dtb, og, x2, yl, wo, gf, wr, br)


def _slots_kernel(rt_ref, tc_ref, dest_ref, blocks_ref, cnt_ref, before_ref):
    i = pl.program_id(0)
    tm = rt_ref.shape[0]
    rt = rt_ref[...]
    lane = lax.broadcasted_iota(jnp.int32, rt.shape, 1)
    e0 = rt[:, 0:1].astype(jnp.int32)
    e1 = rt[:, 1:2].astype(jnp.int32)
    member = ((lane == e0) | (lane == e1)).astype(BF16)

    @pl.when(i == 0)
    def _():
        ri = lax.broadcasted_iota(jnp.int32, (tm, tm), 0)
        ci = lax.broadcasted_iota(jnp.int32, (tm, tm), 1)
        before_ref[...] = (ri > ci).astype(BF16)
        cnt = jnp.broadcast_to(jnp.sum(tc_ref[...], axis=0, keepdims=True) / SUBLANES, cnt_ref.shape)
        padded = jnp.ceil(cnt / MOE_BLOCK) * MOE_BLOCK
        l8 = lax.broadcasted_iota(jnp.int32, cnt.shape, 1)
        incl = padded
        d = 1
        while d < LANES:
            incl = incl + jnp.where(l8 >= d, pltpu.roll(incl, d, axis=1), 0.0)
            d *= 2
        cnt_ref[...] = incl - padded
        seg_start = (incl - padded)[0:1, :]
        seg_end = incl[0:1, :]
        real_end = seg_start + cnt[0:1, :]
        bl = lax.broadcasted_iota(jnp.int32, blocks_ref.shape, 1)
        b0 = (lax.broadcasted_iota(jnp.int32, blocks_ref.shape, 0) * MOE_BLOCK).astype(F32)
        is_expert = bl < N_EXPERTS
        expert = jnp.sum(jnp.where(is_expert & (seg_end <= b0), 1.0, 0.0), axis=-1, keepdims=True)
        real = jnp.maximum(jnp.minimum(real_end, b0 + MOE_BLOCK) - jnp.maximum(seg_start, b0), 0.0)
        n_real = jnp.sum(jnp.where(is_expert, real, 0.0), axis=-1, keepdims=True)
        blocks_ref[...] = jnp.where(bl == 0, jnp.minimum(expert, N_EXPERTS - 1.0),
                                    jnp.where(bl == 1, n_real, 0.0)).astype(jnp.int32)

    start = cnt_ref[...]
    pos = start[0:1, :] + jnp.dot(before_ref[...], member, preferred_element_type=F32)
    d0 = jnp.sum(jnp.where(lane == e0, pos, 0.0), axis=-1, keepdims=True)
    d1 = jnp.sum(jnp.where(lane == e1, pos, 0.0), axis=-1, keepdims=True)
    dest = jnp.where(lane == 0, d0, jnp.where(lane == 1, d1, 0.0))
    dest_ref[...] = dest.T[:SUBLANES].astype(jnp.int32)
    cnt_ref[...] = start + jnp.dot(jnp.ones((SUBLANES, tm), BF16), member, preferred_element_type=F32)


def _slots(rt, tile_counts, n_blocks, tm):
    t = rt.shape[0]
    return pl.pallas_call(
        _slots_kernel,
        grid=(t // tm,),
        in_specs=[pl.BlockSpec((tm, LANES), lambda i: (i, 0)),
                  pl.BlockSpec(tile_counts.shape, lambda i: (0, 0))],
        out_specs=[pl.BlockSpec((SUBLANES, tm), lambda i: (0, i)),
                   pl.BlockSpec((n_blocks, LANES), lambda i: (0, 0))],
        out_shape=[jax.ShapeDtypeStruct((SUBLANES, t), jnp.int32),
                   jax.ShapeDtypeStruct((n_blocks, LANES), jnp.int32)],
        scratch_shapes=[pltpu.VMEM((SUBLANES, LANES), F32), pltpu.VMEM((tm, tm), BF16)],
        compiler_params=_cparams("arbitrary"),
        name="moe_slots",
    )(rt, tile_counts)


def _sc_mesh():
    return plsc.VectorSubcoreMesh(core_axis_name="c", subcore_axis_name="s")


def _sc_worker():
    return lax.axis_index("s") * SC_CORES + lax.axis_index("c")


def _sc_scatter_rows(src, idx, n_rows):
    t = src.shape[0]
    per_w = t // SC_WORKERS
    n_win = per_w // SC_WINDOW
    idx = idx.reshape(TOP_K, SC_WORKERS, n_win, SC_WINDOW)

    @functools.partial(
        pl.kernel, mesh=_sc_mesh(),
        out_type=jax.ShapeDtypeStruct((n_rows,) + src.shape[1:], src.dtype),
        scratch_types=[pltpu.VMEM((TOP_K, n_win, SC_WINDOW), jnp.int32),
                       pltpu.VMEM((2, SC_WINDOW) + src.shape[1:], src.dtype),
                       pltpu.SemaphoreType.DMA((2,)), pltpu.SemaphoreType.DMA((2,))],
        compiler_params=pltpu.CompilerParams(use_tc_tiling_on_sc=True),
        name="sc_dispatch")
    def scatter(src_hbm, idx_hbm, out_hbm, idx_v, rows_v, lsem, ssem):
        wid = _sc_worker()
        base = wid * per_w
        for k in range(TOP_K):
            pltpu.sync_copy(idx_hbm.at[k, wid], idx_v.at[k])

        def load(w, slot):
            return pltpu.make_async_copy(src_hbm.at[pl.ds(base + w * SC_WINDOW, SC_WINDOW)], rows_v.at[slot],
                                         lsem.at[slot])

        def put(w, slot, k):
            return pltpu.make_async_copy(rows_v.at[slot], out_hbm.at[idx_v.at[k, w]], ssem.at[slot])

        load(0, 0).start()

        @pl.loop(0, n_win, step=2)
        def _(w0):
            for s in range(2):
                w = w0 + s

                @pl.when(w + 1 < n_win)
                def _():
                    @pl.when(w >= 1)
                    def _():
                        for k in range(TOP_K):
                            put(w - 1, 1 - s, k).wait()

                    load(w + 1, 1 - s).start()

                load(w, s).wait()
                for k in range(TOP_K):
                    put(w, s, k).start()

        for s in range(2):
            for k in range(TOP_K):
                put(n_win - 2 + s, s, k).wait()

    return scatter(src, idx)


def _sc_gather_rows(table, idx):
    b = idx.shape[0]
    per_w = b // SC_WORKERS
    n_win = per_w // SC_WINDOW
    idx = idx.reshape(SC_WORKERS, n_win, SC_WINDOW)

    @functools.partial(
        pl.kernel, mesh=_sc_mesh(),
        out_type=jax.ShapeDtypeStruct((b,) + table.shape[1:], table.dtype),
        scratch_types=[pltpu.VMEM((n_win, SC_WINDOW), jnp.int32),
                       pltpu.VMEM((2, SC_WINDOW) + table.shape[1:], table.dtype),
                       pltpu.SemaphoreType.DMA((2,)), pltpu.SemaphoreType.DMA((2,))],
        compiler_params=pltpu.CompilerParams(use_tc_tiling_on_sc=True),
        name="sc_combine_gather")
    def gather(table_hbm, idx_hbm, out_hbm, idx_v, rows_v, gsem, psem):
        wid = _sc_worker()
        base = wid * per_w
        pltpu.sync_copy(idx_hbm.at[wid], idx_v)

        def get(w, slot):
            return pltpu.make_async_copy(table_hbm.at[idx_v.at[w]], rows_v.at[slot], gsem.at[slot])

        def put(w, slot):
            return pltpu.make_async_copy(rows_v.at[slot], out_hbm.at[pl.ds(base + w * SC_WINDOW, SC_WINDOW)],
                                         psem.at[slot])

        get(0, 0).start()

        @pl.loop(0, n_win, step=2)
        def _(w0):
            for s in range(2):
                w = w0 + s

                @pl.when(w + 1 < n_win)
                def _():
                    @pl.when(w >= 1)
                    def _():
                        put(w - 1, 1 - s).wait()

                    get(w + 1, 1 - s).start()

                get(w, s).wait()
                put(w, s).start()

        for s in range(2):
            put(n_win - 2 + s, s).wait()

    return gather(table, idx)


def _experts_kernel(be_ref, nv_ref, xs_ref, wg_ref, wu_ref, wd_ref, ys_ref, wgb_ref, wub_ref, wdb_ref):
    b = pl.program_id(0)
    prev = be_ref[jnp.maximum(b - 1, 0)]

    @pl.when((b == 0) | (be_ref[b] != prev))
    def _():
        wgb_ref[...] = wg_ref[0].astype(BF16)
        wub_ref[...] = wu_ref[0].astype(BF16)
        wdb_ref[...] = wd_ref[0].astype(BF16)

    @pl.when(nv_ref[b] > 0)
    def _():
        row = lax.broadcasted_iota(jnp.int32, (MOE_BLOCK, D_MODEL), 0)
        xb = jnp.where(row < nv_ref[b], _load_token_rows(xs_ref, MOE_BLOCK, BF16), 0.0)
        hg = jnp.dot(xb, wgb_ref[...], preferred_element_type=F32)
        hu = jnp.dot(xb, wub_ref[...], preferred_element_type=F32)
        hb = (_silu(hg) * hu).astype(BF16)
        _store_token_rows(ys_ref, jnp.dot(hb, wdb_ref[...], preferred_element_type=F32))

    @pl.when(nv_ref[b] == 0)
    def _():
        ys_ref[...] = jnp.zeros_like(ys_ref)


def _experts(block_expert, n_valid, xs, wg, wu, wd):
    nb = xs.shape[0] // (MOE_BLOCK * ROW_TILES)
    blk = pl.BlockSpec((MOE_BLOCK * ROW_TILES, LANES), lambda b, be, nv: (b, 0))
    wspec = lambda a, c: pl.BlockSpec((1, a, c), lambda b, be, nv: (be[b], 0, 0))
    return pl.pallas_call(
        _experts_kernel,
        grid_spec=pltpu.PrefetchScalarGridSpec(
            num_scalar_prefetch=2,
            grid=(nb,),
            in_specs=[blk, wspec(D_MODEL, D_EXPERT), wspec(D_MODEL, D_EXPERT), wspec(D_EXPERT, D_MODEL)],
            out_specs=blk,
            scratch_shapes=[pltpu.VMEM((D_MODEL, D_EXPERT), BF16), pltpu.VMEM((D_MODEL, D_EXPERT), BF16),
                            pltpu.VMEM((D_EXPERT, D_MODEL), BF16)]),
        out_shape=jax.ShapeDtypeStruct(xs.shape, xs.dtype),
        compiler_params=_cparams("arbitrary"),
        name="moe_experts",
    )(block_expert, n_valid, xs, wg, wu, wd)


def _combine_kernel(h_ref, rt_ref, p_ref, y0_ref, y1_ref, gp_ref, wpg_ref, wp_ref, gf_ref, o_ref, *, sub):
    n_sub = h_ref.shape[0] // sub

    def residual(r):
        rows = slice(r * sub, (r + 1) * sub)
        rt = rt_ref[rows, :]
        h = h_ref[rows, :] + (_load_token_rows(y0_ref, sub, F32, r * sub) * rt[:, 2:3]
                              + _load_token_rows(y1_ref, sub, F32, r * sub) * rt[:, 3:4])
        return h, _rms(h, gp_ref[...]).astype(BF16)

    def products(r, xn):
        return (jnp.dot(xn, wpg_ref[...], preferred_element_type=F32),
                _dot(p_ref[r * sub:(r + 1) * sub, :], wp_ref[...]))

    def finish(r, h, gate_lin, ple):
        h = h + ple * _sigmoid(gate_lin)
        o_ref[r * sub:(r + 1) * sub, :] = _rms(h, gf_ref[...])

    h, xn = residual(0)
    for r in range(n_sub):
        gate_lin, ple = products(r, xn)
        if r + 1 < n_sub:
            h_next, xn = residual(r + 1)
        finish(r, h, gate_lin, ple)
        if r + 1 < n_sub:
            h = h_next


def _combine(h1, rt, p2, y, gp, wpg, wp, gf, tm):
    t = h1.shape[0]
    nt = t // tm
    tile = lambda n: pl.BlockSpec((tm, n), lambda i: (i, 0))
    full = lambda a, b: pl.BlockSpec((a, b), lambda i: (0, 0))
    ytile = lambda k: pl.BlockSpec((tm * ROW_TILES, LANES), lambda i: (i + k * nt, 0))
    return pl.pallas_call(
        functools.partial(_combine_kernel, sub=256),
        grid=(nt,),
        in_specs=[tile(D_MODEL), tile(LANES), tile(D_PLE), ytile(0), ytile(1),
                  full(1, D_MODEL), full(D_MODEL, D_MODEL), full(D_PLE, D_MODEL), full(1, D_MODEL)],
        out_specs=tile(D_MODEL),
        out_shape=jax.ShapeDtypeStruct((t, D_MODEL), F32),
        compiler_params=_cparams("parallel"),
        name="moe_combine_ple",
    )(h1, rt, p2, y, y, gp, wpg, wp, gf)


def _lane_row(vals, offset):
    return jnp.zeros((1, LANES), F32).at[0, offset:offset + vals.shape[0]].set(vals)


def kernel(x, p, norm_mix, w_in, lru_conv_w, lru_conv_b, lru_wa, lru_ba, lru_wi, lru_bi, lru_lambda,
           lru_out_norm, gdn_conv_w, gdn_a_log, gdn_dt_bias, gdn_out_norm, w_out, norm_ffn,
           w_router_group, b_router_group, w_router_expert, b_router_expert, w_exp_gate, w_exp_up,
           w_exp_down, norm_ple, w_ple_gate, w_ple, norm_final):
    bsz, seq, d = x.shape
    t = bsz * seq
    depth = w_in.shape[0]
    assert depth == 1, "the final norm is fused into the last layer's combine kernel"
    n_blocks = -(-t * TOP_K // MOE_BLOCK) + N_EXPERTS
    row = lambda v: v.reshape(1, -1).astype(F32)
    h = x.reshape(t, d).astype(F32)
    for l in range(depth):
        b_gates = jnp.concatenate([lru_ba[l], lru_bi[l]]).reshape(1, -1)
        proj_gdn, y_lru = _proj_lru(h, row(norm_mix[l]), w_in[l], lru_conv_w[l], row(lru_conv_b[l]), lru_wa[l],
                                    lru_wi[l], b_gates, row(lru_lambda[l]), row(lru_out_norm[l]), seq, 512)
        cw = gdn_conv_w[l].reshape(CONV_W, 3, D_GDN).transpose(1, 0, 2)
        w_r = jnp.pad(jnp.concatenate([w_router_group[l], w_router_expert[l]], axis=1),
                      ((0, 0), (0, LANES - N_GROUPS - N_EXPERTS)))
        w_r_hi = w_r.astype(BF16)
        w_r_lo = (w_r - w_r_hi.astype(F32)).astype(BF16)
        w_r = jnp.concatenate([w_r_hi, w_r_hi, w_r_lo], axis=0)
        b_r = _lane_row(jnp.concatenate([b_router_group[l], b_router_expert[l]]), 0)
        h1, xn2, rt, tile_counts = _gdn_router(
            proj_gdn, cw, _lane_row(gdn_a_log[l], GDN_HEADS), _lane_row(gdn_dt_bias[l], GDN_HEADS),
            row(gdn_out_norm[l]), h, y_lru, w_out[l].astype(BF16), row(norm_ffn[l]), w_r, b_r, seq, 512)
        dest8, blocks = _slots(rt, tile_counts, n_blocks, 1024)
        dest = dest8[:TOP_K]
        block_expert, n_valid = blocks[:, 0], blocks[:, 1]
        tiles = lambda a: a.reshape(-1, ROW_TILES, LANES)
        xs = _sc_scatter_rows(tiles(xn2), dest, n_blocks * MOE_BLOCK)
        ys = _experts(block_expert, n_valid, xs.reshape(-1, LANES), w_exp_gate[l], w_exp_up[l], w_exp_down[l])
        y = _sc_gather_rows(tiles(ys), dest.reshape(-1))
        h = _combine(h1, rt, p[l].reshape(t, -1).astype(F32), y.reshape(-1, LANES), row(norm_ple[l]),
                     w_ple_gate[l].astype(BF16), w_ple[l].astype(BF16), row(norm_final), 512)
    return h.reshape(bsz, seq, d).astype(x.dtype)
```

```python
import functools

import jax
import jax.numpy as jnp
from jax import lax
from jax.experimental import pallas as pl
from jax.experimental.pallas import tpu as pltpu
from jax.experimental.pallas import tpu_sc as plsc

D_MODEL = 1024
D_LRU = 512
LRU_BLOCKS = 8
LRU_BLOCK_W = D_LRU // LRU_BLOCKS
LRU_C = 8.0
D_GDN = 512
GDN_HEADS = 4
GDN_HEAD_DIM = D_GDN // GDN_HEADS
CONV_W = 4
CHUNK = 64
N_GROUPS = 4
EXPERTS_PER_GROUP = 8
N_EXPERTS = N_GROUPS * EXPERTS_PER_GROUP
TOP_K = 2
D_EXPERT = 512
MOE_BLOCK = 512
D_PLE = 256
EPS = 1e-6

LANES = 128
SUBLANES = 8
ROW_TILES = D_MODEL // (2 * LANES)
SC_CORES = 2
SC_SUBCORES = 16
SC_WORKERS = SC_CORES * SC_SUBCORES
SC_WINDOW = 64
PROJ_COLS = 2 * D_LRU + 4 * D_GDN + LANES
VMEM_LIMIT = 56 * 1024 * 1024

BF16 = jnp.bfloat16
F32 = jnp.float32


def _cparams(*sem):
    return pltpu.CompilerParams(dimension_semantics=sem, vmem_limit_bytes=VMEM_LIMIT)


def _rms(x, g):
    return x * lax.rsqrt(jnp.mean(x * x, axis=-1, keepdims=True) + EPS) * g


def _sigmoid(x):
    return 0.5 * jnp.tanh(0.5 * x) + 0.5


def _silu(x):
    return x * _sigmoid(x)


def _dot(a, b):
    return jnp.dot(a.astype(BF16), b.astype(BF16), preferred_element_type=F32)


def _dot_nt(a, b):
    return lax.dot_general(a.astype(BF16), b.astype(BF16), (((1,), (1,)), ((), ())),
                           preferred_element_type=F32)


def _dot_tn(a, b):
    return lax.dot_general(a.astype(BF16), b.astype(BF16), (((0,), (0,)), ((), ())),
                           preferred_element_type=F32)


def _conv_scratch(rows, channels):
    return pltpu.VMEM((channels // LANES, 2 * (SUBLANES + rows), LANES), F32)


def _conv_reset(xp_ref):
    for k in range(xp_ref.shape[0]):
        xp_ref.at[k][pl.ds(0, SUBLANES, stride=2), :] = jnp.zeros((SUBLANES, LANES), F32)


def _causal_conv(xp_ref, x_ref, w, r0, rows, col0=0):
    out = []
    for k in range(xp_ref.shape[0]):
        lanes = slice(k * LANES, (k + 1) * LANES)
        slab = xp_ref.at[k]
        xk = x_ref[r0:r0 + rows, col0 + k * LANES:col0 + (k + 1) * LANES]
        slab[pl.ds(2 * (SUBLANES + r0), rows, stride=2), :] = xk
        acc = xk * w[CONV_W - 1:CONV_W, lanes]
        for j in range(1, CONV_W):
            acc = acc + slab[pl.ds(2 * (SUBLANES + r0 - j), rows, stride=2), :] * w[CONV_W - 1 - j:CONV_W - j, lanes]
        out.append(acc)
    return jnp.concatenate(out, axis=1)


def _conv_carry(xp_ref, x_ref, col0=0):
    ts = x_ref.shape[0]
    for k in range(xp_ref.shape[0]):
        xp_ref.at[k][pl.ds(0, SUBLANES, stride=2), :] = x_ref[ts - SUBLANES:, col0 + k * LANES:col0 + (k + 1) * LANES]


def _interleave(*stages):
    live = list(stages)
    while live:
        for g in list(live):
            try:
                next(g)
            except StopIteration:
                live.remove(g)


def _store_token_rows(ref, x, row0=0):
    rows = x.shape[0]
    bits = lax.bitcast_convert_type(x.astype(BF16).astype(F32), jnp.uint32)
    words = (bits[:, :D_MODEL // 2] >> 16) | bits[:, D_MODEL // 2:]
    for j in range(ROW_TILES):
        ref[pl.ds(row0 * ROW_TILES + j, rows, stride=ROW_TILES), :] = words[:, j * LANES:(j + 1) * LANES]


def _load_token_rows(ref, rows, dtype, row0=0):
    words = [ref[pl.ds(row0 * ROW_TILES + j, rows, stride=ROW_TILES), :] for j in range(ROW_TILES)]
    lo = [lax.bitcast_convert_type(w << 16, F32).astype(dtype) for w in words]
    hi = [lax.bitcast_convert_type(w & jnp.uint32(0xFFFF0000), F32).astype(dtype) for w in words]
    return jnp.concatenate(lo + hi, axis=1)


def _proj_lru_kernel(x_ref, gin_ref, win_ref, cw_ref, cb_ref, wa_ref, wi_ref, bg_ref, lam_ref, og_ref,
                     gdn_ref, o_ref, pj_ref, tail_ref, h_ref, wg_ref, wb_ref, *, tiles_per_seq, col_tile):
    g = pl.program_id(0)
    ts = x_ref.shape[0]
    pj_new = pj_ref.at[g % 2]
    pj = pj_ref.at[(g + 1) % 2]

    @pl.when(g == 0)
    def _():
        pj_ref[...] = jnp.zeros_like(pj_ref)
        in_cols = win_ref.shape[1]
        whole = in_cols // LANES * LANES
        for c0 in range(0, whole, col_tile):
            wb_ref[:, c0:c0 + col_tile] = win_ref[:, c0:c0 + col_tile].astype(BF16)
        wb_ref[:, whole:] = jnp.zeros((D_MODEL, PROJ_COLS - whole), BF16)
        wb_ref[:, whole:in_cols] = win_ref[:, whole:].astype(BF16)
        wg_ref[...] = jnp.zeros_like(wg_ref)
        for n in range(LRU_BLOCKS):
            blk = slice(n * LRU_BLOCK_W, (n + 1) * LRU_BLOCK_W)
            wg_ref[blk, blk] = wa_ref[n].astype(BF16)
            wg_ref[blk, D_LRU + n * LRU_BLOCK_W:D_LRU + (n + 1) * LRU_BLOCK_W] = wi_ref[n].astype(BF16)

    @pl.when((g == 0) | (lax.rem(g - 1, tiles_per_seq) == 0))
    def _():
        _conv_reset(tail_ref)
        h_ref[...] = jnp.zeros_like(h_ref)

    done = []

    def after_projection(v):
        return v + jnp.concatenate([done[max(len(done) - 2, 0)]] * (v.shape[-1] // LANES), axis=-1)

    def project():
        xn = _rms(x_ref[...], gin_ref[...]).astype(BF16)
        yield
        for c0 in range(0, PROJ_COLS, col_tile):
            c1 = min(c0 + col_tile, PROJ_COLS)
            y = jnp.dot(xn, wb_ref[:, c0:c1], preferred_element_type=F32)
            if c1 <= 2 * D_LRU:
                pj_new[:, c0:c1] = y
            else:
                gdn_ref[:, c0 - 2 * D_LRU:c1 - 2 * D_LRU] = y
            bits = lax.bitcast_convert_type(y[ts - 1:, c1 - c0 - LANES:], jnp.uint32)
            done.append(lax.bitcast_convert_type((bits >> 16) >> 16, F32))
            yield

    def recur():
        xc = _causal_conv(tail_ref, pj, cw_ref[...], 0, ts) + cb_ref[...]
        _conv_carry(tail_ref, pj)
        yield
        gates = _sigmoid(_dot(xc, wg_ref[...]) + after_projection(bg_ref[...]))
        r = gates[:, :D_LRU]
        i = gates[:, D_LRU:]
        yield
        log_a = LRU_C * r * jax.nn.log_sigmoid(after_projection(lam_ref[...]))
        a = jnp.exp(log_a)
        th = jnp.tanh(log_a)
        u = jnp.sqrt(-2.0 * th) * lax.rsqrt(1.0 - th) * (i * xc)
        yield
        a = a.reshape(ts // SUBLANES, SUBLANES, D_LRU)
        u = u.reshape(ts // SUBLANES, SUBLANES, D_LRU)
        row = lax.broadcasted_iota(jnp.int32, a.shape, 1)
        d = 1
        while d < SUBLANES:
            keep = row >= d
            a_prev = jnp.where(keep, pltpu.roll(a, d, axis=1), 1.0)
            u_prev = jnp.where(keep, pltpu.roll(u, d, axis=1), 0.0)
            u = a * u_prev + u
            a = a * a_prev
            d *= 2
            yield
        carry = after_projection(h_ref[...])
        groups = []
        for n in range(ts // SUBLANES):
            groups.append(a[n] * carry + u[n])
            carry = groups[-1][SUBLANES - 1:]
        h = jnp.concatenate(groups, axis=0)
        h_ref[...] = carry
        yield
        y = h * jax.nn.gelu(pj[:, D_LRU:])
        o_ref[...] = _rms(y, after_projection(og_ref[...])).astype(o_ref.dtype)

    _interleave(project(), recur())


def _proj_lru(x2, gin, win, cw, cb, wa, wi, bg, lam, og, seq, ts):
    t = x2.shape[0]
    n_tiles = t // ts
    this = lambda g: (jnp.minimum(g, n_tiles - 1), 0)
    last = lambda g: (jnp.maximum(g - 1, 0), 0)
    const = lambda *shape: pl.BlockSpec(shape, lambda g: (0,) * len(shape))
    gdn_cols = PROJ_COLS - 2 * D_LRU
    return pl.pallas_call(
        functools.partial(_proj_lru_kernel, tiles_per_seq=seq // ts, col_tile=D_LRU),
        grid=(n_tiles + 1,),
        in_specs=[pl.BlockSpec((ts, D_MODEL), this), const(1, D_MODEL),
                  pl.BlockSpec(win.shape, lambda g: (0, 0), pipeline_mode=pl.Buffered(1)),
                  const(CONV_W, D_LRU), const(1, D_LRU),
                  const(LRU_BLOCKS, LRU_BLOCK_W, LRU_BLOCK_W), const(LRU_BLOCKS, LRU_BLOCK_W, LRU_BLOCK_W),
                  const(1, 2 * D_LRU), const(1, D_LRU), const(1, D_LRU)],
        out_specs=[pl.BlockSpec((ts, gdn_cols), this), pl.BlockSpec((ts, D_LRU), last)],
        out_shape=[jax.ShapeDtypeStruct((t, gdn_cols), F32), jax.ShapeDtypeStruct((t, D_LRU), BF16)],
        scratch_shapes=[pltpu.VMEM((2, ts, 2 * D_LRU), F32), _conv_scratch(ts, D_LRU), pltpu.VMEM((1, D_LRU), F32),
                        pltpu.VMEM((D_LRU, 2 * D_LRU), BF16), pltpu.VMEM((D_MODEL, PROJ_COLS), BF16)],
        compiler_params=_cparams("arbitrary"),
        name="in_proj_rglru",
    )(x2, gin, win, cw, cb, wa, wi, bg, lam, og)


def _gdn_router_kernel(q_ref, k_ref, v_ref, z_ref, ba_ref, cw_ref, alog_ref, dtb_ref, og_ref,
                       x_ref, yl_ref, wo_ref, gf_ref, wr_ref, br_ref, h_ref, xn_ref, rt_ref, cnt_ref,
                       qt_ref, kt_ref, vt_ref, qs_ref, ks_ref, vs_ref, bs_ref, gc_ref, gct_ref, st_ref, yg_ref,
                       *, tiles_per_seq, group_chunks, prep_rows, sub):
    g = pl.program_id(0)
    ts = q_ref.shape[0]
    dk = GDN_HEAD_DIM
    nc = ts // CHUNK
    n_sub = ts // sub
    yg_new = yg_ref.at[g % 2]
    yg = yg_ref.at[(g + 1) % 2]

    @pl.when(g == 0)
    def _():
        yg_ref[...] = jnp.zeros_like(yg_ref)

    @pl.when(lax.rem(g, tiles_per_seq) == 0)
    def _():
        for tail_ref in (qt_ref, kt_ref, vt_ref):
            _conv_reset(tail_ref)
        st_ref[...] = jnp.zeros_like(st_ref)

    done = []
    hs = {}
    counts = []

    def after_projection(v):
        if not done:
            return v
        return v + jnp.concatenate([done[-1]] * (v.shape[-1] // LANES), axis=-1)

    def project_out():
        for r in range(n_sub):
            rows = slice(r * sub, (r + 1) * sub)
            h = x_ref[rows, :] + jnp.dot(yl_ref[rows, :], wo_ref[:D_LRU, :], preferred_element_type=F32) \
                + jnp.dot(yg[rows, :], wo_ref[D_LRU:, :], preferred_element_type=F32)
            h_ref[rows, :] = h
            hs[r] = h
            bits = lax.bitcast_convert_type(h[sub - 1:, D_MODEL - LANES:], jnp.uint32)
            done.append(lax.bitcast_convert_type((bits >> 16) >> 16, F32))
            yield

    def route(r, h):
        xn = _rms(h, gf_ref[...])
        _store_token_rows(xn_ref, xn, r * sub)
        xh = xn.astype(BF16)
        xl = (xn - xh.astype(F32)).astype(BF16)
        logits = jnp.dot(jnp.concatenate([xh, xl, xh], axis=1), wr_ref[...],
                         preferred_element_type=F32) + br_ref[...]
        lane = lax.broadcasted_iota(jnp.int32, logits.shape, 1).astype(F32)
        big = jnp.float32(2 * LANES)
        ninf = jnp.float32(-jnp.inf)

        def top1(vals):
            m = jnp.max(vals, axis=-1, keepdims=True)
            return m, jnp.min(jnp.where(vals == m, lane, big), axis=-1, keepdims=True)

        gl = jnp.where(lane < N_GROUPS, logits, ninf)
        gmax, gsel = top1(gl)
        p_group = 1.0 / jnp.sum(jnp.exp(gl - gmax), axis=-1, keepdims=True)
        lo = N_GROUPS + EXPERTS_PER_GROUP * gsel
        el = jnp.where((lane >= lo) & (lane < lo + EXPERTS_PER_GROUP), logits, ninf)
        m1, i1 = top1(el)
        m2, i2 = top1(jnp.where(lane == i1, ninf, el))
        rr = jnp.exp(m2 - m1)
        g1 = p_group / (1.0 + rr)
        g2 = p_group * rr / (1.0 + rr)
        rt_ref[r * sub:(r + 1) * sub, :] = jnp.where(
            lane == 0, i1 - N_GROUPS,
            jnp.where(lane == 1, i2 - N_GROUPS, jnp.where(lane == 2, g1, jnp.where(lane == 3, g2, 0.0))))
        member = ((lane == i1 - N_GROUPS) | (lane == i2 - N_GROUPS)).astype(F32)
        counts.append(jnp.sum(member, axis=0, keepdims=True))

    def route_subtiles(subtiles):
        for r in subtiles:
            route(r, hs[r])
            yield

    ri = lax.broadcasted_iota(jnp.int32, (CHUNK, CHUNK), 0)
    ci = lax.broadcasted_iota(jnp.int32, (CHUNK, CHUNK), 1)
    causal = ri >= ci
    strict = ri > ci
    tril = causal.astype(F32)
    eye = (ri == ci).astype(F32)
    og = og_ref[...]

    def l2n(x, scale):
        parts = []
        for h in range(GDN_HEADS):
            xh = x[:, h * dk:(h + 1) * dk]
            parts.append(xh * (lax.rsqrt(jnp.sum(xh * xh, axis=-1, keepdims=True) + EPS) * scale))
        return jnp.concatenate(parts, axis=1)

    def prepare(r0, r1):
        for p0 in range(r0, r1, prep_rows):
            rows = slice(p0, p0 + prep_rows)
            conv = lambda x_ref, tail_ref, part: _silu(
                _causal_conv(tail_ref, x_ref, after_projection(cw_ref[part]), p0, prep_rows))
            qs_ref[rows, :] = l2n(conv(q_ref, qt_ref, 0), dk ** -0.5)
            yield
            ks_ref[rows, :] = l2n(conv(k_ref, kt_ref, 1), 1.0)
            yield
            vs_ref[rows, :] = conv(v_ref, vt_ref, 2)
            ba = ba_ref[rows, :]
            bs_ref[rows, :] = _sigmoid(ba)
            g = -jnp.exp(alog_ref[...]) * jax.nn.softplus(ba + dtb_ref[...])
            for c0 in range(0, prep_rows, CHUNK):
                gc_ref[p0 + c0:p0 + c0 + CHUNK, :] = jnp.dot(tril, g[c0:c0 + CHUNK], precision=lax.Precision.HIGHEST,
                                                             preferred_element_type=F32)
            gct_ref[:, rows] = gc_ref[rows, :].T
            yield

    terms = {}

    def chunk_terms(pairs):
        n = range(len(pairs))
        rows = [slice(c * CHUNK, (c + 1) * CHUNK) for c, _ in pairs]
        cols = [slice(h * dk, (h + 1) * dk) for _, h in pairs]
        gl = [GDN_HEADS + h for _, h in pairs]
        kh = [ks_ref[rows[i], cols[i]] for i in n]
        kb = [kh[i] * bs_ref[rows[i], pairs[i][1]:pairs[i][1] + 1] for i in n]
        r = [_dot_nt(jnp.concatenate([kb[i], qs_ref[rows[i], cols[i]]], axis=0), kh[i]) for i in n]
        yield
        gcol = [gc_ref[rows[i], gl[i]:gl[i] + 1] for i in n]
        decay = []
        for i in n:
            diff = gcol[i] - gct_ref[gl[i]:gl[i] + 1, rows[i]]
            decay.append(jnp.where(causal, jnp.exp(jnp.where(causal, diff, 0.0)), 0.0))
        a = [jnp.where(strict, r[i][:CHUNK] * decay[i], 0.0) for i in n]
        qk = [(r[i][CHUNK:] * decay[i]).astype(BF16) for i in n]
        tinv = [eye - a[i] for i in n]
        p = 2
        while p < CHUNK:
            a = [_dot(a[i], a[i]) for i in n]
            yield
            tinv = [tinv[i] + _dot(tinv[i], a[i]) for i in n]
            yield
            p *= 2
        eg = [jnp.exp(gcol[i]) for i in n]
        rhs = [jnp.concatenate([vs_ref[rows[i], cols[i]] * bs_ref[rows[i], pairs[i][1]:pairs[i][1] + 1],
                                kb[i] * eg[i]], axis=1) for i in n]
        uw = [_dot(tinv[i], rhs[i]).astype(BF16) for i in n]
        yield
        qk_uw = [jnp.dot(qk[i], uw[i], preferred_element_type=F32) for i in n]
        glast = [gc_ref[(c + 1) * CHUNK - 1:(c + 1) * CHUNK, gl[i]:gl[i] + 1] for i, (c, _) in enumerate(pairs)]
        kd_uw = [_dot_tn(kh[i] * jnp.exp(glast[i] - gcol[i]), uw[i]) for i in n]
        yield
        for i in n:
            lhs = jnp.concatenate([kd_uw[i][:, dk:], qs_ref[rows[i], cols[i]] * eg[i] - qk_uw[i][:, dk:]],
                                  axis=0).astype(BF16)
            terms[pairs[i]] = (lhs, kd_uw[i][:, :dk], qk_uw[i][:, :dk], jnp.exp(glast[i]))

    state = [st_ref[h] for h in range(GDN_HEADS)]

    def advance(chunks):
        for c in chunks:
            rows = slice(c * CHUNK, (c + 1) * CHUNK)
            r = [jnp.dot(terms[c, h][0], state[h].astype(BF16), preferred_element_type=F32)
                 for h in range(GDN_HEADS)]
            for h in range(GDN_HEADS):
                cols = slice(h * dk, (h + 1) * dk)
                _, c_add, o_add, egl = terms[c, h]
                o = r[h][dk:] + o_add
                state[h] = egl * state[h] - r[h][:dk] + c_add
                yg_new[rows, cols] = (_rms(o, og) * _silu(z_ref[rows, cols])).astype(yg_new.dtype)
            yield

    groups = [range(c0, c0 + group_chunks) for c0 in range(0, nc, group_chunks)]
    pairs_of = lambda chunks: [(c, h) for c in chunks for h in range(GDN_HEADS)]
    span = lambda chunks: (chunks[0] * CHUNK, (chunks[-1] + 1) * CHUNK)
    per_phase = n_sub // (len(groups) + 1)
    _interleave(prepare(*span(groups[0])), project_out())
    for i, chunks in enumerate(groups):
        side = [route_subtiles(range(i * per_phase, (i + 1) * per_phase))]
        if i + 1 < len(groups):
            side.append(prepare(*span(groups[i + 1])))
        if i > 0:
            side.append(advance(groups[i - 1]))
        _interleave(chunk_terms(pairs_of(chunks)), *side)
    _interleave(advance(groups[-1]), route_subtiles(range(len(groups) * per_phase, n_sub)))
    for h in range(GDN_HEADS):
        st_ref[h] = state[h]
    for x_in, tail_ref in ((q_ref, qt_ref), (k_ref, kt_ref), (v_ref, vt_ref)):
        _conv_carry(tail_ref, x_in)
    cnt_ref[...] = jnp.broadcast_to(sum(counts), cnt_ref.shape)


def _gdn_router(proj, cw, alog, dtb, og, x2, yl, wo, gf, wr, br, seq, ts):
    t = x2.shape[0]
    n_tiles = t // ts
    this = lambda c: (lambda g: (jnp.minimum(g, n_tiles - 1), c))
    last = lambda g: (jnp.maximum(g - 1, 0), 0)
    const = lambda *shape: pl.BlockSpec(shape, lambda g: (0,) * len(shape))
    col = lambda c: pl.BlockSpec((ts, D_GDN), this(c))
    return pl.pallas_call(
        functools.partial(_gdn_router_kernel, tiles_per_seq=seq // ts, group_chunks=4, prep_rows=128, sub=128),
        grid=(n_tiles + 1,),
        in_specs=[col(0), col(1), col(2), col(3), pl.BlockSpec((ts, LANES), this(4 * D_GDN // LANES)),
                  const(3, CONV_W, D_GDN), const(1, LANES), const(1, LANES), const(1, GDN_HEAD_DIM),
                  pl.BlockSpec((ts, D_MODEL), last), pl.BlockSpec((ts, D_LRU), last), const(D_MODEL, D_MODEL),
                  const(1, D_MODEL), const(3 * D_MODEL, LANES), const(1, LANES)],
        out_specs=[pl.BlockSpec((ts, D_MODEL), last), pl.BlockSpec((ts * ROW_TILES, LANES), last),
                   pl.BlockSpec((ts, LANES), last), pl.BlockSpec((SUBLANES, LANES), last)],
        out_shape=[jax.ShapeDtypeStruct((t, D_MODEL), F32),
                   jax.ShapeDtypeStruct((t * ROW_TILES, LANES), jnp.uint32),
                   jax.ShapeDtypeStruct((t, LANES), F32),
                   jax.ShapeDtypeStruct((n_tiles * SUBLANES, LANES), F32)],
        scratch_shapes=[_conv_scratch(ts, D_GDN)] * 3
        + [pltpu.VMEM((ts, D_GDN), F32)] * 3
        + [pltpu.VMEM((ts, LANES), F32)] * 2
        + [pltpu.VMEM((LANES, ts), F32)]
        + [pltpu.VMEM((GDN_HEADS, GDN_HEAD_DIM, GDN_HEAD_DIM), F32)]
        + [pltpu.VMEM((2, ts, D_GDN), BF16)],
        compiler_params=_cparams("arbitrary"),
        name="gdn_out_router",
    )(proj, proj, proj, proj, proj, cw, alog, dtb, og, x2, yl, wo, gf, wr, br)


def _slots_kernel(rt_ref, tc_ref, dest_ref, blocks_ref, cnt_ref, before_ref):
    i = pl.program_id(0)
    tm = rt_ref.shape[0]
    rt = rt_ref[...]
    lane = lax.broadcasted_iota(jnp.int32, rt.shape, 1)
    e0 = rt[:, 0:1].astype(jnp.int32)
    e1 = rt[:, 1:2].astype(jnp.int32)
    member = ((lane == e0) | (lane == e1)).astype(F32)

    @pl.when(i == 0)
    def _():
        ri = lax.broadcasted_iota(jnp.int32, before_ref.shape, 0)
        ci = lax.broadcasted_iota(jnp.int32, before_ref.shape, 1)
        before_ref[...] = (ri > ci).astype(BF16)
        cnt = jnp.broadcast_to(jnp.sum(tc_ref[...], axis=0, keepdims=True) / SUBLANES, cnt_ref.shape)
        padded = jnp.ceil(cnt / MOE_BLOCK) * MOE_BLOCK
        l8 = lax.broadcasted_iota(jnp.int32, cnt.shape, 1)
        incl = padded
        d = 1
        while d < LANES:
            incl = incl + jnp.where(l8 >= d, pltpu.roll(incl, d, axis=1), 0.0)
            d *= 2
        cnt_ref[...] = incl - padded
        seg_start = (incl - padded)[0:1, :]
        seg_end = incl[0:1, :]
        real_end = seg_start + cnt[0:1, :]
        bl = lax.broadcasted_iota(jnp.int32, blocks_ref.shape, 1)
        b0 = (lax.broadcasted_iota(jnp.int32, blocks_ref.shape, 0) * MOE_BLOCK).astype(F32)
        is_expert = bl < N_EXPERTS
        expert = jnp.sum(jnp.where(is_expert & (seg_end <= b0), 1.0, 0.0), axis=-1, keepdims=True)
        real = jnp.maximum(jnp.minimum(real_end, b0 + MOE_BLOCK) - jnp.maximum(seg_start, b0), 0.0)
        n_real = jnp.sum(jnp.where(is_expert, real, 0.0), axis=-1, keepdims=True)
        blocks_ref[...] = jnp.where(bl == 0, jnp.minimum(expert, N_EXPERTS - 1.0),
                                    jnp.where(bl == 1, n_real, 0.0)).astype(jnp.int32)

    sub = before_ref.shape[0]
    run = cnt_ref[0:1, :]
    pos = []
    for j in range(tm // sub):
        mj = member[j * sub:(j + 1) * sub]
        pos.append(run + jnp.dot(before_ref[...], mj.astype(BF16), preferred_element_type=F32))
        run = run + jnp.sum(mj, axis=0, keepdims=True)
    pos = jnp.concatenate(pos, axis=0)
    d0 = jnp.sum(jnp.where(lane == e0, pos, 0.0), axis=-1, keepdims=True)
    d1 = jnp.sum(jnp.where(lane == e1, pos, 0.0), axis=-1, keepdims=True)
    dest = jnp.where(lane == 0, d0, jnp.where(lane == 1, d1, 0.0))
    dest_ref[...] = dest.T[:SUBLANES].astype(jnp.int32)
    cnt_ref[...] = jnp.broadcast_to(run, cnt_ref.shape)


def _slots(rt, tile_counts, n_blocks, tm):
    t = rt.shape[0]
    return pl.pallas_call(
        _slots_kernel,
        grid=(t // tm,),
        in_specs=[pl.BlockSpec((tm, LANES), lambda i: (i, 0)),
                  pl.BlockSpec(tile_counts.shape, lambda i: (0, 0))],
        out_specs=[pl.BlockSpec((SUBLANES, tm), lambda i: (0, i)),
                   pl.BlockSpec((n_blocks, LANES), lambda i: (0, 0))],
        out_shape=[jax.ShapeDtypeStruct((SUBLANES, t), jnp.int32),
                   jax.ShapeDtypeStruct((n_blocks, LANES), jnp.int32)],
        scratch_shapes=[pltpu.VMEM((SUBLANES, LANES), F32), pltpu.VMEM((LANES, LANES), BF16)],
        compiler_params=_cparams("arbitrary"),
        name="moe_slots",
    )(rt, tile_counts)


def _sc_mesh():
    return plsc.VectorSubcoreMesh(core_axis_name="c", subcore_axis_name="s")


def _sc_worker():
    return lax.axis_index("s") * SC_CORES + lax.axis_index("c")


def _sc_scatter_rows(src, idx, n_rows):
    t = src.shape[0]
    per_w = t // SC_WORKERS
    n_win = per_w // SC_WINDOW
    idx = idx.reshape(TOP_K, SC_WORKERS, n_win, SC_WINDOW)

    @functools.partial(
        pl.kernel, mesh=_sc_mesh(),
        out_type=jax.ShapeDtypeStruct((n_rows,) + src.shape[1:], src.dtype),
        scratch_types=[pltpu.VMEM((TOP_K, n_win, SC_WINDOW), jnp.int32),
                       pltpu.VMEM((2, SC_WINDOW) + src.shape[1:], src.dtype),
                       pltpu.SemaphoreType.DMA((2,)), pltpu.SemaphoreType.DMA((2,))],
        compiler_params=pltpu.CompilerParams(use_tc_tiling_on_sc=True),
        name="sc_dispatch")
    def scatter(src_hbm, idx_hbm, out_hbm, idx_v, rows_v, lsem, ssem):
        wid = _sc_worker()
        base = wid * per_w
        for k in range(TOP_K):
            pltpu.sync_copy(idx_hbm.at[k, wid], idx_v.at[k])

        def load(w, slot):
            return pltpu.make_async_copy(src_hbm.at[pl.ds(base + w * SC_WINDOW, SC_WINDOW)], rows_v.at[slot],
                                         lsem.at[slot])

        def put(w, slot, k):
            return pltpu.make_async_copy(rows_v.at[slot], out_hbm.at[idx_v.at[k, w]], ssem.at[slot])

        load(0, 0).start()

        @pl.loop(0, n_win, step=2)
        def _(w0):
            for s in range(2):
                w = w0 + s

                @pl.when(w + 1 < n_win)
                def _():
                    @pl.when(w >= 1)
                    def _():
                        for k in range(TOP_K):
                            put(w - 1, 1 - s, k).wait()

                    load(w + 1, 1 - s).start()

                load(w, s).wait()
                for k in range(TOP_K):
                    put(w, s, k).start()

        for s in range(2):
            for k in range(TOP_K):
                put(n_win - 2 + s, s, k).wait()

    return scatter(src, idx)


def _sc_gather_rows(table, idx):
    b = idx.shape[0]
    per_w = b // SC_WORKERS
    n_win = per_w // SC_WINDOW
    idx = idx.reshape(SC_WORKERS, n_win, SC_WINDOW)

    @functools.partial(
        pl.kernel, mesh=_sc_mesh(),
        out_type=jax.ShapeDtypeStruct((b,) + table.shape[1:], table.dtype),
        scratch_types=[pltpu.VMEM((n_win, SC_WINDOW), jnp.int32),
                       pltpu.VMEM((2, SC_WINDOW) + table.shape[1:], table.dtype),
                       pltpu.SemaphoreType.DMA((2,)), pltpu.SemaphoreType.DMA((2,))],
        compiler_params=pltpu.CompilerParams(use_tc_tiling_on_sc=True),
        name="sc_combine_gather")
    def gather(table_hbm, idx_hbm, out_hbm, idx_v, rows_v, gsem, psem):
        wid = _sc_worker()
        base = wid * per_w
        pltpu.sync_copy(idx_hbm.at[wid], idx_v)

        def get(w, slot):
            return pltpu.make_async_copy(table_hbm.at[idx_v.at[w]], rows_v.at[slot], gsem.at[slot])

        def put(w, slot):
            return pltpu.make_async_copy(rows_v.at[slot], out_hbm.at[pl.ds(base + w * SC_WINDOW, SC_WINDOW)],
                                         psem.at[slot])

        get(0, 0).start()

        @pl.loop(0, n_win, step=2)
        def _(w0):
            for s in range(2):
                w = w0 + s

                @pl.when(w + 1 < n_win)
                def _():
                    @pl.when(w >= 1)
                    def _():
                        put(w - 1, 1 - s).wait()

                    get(w + 1, 1 - s).start()

                get(w, s).wait()
                put(w, s).start()

        for s in range(2):
            put(n_win - 2 + s, s).wait()

    return gather(table, idx)


def _experts_kernel(be_ref, nv_ref, xs_ref, wg_ref, wu_ref, wd_ref, ys_ref, wgb_ref, wub_ref, wdb_ref):
    b = pl.program_id(0)
    prev = be_ref[jnp.maximum(b - 1, 0)]

    @pl.when((b == 0) | (be_ref[b] != prev))
    def _():
        wgb_ref[...] = wg_ref[0].astype(BF16)
        wub_ref[...] = wu_ref[0].astype(BF16)
        wdb_ref[...] = wd_ref[0].astype(BF16)

    @pl.when(nv_ref[b] > 0)
    def _():
        row = lax.broadcasted_iota(jnp.int32, (MOE_BLOCK, D_MODEL), 0)
        xb = jnp.where(row < nv_ref[b], _load_token_rows(xs_ref, MOE_BLOCK, BF16), 0.0)
        hg = jnp.dot(xb, wgb_ref[...], preferred_element_type=F32)
        hu = jnp.dot(xb, wub_ref[...], preferred_element_type=F32)
        hb = (_silu(hg) * hu).astype(BF16)
        _store_token_rows(ys_ref, jnp.dot(hb, wdb_ref[...], preferred_element_type=F32))

    @pl.when(nv_ref[b] == 0)
    def _():
        ys_ref[...] = jnp.zeros_like(ys_ref)


def _experts(block_expert, n_valid, xs, wg, wu, wd):
    nb = xs.shape[0] // (MOE_BLOCK * ROW_TILES)
    blk = pl.BlockSpec((MOE_BLOCK * ROW_TILES, LANES), lambda b, be, nv: (b, 0))
    wspec = lambda a, c: pl.BlockSpec((1, a, c), lambda b, be, nv: (be[b], 0, 0))
    return pl.pallas_call(
        _experts_kernel,
        grid_spec=pltpu.PrefetchScalarGridSpec(
            num_scalar_prefetch=2,
            grid=(nb,),
            in_specs=[blk, wspec(D_MODEL, D_EXPERT), wspec(D_MODEL, D_EXPERT), wspec(D_EXPERT, D_MODEL)],
            out_specs=blk,
            scratch_shapes=[pltpu.VMEM((D_MODEL, D_EXPERT), BF16), pltpu.VMEM((D_MODEL, D_EXPERT), BF16),
                            pltpu.VMEM((D_EXPERT, D_MODEL), BF16)]),
        out_shape=jax.ShapeDtypeStruct(xs.shape, xs.dtype),
        compiler_params=_cparams("arbitrary"),
        name="moe_experts",
    )(block_expert, n_valid, xs, wg, wu, wd)


def _combine_kernel(h_ref, rt_ref, p_ref, y0_ref, y1_ref, gp_ref, wpg_ref, wp_ref, gf_ref, o_ref, *, sub):
    n_sub = h_ref.shape[0] // sub

    def residual(r):
        rows = slice(r * sub, (r + 1) * sub)
        rt = rt_ref[rows, :]
        h = h_ref[rows, :] + (_load_token_rows(y0_ref, sub, F32, r * sub) * rt[:, 2:3]
                              + _load_token_rows(y1_ref, sub, F32, r * sub) * rt[:, 3:4])
        return h, _rms(h, gp_ref[...]).astype(BF16)

    def products(r, xn):
        return (jnp.dot(xn, wpg_ref[...], preferred_element_type=F32),
                _dot(p_ref[r * sub:(r + 1) * sub, :], wp_ref[...]))

    def finish(r, h, gate_lin, ple):
        h = h + ple * _sigmoid(gate_lin)
        o_ref[r * sub:(r + 1) * sub, :] = _rms(h, gf_ref[...])

    h, xn = residual(0)
    for r in range(n_sub):
        gate_lin, ple = products(r, xn)
        if r + 1 < n_sub:
            h_next, xn = residual(r + 1)
        finish(r, h, gate_lin, ple)
        if r + 1 < n_sub:
            h = h_next


def _combine(h1, rt, p2, y, gp, wpg, wp, gf, tm):
    t = h1.shape[0]
    nt = t // tm
    tile = lambda n: pl.BlockSpec((tm, n), lambda i: (i, 0))
    full = lambda a, b: pl.BlockSpec((a, b), lambda i: (0, 0))
    ytile = lambda k: pl.BlockSpec((tm * ROW_TILES, LANES), lambda i: (i + k * nt, 0))
    return pl.pallas_call(
        functools.partial(_combine_kernel, sub=256),
        grid=(nt,),
        in_specs=[tile(D_MODEL), tile(LANES), tile(D_PLE), ytile(0), ytile(1),
                  full(1, D_MODEL), full(D_MODEL, D_MODEL), full(D_PLE, D_MODEL), full(1, D_MODEL)],
        out_specs=tile(D_MODEL),
        out_shape=jax.ShapeDtypeStruct((t, D_MODEL), F32),
        compiler_params=_cparams("parallel"),
        name="moe_combine_ple",
    )(h1, rt, p2, y, y, gp, wpg, wp, gf)


def _lane_row(vals, offset):
    return jnp.zeros((1, LANES), F32).at[0, offset:offset + vals.shape[0]].set(vals)


def kernel(x, p, norm_mix, w_in, lru_conv_w, lru_conv_b, lru_wa, lru_ba, lru_wi, lru_bi, lru_lambda,
           lru_out_norm, gdn_conv_w, gdn_a_log, gdn_dt_bias, gdn_out_norm, w_out, norm_ffn,
           w_router_group, b_router_group, w_router_expert, b_router_expert, w_exp_gate, w_exp_up,
           w_exp_down, norm_ple, w_ple_gate, w_ple, norm_final):
    bsz, seq, d = x.shape
    t = bsz * seq
    depth = w_in.shape[0]
    assert depth == 1, "the final norm is fused into the last layer's combine kernel"
    n_blocks = -(-t * TOP_K // MOE_BLOCK) + N_EXPERTS
    row = lambda v: v.reshape(1, -1).astype(F32)
    h = x.reshape(t, d).astype(F32)
    for l in range(depth):
        b_gates = jnp.concatenate([lru_ba[l], lru_bi[l]]).reshape(1, -1)
        proj_gdn, y_lru = _proj_lru(h, row(norm_mix[l]), w_in[l], lru_conv_w[l], row(lru_conv_b[l]), lru_wa[l],
                                    lru_wi[l], b_gates, row(lru_lambda[l]), row(lru_out_norm[l]), seq, 512)
        cw = gdn_conv_w[l].reshape(CONV_W, 3, D_GDN).transpose(1, 0, 2)
        w_r = jnp.pad(jnp.concatenate([w_router_group[l], w_router_expert[l]], axis=1),
                      ((0, 0), (0, LANES - N_GROUPS - N_EXPERTS)))
        w_r_hi = w_r.astype(BF16)
        w_r_lo = (w_r - w_r_hi.astype(F32)).astype(BF16)
        w_r = jnp.concatenate([w_r_hi, w_r_hi, w_r_lo], axis=0)
        b_r = _lane_row(jnp.concatenate([b_router_group[l], b_router_expert[l]]), 0)
        h1, xn2, rt, tile_counts = _gdn_router(
            proj_gdn, cw, _lane_row(gdn_a_log[l], GDN_HEADS), _lane_row(gdn_dt_bias[l], GDN_HEADS),
            row(gdn_out_norm[l]), h, y_lru, w_out[l].astype(BF16), row(norm_ffn[l]), w_r, b_r, seq, 512)
        dest8, blocks = _slots(rt, tile_counts, n_blocks, 2048)
        dest = dest8[:TOP_K]
        block_expert, n_valid = blocks[:, 0], blocks[:, 1]
        tiles = lambda a: a.reshape(-1, ROW_TILES, LANES)
        xs = _sc_scatter_rows(tiles(xn2), dest, n_blocks * MOE_BLOCK)
        ys = _experts(block_expert, n_valid, xs.reshape(-1, LANES), w_exp_gate[l], w_exp_up[l], w_exp_down[l])
        y = _sc_gather_rows(tiles(ys), dest.reshape(-1))
        h = _combine(h1, rt, p[l].reshape(t, -1).astype(F32), y.reshape(-1, LANES), row(norm_ple[l]),
                     w_ple_gate[l].astype(BF16), w_ple[l].astype(BF16), row(norm_final), 1024)
    return h.reshape(bsz, seq, d).astype(x.dtype)
```

```python
import functools

import jax
import jax.numpy as jnp
from jax import lax
from jax.experimental import pallas as pl
from jax.experimental.pallas import tpu as pltpu
from jax.experimental.pallas import tpu_sc as plsc

D_MODEL = 1024
D_LRU = 512
LRU_BLOCKS = 8
LRU_BLOCK_W = D_LRU // LRU_BLOCKS
LRU_C = 8.0
D_GDN = 512
GDN_HEADS = 4
GDN_HEAD_DIM = D_GDN // GDN_HEADS
CONV_W = 4
CHUNK = 64
N_GROUPS = 4
EXPERTS_PER_GROUP = 8
N_EXPERTS = N_GROUPS * EXPERTS_PER_GROUP
TOP_K = 2
D_EXPERT = 512
MOE_BLOCK = 512
D_PLE = 256
EPS = 1e-6

LANES = 128
SUBLANES = 8
ROW_TILES = D_MODEL // (2 * LANES)
SC_CORES = 2
SC_SUBCORES = 16
SC_WORKERS = SC_CORES * SC_SUBCORES
SC_WINDOW = 32
SC_SLOTS = 4
PROJ_COLS = 2 * D_LRU + 4 * D_GDN + LANES
VMEM_LIMIT = 56 * 1024 * 1024

BF16 = jnp.bfloat16
F32 = jnp.float32


def _cparams(*sem):
    return pltpu.CompilerParams(dimension_semantics=sem, vmem_limit_bytes=VMEM_LIMIT)


def _rms(x, g):
    return x * lax.rsqrt(jnp.mean(x * x, axis=-1, keepdims=True) + EPS) * g


def _sigmoid(x):
    return 0.5 * jnp.tanh(0.5 * x) + 0.5


def _silu(x):
    return x * _sigmoid(x)


def _dot(a, b):
    return jnp.dot(a.astype(BF16), b.astype(BF16), preferred_element_type=F32)


def _dot_nt(a, b):
    return lax.dot_general(a.astype(BF16), b.astype(BF16), (((1,), (1,)), ((), ())),
                           preferred_element_type=F32)


def _dot_tn(a, b):
    return lax.dot_general(a.astype(BF16), b.astype(BF16), (((0,), (0,)), ((), ())),
                           preferred_element_type=F32)


def _conv_scratch(rows, channels):
    return pltpu.VMEM((channels // LANES, 2 * (SUBLANES + rows), LANES), F32)


def _conv_reset(xp_ref):
    for k in range(xp_ref.shape[0]):
        xp_ref.at[k][pl.ds(0, SUBLANES, stride=2), :] = jnp.zeros((SUBLANES, LANES), F32)


def _causal_conv(xp_ref, x_ref, w, r0, rows, col0=0):
    out = []
    for k in range(xp_ref.shape[0]):
        lanes = slice(k * LANES, (k + 1) * LANES)
        slab = xp_ref.at[k]
        xk = x_ref[r0:r0 + rows, col0 + k * LANES:col0 + (k + 1) * LANES]
        slab[pl.ds(2 * (SUBLANES + r0), rows, stride=2), :] = xk
        acc = xk * w[CONV_W - 1:CONV_W, lanes]
        for j in range(1, CONV_W):
            acc = acc + slab[pl.ds(2 * (SUBLANES + r0 - j), rows, stride=2), :] * w[CONV_W - 1 - j:CONV_W - j, lanes]
        out.append(acc)
    return jnp.concatenate(out, axis=1)


def _conv_carry(xp_ref, x_ref, col0=0):
    ts = x_ref.shape[0]
    for k in range(xp_ref.shape[0]):
        xp_ref.at[k][pl.ds(0, SUBLANES, stride=2), :] = x_ref[ts - SUBLANES:, col0 + k * LANES:col0 + (k + 1) * LANES]


def _interleave(*stages):
    live = list(stages)
    while live:
        for g in list(live):
            try:
                next(g)
            except StopIteration:
                live.remove(g)


def _store_token_rows(ref, x, row0=0):
    rows = x.shape[0]
    bits = lax.bitcast_convert_type(x.astype(BF16).astype(F32), jnp.uint32)
    words = (bits[:, :D_MODEL // 2] >> 16) | bits[:, D_MODEL // 2:]
    for j in range(ROW_TILES):
        ref[pl.ds(row0 * ROW_TILES + j, rows, stride=ROW_TILES), :] = words[:, j * LANES:(j + 1) * LANES]


def _load_token_rows(ref, rows, dtype, row0=0):
    words = [ref[pl.ds(row0 * ROW_TILES + j, rows, stride=ROW_TILES), :] for j in range(ROW_TILES)]
    lo = [lax.bitcast_convert_type(w << 16, F32).astype(dtype) for w in words]
    hi = [lax.bitcast_convert_type(w & jnp.uint32(0xFFFF0000), F32).astype(dtype) for w in words]
    return jnp.concatenate(lo + hi, axis=1)


def _proj_lru_kernel(x_ref, gin_ref, win_ref, cw_ref, cb_ref, wa_ref, wi_ref, bg_ref, lam_ref, og_ref,
                     gdn_ref, o_ref, pj_ref, tail_ref, h_ref, wg_ref, wb_ref, *, tiles_per_seq, col_tile):
    g = pl.program_id(0)
    ts = x_ref.shape[0]
    pj_new = pj_ref.at[g % 2]
    pj = pj_ref.at[(g + 1) % 2]

    @pl.when(g == 0)
    def _():
        pj_ref[...] = jnp.zeros_like(pj_ref)
        in_cols = win_ref.shape[1]
        whole = in_cols // LANES * LANES
        for c0 in range(0, whole, col_tile):
            wb_ref[:, c0:c0 + col_tile] = win_ref[:, c0:c0 + col_tile].astype(BF16)
        wb_ref[:, whole:] = jnp.zeros((D_MODEL, PROJ_COLS - whole), BF16)
        wb_ref[:, whole:in_cols] = win_ref[:, whole:].astype(BF16)
        wg_ref[...] = jnp.zeros_like(wg_ref)
        for n in range(LRU_BLOCKS):
            blk = slice(n * LRU_BLOCK_W, (n + 1) * LRU_BLOCK_W)
            wg_ref[blk, blk] = wa_ref[n].astype(BF16)
            wg_ref[blk, D_LRU + n * LRU_BLOCK_W:D_LRU + (n + 1) * LRU_BLOCK_W] = wi_ref[n].astype(BF16)

    @pl.when((g == 0) | (lax.rem(g - 1, tiles_per_seq) == 0))
    def _():
        _conv_reset(tail_ref)
        h_ref[...] = jnp.zeros_like(h_ref)

    done = []

    def after_projection(v):
        return v + jnp.concatenate([done[max(len(done) - 2, 0)]] * (v.shape[-1] // LANES), axis=-1)

    def project():
        xn = _rms(x_ref[...], gin_ref[...]).astype(BF16)
        yield
        for c0 in range(0, PROJ_COLS, col_tile):
            c1 = min(c0 + col_tile, PROJ_COLS)
            y = jnp.dot(xn, wb_ref[:, c0:c1], preferred_element_type=F32)
            if c1 <= 2 * D_LRU:
                pj_new[:, c0:c1] = y
            else:
                gdn_ref[:, c0 - 2 * D_LRU:c1 - 2 * D_LRU] = y
            bits = lax.bitcast_convert_type(y[ts - 1:, c1 - c0 - LANES:], jnp.uint32)
            done.append(lax.bitcast_convert_type((bits >> 16) >> 16, F32))
            yield

    def recur():
        xc = _causal_conv(tail_ref, pj, cw_ref[...], 0, ts) + cb_ref[...]
        _conv_carry(tail_ref, pj)
        yield
        gates = _sigmoid(_dot(xc, wg_ref[...]) + after_projection(bg_ref[...]))
        r = gates[:, :D_LRU]
        i = gates[:, D_LRU:]
        yield
        log_a = LRU_C * r * jax.nn.log_sigmoid(after_projection(lam_ref[...]))
        a = jnp.exp(log_a)
        th = jnp.tanh(log_a)
        u = jnp.sqrt(-2.0 * th) * lax.rsqrt(1.0 - th) * (i * xc)
        yield
        a = a.reshape(ts // SUBLANES, SUBLANES, D_LRU)
        u = u.reshape(ts // SUBLANES, SUBLANES, D_LRU)
        row = lax.broadcasted_iota(jnp.int32, a.shape, 1)
        d = 1
        while d < SUBLANES:
            keep = row >= d
            a_prev = jnp.where(keep, pltpu.roll(a, d, axis=1), 1.0)
            u_prev = jnp.where(keep, pltpu.roll(u, d, axis=1), 0.0)
            u = a * u_prev + u
            a = a * a_prev
            d *= 2
            yield
        carry = after_projection(h_ref[...])
        groups = []
        for n in range(ts // SUBLANES):
            groups.append(a[n] * carry + u[n])
            carry = groups[-1][SUBLANES - 1:]
        h = jnp.concatenate(groups, axis=0)
        h_ref[...] = carry
        yield
        y = h * jax.nn.gelu(pj[:, D_LRU:])
        o_ref[...] = _rms(y, after_projection(og_ref[...])).astype(o_ref.dtype)

    _interleave(project(), recur())


def _proj_lru(x2, gin, win, cw, cb, wa, wi, bg, lam, og, seq, ts):
    t = x2.shape[0]
    n_tiles = t // ts
    this = lambda g: (jnp.minimum(g, n_tiles - 1), 0)
    last = lambda g: (jnp.maximum(g - 1, 0), 0)
    const = lambda *shape: pl.BlockSpec(shape, lambda g: (0,) * len(shape))
    gdn_cols = PROJ_COLS - 2 * D_LRU
    return pl.pallas_call(
        functools.partial(_proj_lru_kernel, tiles_per_seq=seq // ts, col_tile=D_LRU),
        grid=(n_tiles + 1,),
        in_specs=[pl.BlockSpec((ts, D_MODEL), this), const(1, D_MODEL),
                  pl.BlockSpec(win.shape, lambda g: (0, 0), pipeline_mode=pl.Buffered(1)),
                  const(CONV_W, D_LRU), const(1, D_LRU),
                  const(LRU_BLOCKS, LRU_BLOCK_W, LRU_BLOCK_W), const(LRU_BLOCKS, LRU_BLOCK_W, LRU_BLOCK_W),
                  const(1, 2 * D_LRU), const(1, D_LRU), const(1, D_LRU)],
        out_specs=[pl.BlockSpec((ts, gdn_cols), this), pl.BlockSpec((ts, D_LRU), last)],
        out_shape=[jax.ShapeDtypeStruct((t, gdn_cols), F32), jax.ShapeDtypeStruct((t, D_LRU), BF16)],
        scratch_shapes=[pltpu.VMEM((2, ts, 2 * D_LRU), F32), _conv_scratch(ts, D_LRU), pltpu.VMEM((1, D_LRU), F32),
                        pltpu.VMEM((D_LRU, 2 * D_LRU), BF16), pltpu.VMEM((D_MODEL, PROJ_COLS), BF16)],
        compiler_params=_cparams("arbitrary"),
        name="in_proj_rglru",
    )(x2, gin, win, cw, cb, wa, wi, bg, lam, og)


def _gdn_router_kernel(q_ref, k_ref, v_ref, z_ref, ba_ref, cw_ref, alog_ref, dtb_ref, og_ref,
                       x_ref, yl_ref, wo_ref, gf_ref, wr_ref, br_ref, h_ref, xn_ref, rt_ref, cnt_ref,
                       qt_ref, kt_ref, vt_ref, qs_ref, ks_ref, vs_ref, bs_ref, gc_ref, gct_ref, st_ref, yg_ref,
                       *, tiles_per_seq, group_chunks, prep_rows, sub):
    g = pl.program_id(0)
    ts = q_ref.shape[0]
    dk = GDN_HEAD_DIM
    nc = ts // CHUNK
    n_sub = ts // sub
    yg_new = yg_ref.at[g % 2]
    yg = yg_ref.at[(g + 1) % 2]

    @pl.when(g == 0)
    def _():
        yg_ref[...] = jnp.zeros_like(yg_ref)

    @pl.when(lax.rem(g, tiles_per_seq) == 0)
    def _():
        for tail_ref in (qt_ref, kt_ref, vt_ref):
            _conv_reset(tail_ref)
        st_ref[...] = jnp.zeros_like(st_ref)

    done = []
    hs = {}
    counts = []

    def after_projection(v):
        if not done:
            return v
        return v + jnp.concatenate([done[-1]] * (v.shape[-1] // LANES), axis=-1)

    def project_out():
        for r in range(n_sub):
            rows = slice(r * sub, (r + 1) * sub)
            h = x_ref[rows, :] + jnp.dot(yl_ref[rows, :], wo_ref[:D_LRU, :], preferred_element_type=F32) \
                + jnp.dot(yg[rows, :], wo_ref[D_LRU:, :], preferred_element_type=F32)
            h_ref[rows, :] = h
            hs[r] = h
            bits = lax.bitcast_convert_type(h[sub - 1:, D_MODEL - LANES:], jnp.uint32)
            done.append(lax.bitcast_convert_type((bits >> 16) >> 16, F32))
            yield

    def route(r, h):
        xn = _rms(h, gf_ref[...])
        _store_token_rows(xn_ref, xn, r * sub)
        xh = xn.astype(BF16)
        xl = (xn - xh.astype(F32)).astype(BF16)
        logits = jnp.dot(jnp.concatenate([xh, xl, xh], axis=1), wr_ref[...],
                         preferred_element_type=F32) + br_ref[...]
        lane = lax.broadcasted_iota(jnp.int32, logits.shape, 1).astype(F32)
        big = jnp.float32(2 * LANES)
        ninf = jnp.float32(-jnp.inf)

        def top1(vals):
            m = jnp.max(vals, axis=-1, keepdims=True)
            return m, jnp.min(jnp.where(vals == m, lane, big), axis=-1, keepdims=True)

        gl = jnp.where(lane < N_GROUPS, logits, ninf)
        gmax, gsel = top1(gl)
        p_group = 1.0 / jnp.sum(jnp.exp(gl - gmax), axis=-1, keepdims=True)
        lo = N_GROUPS + EXPERTS_PER_GROUP * gsel
        el = jnp.where((lane >= lo) & (lane < lo + EXPERTS_PER_GROUP), logits, ninf)
        m1, i1 = top1(el)
        m2, i2 = top1(jnp.where(lane == i1, ninf, el))
        rr = jnp.exp(m2 - m1)
        g1 = p_group / (1.0 + rr)
        g2 = p_group * rr / (1.0 + rr)
        rt_ref[r * sub:(r + 1) * sub, :] = jnp.where(
            lane == 0, i1 - N_GROUPS,
            jnp.where(lane == 1, i2 - N_GROUPS, jnp.where(lane == 2, g1, jnp.where(lane == 3, g2, 0.0))))
        member = ((lane == i1 - N_GROUPS) | (lane == i2 - N_GROUPS)).astype(F32)
        counts.append(jnp.sum(member, axis=0, keepdims=True))

    def route_subtiles(subtiles):
        for r in subtiles:
            route(r, hs[r])
            yield

    ri = lax.broadcasted_iota(jnp.int32, (CHUNK, CHUNK), 0)
    ci = lax.broadcasted_iota(jnp.int32, (CHUNK, CHUNK), 1)
    causal = ri >= ci
    strict = ri > ci
    tril = causal.astype(F32)
    eye = (ri == ci).astype(F32)
    og = og_ref[...]

    def l2n(x, scale):
        parts = []
        for h in range(GDN_HEADS):
            xh = x[:, h * dk:(h + 1) * dk]
            parts.append(xh * (lax.rsqrt(jnp.sum(xh * xh, axis=-1, keepdims=True) + EPS) * scale))
        return jnp.concatenate(parts, axis=1)

    def prepare(r0, r1):
        for p0 in range(r0, r1, prep_rows):
            rows = slice(p0, p0 + prep_rows)
            conv = lambda x_ref, tail_ref, part: _silu(
                _causal_conv(tail_ref, x_ref, after_projection(cw_ref[part]), p0, prep_rows))
            qs_ref[rows, :] = l2n(conv(q_ref, qt_ref, 0), dk ** -0.5)
            yield
            ks_ref[rows, :] = l2n(conv(k_ref, kt_ref, 1), 1.0)
            yield
            vs_ref[rows, :] = conv(v_ref, vt_ref, 2)
            ba = ba_ref[rows, :]
            bs_ref[rows, :] = _sigmoid(ba)
            g = -jnp.exp(alog_ref[...]) * jax.nn.softplus(ba + dtb_ref[...])
            for c0 in range(0, prep_rows, CHUNK):
                gc_ref[p0 + c0:p0 + c0 + CHUNK, :] = jnp.dot(tril, g[c0:c0 + CHUNK], precision=lax.Precision.HIGHEST,
                                                             preferred_element_type=F32)
            gct_ref[:, rows] = gc_ref[rows, :].T
            yield

    terms = {}

    def chunk_terms(pairs):
        n = range(len(pairs))
        rows = [slice(c * CHUNK, (c + 1) * CHUNK) for c, _ in pairs]
        cols = [slice(h * dk, (h + 1) * dk) for _, h in pairs]
        gl = [GDN_HEADS + h for _, h in pairs]
        kh = [ks_ref[rows[i], cols[i]] for i in n]
        kb = [kh[i] * bs_ref[rows[i], pairs[i][1]:pairs[i][1] + 1] for i in n]
        r = [_dot_nt(jnp.concatenate([kb[i], qs_ref[rows[i], cols[i]]], axis=0), kh[i]) for i in n]
        yield
        gcol = [gc_ref[rows[i], gl[i]:gl[i] + 1] for i in n]
        decay = []
        for i in n:
            diff = gcol[i] - gct_ref[gl[i]:gl[i] + 1, rows[i]]
            decay.append(jnp.where(causal, jnp.exp(jnp.where(causal, diff, 0.0)), 0.0))
        a = [jnp.where(strict, r[i][:CHUNK] * decay[i], 0.0) for i in n]
        qk = [(r[i][CHUNK:] * decay[i]).astype(BF16) for i in n]
        tinv = [eye - a[i] for i in n]
        p = 2
        while p < CHUNK:
            a = [_dot(a[i], a[i]) for i in n]
            yield
            tinv = [tinv[i] + _dot(tinv[i], a[i]) for i in n]
            yield
            p *= 2
        eg = [jnp.exp(gcol[i]) for i in n]
        rhs = [jnp.concatenate([vs_ref[rows[i], cols[i]] * bs_ref[rows[i], pairs[i][1]:pairs[i][1] + 1],
                                kb[i] * eg[i]], axis=1) for i in n]
        uw = [_dot(tinv[i], rhs[i]).astype(BF16) for i in n]
        yield
        qk_uw = [jnp.dot(qk[i], uw[i], preferred_element_type=F32) for i in n]
        glast = [gc_ref[(c + 1) * CHUNK - 1:(c + 1) * CHUNK, gl[i]:gl[i] + 1] for i, (c, _) in enumerate(pairs)]
        kd_uw = [_dot_tn(kh[i] * jnp.exp(glast[i] - gcol[i]), uw[i]) for i in n]
        yield
        for i in n:
            lhs = jnp.concatenate([kd_uw[i][:, dk:], qs_ref[rows[i], cols[i]] * eg[i] - qk_uw[i][:, dk:]],
                                  axis=0).astype(BF16)
            terms[pairs[i]] = (lhs, kd_uw[i][:, :dk], qk_uw[i][:, :dk], jnp.exp(glast[i]))

    state = [st_ref[h] for h in range(GDN_HEADS)]

    def advance(chunks):
        for c in chunks:
            rows = slice(c * CHUNK, (c + 1) * CHUNK)
            r = [jnp.dot(terms[c, h][0], state[h].astype(BF16), preferred_element_type=F32)
                 for h in range(GDN_HEADS)]
            for h in range(GDN_HEADS):
                cols = slice(h * dk, (h + 1) * dk)
                _, c_add, o_add, egl = terms[c, h]
                o = r[h][dk:] + o_add
                state[h] = egl * state[h] - r[h][:dk] + c_add
                yg_new[rows, cols] = (_rms(o, og) * _silu(z_ref[rows, cols])).astype(yg_new.dtype)
            yield

    groups = [range(c0, c0 + group_chunks) for c0 in range(0, nc, group_chunks)]
    pairs_of = lambda chunks: [(c, h) for c in chunks for h in range(GDN_HEADS)]
    span = lambda chunks: (chunks[0] * CHUNK, (chunks[-1] + 1) * CHUNK)
    per_phase = max(1, n_sub // (len(groups) + 1))
    _interleave(prepare(*span(groups[0])), project_out())
    for i, chunks in enumerate(groups):
        side = [route_subtiles(range(i * per_phase, (i + 1) * per_phase))]
        if i + 1 < len(groups):
            side.append(prepare(*span(groups[i + 1])))
        if i > 0:
            side.append(advance(groups[i - 1]))
        _interleave(chunk_terms(pairs_of(chunks)), *side)
    _interleave(advance(groups[-1]), route_subtiles(range(len(groups) * per_phase, n_sub)))
    for h in range(GDN_HEADS):
        st_ref[h] = state[h]
    for x_in, tail_ref in ((q_ref, qt_ref), (k_ref, kt_ref), (v_ref, vt_ref)):
        _conv_carry(tail_ref, x_in)
    cnt_ref[...] = jnp.broadcast_to(sum(counts), cnt_ref.shape)


def _gdn_router(proj, cw, alog, dtb, og, x2, yl, wo, gf, wr, br, seq, ts):
    t = x2.shape[0]
    n_tiles = t // ts
    this = lambda c: (lambda g: (jnp.minimum(g, n_tiles - 1), c))
    last = lambda g: (jnp.maximum(g - 1, 0), 0)
    const = lambda *shape: pl.BlockSpec(shape, lambda g: (0,) * len(shape))
    col = lambda c: pl.BlockSpec((ts, D_GDN), this(c))
    return pl.pallas_call(
        functools.partial(_gdn_router_kernel, tiles_per_seq=seq // ts, group_chunks=4, prep_rows=128, sub=256),
        grid=(n_tiles + 1,),
        in_specs=[col(0), col(1), col(2), col(3), pl.BlockSpec((ts, LANES), this(4 * D_GDN // LANES)),
                  const(3, CONV_W, D_GDN), const(1, LANES), const(1, LANES), const(1, GDN_HEAD_DIM),
                  pl.BlockSpec((ts, D_MODEL), last), pl.BlockSpec((ts, D_LRU), last), const(D_MODEL, D_MODEL),
                  const(1, D_MODEL), const(3 * D_MODEL, LANES), const(1, LANES)],
        out_specs=[pl.BlockSpec((ts, D_MODEL), last), pl.BlockSpec((ts * ROW_TILES, LANES), last),
                   pl.BlockSpec((ts, LANES), last), pl.BlockSpec((SUBLANES, LANES), last)],
        out_shape=[jax.ShapeDtypeStruct((t, D_MODEL), F32),
                   jax.ShapeDtypeStruct((t * ROW_TILES, LANES), jnp.uint32),
                   jax.ShapeDtypeStruct((t, LANES), F32),
                   jax.ShapeDtypeStruct((n_tiles * SUBLANES, LANES), F32)],
        scratch_shapes=[_conv_scratch(ts, D_GDN)] * 3
        + [pltpu.VMEM((ts, D_GDN), F32)] * 3
        + [pltpu.VMEM((ts, LANES), F32)] * 2
        + [pltpu.VMEM((LANES, ts), F32)]
        + [pltpu.VMEM((GDN_HEADS, GDN_HEAD_DIM, GDN_HEAD_DIM), F32)]
        + [pltpu.VMEM((2, ts, D_GDN), BF16)],
        compiler_params=_cparams("arbitrary"),
        name="gdn_out_router",
    )(proj, proj, proj, proj, proj, cw, alog, dtb, og, x2, yl, wo, gf, wr, br)


def _slots_kernel(rt_ref, tc_ref, dest_ref, blocks_ref, cnt_ref, before_ref):
    i = pl.program_id(0)
    tm = rt_ref.shape[0]
    rt = rt_ref[...]
    lane = lax.broadcasted_iota(jnp.int32, rt.shape, 1)
    e0 = rt[:, 0:1].astype(jnp.int32)
    e1 = rt[:, 1:2].astype(jnp.int32)
    member = ((lane == e0) | (lane == e1)).astype(F32)

    @pl.when(i == 0)
    def _():
        ri = lax.broadcasted_iota(jnp.int32, before_ref.shape, 0)
        ci = lax.broadcasted_iota(jnp.int32, before_ref.shape, 1)
        before_ref[...] = (ri > ci).astype(BF16)
        cnt = jnp.broadcast_to(jnp.sum(tc_ref[...], axis=0, keepdims=True) / SUBLANES, cnt_ref.shape)
        padded = jnp.ceil(cnt / MOE_BLOCK) * MOE_BLOCK
        l8 = lax.broadcasted_iota(jnp.int32, cnt.shape, 1)
        incl = padded
        d = 1
        while d < LANES:
            incl = incl + jnp.where(l8 >= d, pltpu.roll(incl, d, axis=1), 0.0)
            d *= 2
        cnt_ref[...] = incl - padded
        seg_start = (incl - padded)[0:1, :]
        seg_end = incl[0:1, :]
        real_end = seg_start + cnt[0:1, :]
        bl = lax.broadcasted_iota(jnp.int32, blocks_ref.shape, 1)
        b0 = (lax.broadcasted_iota(jnp.int32, blocks_ref.shape, 0) * MOE_BLOCK).astype(F32)
        is_expert = bl < N_EXPERTS
        expert = jnp.sum(jnp.where(is_expert & (seg_end <= b0), 1.0, 0.0), axis=-1, keepdims=True)
        real = jnp.maximum(jnp.minimum(real_end, b0 + MOE_BLOCK) - jnp.maximum(seg_start, b0), 0.0)
        n_real = jnp.sum(jnp.where(is_expert, real, 0.0), axis=-1, keepdims=True)
        expert = jnp.minimum(expert, N_EXPERTS - 1.0)
        later = is_expert & (cnt[0:1, :] > 0.0) & (bl.astype(F32) > expert)
        nxt = jnp.min(jnp.where(later, bl.astype(F32), float(N_EXPERTS)), axis=-1, keepdims=True)
        nxt = jnp.where(nxt < N_EXPERTS, nxt, expert)
        blocks_ref[...] = jnp.where(bl == 0, expert, jnp.where(bl == 1, n_real, jnp.where(bl == 2, nxt, 0.0))
                                    ).astype(jnp.int32)

    sub = before_ref.shape[0]
    run = cnt_ref[0:1, :]
    pos = []
    for j in range(tm // sub):
        mj = member[j * sub:(j + 1) * sub]
        pos.append(run + jnp.dot(before_ref[...], mj.astype(BF16), preferred_element_type=F32))
        run = run + jnp.sum(mj, axis=0, keepdims=True)
    pos = jnp.concatenate(pos, axis=0)
    d0 = jnp.sum(jnp.where(lane == e0, pos, 0.0), axis=-1, keepdims=True)
    d1 = jnp.sum(jnp.where(lane == e1, pos, 0.0), axis=-1, keepdims=True)
    dest = jnp.where(lane == 0, d0, jnp.where(lane == 1, d1, 0.0))
    dest_ref[...] = dest.T[:SUBLANES].astype(jnp.int32)
    cnt_ref[...] = jnp.broadcast_to(run, cnt_ref.shape)


def _slots(rt, tile_counts, n_blocks, tm):
    t = rt.shape[0]
    return pl.pallas_call(
        _slots_kernel,
        grid=(t // tm,),
        in_specs=[pl.BlockSpec((tm, LANES), lambda i: (i, 0)),
                  pl.BlockSpec(tile_counts.shape, lambda i: (0, 0))],
        out_specs=[pl.BlockSpec((SUBLANES, tm), lambda i: (0, i)),
                   pl.BlockSpec((n_blocks, LANES), lambda i: (0, 0))],
        out_shape=[jax.ShapeDtypeStruct((SUBLANES, t), jnp.int32),
                   jax.ShapeDtypeStruct((n_blocks, LANES), jnp.int32)],
        scratch_shapes=[pltpu.VMEM((SUBLANES, LANES), F32), pltpu.VMEM((LANES, LANES), BF16)],
        compiler_params=_cparams("arbitrary"),
        name="moe_slots",
    )(rt, tile_counts)


def _sc_mesh():
    return plsc.VectorSubcoreMesh(core_axis_name="c", subcore_axis_name="s")


def _sc_worker():
    return lax.axis_index("s") * SC_CORES + lax.axis_index("c")


def _sc_scatter_rows(src, idx, n_rows):
    t = src.shape[0]
    per_w = t // SC_WORKERS
    n_win = per_w // SC_WINDOW
    assert per_w % SC_WINDOW == 0 and n_win % SC_SLOTS == 0, "every worker walks whole rings of windows"
    idx = idx.reshape(TOP_K, SC_WORKERS, n_win, SC_WINDOW)

    @functools.partial(
        pl.kernel, mesh=_sc_mesh(),
        out_type=jax.ShapeDtypeStruct((n_rows,) + src.shape[1:], src.dtype),
        scratch_types=[pltpu.VMEM((TOP_K, n_win, SC_WINDOW), jnp.int32),
                       pltpu.VMEM((SC_SLOTS, SC_WINDOW) + src.shape[1:], src.dtype),
                       pltpu.SemaphoreType.DMA((SC_SLOTS,)), pltpu.SemaphoreType.DMA((SC_SLOTS,))],
        compiler_params=pltpu.CompilerParams(use_tc_tiling_on_sc=True),
        name="sc_dispatch")
    def scatter(src_hbm, idx_hbm, out_hbm, idx_v, rows_v, lsem, ssem):
        wid = _sc_worker()
        base = wid * per_w
        for k in range(TOP_K):
            pltpu.sync_copy(idx_hbm.at[k, wid], idx_v.at[k])

        def load(w, slot):
            return pltpu.make_async_copy(src_hbm.at[pl.ds(base + w * SC_WINDOW, SC_WINDOW)], rows_v.at[slot],
                                         lsem.at[slot])

        def put(w, slot, k):
            return pltpu.make_async_copy(rows_v.at[slot], out_hbm.at[idx_v.at[k, w]], ssem.at[slot])

        for s in range(SC_SLOTS - 1):
            load(s, s).start()

        @pl.loop(0, n_win, step=SC_SLOTS)
        def _(w0):
            for s in range(SC_SLOTS):
                w = w0 + s
                prev = (s - 1) % SC_SLOTS

                @pl.when(w + SC_SLOTS - 1 < n_win)
                def _():
                    @pl.when(w >= 1)
                    def _():
                        for k in range(TOP_K):
                            put(w - 1, prev, k).wait()

                    load(w + SC_SLOTS - 1, prev).start()

                load(w, s).wait()
                for k in range(TOP_K):
                    put(w, s, k).start()

        for s in range(SC_SLOTS):
            for k in range(TOP_K):
                put(n_win - SC_SLOTS + s, s, k).wait()

    return scatter(src, idx)


def _sc_gather_rows(table, idx):
    b = idx.shape[0]
    per_w = b // SC_WORKERS
    n_win = per_w // SC_WINDOW
    assert per_w % SC_WINDOW == 0 and n_win % SC_SLOTS == 0, "every worker walks whole rings of windows"
    idx = idx.reshape(SC_WORKERS, n_win, SC_WINDOW)

    @functools.partial(
        pl.kernel, mesh=_sc_mesh(),
        out_type=jax.ShapeDtypeStruct((b,) + table.shape[1:], table.dtype),
        scratch_types=[pltpu.VMEM((n_win, SC_WINDOW), jnp.int32),
                       pltpu.VMEM((SC_SLOTS, SC_WINDOW) + table.shape[1:], table.dtype),
                       pltpu.SemaphoreType.DMA((SC_SLOTS,)), pltpu.SemaphoreType.DMA((SC_SLOTS,))],
        compiler_params=pltpu.CompilerParams(use_tc_tiling_on_sc=True),
        name="sc_combine_gather")
    def gather(table_hbm, idx_hbm, out_hbm, idx_v, rows_v, gsem, psem):
        wid = _sc_worker()
        base = wid * per_w
        pltpu.sync_copy(idx_hbm.at[wid], idx_v)

        def get(w, slot):
            return pltpu.make_async_copy(table_hbm.at[idx_v.at[w]], rows_v.at[slot], gsem.at[slot])

        def put(w, slot):
            return pltpu.make_async_copy(rows_v.at[slot], out_hbm.at[pl.ds(base + w * SC_WINDOW, SC_WINDOW)],
                                         psem.at[slot])

        for s in range(SC_SLOTS - 1):
            get(s, s).start()

        @pl.loop(0, n_win, step=SC_SLOTS)
        def _(w0):
            for s in range(SC_SLOTS):
                w = w0 + s
                prev = (s - 1) % SC_SLOTS

                @pl.when(w + SC_SLOTS - 1 < n_win)
                def _():
                    @pl.when(w >= 1)
                    def _():
                        put(w - 1, prev).wait()

                    get(w + SC_SLOTS - 1, prev).start()

                get(w, s).wait()
                put(w, s).start()

        for s in range(SC_SLOTS):
            put(n_win - SC_SLOTS + s, s).wait()

    return gather(table, idx)


def _expert_weight_copies(e, slot, w_hbm, w_buf, sem):
    return [pltpu.make_async_copy(w.at[e], buf.at[slot], sem.at[slot, i]) for i, (w, buf) in enumerate(zip(w_hbm, w_buf))]


def _experts_kernel(be_ref, nv_ref, nx_ref, xs_ref, wg_hbm, wu_hbm, wd_hbm, ys_ref,
                    wgf_ref, wuf_ref, wdf_ref, wgb_ref, wub_ref, wdb_ref, slot_ref, sem):
    b = pl.program_id(0)
    e = be_ref[b]
    active = nv_ref[b] > 0
    w_hbm = (wg_hbm, wu_hbm, wd_hbm)
    w_buf = (wgf_ref, wuf_ref, wdf_ref)

    @pl.when(b == 0)
    def _():
        slot_ref[0] = 0
        for c in _expert_weight_copies(e, 0, w_hbm, w_buf, sem):
            c.start()

    @pl.when(active & ((b == 0) | (be_ref[jnp.maximum(b - 1, 0)] != e)))
    def _():
        slot = slot_ref[0]
        for c in _expert_weight_copies(e, slot, w_hbm, w_buf, sem):
            c.wait()
        wgb_ref[...] = wgf_ref[slot].astype(BF16)
        wub_ref[...] = wuf_ref[slot].astype(BF16)
        wdb_ref[...] = wdf_ref[slot].astype(BF16)
        nxt = nx_ref[b]

        @pl.when(nxt != e)
        def _():
            for c in _expert_weight_copies(nxt, 1 - slot, w_hbm, w_buf, sem):
                c.start()

        slot_ref[0] = 1 - slot

    @pl.when(active)
    def _():
        row = lax.broadcasted_iota(jnp.int32, (MOE_BLOCK, D_MODEL), 0)
        xb = jnp.where(row < nv_ref[b], _load_token_rows(xs_ref, MOE_BLOCK, BF16), 0.0)
        hg = jnp.dot(xb, wgb_ref[...], preferred_element_type=F32)
        hu = jnp.dot(xb, wub_ref[...], preferred_element_type=F32)
        hb = (_silu(hg) * hu).astype(BF16)
        _store_token_rows(ys_ref, jnp.dot(hb, wdb_ref[...], preferred_element_type=F32))

    @pl.when(jnp.logical_not(active))
    def _():
        ys_ref[...] = jnp.zeros_like(ys_ref)


def _experts(block_expert, n_valid, next_expert, xs, wg, wu, wd):
    nb = xs.shape[0] // (MOE_BLOCK * ROW_TILES)
    blk = pl.BlockSpec((MOE_BLOCK * ROW_TILES, LANES), lambda b, be, nv, nx: (b, 0))
    hbm = pl.BlockSpec(memory_space=pl.ANY)
    return pl.pallas_call(
        _experts_kernel,
        grid_spec=pltpu.PrefetchScalarGridSpec(
            num_scalar_prefetch=3,
            grid=(nb,),
            in_specs=[blk, hbm, hbm, hbm],
            out_specs=blk,
            scratch_shapes=[pltpu.VMEM((2, D_MODEL, D_EXPERT), F32), pltpu.VMEM((2, D_MODEL, D_EXPERT), F32),
                            pltpu.VMEM((2, D_EXPERT, D_MODEL), F32),
                            pltpu.VMEM((D_MODEL, D_EXPERT), BF16), pltpu.VMEM((D_MODEL, D_EXPERT), BF16),
                            pltpu.VMEM((D_EXPERT, D_MODEL), BF16),
                            pltpu.SMEM((1,), jnp.int32), pltpu.SemaphoreType.DMA((2, 3))]),
        out_shape=jax.ShapeDtypeStruct(xs.shape, xs.dtype),
        compiler_params=_cparams("arbitrary"),
        name="moe_experts",
    )(block_expert, n_valid, next_expert, xs, wg, wu, wd)


def _combine_kernel(h_ref, rt_ref, p_ref, y0_ref, y1_ref, gp_ref, wpg_ref, wp_ref, gf_ref, o_ref, *, sub):
    n_sub = h_ref.shape[0] // sub

    def residual(r):
        rows = slice(r * sub, (r + 1) * sub)
        rt = rt_ref[rows, :]
        h = h_ref[rows, :] + (_load_token_rows(y0_ref, sub, F32, r * sub) * rt[:, 2:3]
                              + _load_token_rows(y1_ref, sub, F32, r * sub) * rt[:, 3:4])
        return h, _rms(h, gp_ref[...]).astype(BF16)

    def products(r, xn):
        return (jnp.dot(xn, wpg_ref[...], preferred_element_type=F32),
                _dot(p_ref[r * sub:(r + 1) * sub, :], wp_ref[...]))

    def finish(r, h, gate_lin, ple):
        h = h + ple * _sigmoid(gate_lin)
        o_ref[r * sub:(r + 1) * sub, :] = _rms(h, gf_ref[...])

    h, xn = residual(0)
    for r in range(n_sub):
        gate_lin, ple = products(r, xn)
        if r + 1 < n_sub:
            h_next, xn = residual(r + 1)
        finish(r, h, gate_lin, ple)
        if r + 1 < n_sub:
            h = h_next


def _combine(h1, rt, p2, y, gp, wpg, wp, gf, tm):
    t = h1.shape[0]
    nt = t // tm
    tile = lambda n: pl.BlockSpec((tm, n), lambda i: (i, 0))
    full = lambda a, b: pl.BlockSpec((a, b), lambda i: (0, 0))
    ytile = lambda k: pl.BlockSpec((tm * ROW_TILES, LANES), lambda i: (i + k * nt, 0))
    return pl.pallas_call(
        functools.partial(_combine_kernel, sub=256),
        grid=(nt,),
        in_specs=[tile(D_MODEL), tile(LANES), tile(D_PLE), ytile(0), ytile(1),
                  full(1, D_MODEL), full(D_MODEL, D_MODEL), full(D_PLE, D_MODEL), full(1, D_MODEL)],
        out_specs=tile(D_MODEL),
        out_shape=jax.ShapeDtypeStruct((t, D_MODEL), F32),
        compiler_params=_cparams("parallel"),
        name="moe_combine_ple",
    )(h1, rt, p2, y, y, gp, wpg, wp, gf)


def _lane_row(vals, offset):
    return jnp.zeros((1, LANES), F32).at[0, offset:offset + vals.shape[0]].set(vals)


def kernel(x, p, norm_mix, w_in, lru_conv_w, lru_conv_b, lru_wa, lru_ba, lru_wi, lru_bi, lru_lambda,
           lru_out_norm, gdn_conv_w, gdn_a_log, gdn_dt_bias, gdn_out_norm, w_out, norm_ffn,
           w_router_group, b_router_group, w_router_expert, b_router_expert, w_exp_gate, w_exp_up,
           w_exp_down, norm_ple, w_ple_gate, w_ple, norm_final):
    bsz, seq, d = x.shape
    t = bsz * seq
    depth = w_in.shape[0]
    assert depth == 1, "the final norm is fused into the last layer's combine kernel"
    n_blocks = -(-t * TOP_K // MOE_BLOCK) + N_EXPERTS
    row = lambda v: v.reshape(1, -1).astype(F32)
    h = x.reshape(t, d).astype(F32)
    for l in range(depth):
        b_gates = jnp.concatenate([lru_ba[l], lru_bi[l]]).reshape(1, -1)
        proj_gdn, y_lru = _proj_lru(h, row(norm_mix[l]), w_in[l], lru_conv_w[l], row(lru_conv_b[l]), lru_wa[l],
                                    lru_wi[l], b_gates, row(lru_lambda[l]), row(lru_out_norm[l]), seq, 512)
        cw = gdn_conv_w[l].reshape(CONV_W, 3, D_GDN).transpose(1, 0, 2)
        w_r = jnp.pad(jnp.concatenate([w_router_group[l], w_router_expert[l]], axis=1),
                      ((0, 0), (0, LANES - N_GROUPS - N_EXPERTS)))
        w_r_hi = w_r.astype(BF16)
        w_r_lo = (w_r - w_r_hi.astype(F32)).astype(BF16)
        w_r = jnp.concatenate([w_r_hi, w_r_hi, w_r_lo], axis=0)
        b_r = _lane_row(jnp.concatenate([b_router_group[l], b_router_expert[l]]), 0)
        h1, xn2, rt, tile_counts = _gdn_router(
            proj_gdn, cw, _lane_row(gdn_a_log[l], GDN_HEADS), _lane_row(gdn_dt_bias[l], GDN_HEADS),
            row(gdn_out_norm[l]), h, y_lru, w_out[l].astype(BF16), row(norm_ffn[l]), w_r, b_r, seq, 512)
        dest8, blocks = _slots(rt, tile_counts, n_blocks, 2048)
        dest = dest8[:TOP_K]
        block_expert, n_valid, next_expert = blocks[:, 0], blocks[:, 1], blocks[:, 2]
        tiles = lambda a: a.reshape(-1, ROW_TILES, LANES)
        xs = _sc_scatter_rows(tiles(xn2), dest, n_blocks * MOE_BLOCK)
        ys = _experts(block_expert, n_valid, next_expert, xs.reshape(-1, LANES), w_exp_gate[l], w_exp_up[l],
                      w_exp_down[l])
        y = _sc_gather_rows(tiles(ys), dest.reshape(-1))
        h = _combine(h1, rt, p[l].reshape(t, -1).astype(F32), y.reshape(-1, LANES), row(norm_ple[l]),
                     w_ple_gate[l].astype(BF16), w_ple[l].astype(BF16), row(norm_final), 1024)
    return h.reshape(bsz, seq, d).astype(x.dtype)
```

```python
import functools

import jax
import jax.numpy as jnp
from jax import lax
from jax.experimental import pallas as pl
from jax.experimental.pallas import tpu as pltpu
from jax.experimental.pallas import tpu_sc as plsc

D_MODEL = 1024
D_LRU = 512
LRU_BLOCKS = 8
LRU_BLOCK_W = D_LRU // LRU_BLOCKS
LRU_C = 8.0
D_GDN = 512
GDN_HEADS = 4
GDN_HEAD_DIM = D_GDN // GDN_HEADS
CONV_W = 4
CHUNK = 64
N_GROUPS = 4
EXPERTS_PER_GROUP = 8
N_EXPERTS = N_GROUPS * EXPERTS_PER_GROUP
TOP_K = 2
D_EXPERT = 512
MOE_BLOCK = 512
D_PLE = 256
EPS = 1e-6

LANES = 128
SUBLANES = 8
ROW_TILES = D_MODEL // (2 * LANES)
SC_CORES = 2
SC_SUBCORES = 16
SC_WORKERS = SC_CORES * SC_SUBCORES
SC_WINDOW = 32
SC_SLOTS = 4
PROJ_COLS = 2 * D_LRU + 4 * D_GDN + LANES
VMEM_LIMIT = 56 * 1024 * 1024

BF16 = jnp.bfloat16
F32 = jnp.float32


def _cparams(*sem):
    return pltpu.CompilerParams(dimension_semantics=sem, vmem_limit_bytes=VMEM_LIMIT)


def _rms(x, g):
    return x * lax.rsqrt(jnp.mean(x * x, axis=-1, keepdims=True) + EPS) * g


def _sigmoid(x):
    return 0.5 * jnp.tanh(0.5 * x) + 0.5


def _silu(x):
    return x * _sigmoid(x)


def _dot(a, b):
    return jnp.dot(a.astype(BF16), b.astype(BF16), preferred_element_type=F32)


def _dot_nt(a, b):
    return lax.dot_general(a.astype(BF16), b.astype(BF16), (((1,), (1,)), ((), ())),
                           preferred_element_type=F32)


def _dot_tn(a, b):
    return lax.dot_general(a.astype(BF16), b.astype(BF16), (((0,), (0,)), ((), ())),
                           preferred_element_type=F32)


def _conv_scratch(rows, channels):
    return pltpu.VMEM((channels // LANES, 2 * (SUBLANES + rows), LANES), F32)


def _conv_reset(xp_ref):
    for k in range(xp_ref.shape[0]):
        xp_ref.at[k][pl.ds(0, SUBLANES, stride=2), :] = jnp.zeros((SUBLANES, LANES), F32)


def _causal_conv(xp_ref, x_ref, w, r0, rows, col0=0):
    out = []
    for k in range(xp_ref.shape[0]):
        lanes = slice(k * LANES, (k + 1) * LANES)
        slab = xp_ref.at[k]
        xk = x_ref[r0:r0 + rows, col0 + k * LANES:col0 + (k + 1) * LANES]
        slab[pl.ds(2 * (SUBLANES + r0), rows, stride=2), :] = xk
        acc = xk * w[CONV_W - 1:CONV_W, lanes]
        for j in range(1, CONV_W):
            acc = acc + slab[pl.ds(2 * (SUBLANES + r0 - j), rows, stride=2), :] * w[CONV_W - 1 - j:CONV_W - j, lanes]
        out.append(acc)
    return jnp.concatenate(out, axis=1)


def _conv_carry(xp_ref, x_ref, col0=0):
    ts = x_ref.shape[0]
    for k in range(xp_ref.shape[0]):
        xp_ref.at[k][pl.ds(0, SUBLANES, stride=2), :] = x_ref[ts - SUBLANES:, col0 + k * LANES:col0 + (k + 1) * LANES]


def _interleave(*stages):
    live = list(stages)
    while live:
        for g in list(live):
            try:
                next(g)
            except StopIteration:
                live.remove(g)


def _store_token_rows(ref, x, row0=0):
    rows = x.shape[0]
    bits = lax.bitcast_convert_type(x.astype(BF16).astype(F32), jnp.uint32)
    words = (bits[:, :D_MODEL // 2] >> 16) | bits[:, D_MODEL // 2:]
    for j in range(ROW_TILES):
        ref[pl.ds(row0 * ROW_TILES + j, rows, stride=ROW_TILES), :] = words[:, j * LANES:(j + 1) * LANES]


def _load_token_rows(ref, rows, dtype, row0=0):
    words = [ref[pl.ds(row0 * ROW_TILES + j, rows, stride=ROW_TILES), :] for j in range(ROW_TILES)]
    lo = [lax.bitcast_convert_type(w << 16, F32).astype(dtype) for w in words]
    hi = [lax.bitcast_convert_type(w & jnp.uint32(0xFFFF0000), F32).astype(dtype) for w in words]
    return jnp.concatenate(lo + hi, axis=1)


def _proj_lru_kernel(x_ref, gin_ref, win_ref, cw_ref, cb_ref, wa_ref, wi_ref, bg_ref, lam_ref, og_ref,
                     gdn_ref, o_ref, pj_ref, tail_ref, h_ref, wg_ref, wb_ref, *, tiles_per_seq, col_tile):
    g = pl.program_id(0)
    ts = x_ref.shape[0]
    pj_new = pj_ref.at[g % 2]
    pj = pj_ref.at[(g + 1) % 2]

    @pl.when(g == 0)
    def _():
        pj_ref[...] = jnp.zeros_like(pj_ref)
        in_cols = win_ref.shape[1]
        whole = in_cols // LANES * LANES
        for c0 in range(0, whole, col_tile):
            wb_ref[:, c0:c0 + col_tile] = win_ref[:, c0:c0 + col_tile].astype(BF16)
        wb_ref[:, whole:] = jnp.zeros((D_MODEL, PROJ_COLS - whole), BF16)
        wb_ref[:, whole:in_cols] = win_ref[:, whole:].astype(BF16)
        wg_ref[...] = jnp.zeros_like(wg_ref)
        for n in range(LRU_BLOCKS):
            blk = slice(n * LRU_BLOCK_W, (n + 1) * LRU_BLOCK_W)
            wg_ref[blk, blk] = wa_ref[n].astype(BF16)
            wg_ref[blk, D_LRU + n * LRU_BLOCK_W:D_LRU + (n + 1) * LRU_BLOCK_W] = wi_ref[n].astype(BF16)

    @pl.when((g == 0) | (lax.rem(g - 1, tiles_per_seq) == 0))
    def _():
        _conv_reset(tail_ref)
        h_ref[...] = jnp.zeros_like(h_ref)

    done = []

    def after_projection(v):
        return v + jnp.concatenate([done[max(len(done) - 2, 0)]] * (v.shape[-1] // LANES), axis=-1)

    def project():
        xn = _rms(x_ref[...], gin_ref[...]).astype(BF16)
        yield
        for c0 in range(0, PROJ_COLS, col_tile):
            c1 = min(c0 + col_tile, PROJ_COLS)
            y = jnp.dot(xn, wb_ref[:, c0:c1], preferred_element_type=F32)
            if c1 <= 2 * D_LRU:
                pj_new[:, c0:c1] = y
            else:
                gdn_ref[:, c0 - 2 * D_LRU:c1 - 2 * D_LRU] = y
            bits = lax.bitcast_convert_type(y[ts - 1:, c1 - c0 - LANES:], jnp.uint32)
            done.append(lax.bitcast_convert_type((bits >> 16) >> 16, F32))
            yield

    def recur():
        xc = _causal_conv(tail_ref, pj, cw_ref[...], 0, ts) + cb_ref[...]
        _conv_carry(tail_ref, pj)
        yield
        gates = _sigmoid(_dot(xc, wg_ref[...]) + after_projection(bg_ref[...]))
        r = gates[:, :D_LRU]
        i = gates[:, D_LRU:]
        yield
        log_a = LRU_C * r * jax.nn.log_sigmoid(after_projection(lam_ref[...]))
        a = jnp.exp(log_a)
        th = jnp.tanh(log_a)
        u = jnp.sqrt(-2.0 * th) * lax.rsqrt(1.0 - th) * (i * xc)
        yield
        a = a.reshape(ts // SUBLANES, SUBLANES, D_LRU)
        u = u.reshape(ts // SUBLANES, SUBLANES, D_LRU)
        row = lax.broadcasted_iota(jnp.int32, a.shape, 1)
        d = 1
        while d < SUBLANES:
            keep = row >= d
            a_prev = jnp.where(keep, pltpu.roll(a, d, axis=1), 1.0)
            u_prev = jnp.where(keep, pltpu.roll(u, d, axis=1), 0.0)
            u = a * u_prev + u
            a = a * a_prev
            d *= 2
            yield
        carry = after_projection(h_ref[...])
        groups = []
        for n in range(ts // SUBLANES):
            groups.append(a[n] * carry + u[n])
            carry = groups[-1][SUBLANES - 1:]
        h = jnp.concatenate(groups, axis=0)
        h_ref[...] = carry
        yield
        y = h * jax.nn.gelu(pj[:, D_LRU:])
        o_ref[...] = _rms(y, after_projection(og_ref[...])).astype(o_ref.dtype)

    _interleave(project(), recur())


def _proj_lru(x2, gin, win, cw, cb, wa, wi, bg, lam, og, seq, ts):
    t = x2.shape[0]
    n_tiles = t // ts
    this = lambda g: (jnp.minimum(g, n_tiles - 1), 0)
    last = lambda g: (jnp.maximum(g - 1, 0), 0)
    const = lambda *shape: pl.BlockSpec(shape, lambda g: (0,) * len(shape))
    gdn_cols = PROJ_COLS - 2 * D_LRU
    return pl.pallas_call(
        functools.partial(_proj_lru_kernel, tiles_per_seq=seq // ts, col_tile=D_LRU),
        grid=(n_tiles + 1,),
        in_specs=[pl.BlockSpec((ts, D_MODEL), this), const(1, D_MODEL),
                  pl.BlockSpec(win.shape, lambda g: (0, 0), pipeline_mode=pl.Buffered(1)),
                  const(CONV_W, D_LRU), const(1, D_LRU),
                  const(LRU_BLOCKS, LRU_BLOCK_W, LRU_BLOCK_W), const(LRU_BLOCKS, LRU_BLOCK_W, LRU_BLOCK_W),
                  const(1, 2 * D_LRU), const(1, D_LRU), const(1, D_LRU)],
        out_specs=[pl.BlockSpec((ts, gdn_cols), this), pl.BlockSpec((ts, D_LRU), last)],
        out_shape=[jax.ShapeDtypeStruct((t, gdn_cols), F32), jax.ShapeDtypeStruct((t, D_LRU), BF16)],
        scratch_shapes=[pltpu.VMEM((2, ts, 2 * D_LRU), F32), _conv_scratch(ts, D_LRU), pltpu.VMEM((1, D_LRU), F32),
                        pltpu.VMEM((D_LRU, 2 * D_LRU), BF16), pltpu.VMEM((D_MODEL, PROJ_COLS), BF16)],
        compiler_params=_cparams("arbitrary"),
        name="in_proj_rglru",
    )(x2, gin, win, cw, cb, wa, wi, bg, lam, og)


def _gdn_router_kernel(pj_ref, cw_ref, alog_ref, dtb_ref, og_ref,
                       x_ref, yl_ref, wo_ref, gf_ref, wr_ref, br_ref, h_ref, xn_ref, rt_ref, cnt_ref,
                       qt_ref, kt_ref, vt_ref, qs_ref, ks_ref, vs_ref, bs_ref, gc_ref, gct_ref, st_ref, yg_ref,
                       *, tiles_per_seq, group_chunks, prep_rows, sub):
    g = pl.program_id(0)
    ts = pj_ref.shape[0]
    dk = GDN_HEAD_DIM
    nc = ts // CHUNK
    n_sub = ts // sub
    q0, k0, v0, z0, ba0 = 0, D_GDN, 2 * D_GDN, 3 * D_GDN, 4 * D_GDN
    yg_new = yg_ref.at[g % 2]
    yg = yg_ref.at[(g + 1) % 2]

    @pl.when(g == 0)
    def _():
        yg_ref[...] = jnp.zeros_like(yg_ref)

    @pl.when(lax.rem(g, tiles_per_seq) == 0)
    def _():
        for tail_ref in (qt_ref, kt_ref, vt_ref):
            _conv_reset(tail_ref)
        st_ref[...] = jnp.zeros_like(st_ref)

    done = []
    hs = {}
    counts = []

    def after_projection(v):
        if not done:
            return v
        return v + jnp.concatenate([done[-1]] * (v.shape[-1] // LANES), axis=-1)

    def project_out():
        for r in range(n_sub):
            rows = slice(r * sub, (r + 1) * sub)
            h = x_ref[rows, :] + jnp.dot(yl_ref[rows, :], wo_ref[:D_LRU, :], preferred_element_type=F32) \
                + jnp.dot(yg[rows, :], wo_ref[D_LRU:, :], preferred_element_type=F32)
            h_ref[rows, :] = h
            hs[r] = h
            bits = lax.bitcast_convert_type(h[sub - 1:, D_MODEL - LANES:], jnp.uint32)
            done.append(lax.bitcast_convert_type((bits >> 16) >> 16, F32))
            yield

    def route(r, h):
        xn = _rms(h, gf_ref[...])
        _store_token_rows(xn_ref, xn, r * sub)
        xh = xn.astype(BF16)
        xl = (xn - xh.astype(F32)).astype(BF16)
        logits = jnp.dot(jnp.concatenate([xh, xl, xh], axis=1), wr_ref[...],
                         preferred_element_type=F32) + br_ref[...]
        lane = lax.broadcasted_iota(jnp.int32, logits.shape, 1).astype(F32)
        big = jnp.float32(2 * LANES)
        ninf = jnp.float32(-jnp.inf)

        def top1(vals):
            m = jnp.max(vals, axis=-1, keepdims=True)
            return m, jnp.min(jnp.where(vals == m, lane, big), axis=-1, keepdims=True)

        gl = jnp.where(lane < N_GROUPS, logits, ninf)
        gmax, gsel = top1(gl)
        p_group = 1.0 / jnp.sum(jnp.exp(gl - gmax), axis=-1, keepdims=True)
        lo = N_GROUPS + EXPERTS_PER_GROUP * gsel
        el = jnp.where((lane >= lo) & (lane < lo + EXPERTS_PER_GROUP), logits, ninf)
        m1, i1 = top1(el)
        m2, i2 = top1(jnp.where(lane == i1, ninf, el))
        rr = jnp.exp(m2 - m1)
        g1 = p_group / (1.0 + rr)
        g2 = p_group * rr / (1.0 + rr)
        rt_ref[r * sub:(r + 1) * sub, :] = jnp.where(
            lane == 0, i1 - N_GROUPS,
            jnp.where(lane == 1, i2 - N_GROUPS, jnp.where(lane == 2, g1, jnp.where(lane == 3, g2, 0.0))))
        member = ((lane == i1 - N_GROUPS) | (lane == i2 - N_GROUPS)).astype(F32)
        counts.append(jnp.sum(member, axis=0, keepdims=True))

    def route_subtiles(subtiles):
        for r in subtiles:
            route(r, hs[r])
            yield

    ri = lax.broadcasted_iota(jnp.int32, (CHUNK, CHUNK), 0)
    ci = lax.broadcasted_iota(jnp.int32, (CHUNK, CHUNK), 1)
    causal = ri >= ci
    strict = ri > ci
    tril = causal.astype(F32)
    eye = (ri == ci).astype(F32)
    og = og_ref[...]

    def l2n(x, scale):
        parts = []
        for h in range(GDN_HEADS):
            xh = x[:, h * dk:(h + 1) * dk]
            parts.append(xh * (lax.rsqrt(jnp.sum(xh * xh, axis=-1, keepdims=True) + EPS) * scale))
        return jnp.concatenate(parts, axis=1)

    def prepare(r0, r1):
        for p0 in range(r0, r1, prep_rows):
            rows = slice(p0, p0 + prep_rows)
            conv = lambda tail_ref, part, col0: _silu(
                _causal_conv(tail_ref, pj_ref, after_projection(cw_ref[part]), p0, prep_rows, col0))
            qs_ref[rows, :] = l2n(conv(qt_ref, 0, q0), dk ** -0.5)
            yield
            ks_ref[rows, :] = l2n(conv(kt_ref, 1, k0), 1.0)
            yield
            vs_ref[rows, :] = conv(vt_ref, 2, v0)
            ba = pj_ref[rows, ba0:ba0 + LANES]
            bs_ref[rows, :] = _sigmoid(ba)
            g = -jnp.exp(alog_ref[...]) * jax.nn.softplus(ba + dtb_ref[...])
            for c0 in range(0, prep_rows, CHUNK):
                gc_ref[p0 + c0:p0 + c0 + CHUNK, :] = jnp.dot(tril, g[c0:c0 + CHUNK], precision=lax.Precision.HIGHEST,
                                                             preferred_element_type=F32)
            gct_ref[:, rows] = gc_ref[rows, :].T
            yield

    terms = {}

    def chunk_terms(pairs):
        n = range(len(pairs))
        rows = [slice(c * CHUNK, (c + 1) * CHUNK) for c, _ in pairs]
        cols = [slice(h * dk, (h + 1) * dk) for _, h in pairs]
        gl = [GDN_HEADS + h for _, h in pairs]
        kh = [ks_ref[rows[i], cols[i]] for i in n]
        kb = [kh[i] * bs_ref[rows[i], pairs[i][1]:pairs[i][1] + 1] for i in n]
        r = [_dot_nt(jnp.concatenate([kb[i], qs_ref[rows[i], cols[i]]], axis=0), kh[i]) for i in n]
        yield
        gcol = [gc_ref[rows[i], gl[i]:gl[i] + 1] for i in n]
        decay = []
        for i in n:
            diff = gcol[i] - gct_ref[gl[i]:gl[i] + 1, rows[i]]
            decay.append(jnp.where(causal, jnp.exp(jnp.where(causal, diff, 0.0)), 0.0))
        a = [jnp.where(strict, r[i][:CHUNK] * decay[i], 0.0) for i in n]
        qk = [(r[i][CHUNK:] * decay[i]).astype(BF16) for i in n]
        tinv = [eye - a[i] for i in n]
        p = 2
        while p < CHUNK:
            a = [_dot(a[i], a[i]) for i in n]
            yield
            tinv = [tinv[i] + _dot(tinv[i], a[i]) for i in n]
            yield
            p *= 2
        eg = [jnp.exp(gcol[i]) for i in n]
        rhs = [jnp.concatenate([vs_ref[rows[i], cols[i]] * bs_ref[rows[i], pairs[i][1]:pairs[i][1] + 1],
                                kb[i] * eg[i]], axis=1) for i in n]
        uw = [_dot(tinv[i], rhs[i]).astype(BF16) for i in n]
        yield
        qk_uw = [jnp.dot(qk[i], uw[i], preferred_element_type=F32) for i in n]
        glast = [gc_ref[(c + 1) * CHUNK - 1:(c + 1) * CHUNK, gl[i]:gl[i] + 1] for i, (c, _) in enumerate(pairs)]
        kd_uw = [_dot_tn(kh[i] * jnp.exp(glast[i] - gcol[i]), uw[i]) for i in n]
        yield
        for i in n:
            lhs = jnp.concatenate([kd_uw[i][:, dk:], qs_ref[rows[i], cols[i]] * eg[i] - qk_uw[i][:, dk:]],
                                  axis=0).astype(BF16)
            terms[pairs[i]] = (lhs, kd_uw[i][:, :dk], qk_uw[i][:, :dk], jnp.exp(glast[i]))

    state = [st_ref[h] for h in range(GDN_HEADS)]

    def advance(chunks):
        for c in chunks:
            rows = slice(c * CHUNK, (c + 1) * CHUNK)
            r = [jnp.dot(terms[c, h][0], state[h].astype(BF16), preferred_element_type=F32)
                 for h in range(GDN_HEADS)]
            for h in range(GDN_HEADS):
                cols = slice(h * dk, (h + 1) * dk)
                _, c_add, o_add, egl = terms[c, h]
                o = r[h][dk:] + o_add
                state[h] = egl * state[h] - r[h][:dk] + c_add
                zh = pj_ref[rows, z0 + h * dk:z0 + (h + 1) * dk]
                yg_new[rows, cols] = (_rms(o, og) * _silu(zh)).astype(yg_new.dtype)
            yield

    groups = [range(c0, c0 + group_chunks) for c0 in range(0, nc, group_chunks)]
    pairs_of = lambda chunks: [(c, h) for c in chunks for h in range(GDN_HEADS)]
    span = lambda chunks: (chunks[0] * CHUNK, (chunks[-1] + 1) * CHUNK)
    per_phase = max(1, n_sub // (len(groups) + 1))
    _interleave(prepare(*span(groups[0])), project_out())
    for i, chunks in enumerate(groups):
        side = [route_subtiles(range(i * per_phase, (i + 1) * per_phase))]
        if i + 1 < len(groups):
            side.append(prepare(*span(groups[i + 1])))
        if i > 0:
            side.append(advance(groups[i - 1]))
        _interleave(chunk_terms(pairs_of(chunks)), *side)
    _interleave(advance(groups[-1]), route_subtiles(range(len(groups) * per_phase, n_sub)))
    for h in range(GDN_HEADS):
        st_ref[h] = state[h]
    for tail_ref, col0 in ((qt_ref, q0), (kt_ref, k0), (vt_ref, v0)):
        _conv_carry(tail_ref, pj_ref, col0)
    cnt_ref[...] = jnp.broadcast_to(sum(counts), cnt_ref.shape)


def _gdn_router(proj, cw, alog, dtb, og, x2, yl, wo, gf, wr, br, seq, ts):
    t = x2.shape[0]
    n_tiles = t // ts
    this = lambda g: (jnp.minimum(g, n_tiles - 1), 0)
    last = lambda g: (jnp.maximum(g - 1, 0), 0)
    const = lambda *shape: pl.BlockSpec(shape, lambda g: (0,) * len(shape))
    return pl.pallas_call(
        functools.partial(_gdn_router_kernel, tiles_per_seq=seq // ts, group_chunks=4, prep_rows=128, sub=256),
        grid=(n_tiles + 1,),
        in_specs=[pl.BlockSpec((ts, proj.shape[1]), this),
                  const(3, CONV_W, D_GDN), const(1, LANES), const(1, LANES), const(1, GDN_HEAD_DIM),
                  pl.BlockSpec((ts, D_MODEL), last), pl.BlockSpec((ts, D_LRU), last), const(D_MODEL, D_MODEL),
                  const(1, D_MODEL), const(3 * D_MODEL, LANES), const(1, LANES)],
        out_specs=[pl.BlockSpec((ts, D_MODEL), last), pl.BlockSpec((ts * ROW_TILES, LANES), last),
                   pl.BlockSpec((ts, LANES), last), pl.BlockSpec((SUBLANES, LANES), last)],
        out_shape=[jax.ShapeDtypeStruct((t, D_MODEL), F32),
                   jax.ShapeDtypeStruct((t * ROW_TILES, LANES), jnp.uint32),
                   jax.ShapeDtypeStruct((t, LANES), F32),
                   jax.ShapeDtypeStruct((n_tiles * SUBLANES, LANES), F32)],
        scratch_shapes=[_conv_scratch(ts, D_GDN)] * 3
        + [pltpu.VMEM((ts, D_GDN), F32)] * 3
        + [pltpu.VMEM((ts, LANES), F32)] * 2
        + [pltpu.VMEM((LANES, ts), F32)]
        + [pltpu.VMEM((GDN_HEADS, GDN_HEAD_DIM, GDN_HEAD_DIM), F32)]
        + [pltpu.VMEM((2, ts, D_GDN), BF16)],
        compiler_params=_cparams("arbitrary"),
        name="gdn_out_router",
    )(proj, cw, alog, dtb, og, x2, yl, wo, gf, wr, br)


def _slots_kernel(rt_ref, tc_ref, dest_ref, blocks_ref, cnt_ref, before_ref):
    i = pl.program_id(0)
    tm = rt_ref.shape[0]
    rt = rt_ref[...]
    lane = lax.broadcasted_iota(jnp.int32, rt.shape, 1)
    e0 = rt[:, 0:1].astype(jnp.int32)
    e1 = rt[:, 1:2].astype(jnp.int32)
    member = ((lane == e0) | (lane == e1)).astype(F32)

    @pl.when(i == 0)
    def _():
        ri = lax.broadcasted_iota(jnp.int32, before_ref.shape, 0)
        ci = lax.broadcasted_iota(jnp.int32, before_ref.shape, 1)
        before_ref[...] = (ri > ci).astype(BF16)
        cnt = jnp.broadcast_to(jnp.sum(tc_ref[...], axis=0, keepdims=True) / SUBLANES, cnt_ref.shape)
        padded = jnp.ceil(cnt / MOE_BLOCK) * MOE_BLOCK
        l8 = lax.broadcasted_iota(jnp.int32, cnt.shape, 1)
        incl = padded
        d = 1
        while d < LANES:
            incl = incl + jnp.where(l8 >= d, pltpu.roll(incl, d, axis=1), 0.0)
            d *= 2
        cnt_ref[...] = incl - padded
        seg_start = (incl - padded)[0:1, :]
        seg_end = incl[0:1, :]
        real_end = seg_start + cnt[0:1, :]
        bl = lax.broadcasted_iota(jnp.int32, blocks_ref.shape, 1)
        b0 = (lax.broadcasted_iota(jnp.int32, blocks_ref.shape, 0) * MOE_BLOCK).astype(F32)
        is_expert = bl < N_EXPERTS
        expert = jnp.sum(jnp.where(is_expert & (seg_end <= b0), 1.0, 0.0), axis=-1, keepdims=True)
        real = jnp.maximum(jnp.minimum(real_end, b0 + MOE_BLOCK) - jnp.maximum(seg_start, b0), 0.0)
        n_real = jnp.sum(jnp.where(is_expert, real, 0.0), axis=-1, keepdims=True)
        expert = jnp.minimum(expert, N_EXPERTS - 1.0)
        later = is_expert & (cnt[0:1, :] > 0.0) & (bl.astype(F32) > expert)
        nxt = jnp.min(jnp.where(later, bl.astype(F32), float(N_EXPERTS)), axis=-1, keepdims=True)
        nxt = jnp.where(nxt < N_EXPERTS, nxt, expert)
        blocks_ref[...] = jnp.where(bl == 0, expert, jnp.where(bl == 1, n_real, jnp.where(bl == 2, nxt, 0.0))
                                    ).astype(jnp.int32)

    sub = before_ref.shape[0]
    run = cnt_ref[0:1, :]
    pos = []
    for j in range(tm // sub):
        mj = member[j * sub:(j + 1) * sub]
        pos.append(run + jnp.dot(before_ref[...], mj.astype(BF16), preferred_element_type=F32))
        run = run + jnp.sum(mj, axis=0, keepdims=True)
    pos = jnp.concatenate(pos, axis=0)
    d0 = jnp.sum(jnp.where(lane == e0, pos, 0.0), axis=-1, keepdims=True)
    d1 = jnp.sum(jnp.where(lane == e1, pos, 0.0), axis=-1, keepdims=True)
    dest = jnp.where(lane == 0, d0, jnp.where(lane == 1, d1, 0.0))
    dest_ref[...] = dest.T[:SUBLANES].astype(jnp.int32)
    cnt_ref[...] = jnp.broadcast_to(run, cnt_ref.shape)


def _slots(rt, tile_counts, n_blocks, tm):
    t = rt.shape[0]
    return pl.pallas_call(
        _slots_kernel,
        grid=(t // tm,),
        in_specs=[pl.BlockSpec((tm, LANES), lambda i: (i, 0)),
                  pl.BlockSpec(tile_counts.shape, lambda i: (0, 0))],
        out_specs=[pl.BlockSpec((SUBLANES, tm), lambda i: (0, i)),
                   pl.BlockSpec((n_blocks, LANES), lambda i: (0, 0))],
        out_shape=[jax.ShapeDtypeStruct((SUBLANES, t), jnp.int32),
                   jax.ShapeDtypeStruct((n_blocks, LANES), jnp.int32)],
        scratch_shapes=[pltpu.VMEM((SUBLANES, LANES), F32), pltpu.VMEM((LANES, LANES), BF16)],
        compiler_params=_cparams("arbitrary"),
        name="moe_slots",
    )(rt, tile_counts)


def _sc_mesh():
    return plsc.VectorSubcoreMesh(core_axis_name="c", subcore_axis_name="s")


def _sc_worker():
    return lax.axis_index("s") * SC_CORES + lax.axis_index("c")


def _sc_scatter_rows(src, idx, n_rows):
    t = src.shape[0]
    per_w = t // SC_WORKERS
    n_win = per_w // SC_WINDOW
    assert per_w % SC_WINDOW == 0 and n_win % SC_SLOTS == 0, "every worker walks whole rings of windows"
    idx = idx.reshape(TOP_K, SC_WORKERS, n_win, SC_WINDOW)

    @functools.partial(
        pl.kernel, mesh=_sc_mesh(),
        out_type=jax.ShapeDtypeStruct((n_rows,) + src.shape[1:], src.dtype),
        scratch_types=[pltpu.VMEM((TOP_K, n_win, SC_WINDOW), jnp.int32),
                       pltpu.VMEM((SC_SLOTS, SC_WINDOW) + src.shape[1:], src.dtype),
                       pltpu.SemaphoreType.DMA((SC_SLOTS,)), pltpu.SemaphoreType.DMA((SC_SLOTS,))],
        compiler_params=pltpu.CompilerParams(use_tc_tiling_on_sc=True),
        name="sc_dispatch")
    def scatter(src_hbm, idx_hbm, out_hbm, idx_v, rows_v, lsem, ssem):
        wid = _sc_worker()
        base = wid * per_w
        for k in range(TOP_K):
            pltpu.sync_copy(idx_hbm.at[k, wid], idx_v.at[k])

        def load(w, slot):
            return pltpu.make_async_copy(src_hbm.at[pl.ds(base + w * SC_WINDOW, SC_WINDOW)], rows_v.at[slot],
                                         lsem.at[slot])

        def put(w, slot, k):
            return pltpu.make_async_copy(rows_v.at[slot], out_hbm.at[idx_v.at[k, w]], ssem.at[slot])

        for s in range(SC_SLOTS - 1):
            load(s, s).start()

        @pl.loop(0, n_win, step=SC_SLOTS)
        def _(w0):
            for s in range(SC_SLOTS):
                w = w0 + s
                prev = (s - 1) % SC_SLOTS

                @pl.when(w + SC_SLOTS - 1 < n_win)
                def _():
                    @pl.when(w >= 1)
                    def _():
                        for k in range(TOP_K):
                            put(w - 1, prev, k).wait()

                    load(w + SC_SLOTS - 1, prev).start()

                load(w, s).wait()
                for k in range(TOP_K):
                    put(w, s, k).start()

        for s in range(SC_SLOTS):
            for k in range(TOP_K):
                put(n_win - SC_SLOTS + s, s, k).wait()

    return scatter(src, idx)


def _sc_gather_rows(table, idx):
    b = idx.shape[0]
    per_w = b // SC_WORKERS
    n_win = per_w // SC_WINDOW
    assert per_w % SC_WINDOW == 0 and n_win % SC_SLOTS == 0, "every worker walks whole rings of windows"
    idx = idx.reshape(SC_WORKERS, n_win, SC_WINDOW)

    @functools.partial(
        pl.kernel, mesh=_sc_mesh(),
        out_type=jax.ShapeDtypeStruct((b,) + table.shape[1:], table.dtype),
        scratch_types=[pltpu.VMEM((n_win, SC_WINDOW), jnp.int32),
                       pltpu.VMEM((SC_SLOTS, SC_WINDOW) + table.shape[1:], table.dtype),
                       pltpu.SemaphoreType.DMA((SC_SLOTS,)), pltpu.SemaphoreType.DMA((SC_SLOTS,))],
        compiler_params=pltpu.CompilerParams(use_tc_tiling_on_sc=True),
        name="sc_combine_gather")
    def gather(table_hbm, idx_hbm, out_hbm, idx_v, rows_v, gsem, psem):
        wid = _sc_worker()
        base = wid * per_w
        pltpu.sync_copy(idx_hbm.at[wid], idx_v)

        def get(w, slot):
            return pltpu.make_async_copy(table_hbm.at[idx_v.at[w]], rows_v.at[slot], gsem.at[slot])

        def put(w, slot):
            return pltpu.make_async_copy(rows_v.at[slot], out_hbm.at[pl.ds(base + w * SC_WINDOW, SC_WINDOW)],
                                         psem.at[slot])

        for s in range(SC_SLOTS - 1):
            get(s, s).start()

        @pl.loop(0, n_win, step=SC_SLOTS)
        def _(w0):
            for s in range(SC_SLOTS):
                w = w0 + s
                prev = (s - 1) % SC_SLOTS

                @pl.when(w + SC_SLOTS - 1 < n_win)
                def _():
                    @pl.when(w >= 1)
                    def _():
                        put(w - 1, prev).wait()

                    get(w + SC_SLOTS - 1, prev).start()

                get(w, s).wait()
                put(w, s).start()

        for s in range(SC_SLOTS):
            put(n_win - SC_SLOTS + s, s).wait()

    return gather(table, idx)


def _expert_weight_copies(e, slot, w_hbm, w_buf, sem):
    return [pltpu.make_async_copy(w.at[e], buf.at[slot], sem.at[slot, i]) for i, (w, buf) in enumerate(zip(w_hbm, w_buf))]


def _experts_kernel(be_ref, nv_ref, nx_ref, xs_ref, wg_hbm, wu_hbm, wd_hbm, ys_ref,
                    wgf_ref, wuf_ref, wdf_ref, wgb_ref, wub_ref, wdb_ref, slot_ref, sem):
    b = pl.program_id(0)
    e = be_ref[b]
    active = nv_ref[b] > 0
    w_hbm = (wg_hbm, wu_hbm, wd_hbm)
    w_buf = (wgf_ref, wuf_ref, wdf_ref)

    @pl.when(b == 0)
    def _():
        slot_ref[0] = 0
        for c in _expert_weight_copies(e, 0, w_hbm, w_buf, sem):
            c.start()

    @pl.when(active & ((b == 0) | (be_ref[jnp.maximum(b - 1, 0)] != e)))
    def _():
        slot = slot_ref[0]
        for c in _expert_weight_copies(e, slot, w_hbm, w_buf, sem):
            c.wait()
        wgb_ref[...] = wgf_ref[slot].astype(BF16)
        wub_ref[...] = wuf_ref[slot].astype(BF16)
        wdb_ref[...] = wdf_ref[slot].astype(BF16)
        nxt = nx_ref[b]

        @pl.when(nxt != e)
        def _():
            for c in _expert_weight_copies(nxt, 1 - slot, w_hbm, w_buf, sem):
                c.start()

        slot_ref[0] = 1 - slot

    @pl.when(active)
    def _():
        row = lax.broadcasted_iota(jnp.int32, (MOE_BLOCK, D_MODEL), 0)
        xb = jnp.where(row < nv_ref[b], _load_token_rows(xs_ref, MOE_BLOCK, BF16), 0.0)
        hg = jnp.dot(xb, wgb_ref[...], preferred_element_type=F32)
        hu = jnp.dot(xb, wub_ref[...], preferred_element_type=F32)
        hb = (_silu(hg) * hu).astype(BF16)
        _store_token_rows(ys_ref, jnp.dot(hb, wdb_ref[...], preferred_element_type=F32))

    @pl.when(jnp.logical_not(active))
    def _():
        ys_ref[...] = jnp.zeros_like(ys_ref)


def _experts(block_expert, n_valid, next_expert, xs, wg, wu, wd):
    nb = xs.shape[0] // (MOE_BLOCK * ROW_TILES)
    blk = pl.BlockSpec((MOE_BLOCK * ROW_TILES, LANES), lambda b, be, nv, nx: (b, 0))
    hbm = pl.BlockSpec(memory_space=pl.ANY)
    return pl.pallas_call(
        _experts_kernel,
        grid_spec=pltpu.PrefetchScalarGridSpec(
            num_scalar_prefetch=3,
            grid=(nb,),
            in_specs=[blk, hbm, hbm, hbm],
            out_specs=blk,
            scratch_shapes=[pltpu.VMEM((2, D_MODEL, D_EXPERT), F32), pltpu.VMEM((2, D_MODEL, D_EXPERT), F32),
                            pltpu.VMEM((2, D_EXPERT, D_MODEL), F32),
                            pltpu.VMEM((D_MODEL, D_EXPERT), BF16), pltpu.VMEM((D_MODEL, D_EXPERT), BF16),
                            pltpu.VMEM((D_EXPERT, D_MODEL), BF16),
                            pltpu.SMEM((1,), jnp.int32), pltpu.SemaphoreType.DMA((2, 3))]),
        out_shape=jax.ShapeDtypeStruct(xs.shape, xs.dtype),
        compiler_params=_cparams("arbitrary"),
        name="moe_experts",
    )(block_expert, n_valid, next_expert, xs, wg, wu, wd)


def _combine_kernel(h_ref, rt_ref, p_ref, y0_ref, y1_ref, gp_ref, wpg_ref, wp_ref, gf_ref, o_ref, *, sub):
    n_sub = h_ref.shape[0] // sub

    def residual(r):
        rows = slice(r * sub, (r + 1) * sub)
        rt = rt_ref[rows, :]
        h = h_ref[rows, :] + (_load_token_rows(y0_ref, sub, F32, r * sub) * rt[:, 2:3]
                              + _load_token_rows(y1_ref, sub, F32, r * sub) * rt[:, 3:4])
        return h, _rms(h, gp_ref[...]).astype(BF16)

    def products(r, xn):
        return (jnp.dot(xn, wpg_ref[...], preferred_element_type=F32),
                _dot(p_ref[r * sub:(r + 1) * sub, :], wp_ref[...]))

    def finish(r, h, gate_lin, ple):
        h = h + ple * _sigmoid(gate_lin)
        o_ref[r * sub:(r + 1) * sub, :] = _rms(h, gf_ref[...])

    h, xn = residual(0)
    for r in range(n_sub):
        gate_lin, ple = products(r, xn)
        if r + 1 < n_sub:
            h_next, xn = residual(r + 1)
        finish(r, h, gate_lin, ple)
        if r + 1 < n_sub:
            h = h_next


def _combine(h1, rt, p2, y, gp, wpg, wp, gf, tm):
    t = h1.shape[0]
    nt = t // tm
    tile = lambda n: pl.BlockSpec((tm, n), lambda i: (i, 0))
    full = lambda a, b: pl.BlockSpec((a, b), lambda i: (0, 0))
    ytile = lambda k: pl.BlockSpec((tm * ROW_TILES, LANES), lambda i: (i + k * nt, 0))
    return pl.pallas_call(
        functools.partial(_combine_kernel, sub=256),
        grid=(nt,),
        in_specs=[tile(D_MODEL), tile(LANES), tile(D_PLE), ytile(0), ytile(1),
                  full(1, D_MODEL), full(D_MODEL, D_MODEL), full(D_PLE, D_MODEL), full(1, D_MODEL)],
        out_specs=tile(D_MODEL),
        out_shape=jax.ShapeDtypeStruct((t, D_MODEL), F32),
        compiler_params=_cparams("parallel"),
        name="moe_combine_ple",
    )(h1, rt, p2, y, y, gp, wpg, wp, gf)


def _lane_row(vals, offset):
    return jnp.zeros((1, LANES), F32).at[0, offset:offset + vals.shape[0]].set(vals)


def kernel(x, p, norm_mix, w_in, lru_conv_w, lru_conv_b, lru_wa, lru_ba, lru_wi, lru_bi, lru_lambda,
           lru_out_norm, gdn_conv_w, gdn_a_log, gdn_dt_bias, gdn_out_norm, w_out, norm_ffn,
           w_router_group, b_router_group, w_router_expert, b_router_expert, w_exp_gate, w_exp_up,
           w_exp_down, norm_ple, w_ple_gate, w_ple, norm_final):
    bsz, seq, d = x.shape
    t = bsz * seq
    depth = w_in.shape[0]
    assert depth == 1, "the final norm is fused into the last layer's combine kernel"
    n_blocks = -(-t * TOP_K // MOE_BLOCK) + N_EXPERTS
    row = lambda v: v.reshape(1, -1).astype(F32)
    h = x.reshape(t, d).astype(F32)
    for l in range(depth):
        b_gates = jnp.concatenate([lru_ba[l], lru_bi[l]]).reshape(1, -1)
        proj_gdn, y_lru = _proj_lru(h, row(norm_mix[l]), w_in[l], lru_conv_w[l], row(lru_conv_b[l]), lru_wa[l],
                                    lru_wi[l], b_gates, row(lru_lambda[l]), row(lru_out_norm[l]), seq, 512)
        cw = gdn_conv_w[l].reshape(CONV_W, 3, D_GDN).transpose(1, 0, 2)
        w_r = jnp.pad(jnp.concatenate([w_router_group[l], w_router_expert[l]], axis=1),
                      ((0, 0), (0, LANES - N_GROUPS - N_EXPERTS)))
        w_r_hi = w_r.astype(BF16)
        w_r_lo = (w_r - w_r_hi.astype(F32)).astype(BF16)
        w_r = jnp.concatenate([w_r_hi, w_r_hi, w_r_lo], axis=0)
        b_r = _lane_row(jnp.concatenate([b_router_group[l], b_router_expert[l]]), 0)
        h1, xn2, rt, tile_counts = _gdn_router(
            proj_gdn, cw, _lane_row(gdn_a_log[l], GDN_HEADS), _lane_row(gdn_dt_bias[l], GDN_HEADS),
            row(gdn_out_norm[l]), h, y_lru, w_out[l].astype(BF16), row(norm_ffn[l]), w_r, b_r, seq, 512)
        dest8, blocks = _slots(rt, tile_counts, n_blocks, 2048)
        dest = dest8[:TOP_K]
        block_expert, n_valid, next_expert = blocks[:, 0], blocks[:, 1], blocks[:, 2]
        tiles = lambda a: a.reshape(-1, ROW_TILES, LANES)
        xs = _sc_scatter_rows(tiles(xn2), dest, n_blocks * MOE_BLOCK)
        ys = _experts(block_expert, n_valid, next_expert, xs.reshape(-1, LANES), w_exp_gate[l], w_exp_up[l],
                      w_exp_down[l])
        y = _sc_gather_rows(tiles(ys), dest.reshape(-1))
        h = _combine(h1, rt, p[l].reshape(t, -1).astype(F32), y.reshape(-1, LANES), row(norm_ple[l]),
                     w_ple_gate[l].astype(BF16), w_ple[l].astype(BF16), row(norm_final), 1024)
    return h.reshape(bsz, seq, d).astype(x.dtype)
```

```python
import functools

import jax
import jax.numpy as jnp
from jax import lax
from jax.experimental import pallas as pl
from jax.experimental.pallas import tpu as pltpu
from jax.experimental.pallas import tpu_sc as plsc

D_MODEL = 1024
D_LRU = 512
LRU_BLOCKS = 8
LRU_BLOCK_W = D_LRU // LRU_BLOCKS
LRU_C = 8.0
D_GDN = 512
GDN_HEADS = 4
GDN_HEAD_DIM = D_GDN // GDN_HEADS
CONV_W = 4
CHUNK = 64
N_GROUPS = 4
EXPERTS_PER_GROUP = 8
N_EXPERTS = N_GROUPS * EXPERTS_PER_GROUP
TOP_K = 2
D_EXPERT = 512
MOE_BLOCK = 512
D_PLE = 256
EPS = 1e-6

LANES = 128
SUBLANES = 8
ROW_TILES = D_MODEL // (2 * LANES)
SC_CORES = 2
SC_SUBCORES = 16
SC_WORKERS = SC_CORES * SC_SUBCORES
SC_WINDOW = 32
SC_SLOTS = 4
COMBINE_SLOTS = 3
PROJ_COLS = 2 * D_LRU + 4 * D_GDN + LANES
VMEM_LIMIT = 56 * 1024 * 1024

BF16 = jnp.bfloat16
F32 = jnp.float32


def _cparams(*sem):
    return pltpu.CompilerParams(dimension_semantics=sem, vmem_limit_bytes=VMEM_LIMIT)


def _rms(x, g):
    return x * lax.rsqrt(jnp.mean(x * x, axis=-1, keepdims=True) + EPS) * g


def _sigmoid(x):
    return 0.5 * jnp.tanh(0.5 * x) + 0.5


def _silu(x):
    return x * _sigmoid(x)


def _dot(a, b):
    return jnp.dot(a.astype(BF16), b.astype(BF16), preferred_element_type=F32)


def _dot_nt(a, b):
    return lax.dot_general(a.astype(BF16), b.astype(BF16), (((1,), (1,)), ((), ())),
                           preferred_element_type=F32)


def _dot_tn(a, b):
    return lax.dot_general(a.astype(BF16), b.astype(BF16), (((0,), (0,)), ((), ())),
                           preferred_element_type=F32)


def _conv_scratch(rows, channels):
    return pltpu.VMEM((channels // LANES, 2 * (SUBLANES + rows), LANES), F32)


def _conv_reset(xp_ref):
    for k in range(xp_ref.shape[0]):
        xp_ref.at[k][pl.ds(0, SUBLANES, stride=2), :] = jnp.zeros((SUBLANES, LANES), F32)


def _causal_conv(xp_ref, x_ref, w, r0, rows, col0=0):
    out = []
    for k in range(xp_ref.shape[0]):
        lanes = slice(k * LANES, (k + 1) * LANES)
        slab = xp_ref.at[k]
        xk = x_ref[r0:r0 + rows, col0 + k * LANES:col0 + (k + 1) * LANES]
        slab[pl.ds(2 * (SUBLANES + r0), rows, stride=2), :] = xk
        acc = xk * w[CONV_W - 1:CONV_W, lanes]
        for j in range(1, CONV_W):
            acc = acc + slab[pl.ds(2 * (SUBLANES + r0 - j), rows, stride=2), :] * w[CONV_W - 1 - j:CONV_W - j, lanes]
        out.append(acc)
    return jnp.concatenate(out, axis=1)


def _conv_carry(xp_ref, x_ref, col0=0):
    ts = x_ref.shape[0]
    for k in range(xp_ref.shape[0]):
        xp_ref.at[k][pl.ds(0, SUBLANES, stride=2), :] = x_ref[ts - SUBLANES:, col0 + k * LANES:col0 + (k + 1) * LANES]


def _interleave(*stages):
    live = list(stages)
    while live:
        for g in list(live):
            try:
                next(g)
            except StopIteration:
                live.remove(g)


def _store_token_rows(ref, x, row0=0):
    rows = x.shape[0]
    bits = lax.bitcast_convert_type(x.astype(BF16).astype(F32), jnp.uint32)
    words = (bits[:, :D_MODEL // 2] >> 16) | bits[:, D_MODEL // 2:]
    for j in range(ROW_TILES):
        ref[pl.ds(row0 * ROW_TILES + j, rows, stride=ROW_TILES), :] = words[:, j * LANES:(j + 1) * LANES]


def _load_token_rows(ref, rows, dtype, row0=0):
    words = [ref[pl.ds(row0 * ROW_TILES + j, rows, stride=ROW_TILES), :] for j in range(ROW_TILES)]
    lo = [lax.bitcast_convert_type(w << 16, F32).astype(dtype) for w in words]
    hi = [lax.bitcast_convert_type(w & jnp.uint32(0xFFFF0000), F32).astype(dtype) for w in words]
    return jnp.concatenate(lo + hi, axis=1)


def _proj_lru_kernel(x_ref, gin_ref, win_ref, cw_ref, cb_ref, wa_ref, wi_ref, bg_ref, lam_ref, og_ref,
                     gdn_ref, o_ref, pj_ref, tail_ref, h_ref, wg_ref, wb_ref, *, tiles_per_seq, col_tile):
    g = pl.program_id(0)
    ts = x_ref.shape[0]
    pj_new = pj_ref.at[g % 2]
    pj = pj_ref.at[(g + 1) % 2]

    @pl.when(g == 0)
    def _():
        pj_ref[...] = jnp.zeros_like(pj_ref)
        in_cols = win_ref.shape[1]
        whole = in_cols // LANES * LANES
        for c0 in range(0, whole, col_tile):
            wb_ref[:, c0:c0 + col_tile] = win_ref[:, c0:c0 + col_tile].astype(BF16)
        wb_ref[:, whole:] = jnp.zeros((D_MODEL, PROJ_COLS - whole), BF16)
        wb_ref[:, whole:in_cols] = win_ref[:, whole:].astype(BF16)
        wg_ref[...] = jnp.zeros_like(wg_ref)
        for n in range(LRU_BLOCKS):
            blk = slice(n * LRU_BLOCK_W, (n + 1) * LRU_BLOCK_W)
            wg_ref[blk, blk] = wa_ref[n].astype(BF16)
            wg_ref[blk, D_LRU + n * LRU_BLOCK_W:D_LRU + (n + 1) * LRU_BLOCK_W] = wi_ref[n].astype(BF16)

    @pl.when((g == 0) | (lax.rem(g - 1, tiles_per_seq) == 0))
    def _():
        _conv_reset(tail_ref)
        h_ref[...] = jnp.zeros_like(h_ref)

    done = []

    def after_projection(v):
        return v + jnp.concatenate([done[max(len(done) - 2, 0)]] * (v.shape[-1] // LANES), axis=-1)

    def project():
        xn = _rms(x_ref[...], gin_ref[...]).astype(BF16)
        yield
        for c0 in range(0, PROJ_COLS, col_tile):
            c1 = min(c0 + col_tile, PROJ_COLS)
            y = jnp.dot(xn, wb_ref[:, c0:c1], preferred_element_type=F32)
            if c1 <= 2 * D_LRU:
                pj_new[:, c0:c1] = y
            else:
                gdn_ref[:, c0 - 2 * D_LRU:c1 - 2 * D_LRU] = y
            bits = lax.bitcast_convert_type(y[ts - 1:, c1 - c0 - LANES:], jnp.uint32)
            done.append(lax.bitcast_convert_type((bits >> 16) >> 16, F32))
            yield

    def recur():
        xc = _causal_conv(tail_ref, pj, cw_ref[...], 0, ts) + cb_ref[...]
        _conv_carry(tail_ref, pj)
        yield
        gates = _sigmoid(_dot(xc, wg_ref[...]) + after_projection(bg_ref[...]))
        r = gates[:, :D_LRU]
        i = gates[:, D_LRU:]
        yield
        log_a = LRU_C * r * jax.nn.log_sigmoid(after_projection(lam_ref[...]))
        a = jnp.exp(log_a)
        th = jnp.tanh(log_a)
        u = jnp.sqrt(-2.0 * th) * lax.rsqrt(1.0 - th) * (i * xc)
        yield
        a = a.reshape(ts // SUBLANES, SUBLANES, D_LRU)
        u = u.reshape(ts // SUBLANES, SUBLANES, D_LRU)
        row = lax.broadcasted_iota(jnp.int32, a.shape, 1)
        d = 1
        while d < SUBLANES:
            keep = row >= d
            a_prev = jnp.where(keep, pltpu.roll(a, d, axis=1), 1.0)
            u_prev = jnp.where(keep, pltpu.roll(u, d, axis=1), 0.0)
            u = a * u_prev + u
            a = a * a_prev
            d *= 2
            yield
        carry = after_projection(h_ref[...])
        groups = []
        for n in range(ts // SUBLANES):
            groups.append(a[n] * carry + u[n])
            carry = groups[-1][SUBLANES - 1:]
        h = jnp.concatenate(groups, axis=0)
        h_ref[...] = carry
        yield
        y = h * jax.nn.gelu(pj[:, D_LRU:])
        o_ref[...] = _rms(y, after_projection(og_ref[...])).astype(o_ref.dtype)

    _interleave(project(), recur())


def _proj_lru(x2, gin, win, cw, cb, wa, wi, bg, lam, og, seq, ts):
    t = x2.shape[0]
    n_tiles = t // ts
    this = lambda g: (jnp.minimum(g, n_tiles - 1), 0)
    last = lambda g: (jnp.maximum(g - 1, 0), 0)
    const = lambda *shape: pl.BlockSpec(shape, lambda g: (0,) * len(shape))
    gdn_cols = PROJ_COLS - 2 * D_LRU
    return pl.pallas_call(
        functools.partial(_proj_lru_kernel, tiles_per_seq=seq // ts, col_tile=D_LRU),
        grid=(n_tiles + 1,),
        in_specs=[pl.BlockSpec((ts, D_MODEL), this), const(1, D_MODEL),
                  pl.BlockSpec(win.shape, lambda g: (0, 0), pipeline_mode=pl.Buffered(1)),
                  const(CONV_W, D_LRU), const(1, D_LRU),
                  const(LRU_BLOCKS, LRU_BLOCK_W, LRU_BLOCK_W), const(LRU_BLOCKS, LRU_BLOCK_W, LRU_BLOCK_W),
                  const(1, 2 * D_LRU), const(1, D_LRU), const(1, D_LRU)],
        out_specs=[pl.BlockSpec((ts, gdn_cols), this), pl.BlockSpec((ts, D_LRU), last)],
        out_shape=[jax.ShapeDtypeStruct((t, gdn_cols), F32), jax.ShapeDtypeStruct((t, D_LRU), BF16)],
        scratch_shapes=[pltpu.VMEM((2, ts, 2 * D_LRU), F32), _conv_scratch(ts, D_LRU), pltpu.VMEM((1, D_LRU), F32),
                        pltpu.VMEM((D_LRU, 2 * D_LRU), BF16), pltpu.VMEM((D_MODEL, PROJ_COLS), BF16)],
        compiler_params=_cparams("arbitrary"),
        name="in_proj_rglru",
    )(x2, gin, win, cw, cb, wa, wi, bg, lam, og)


def _gdn_router_kernel(pj_ref, cw_ref, alog_ref, dtb_ref, og_ref,
                       x_ref, yl_ref, wo_ref, gf_ref, wr_ref, br_ref, h_ref, xn_ref, rt_ref, cnt_ref,
                       qt_ref, kt_ref, vt_ref, qs_ref, ks_ref, vs_ref, bs_ref, gc_ref, gct_ref, st_ref, yg_ref,
                       *, tiles_per_seq, group_chunks, prep_rows, sub):
    g = pl.program_id(0)
    ts = pj_ref.shape[0]
    dk = GDN_HEAD_DIM
    nc = ts // CHUNK
    n_sub = ts // sub
    q0, k0, v0, z0, ba0 = 0, D_GDN, 2 * D_GDN, 3 * D_GDN, 4 * D_GDN
    yg_new = yg_ref.at[g % 2]
    yg = yg_ref.at[(g + 1) % 2]

    @pl.when(g == 0)
    def _():
        yg_ref[...] = jnp.zeros_like(yg_ref)

    @pl.when(lax.rem(g, tiles_per_seq) == 0)
    def _():
        for tail_ref in (qt_ref, kt_ref, vt_ref):
            _conv_reset(tail_ref)
        st_ref[...] = jnp.zeros_like(st_ref)

    done = []
    hs = {}
    counts = []

    def after_projection(v):
        if not done:
            return v
        return v + jnp.concatenate([done[-1]] * (v.shape[-1] // LANES), axis=-1)

    def project_out():
        for r in range(n_sub):
            rows = slice(r * sub, (r + 1) * sub)
            h = x_ref[rows, :] + jnp.dot(yl_ref[rows, :], wo_ref[:D_LRU, :], preferred_element_type=F32) \
                + jnp.dot(yg[rows, :], wo_ref[D_LRU:, :], preferred_element_type=F32)
            h_ref[rows, :] = h
            hs[r] = h
            bits = lax.bitcast_convert_type(h[sub - 1:, D_MODEL - LANES:], jnp.uint32)
            done.append(lax.bitcast_convert_type((bits >> 16) >> 16, F32))
            yield

    def route(r, h):
        xn = _rms(h, gf_ref[...])
        _store_token_rows(xn_ref, xn, r * sub)
        xh = xn.astype(BF16)
        xl = (xn - xh.astype(F32)).astype(BF16)
        logits = jnp.dot(jnp.concatenate([xh, xl, xh], axis=1), wr_ref[...],
                         preferred_element_type=F32) + br_ref[...]
        lane = lax.broadcasted_iota(jnp.int32, logits.shape, 1).astype(F32)
        big = jnp.float32(2 * LANES)
        ninf = jnp.float32(-jnp.inf)

        def top1(vals):
            m = jnp.max(vals, axis=-1, keepdims=True)
            return m, jnp.min(jnp.where(vals == m, lane, big), axis=-1, keepdims=True)

        gl = jnp.where(lane < N_GROUPS, logits, ninf)
        gmax, gsel = top1(gl)
        p_group = 1.0 / jnp.sum(jnp.exp(gl - gmax), axis=-1, keepdims=True)
        lo = N_GROUPS + EXPERTS_PER_GROUP * gsel
        el = jnp.where((lane >= lo) & (lane < lo + EXPERTS_PER_GROUP), logits, ninf)
        m1, i1 = top1(el)
        m2, i2 = top1(jnp.where(lane == i1, ninf, el))
        rr = jnp.exp(m2 - m1)
        g1 = p_group / (1.0 + rr)
        g2 = p_group * rr / (1.0 + rr)
        rt_ref[r * sub:(r + 1) * sub, :] = jnp.where(
            lane == 0, i1 - N_GROUPS,
            jnp.where(lane == 1, i2 - N_GROUPS, jnp.where(lane == 2, g1, jnp.where(lane == 3, g2, 0.0))))
        member = ((lane == i1 - N_GROUPS) | (lane == i2 - N_GROUPS)).astype(F32)
        counts.append(jnp.sum(member, axis=0, keepdims=True))

    def route_subtiles(subtiles):
        for r in subtiles:
            route(r, hs[r])
            yield

    ri = lax.broadcasted_iota(jnp.int32, (CHUNK, CHUNK), 0)
    ci = lax.broadcasted_iota(jnp.int32, (CHUNK, CHUNK), 1)
    causal = ri >= ci
    strict = ri > ci
    tril = causal.astype(F32)
    eye = (ri == ci).astype(F32)
    og = og_ref[...]

    def l2n(x, scale):
        parts = []
        for h in range(GDN_HEADS):
            xh = x[:, h * dk:(h + 1) * dk]
            parts.append(xh * (lax.rsqrt(jnp.sum(xh * xh, axis=-1, keepdims=True) + EPS) * scale))
        return jnp.concatenate(parts, axis=1)

    def prepare(r0, r1):
        for p0 in range(r0, r1, prep_rows):
            rows = slice(p0, p0 + prep_rows)
            conv = lambda tail_ref, part, col0: _silu(
                _causal_conv(tail_ref, pj_ref, after_projection(cw_ref[part]), p0, prep_rows, col0))
            qs_ref[rows, :] = l2n(conv(qt_ref, 0, q0), dk ** -0.5)
            yield
            ks_ref[rows, :] = l2n(conv(kt_ref, 1, k0), 1.0)
            yield
            vs_ref[rows, :] = conv(vt_ref, 2, v0)
            ba = pj_ref[rows, ba0:ba0 + LANES]
            bs_ref[rows, :] = _sigmoid(ba)
            g = -jnp.exp(alog_ref[...]) * jax.nn.softplus(ba + dtb_ref[...])
            for c0 in range(0, prep_rows, CHUNK):
                gc_ref[p0 + c0:p0 + c0 + CHUNK, :] = jnp.dot(tril, g[c0:c0 + CHUNK], precision=lax.Precision.HIGHEST,
                                                             preferred_element_type=F32)
            gct_ref[:, rows] = gc_ref[rows, :].T
            yield

    terms = {}

    def chunk_terms(pairs):
        n = range(len(pairs))
        rows = [slice(c * CHUNK, (c + 1) * CHUNK) for c, _ in pairs]
        cols = [slice(h * dk, (h + 1) * dk) for _, h in pairs]
        gl = [GDN_HEADS + h for _, h in pairs]
        kh = [ks_ref[rows[i], cols[i]] for i in n]
        kb = [kh[i] * bs_ref[rows[i], pairs[i][1]:pairs[i][1] + 1] for i in n]
        r = [_dot_nt(jnp.concatenate([kb[i], qs_ref[rows[i], cols[i]]], axis=0), kh[i]) for i in n]
        yield
        gcol = [gc_ref[rows[i], gl[i]:gl[i] + 1] for i in n]
        decay = []
        for i in n:
            diff = gcol[i] - gct_ref[gl[i]:gl[i] + 1, rows[i]]
            decay.append(jnp.where(causal, jnp.exp(jnp.where(causal, diff, 0.0)), 0.0))
        a = [jnp.where(strict, r[i][:CHUNK] * decay[i], 0.0) for i in n]
        qk = [(r[i][CHUNK:] * decay[i]).astype(BF16) for i in n]
        tinv = [eye - a[i] for i in n]
        p = 2
        while p < CHUNK:
            a = [_dot(a[i], a[i]) for i in n]
            yield
            tinv = [tinv[i] + _dot(tinv[i], a[i]) for i in n]
            yield
            p *= 2
        eg = [jnp.exp(gcol[i]) for i in n]
        rhs = [jnp.concatenate([vs_ref[rows[i], cols[i]] * bs_ref[rows[i], pairs[i][1]:pairs[i][1] + 1],
                                kb[i] * eg[i]], axis=1) for i in n]
        uw = [_dot(tinv[i], rhs[i]).astype(BF16) for i in n]
        yield
        qk_uw = [jnp.dot(qk[i], uw[i], preferred_element_type=F32) for i in n]
        glast = [gc_ref[(c + 1) * CHUNK - 1:(c + 1) * CHUNK, gl[i]:gl[i] + 1] for i, (c, _) in enumerate(pairs)]
        kd_uw = [_dot_tn(kh[i] * jnp.exp(glast[i] - gcol[i]), uw[i]) for i in n]
        yield
        for i in n:
            lhs = jnp.concatenate([kd_uw[i][:, dk:], qs_ref[rows[i], cols[i]] * eg[i] - qk_uw[i][:, dk:]],
                                  axis=0).astype(BF16)
            terms[pairs[i]] = (lhs, kd_uw[i][:, :dk], qk_uw[i][:, :dk], jnp.exp(glast[i]))

    state = [st_ref[h] for h in range(GDN_HEADS)]

    def advance(chunks):
        for c in chunks:
            rows = slice(c * CHUNK, (c + 1) * CHUNK)
            r = [jnp.dot(terms[c, h][0], state[h].astype(BF16), preferred_element_type=F32)
                 for h in range(GDN_HEADS)]
            for h in range(GDN_HEADS):
                cols = slice(h * dk, (h + 1) * dk)
                _, c_add, o_add, egl = terms[c, h]
                o = r[h][dk:] + o_add
                state[h] = egl * state[h] - r[h][:dk] + c_add
                zh = pj_ref[rows, z0 + h * dk:z0 + (h + 1) * dk]
                yg_new[rows, cols] = (_rms(o, og) * _silu(zh)).astype(yg_new.dtype)
            yield

    groups = [range(c0, c0 + group_chunks) for c0 in range(0, nc, group_chunks)]
    pairs_of = lambda chunks: [(c, h) for c in chunks for h in range(GDN_HEADS)]
    span = lambda chunks: (chunks[0] * CHUNK, (chunks[-1] + 1) * CHUNK)
    per_phase = max(1, n_sub // (len(groups) + 1))
    _interleave(prepare(*span(groups[0])), project_out())
    for i, chunks in enumerate(groups):
        side = [route_subtiles(range(i * per_phase, (i + 1) * per_phase))]
        if i + 1 < len(groups):
            side.append(prepare(*span(groups[i + 1])))
        if i > 0:
            side.append(advance(groups[i - 1]))
        _interleave(chunk_terms(pairs_of(chunks)), *side)
    _interleave(advance(groups[-1]), route_subtiles(range(len(groups) * per_phase, n_sub)))
    for h in range(GDN_HEADS):
        st_ref[h] = state[h]
    for tail_ref, col0 in ((qt_ref, q0), (kt_ref, k0), (vt_ref, v0)):
        _conv_carry(tail_ref, pj_ref, col0)
    cnt_ref[...] = jnp.broadcast_to(sum(counts), cnt_ref.shape)


def _gdn_router(proj, cw, alog, dtb, og, x2, yl, wo, gf, wr, br, seq, ts):
    t = x2.shape[0]
    n_tiles = t // ts
    this = lambda g: (jnp.minimum(g, n_tiles - 1), 0)
    last = lambda g: (jnp.maximum(g - 1, 0), 0)
    const = lambda *shape: pl.BlockSpec(shape, lambda g: (0,) * len(shape))
    return pl.pallas_call(
        functools.partial(_gdn_router_kernel, tiles_per_seq=seq // ts, group_chunks=4, prep_rows=128, sub=256),
        grid=(n_tiles + 1,),
        in_specs=[pl.BlockSpec((ts, proj.shape[1]), this),
                  const(3, CONV_W, D_GDN), const(1, LANES), const(1, LANES), const(1, GDN_HEAD_DIM),
                  pl.BlockSpec((ts, D_MODEL), last), pl.BlockSpec((ts, D_LRU), last), const(D_MODEL, D_MODEL),
                  const(1, D_MODEL), const(3 * D_MODEL, LANES), const(1, LANES)],
        out_specs=[pl.BlockSpec((ts, D_MODEL), last), pl.BlockSpec((ts * ROW_TILES, LANES), last),
                   pl.BlockSpec((ts, LANES), last), pl.BlockSpec((SUBLANES, LANES), last)],
        out_shape=[jax.ShapeDtypeStruct((t, D_MODEL), F32),
                   jax.ShapeDtypeStruct((t * ROW_TILES, LANES), jnp.uint32),
                   jax.ShapeDtypeStruct((t, LANES), F32),
                   jax.ShapeDtypeStruct((n_tiles * SUBLANES, LANES), F32)],
        scratch_shapes=[_conv_scratch(ts, D_GDN)] * 3
        + [pltpu.VMEM((ts, D_GDN), F32)] * 3
        + [pltpu.VMEM((ts, LANES), F32)] * 2
        + [pltpu.VMEM((LANES, ts), F32)]
        + [pltpu.VMEM((GDN_HEADS, GDN_HEAD_DIM, GDN_HEAD_DIM), F32)]
        + [pltpu.VMEM((2, ts, D_GDN), BF16)],
        compiler_params=_cparams("arbitrary"),
        name="gdn_out_router",
    )(proj, cw, alog, dtb, og, x2, yl, wo, gf, wr, br)


def _slots_kernel(rt_ref, tc_ref, dest_ref, blocks_ref, cnt_ref, before_ref):
    i = pl.program_id(0)
    tm = rt_ref.shape[0]
    rt = rt_ref[...]
    lane = lax.broadcasted_iota(jnp.int32, rt.shape, 1)
    e0 = rt[:, 0:1].astype(jnp.int32)
    e1 = rt[:, 1:2].astype(jnp.int32)
    member = ((lane == e0) | (lane == e1)).astype(F32)

    @pl.when(i == 0)
    def _():
        ri = lax.broadcasted_iota(jnp.int32, before_ref.shape, 0)
        ci = lax.broadcasted_iota(jnp.int32, before_ref.shape, 1)
        before_ref[...] = (ri > ci).astype(BF16)
        cnt = jnp.broadcast_to(jnp.sum(tc_ref[...], axis=0, keepdims=True) / SUBLANES, cnt_ref.shape)
        padded = jnp.ceil(cnt / MOE_BLOCK) * MOE_BLOCK
        l8 = lax.broadcasted_iota(jnp.int32, cnt.shape, 1)
        incl = padded
        d = 1
        while d < LANES:
            incl = incl + jnp.where(l8 >= d, pltpu.roll(incl, d, axis=1), 0.0)
            d *= 2
        cnt_ref[...] = incl - padded
        seg_start = (incl - padded)[0:1, :]
        seg_end = incl[0:1, :]
        real_end = seg_start + cnt[0:1, :]
        bl = lax.broadcasted_iota(jnp.int32, blocks_ref.shape, 1)
        b0 = (lax.broadcasted_iota(jnp.int32, blocks_ref.shape, 0) * MOE_BLOCK).astype(F32)
        is_expert = bl < N_EXPERTS
        expert = jnp.sum(jnp.where(is_expert & (seg_end <= b0), 1.0, 0.0), axis=-1, keepdims=True)
        real = jnp.maximum(jnp.minimum(real_end, b0 + MOE_BLOCK) - jnp.maximum(seg_start, b0), 0.0)
        n_real = jnp.sum(jnp.where(is_expert, real, 0.0), axis=-1, keepdims=True)
        expert = jnp.minimum(expert, N_EXPERTS - 1.0)
        later = is_expert & (cnt[0:1, :] > 0.0) & (bl.astype(F32) > expert)
        nxt = jnp.min(jnp.where(later, bl.astype(F32), float(N_EXPERTS)), axis=-1, keepdims=True)
        nxt = jnp.where(nxt < N_EXPERTS, nxt, expert)
        blocks_ref[...] = jnp.where(bl == 0, expert, jnp.where(bl == 1, n_real, jnp.where(bl == 2, nxt, 0.0))
                                    ).astype(jnp.int32)

    sub = before_ref.shape[0]
    run = cnt_ref[0:1, :]
    pos = []
    for j in range(tm // sub):
        mj = member[j * sub:(j + 1) * sub]
        pos.append(run + jnp.dot(before_ref[...], mj.astype(BF16), preferred_element_type=F32))
        run = run + jnp.sum(mj, axis=0, keepdims=True)
    pos = jnp.concatenate(pos, axis=0)
    d0 = jnp.sum(jnp.where(lane == e0, pos, 0.0), axis=-1, keepdims=True)
    d1 = jnp.sum(jnp.where(lane == e1, pos, 0.0), axis=-1, keepdims=True)
    dest = jnp.where(lane == 0, d0, jnp.where(lane == 1, d1, 0.0))
    dest_ref[...] = dest.T[:SUBLANES].astype(jnp.int32)
    cnt_ref[...] = jnp.broadcast_to(run, cnt_ref.shape)


def _slots(rt, tile_counts, n_blocks, tm):
    t = rt.shape[0]
    return pl.pallas_call(
        _slots_kernel,
        grid=(t // tm,),
        in_specs=[pl.BlockSpec((tm, LANES), lambda i: (i, 0)),
                  pl.BlockSpec(tile_counts.shape, lambda i: (0, 0))],
        out_specs=[pl.BlockSpec((SUBLANES, tm), lambda i: (0, i)),
                   pl.BlockSpec((n_blocks, LANES), lambda i: (0, 0))],
        out_shape=[jax.ShapeDtypeStruct((SUBLANES, t), jnp.int32),
                   jax.ShapeDtypeStruct((n_blocks, LANES), jnp.int32)],
        scratch_shapes=[pltpu.VMEM((SUBLANES, LANES), F32), pltpu.VMEM((LANES, LANES), BF16)],
        compiler_params=_cparams("arbitrary"),
        name="moe_slots",
    )(rt, tile_counts)


def _sc_mesh():
    return plsc.VectorSubcoreMesh(core_axis_name="c", subcore_axis_name="s")


def _sc_worker():
    return lax.axis_index("s") * SC_CORES + lax.axis_index("c")


def _sc_scatter_rows(src, idx, n_rows):
    t = src.shape[0]
    per_w = t // SC_WORKERS
    n_win = per_w // SC_WINDOW
    assert per_w % SC_WINDOW == 0 and n_win % SC_SLOTS == 0, "every worker walks whole rings of windows"
    idx = idx.reshape(TOP_K, SC_WORKERS, n_win, SC_WINDOW)

    @functools.partial(
        pl.kernel, mesh=_sc_mesh(),
        out_type=jax.ShapeDtypeStruct((n_rows,) + src.shape[1:], src.dtype),
        scratch_types=[pltpu.VMEM((TOP_K, n_win, SC_WINDOW), jnp.int32),
                       pltpu.VMEM((SC_SLOTS, SC_WINDOW) + src.shape[1:], src.dtype),
                       pltpu.SemaphoreType.DMA((SC_SLOTS,)), pltpu.SemaphoreType.DMA((SC_SLOTS,))],
        compiler_params=pltpu.CompilerParams(use_tc_tiling_on_sc=True),
        name="sc_dispatch")
    def scatter(src_hbm, idx_hbm, out_hbm, idx_v, rows_v, lsem, ssem):
        wid = _sc_worker()
        base = wid * per_w
        for k in range(TOP_K):
            pltpu.sync_copy(idx_hbm.at[k, wid], idx_v.at[k])

        def load(w, slot):
            return pltpu.make_async_copy(src_hbm.at[pl.ds(base + w * SC_WINDOW, SC_WINDOW)], rows_v.at[slot],
                                         lsem.at[slot])

        def put(w, slot, k):
            return pltpu.make_async_copy(rows_v.at[slot], out_hbm.at[idx_v.at[k, w]], ssem.at[slot])

        for s in range(SC_SLOTS - 1):
            load(s, s).start()

        @pl.loop(0, n_win, step=SC_SLOTS)
        def _(w0):
            for s in range(SC_SLOTS):
                w = w0 + s
                prev = (s - 1) % SC_SLOTS

                @pl.when(w + SC_SLOTS - 1 < n_win)
                def _():
                    @pl.when(w >= 1)
                    def _():
                        for k in range(TOP_K):
                            put(w - 1, prev, k).wait()

                    load(w + SC_SLOTS - 1, prev).start()

                load(w, s).wait()
                for k in range(TOP_K):
                    put(w, s, k).start()

        for s in range(SC_SLOTS):
            for k in range(TOP_K):
                put(n_win - SC_SLOTS + s, s, k).wait()

    return scatter(src, idx)


def _sc_gather_rows(table, idx):
    b = idx.shape[0]
    per_w = b // SC_WORKERS
    n_win = per_w // SC_WINDOW
    assert per_w % SC_WINDOW == 0 and n_win % SC_SLOTS == 0, "every worker walks whole rings of windows"
    idx = idx.reshape(SC_WORKERS, n_win, SC_WINDOW)

    @functools.partial(
        pl.kernel, mesh=_sc_mesh(),
        out_type=jax.ShapeDtypeStruct((b,) + table.shape[1:], table.dtype),
        scratch_types=[pltpu.VMEM((n_win, SC_WINDOW), jnp.int32),
                       pltpu.VMEM((SC_SLOTS, SC_WINDOW) + table.shape[1:], table.dtype),
                       pltpu.SemaphoreType.DMA((SC_SLOTS,)), pltpu.SemaphoreType.DMA((SC_SLOTS,))],
        compiler_params=pltpu.CompilerParams(use_tc_tiling_on_sc=True),
        name="sc_combine_gather")
    def gather(table_hbm, idx_hbm, out_hbm, idx_v, rows_v, gsem, psem):
        wid = _sc_worker()
        base = wid * per_w
        pltpu.sync_copy(idx_hbm.at[wid], idx_v)

        def get(w, slot):
            return pltpu.make_async_copy(table_hbm.at[idx_v.at[w]], rows_v.at[slot], gsem.at[slot])

        def put(w, slot):
            return pltpu.make_async_copy(rows_v.at[slot], out_hbm.at[pl.ds(base + w * SC_WINDOW, SC_WINDOW)],
                                         psem.at[slot])

        for s in range(SC_SLOTS - 1):
            get(s, s).start()

        @pl.loop(0, n_win, step=SC_SLOTS)
        def _(w0):
            for s in range(SC_SLOTS):
                w = w0 + s
                prev = (s - 1) % SC_SLOTS

                @pl.when(w + SC_SLOTS - 1 < n_win)
                def _():
                    @pl.when(w >= 1)
                    def _():
                        put(w - 1, prev).wait()

                    get(w + SC_SLOTS - 1, prev).start()

                get(w, s).wait()
                put(w, s).start()

        for s in range(SC_SLOTS):
            put(n_win - SC_SLOTS + s, s).wait()

    return gather(table, idx)


def _expert_weight_copies(e, slot, w_hbm, w_buf, sem):
    return [pltpu.make_async_copy(w.at[e], buf.at[slot], sem.at[slot, i]) for i, (w, buf) in enumerate(zip(w_hbm, w_buf))]


def _experts_kernel(be_ref, nv_ref, nx_ref, xs_ref, wg_hbm, wu_hbm, wd_hbm, ys_ref,
                    wgf_ref, wuf_ref, wdf_ref, wgb_ref, wub_ref, wdb_ref, slot_ref, sem):
    b = pl.program_id(0)
    e = be_ref[b]
    active = nv_ref[b] > 0
    w_hbm = (wg_hbm, wu_hbm, wd_hbm)
    w_buf = (wgf_ref, wuf_ref, wdf_ref)

    @pl.when(b == 0)
    def _():
        slot_ref[0] = 0
        for c in _expert_weight_copies(e, 0, w_hbm, w_buf, sem):
            c.start()

    @pl.when(active & ((b == 0) | (be_ref[jnp.maximum(b - 1, 0)] != e)))
    def _():
        slot = slot_ref[0]
        for c in _expert_weight_copies(e, slot, w_hbm, w_buf, sem):
            c.wait()
        wgb_ref[...] = wgf_ref[slot].astype(BF16)
        wub_ref[...] = wuf_ref[slot].astype(BF16)
        wdb_ref[...] = wdf_ref[slot].astype(BF16)
        nxt = nx_ref[b]

        @pl.when(nxt != e)
        def _():
            for c in _expert_weight_copies(nxt, 1 - slot, w_hbm, w_buf, sem):
                c.start()

        slot_ref[0] = 1 - slot

    @pl.when(active)
    def _():
        row = lax.broadcasted_iota(jnp.int32, (MOE_BLOCK, D_MODEL), 0)
        xb = jnp.where(row < nv_ref[b], _load_token_rows(xs_ref, MOE_BLOCK, BF16), 0.0)
        hg = jnp.dot(xb, wgb_ref[...], preferred_element_type=F32)
        hu = jnp.dot(xb, wub_ref[...], preferred_element_type=F32)
        hb = (_silu(hg) * hu).astype(BF16)
        _store_token_rows(ys_ref, jnp.dot(hb, wdb_ref[...], preferred_element_type=F32))

    @pl.when(jnp.logical_not(active))
    def _():
        ys_ref[...] = jnp.zeros_like(ys_ref)


def _experts(block_expert, n_valid, next_expert, xs, wg, wu, wd):
    nb = xs.shape[0] // (MOE_BLOCK * ROW_TILES)
    blk = pl.BlockSpec((MOE_BLOCK * ROW_TILES, LANES), lambda b, be, nv, nx: (b, 0))
    hbm = pl.BlockSpec(memory_space=pl.ANY)
    return pl.pallas_call(
        _experts_kernel,
        grid_spec=pltpu.PrefetchScalarGridSpec(
            num_scalar_prefetch=3,
            grid=(nb,),
            in_specs=[blk, hbm, hbm, hbm],
            out_specs=blk,
            scratch_shapes=[pltpu.VMEM((2, D_MODEL, D_EXPERT), F32), pltpu.VMEM((2, D_MODEL, D_EXPERT), F32),
                            pltpu.VMEM((2, D_EXPERT, D_MODEL), F32),
                            pltpu.VMEM((D_MODEL, D_EXPERT), BF16), pltpu.VMEM((D_MODEL, D_EXPERT), BF16),
                            pltpu.VMEM((D_EXPERT, D_MODEL), BF16),
                            pltpu.SMEM((1,), jnp.int32), pltpu.SemaphoreType.DMA((2, 3))]),
        out_shape=jax.ShapeDtypeStruct(xs.shape, xs.dtype),
        compiler_params=_cparams("arbitrary"),
        name="moe_experts",
    )(block_expert, n_valid, next_expert, xs, wg, wu, wd)


def _combine_kernel(rt_ref, p_ref, h_hbm, y_hbm, gp_ref, wpg_ref, wp_ref, gf_ref, o_ref,
                    h_ring, y0_ring, y1_ring, sem, *, sub):
    s = pl.program_id(0)
    n_steps = pl.num_programs(0)
    tm = o_ref.shape[0]
    n_sub = tm // sub

    def copies(step, slot):
        y_rows = tm * ROW_TILES
        return [pltpu.make_async_copy(h_hbm.at[pl.ds(step * tm, tm)], h_ring.at[slot], sem.at[slot, 0]),
                pltpu.make_async_copy(y_hbm.at[pl.ds(step * y_rows, y_rows)], y0_ring.at[slot], sem.at[slot, 1]),
                pltpu.make_async_copy(y_hbm.at[pl.ds((step + n_steps) * y_rows, y_rows)], y1_ring.at[slot],
                                      sem.at[slot, 2])]

    @pl.when(s == 0)
    def _():
        for ahead in range(COMBINE_SLOTS - 1):
            for c in copies(ahead, ahead):
                c.start()

    @pl.when(s + COMBINE_SLOTS - 1 < n_steps)
    def _():
        for c in copies(s + COMBINE_SLOTS - 1, lax.rem(s + COMBINE_SLOTS - 1, COMBINE_SLOTS)):
            c.start()

    slot = lax.rem(s, COMBINE_SLOTS)
    for c in copies(s, slot):
        c.wait()
    h_ref, y0_ref, y1_ref = h_ring.at[slot], y0_ring.at[slot], y1_ring.at[slot]

    def residual(r):
        rows = slice(r * sub, (r + 1) * sub)
        rt = rt_ref[rows, :]
        h = h_ref[rows, :] + (_load_token_rows(y0_ref, sub, F32, r * sub) * rt[:, 2:3]
                              + _load_token_rows(y1_ref, sub, F32, r * sub) * rt[:, 3:4])
        return h, _rms(h, gp_ref[...]).astype(BF16)

    def products(r, xn):
        return (jnp.dot(xn, wpg_ref[...], preferred_element_type=F32),
                _dot(p_ref[r * sub:(r + 1) * sub, :], wp_ref[...]))

    def finish(r, h, gate_lin, ple):
        h = h + ple * _sigmoid(gate_lin)
        o_ref[r * sub:(r + 1) * sub, :] = _rms(h, gf_ref[...])

    h, xn = residual(0)
    for r in range(n_sub):
        gate_lin, ple = products(r, xn)
        if r + 1 < n_sub:
            h_next, xn = residual(r + 1)
        finish(r, h, gate_lin, ple)
        if r + 1 < n_sub:
            h = h_next


def _combine(h1, rt, p2, y, gp, wpg, wp, gf, tm):
    t = h1.shape[0]
    nt = t // tm
    tile = lambda n: pl.BlockSpec((tm, n), lambda i: (i, 0))
    full = lambda a, b: pl.BlockSpec((a, b), lambda i: (0, 0))
    hbm = pl.BlockSpec(memory_space=pl.ANY)
    assert nt >= COMBINE_SLOTS - 1
    return pl.pallas_call(
        functools.partial(_combine_kernel, sub=256),
        grid=(nt,),
        in_specs=[tile(LANES), tile(D_PLE), hbm, hbm,
                  full(1, D_MODEL), full(D_MODEL, D_MODEL), full(D_PLE, D_MODEL), full(1, D_MODEL)],
        out_specs=tile(D_MODEL),
        out_shape=jax.ShapeDtypeStruct((t, D_MODEL), F32),
        scratch_shapes=[pltpu.VMEM((COMBINE_SLOTS, tm, D_MODEL), F32),
                        pltpu.VMEM((COMBINE_SLOTS, tm * ROW_TILES, LANES), y.dtype),
                        pltpu.VMEM((COMBINE_SLOTS, tm * ROW_TILES, LANES), y.dtype),
                        pltpu.SemaphoreType.DMA((COMBINE_SLOTS, 3))],
        compiler_params=_cparams("arbitrary"),
        name="moe_combine_ple",
    )(rt, p2, h1, y, gp, wpg, wp, gf)


def _lane_row(vals, offset):
    return jnp.zeros((1, LANES), F32).at[0, offset:offset + vals.shape[0]].set(vals)


def kernel(x, p, norm_mix, w_in, lru_conv_w, lru_conv_b, lru_wa, lru_ba, lru_wi, lru_bi, lru_lambda,
           lru_out_norm, gdn_conv_w, gdn_a_log, gdn_dt_bias, gdn_out_norm, w_out, norm_ffn,
           w_router_group, b_router_group, w_router_expert, b_router_expert, w_exp_gate, w_exp_up,
           w_exp_down, norm_ple, w_ple_gate, w_ple, norm_final):
    bsz, seq, d = x.shape
    t = bsz * seq
    depth = w_in.shape[0]
    assert depth == 1, "the final norm is fused into the last layer's combine kernel"
    n_blocks = -(-t * TOP_K // MOE_BLOCK) + N_EXPERTS
    row = lambda v: v.reshape(1, -1).astype(F32)
    h = x.reshape(t, d).astype(F32)
    for l in range(depth):
        b_gates = jnp.concatenate([lru_ba[l], lru_bi[l]]).reshape(1, -1)
        proj_gdn, y_lru = _proj_lru(h, row(norm_mix[l]), w_in[l], lru_conv_w[l], row(lru_conv_b[l]), lru_wa[l],
                                    lru_wi[l], b_gates, row(lru_lambda[l]), row(lru_out_norm[l]), seq, 512)
        cw = gdn_conv_w[l].reshape(CONV_W, 3, D_GDN).transpose(1, 0, 2)
        w_r = jnp.pad(jnp.concatenate([w_router_group[l], w_router_expert[l]], axis=1),
                      ((0, 0), (0, LANES - N_GROUPS - N_EXPERTS)))
        w_r_hi = w_r.astype(BF16)
        w_r_lo = (w_r - w_r_hi.astype(F32)).astype(BF16)
        w_r = jnp.concatenate([w_r_hi, w_r_hi, w_r_lo], axis=0)
        b_r = _lane_row(jnp.concatenate([b_router_group[l], b_router_expert[l]]), 0)
        h1, xn2, rt, tile_counts = _gdn_router(
            proj_gdn, cw, _lane_row(gdn_a_log[l], GDN_HEADS), _lane_row(gdn_dt_bias[l], GDN_HEADS),
            row(gdn_out_norm[l]), h, y_lru, w_out[l].astype(BF16), row(norm_ffn[l]), w_r, b_r, seq, 512)
        dest8, blocks = _slots(rt, tile_counts, n_blocks, 2048)
        dest = dest8[:TOP_K]
        block_expert, n_valid, next_expert = blocks[:, 0], blocks[:, 1], blocks[:, 2]
        tiles = lambda a: a.reshape(-1, ROW_TILES, LANES)
        xs = _sc_scatter_rows(tiles(xn2), dest, n_blocks * MOE_BLOCK)
        ys = _experts(block_expert, n_valid, next_expert, xs.reshape(-1, LANES), w_exp_gate[l], w_exp_up[l],
                      w_exp_down[l])
        y = _sc_gather_rows(tiles(ys), dest.reshape(-1))
        h = _combine(h1, rt, p[l].reshape(t, -1).astype(F32), y.reshape(-1, LANES), row(norm_ple[l]),
                     w_ple_gate[l].astype(BF16), w_ple[l].astype(BF16), row(norm_final), 1024)
    return h.reshape(bsz, seq, d).astype(x.dtype)
```

```python
import functools

import jax
import jax.numpy as jnp
from jax import lax
from jax.experimental import pallas as pl
from jax.experimental.pallas import tpu as pltpu
from jax.experimental.pallas import tpu_sc as plsc

D_MODEL = 1024
D_LRU = 512
LRU_BLOCKS = 8
LRU_BLOCK_W = D_LRU // LRU_BLOCKS
LRU_C = 8.0
D_GDN = 512
GDN_HEADS = 4
GDN_HEAD_DIM = D_GDN // GDN_HEADS
CONV_W = 4
CHUNK = 64
N_GROUPS = 4
EXPERTS_PER_GROUP = 8
N_EXPERTS = N_GROUPS * EXPERTS_PER_GROUP
TOP_K = 2
D_EXPERT = 512
MOE_BLOCK = 512
D_PLE = 256
EPS = 1e-6

LANES = 128
SUBLANES = 8
ROW_TILES = D_MODEL // (2 * LANES)
SC_CORES = 2
SC_SUBCORES = 16
SC_WORKERS = SC_CORES * SC_SUBCORES
SC_WINDOW = 32
SC_SLOTS = 4
COMBINE_SLOTS = 3
PROJ_COLS = 2 * D_LRU + 4 * D_GDN + LANES
VMEM_LIMIT = 56 * 1024 * 1024

BF16 = jnp.bfloat16
F32 = jnp.float32


def _cparams(*sem):
    return pltpu.CompilerParams(dimension_semantics=sem, vmem_limit_bytes=VMEM_LIMIT)


def _rms(x, g):
    return x * lax.rsqrt(jnp.mean(x * x, axis=-1, keepdims=True) + EPS) * g


def _sigmoid(x):
    return 0.5 * jnp.tanh(0.5 * x) + 0.5


def _silu(x):
    h = 0.5 * x
    return h * (1.0 + jnp.tanh(h))


def _dot(a, b):
    return jnp.dot(a.astype(BF16), b.astype(BF16), preferred_element_type=F32)


def _dot_nt(a, b):
    return lax.dot_general(a.astype(BF16), b.astype(BF16), (((1,), (1,)), ((), ())),
                           preferred_element_type=F32)


def _dot_tn(a, b):
    return lax.dot_general(a.astype(BF16), b.astype(BF16), (((0,), (0,)), ((), ())),
                           preferred_element_type=F32)


def _conv_scratch(rows, channels):
    return pltpu.VMEM((channels // LANES, 2 * (SUBLANES + rows), LANES), F32)


def _conv_reset(xp_ref):
    for k in range(xp_ref.shape[0]):
        xp_ref.at[k][pl.ds(0, SUBLANES, stride=2), :] = jnp.zeros((SUBLANES, LANES), F32)


def _causal_conv(xp_ref, x_ref, w, r0, rows, col0=0):
    out = []
    for k in range(xp_ref.shape[0]):
        lanes = slice(k * LANES, (k + 1) * LANES)
        slab = xp_ref.at[k]
        xk = x_ref[r0:r0 + rows, col0 + k * LANES:col0 + (k + 1) * LANES]
        slab[pl.ds(2 * (SUBLANES + r0), rows, stride=2), :] = xk
        acc = xk * w[CONV_W - 1:CONV_W, lanes]
        for j in range(1, CONV_W):
            acc = acc + slab[pl.ds(2 * (SUBLANES + r0 - j), rows, stride=2), :] * w[CONV_W - 1 - j:CONV_W - j, lanes]
        out.append(acc)
    return jnp.concatenate(out, axis=1)


def _conv_carry(xp_ref, x_ref, col0=0):
    ts = x_ref.shape[0]
    for k in range(xp_ref.shape[0]):
        xp_ref.at[k][pl.ds(0, SUBLANES, stride=2), :] = x_ref[ts - SUBLANES:, col0 + k * LANES:col0 + (k + 1) * LANES]


def _interleave(*stages):
    live = list(stages)
    while live:
        for g in list(live):
            try:
                next(g)
            except StopIteration:
                live.remove(g)


def _store_token_rows(ref, x, row0=0):
    rows = x.shape[0]
    bits = lax.bitcast_convert_type(x.astype(BF16).astype(F32), jnp.uint32)
    words = (bits[:, :D_MODEL // 2] >> 16) | bits[:, D_MODEL // 2:]
    for j in range(ROW_TILES):
        ref[pl.ds(row0 * ROW_TILES + j, rows, stride=ROW_TILES), :] = words[:, j * LANES:(j + 1) * LANES]


def _load_token_rows(ref, rows, dtype, row0=0):
    words = [ref[pl.ds(row0 * ROW_TILES + j, rows, stride=ROW_TILES), :] for j in range(ROW_TILES)]
    lo = [lax.bitcast_convert_type(w << 16, F32).astype(dtype) for w in words]
    hi = [lax.bitcast_convert_type(w & jnp.uint32(0xFFFF0000), F32).astype(dtype) for w in words]
    return jnp.concatenate(lo + hi, axis=1)


def _proj_lru_kernel(x_ref, gin_ref, win_ref, cw_ref, cb_ref, wa_ref, wi_ref, bg_ref, lam_ref, og_ref,
                     gdn_ref, o_ref, pj_ref, tail_ref, h_ref, wg_ref, wb_ref, *, tiles_per_seq, col_tile):
    g = pl.program_id(0)
    ts = x_ref.shape[0]
    pj_new = pj_ref.at[g % 2]
    pj = pj_ref.at[(g + 1) % 2]

    @pl.when(g == 0)
    def _():
        pj_ref[...] = jnp.zeros_like(pj_ref)
        in_cols = win_ref.shape[1]
        whole = in_cols // LANES * LANES
        for c0 in range(0, whole, col_tile):
            wb_ref[:, c0:c0 + col_tile] = win_ref[:, c0:c0 + col_tile].astype(BF16)
        wb_ref[:, whole:] = jnp.zeros((D_MODEL, PROJ_COLS - whole), BF16)
        wb_ref[:, whole:in_cols] = win_ref[:, whole:].astype(BF16)
        wg_ref[...] = jnp.zeros_like(wg_ref)
        for n in range(LRU_BLOCKS):
            blk = slice(n * LRU_BLOCK_W, (n + 1) * LRU_BLOCK_W)
            wg_ref[blk, blk] = wa_ref[n].astype(BF16)
            wg_ref[blk, D_LRU + n * LRU_BLOCK_W:D_LRU + (n + 1) * LRU_BLOCK_W] = wi_ref[n].astype(BF16)

    @pl.when((g == 0) | (lax.rem(g - 1, tiles_per_seq) == 0))
    def _():
        _conv_reset(tail_ref)
        h_ref[...] = jnp.zeros_like(h_ref)

    done = []

    def after_projection(v):
        return v + jnp.concatenate([done[max(len(done) - 2, 0)]] * (v.shape[-1] // LANES), axis=-1)

    def project():
        xn = _rms(x_ref[...], gin_ref[...]).astype(BF16)
        yield
        for c0 in range(0, PROJ_COLS, col_tile):
            c1 = min(c0 + col_tile, PROJ_COLS)
            y = jnp.dot(xn, wb_ref[:, c0:c1], preferred_element_type=F32)
            if c1 <= 2 * D_LRU:
                pj_new[:, c0:c1] = y
            else:
                gdn_ref[:, c0 - 2 * D_LRU:c1 - 2 * D_LRU] = y
            bits = lax.bitcast_convert_type(y[ts - 1:, c1 - c0 - LANES:], jnp.uint32)
            done.append(lax.bitcast_convert_type((bits >> 16) >> 16, F32))
            yield

    def recur():
        xc = _causal_conv(tail_ref, pj, cw_ref[...], 0, ts) + cb_ref[...]
        _conv_carry(tail_ref, pj)
        yield
        gates = _sigmoid(_dot(xc, wg_ref[...]) + after_projection(bg_ref[...]))
        r = gates[:, :D_LRU]
        i = gates[:, D_LRU:]
        yield
        log_a = LRU_C * r * jax.nn.log_sigmoid(after_projection(lam_ref[...]))
        a = jnp.exp(log_a)
        th = jnp.tanh(log_a)
        u = jnp.sqrt(-2.0 * th) * lax.rsqrt(1.0 - th) * (i * xc)
        yield
        a = a.reshape(ts // SUBLANES, SUBLANES, D_LRU)
        u = u.reshape(ts // SUBLANES, SUBLANES, D_LRU)
        row = lax.broadcasted_iota(jnp.int32, a.shape, 1)
        d = 1
        while d < SUBLANES:
            keep = row >= d
            a_prev = jnp.where(keep, pltpu.roll(a, d, axis=1), 1.0)
            u_prev = jnp.where(keep, pltpu.roll(u, d, axis=1), 0.0)
            u = a * u_prev + u
            a = a * a_prev
            d *= 2
            yield
        carry = after_projection(h_ref[...])
        groups = []
        for n in range(ts // SUBLANES):
            groups.append(a[n] * carry + u[n])
            carry = groups[-1][SUBLANES - 1:]
        h = jnp.concatenate(groups, axis=0)
        h_ref[...] = carry
        yield
        y = h * jax.nn.gelu(pj[:, D_LRU:])
        o_ref[...] = _rms(y, after_projection(og_ref[...])).astype(o_ref.dtype)

    _interleave(project(), recur())


def _proj_lru(x2, gin, win, cw, cb, wa, wi, bg, lam, og, seq, ts):
    t = x2.shape[0]
    n_tiles = t // ts
    this = lambda g: (jnp.minimum(g, n_tiles - 1), 0)
    last = lambda g: (jnp.maximum(g - 1, 0), 0)
    const = lambda *shape: pl.BlockSpec(shape, lambda g: (0,) * len(shape))
    gdn_cols = PROJ_COLS - 2 * D_LRU
    return pl.pallas_call(
        functools.partial(_proj_lru_kernel, tiles_per_seq=seq // ts, col_tile=D_LRU),
        grid=(n_tiles + 1,),
        in_specs=[pl.BlockSpec((ts, D_MODEL), this), const(1, D_MODEL),
                  pl.BlockSpec(win.shape, lambda g: (0, 0), pipeline_mode=pl.Buffered(1)),
                  const(CONV_W, D_LRU), const(1, D_LRU),
                  const(LRU_BLOCKS, LRU_BLOCK_W, LRU_BLOCK_W), const(LRU_BLOCKS, LRU_BLOCK_W, LRU_BLOCK_W),
                  const(1, 2 * D_LRU), const(1, D_LRU), const(1, D_LRU)],
        out_specs=[pl.BlockSpec((ts, gdn_cols), this), pl.BlockSpec((ts, D_LRU), last)],
        out_shape=[jax.ShapeDtypeStruct((t, gdn_cols), F32), jax.ShapeDtypeStruct((t, D_LRU), BF16)],
        scratch_shapes=[pltpu.VMEM((2, ts, 2 * D_LRU), F32), _conv_scratch(ts, D_LRU), pltpu.VMEM((1, D_LRU), F32),
                        pltpu.VMEM((D_LRU, 2 * D_LRU), BF16), pltpu.VMEM((D_MODEL, PROJ_COLS), BF16)],
        compiler_params=_cparams("arbitrary"),
        name="in_proj_rglru",
    )(x2, gin, win, cw, cb, wa, wi, bg, lam, og)


def _gdn_router_kernel(pj_ref, cw_ref, alog_ref, dtb_ref, og_ref,
                       x_ref, yl_ref, wo_ref, gf_ref, wr_ref, br_ref, h_ref, xn_ref, rt_ref, cnt_ref,
                       qt_ref, kt_ref, vt_ref, qs_ref, ks_ref, vs_ref, bs_ref, gc_ref, gct_ref, st_ref, yg_ref,
                       *, tiles_per_seq, group_chunks, prep_rows, sub):
    g = pl.program_id(0)
    ts = pj_ref.shape[0]
    dk = GDN_HEAD_DIM
    nc = ts // CHUNK
    n_sub = ts // sub
    q0, k0, v0, z0, ba0 = 0, D_GDN, 2 * D_GDN, 3 * D_GDN, 4 * D_GDN
    yg_new = yg_ref.at[g % 2]
    yg = yg_ref.at[(g + 1) % 2]

    @pl.when(g == 0)
    def _():
        yg_ref[...] = jnp.zeros_like(yg_ref)

    @pl.when(lax.rem(g, tiles_per_seq) == 0)
    def _():
        for tail_ref in (qt_ref, kt_ref, vt_ref):
            _conv_reset(tail_ref)
        st_ref[...] = jnp.zeros_like(st_ref)

    done = []
    hs = {}
    counts = []

    def after_projection(v):
        if not done:
            return v
        return v + jnp.concatenate([done[-1]] * (v.shape[-1] // LANES), axis=-1)

    def project_out():
        for r in range(n_sub):
            rows = slice(r * sub, (r + 1) * sub)
            h = x_ref[rows, :] + jnp.dot(yl_ref[rows, :], wo_ref[:D_LRU, :], preferred_element_type=F32) \
                + jnp.dot(yg[rows, :], wo_ref[D_LRU:, :], preferred_element_type=F32)
            h_ref[rows, :] = h
            hs[r] = h
            bits = lax.bitcast_convert_type(h[sub - 1:, D_MODEL - LANES:], jnp.uint32)
            done.append(lax.bitcast_convert_type((bits >> 16) >> 16, F32))
            yield

    def route(r, h):
        xn = _rms(h, gf_ref[...])
        _store_token_rows(xn_ref, xn, r * sub)
        xh = xn.astype(BF16)
        xl = (xn - xh.astype(F32)).astype(BF16)
        logits = jnp.dot(jnp.concatenate([xh, xl, xh], axis=1), wr_ref[...],
                         preferred_element_type=F32) + br_ref[...]
        lane = lax.broadcasted_iota(jnp.int32, logits.shape, 1).astype(F32)
        big = jnp.float32(2 * LANES)
        ninf = jnp.float32(-jnp.inf)

        def top1(vals):
            m = jnp.max(vals, axis=-1, keepdims=True)
            return m, jnp.min(jnp.where(vals == m, lane, big), axis=-1, keepdims=True)

        gl = jnp.where(lane < N_GROUPS, logits, ninf)
        gmax, gsel = top1(gl)
        p_group = 1.0 / jnp.sum(jnp.exp(gl - gmax), axis=-1, keepdims=True)
        lo = N_GROUPS + EXPERTS_PER_GROUP * gsel
        el = jnp.where((lane >= lo) & (lane < lo + EXPERTS_PER_GROUP), logits, ninf)
        m1, i1 = top1(el)
        m2, i2 = top1(jnp.where(lane == i1, ninf, el))
        rr = jnp.exp(m2 - m1)
        g1 = p_group / (1.0 + rr)
        g2 = p_group * rr / (1.0 + rr)
        rt_ref[r * sub:(r + 1) * sub, :] = jnp.where(
            lane == 0, i1 - N_GROUPS,
            jnp.where(lane == 1, i2 - N_GROUPS, jnp.where(lane == 2, g1, jnp.where(lane == 3, g2, 0.0))))
        member = ((lane == i1 - N_GROUPS) | (lane == i2 - N_GROUPS)).astype(F32)
        counts.append(jnp.sum(member, axis=0, keepdims=True))

    def route_subtiles(subtiles):
        for r in subtiles:
            route(r, hs[r])
            yield

    ri = lax.broadcasted_iota(jnp.int32, (CHUNK, CHUNK), 0)
    ci = lax.broadcasted_iota(jnp.int32, (CHUNK, CHUNK), 1)
    causal = ri >= ci
    strict = ri > ci
    tril = causal.astype(F32)
    eye = (ri == ci).astype(F32)
    og = og_ref[...]

    def l2n(x, scale):
        parts = []
        for h in range(GDN_HEADS):
            xh = x[:, h * dk:(h + 1) * dk]
            parts.append(xh * (lax.rsqrt(jnp.sum(xh * xh, axis=-1, keepdims=True) + EPS) * scale))
        return jnp.concatenate(parts, axis=1)

    def prepare(r0, r1):
        for p0 in range(r0, r1, prep_rows):
            rows = slice(p0, p0 + prep_rows)
            conv = lambda tail_ref, part, col0: _silu(
                _causal_conv(tail_ref, pj_ref, after_projection(cw_ref[part]), p0, prep_rows, col0))
            qs_ref[rows, :] = l2n(conv(qt_ref, 0, q0), dk ** -0.5)
            yield
            ks_ref[rows, :] = l2n(conv(kt_ref, 1, k0), 1.0)
            yield
            vs_ref[rows, :] = conv(vt_ref, 2, v0)
            ba = pj_ref[rows, ba0:ba0 + LANES]
            bs_ref[rows, :] = _sigmoid(ba)
            g = -jnp.exp(alog_ref[...]) * jax.nn.softplus(ba + dtb_ref[...])
            for c0 in range(0, prep_rows, CHUNK):
                gc_ref[p0 + c0:p0 + c0 + CHUNK, :] = jnp.dot(tril, g[c0:c0 + CHUNK], precision=lax.Precision.HIGHEST,
                                                             preferred_element_type=F32)
            gct_ref[:, rows] = gc_ref[rows, :].T
            yield

    terms = {}

    def chunk_terms(pairs):
        n = range(len(pairs))
        rows = [slice(c * CHUNK, (c + 1) * CHUNK) for c, _ in pairs]
        cols = [slice(h * dk, (h + 1) * dk) for _, h in pairs]
        gl = [GDN_HEADS + h for _, h in pairs]
        kh = [ks_ref[rows[i], cols[i]] for i in n]
        kb = [kh[i] * bs_ref[rows[i], pairs[i][1]:pairs[i][1] + 1] for i in n]
        r = [_dot_nt(jnp.concatenate([kb[i], qs_ref[rows[i], cols[i]]], axis=0), kh[i]) for i in n]
        yield
        gcol = [gc_ref[rows[i], gl[i]:gl[i] + 1] for i in n]
        decay = []
        for i in n:
            diff = gcol[i] - gct_ref[gl[i]:gl[i] + 1, rows[i]]
            decay.append(jnp.where(causal, jnp.exp(jnp.where(causal, diff, 0.0)), 0.0))
        a = [jnp.where(strict, r[i][:CHUNK] * decay[i], 0.0) for i in n]
        qk = [(r[i][CHUNK:] * decay[i]).astype(BF16) for i in n]
        tinv = [eye - a[i] for i in n]
        p = 2
        while p < CHUNK:
            a = [_dot(a[i], a[i]) for i in n]
            yield
            tinv = [tinv[i] + _dot(tinv[i], a[i]) for i in n]
            yield
            p *= 2
        eg = [jnp.exp(gcol[i]) for i in n]
        rhs = [jnp.concatenate([vs_ref[rows[i], cols[i]] * bs_ref[rows[i], pairs[i][1]:pairs[i][1] + 1],
                                kb[i] * eg[i]], axis=1) for i in n]
        uw = [_dot(tinv[i], rhs[i]).astype(BF16) for i in n]
        yield
        qk_uw = [jnp.dot(qk[i], uw[i], preferred_element_type=F32) for i in n]
        glast = [gc_ref[(c + 1) * CHUNK - 1:(c + 1) * CHUNK, gl[i]:gl[i] + 1] for i, (c, _) in enumerate(pairs)]
        kd_uw = [_dot_tn(kh[i] * jnp.exp(glast[i] - gcol[i]), uw[i]) for i in n]
        yield
        for i in n:
            lhs = jnp.concatenate([kd_uw[i][:, dk:], qs_ref[rows[i], cols[i]] * eg[i] - qk_uw[i][:, dk:]],
                                  axis=0).astype(BF16)
            terms[pairs[i]] = (lhs, kd_uw[i][:, :dk], qk_uw[i][:, :dk], jnp.exp(glast[i]))

    state = [st_ref[h] for h in range(GDN_HEADS)]

    def advance(chunks):
        for c in chunks:
            rows = slice(c * CHUNK, (c + 1) * CHUNK)
            r = [jnp.dot(terms[c, h][0], state[h].astype(BF16), preferred_element_type=F32)
                 for h in range(GDN_HEADS)]
            for h in range(GDN_HEADS):
                cols = slice(h * dk, (h + 1) * dk)
                _, c_add, o_add, egl = terms[c, h]
                o = r[h][dk:] + o_add
                state[h] = egl * state[h] - r[h][:dk] + c_add
                zh = pj_ref[rows, z0 + h * dk:z0 + (h + 1) * dk]
                yg_new[rows, cols] = (_rms(o, og) * _silu(zh)).astype(yg_new.dtype)
            yield

    groups = [range(c0, c0 + group_chunks) for c0 in range(0, nc, group_chunks)]
    pairs_of = lambda chunks: [(c, h) for c in chunks for h in range(GDN_HEADS)]
    span = lambda chunks: (chunks[0] * CHUNK, (chunks[-1] + 1) * CHUNK)
    per_phase = max(1, n_sub // (len(groups) + 1))
    _interleave(prepare(*span(groups[0])), project_out())
    for i, chunks in enumerate(groups):
        side = [route_subtiles(range(i * per_phase, (i + 1) * per_phase))]
        if i + 1 < len(groups):
            side.append(prepare(*span(groups[i + 1])))
        if i > 0:
            side.append(advance(groups[i - 1]))
        _interleave(chunk_terms(pairs_of(chunks)), *side)
    _interleave(advance(groups[-1]), route_subtiles(range(len(groups) * per_phase, n_sub)))
    for h in range(GDN_HEADS):
        st_ref[h] = state[h]
    for tail_ref, col0 in ((qt_ref, q0), (kt_ref, k0), (vt_ref, v0)):
        _conv_carry(tail_ref, pj_ref, col0)
    cnt_ref[...] = jnp.broadcast_to(sum(counts), cnt_ref.shape)


def _gdn_router(proj, cw, alog, dtb, og, x2, yl, wo, gf, wr, br, seq, ts):
    t = x2.shape[0]
    n_tiles = t // ts
    this = lambda g: (jnp.minimum(g, n_tiles - 1), 0)
    last = lambda g: (jnp.maximum(g - 1, 0), 0)
    const = lambda *shape: pl.BlockSpec(shape, lambda g: (0,) * len(shape))
    return pl.pallas_call(
        functools.partial(_gdn_router_kernel, tiles_per_seq=seq // ts, group_chunks=4, prep_rows=128, sub=256),
        grid=(n_tiles + 1,),
        in_specs=[pl.BlockSpec((ts, proj.shape[1]), this),
                  const(3, CONV_W, D_GDN), const(1, LANES), const(1, LANES), const(1, GDN_HEAD_DIM),
                  pl.BlockSpec((ts, D_MODEL), last), pl.BlockSpec((ts, D_LRU), last), const(D_MODEL, D_MODEL),
                  const(1, D_MODEL), const(3 * D_MODEL, LANES), const(1, LANES)],
        out_specs=[pl.BlockSpec((ts, D_MODEL), last), pl.BlockSpec((ts * ROW_TILES, LANES), last),
                   pl.BlockSpec((ts, LANES), last), pl.BlockSpec((SUBLANES, LANES), last)],
        out_shape=[jax.ShapeDtypeStruct((t, D_MODEL), F32),
                   jax.ShapeDtypeStruct((t * ROW_TILES, LANES), jnp.uint32),
                   jax.ShapeDtypeStruct((t, LANES), F32),
                   jax.ShapeDtypeStruct((n_tiles * SUBLANES, LANES), F32)],
        scratch_shapes=[_conv_scratch(ts, D_GDN)] * 3
        + [pltpu.VMEM((ts, D_GDN), F32)] * 3
        + [pltpu.VMEM((ts, LANES), F32)] * 2
        + [pltpu.VMEM((LANES, ts), F32)]
        + [pltpu.VMEM((GDN_HEADS, GDN_HEAD_DIM, GDN_HEAD_DIM), F32)]
        + [pltpu.VMEM((2, ts, D_GDN), BF16)],
        compiler_params=_cparams("arbitrary"),
        name="gdn_out_router",
    )(proj, cw, alog, dtb, og, x2, yl, wo, gf, wr, br)


def _slots_kernel(rt_ref, tc_ref, dest_ref, blocks_ref, cnt_ref, before_ref):
    i = pl.program_id(0)
    tm = rt_ref.shape[0]
    rt = rt_ref[...]
    lane = lax.broadcasted_iota(jnp.int32, rt.shape, 1)
    e0 = rt[:, 0:1].astype(jnp.int32)
    e1 = rt[:, 1:2].astype(jnp.int32)
    member = ((lane == e0) | (lane == e1)).astype(F32)

    @pl.when(i == 0)
    def _():
        ri = lax.broadcasted_iota(jnp.int32, before_ref.shape, 0)
        ci = lax.broadcasted_iota(jnp.int32, before_ref.shape, 1)
        before_ref[...] = (ri > ci).astype(BF16)
        cnt = jnp.broadcast_to(jnp.sum(tc_ref[...], axis=0, keepdims=True) / SUBLANES, cnt_ref.shape)
        padded = jnp.ceil(cnt / MOE_BLOCK) * MOE_BLOCK
        l8 = lax.broadcasted_iota(jnp.int32, cnt.shape, 1)
        incl = padded
        d = 1
        while d < LANES:
            incl = incl + jnp.where(l8 >= d, pltpu.roll(incl, d, axis=1), 0.0)
            d *= 2
        cnt_ref[...] = incl - padded
        seg_start = (incl - padded)[0:1, :]
        seg_end = incl[0:1, :]
        real_end = seg_start + cnt[0:1, :]
        bl = lax.broadcasted_iota(jnp.int32, blocks_ref.shape, 1)
        b0 = (lax.broadcasted_iota(jnp.int32, blocks_ref.shape, 0) * MOE_BLOCK).astype(F32)
        is_expert = bl < N_EXPERTS
        expert = jnp.sum(jnp.where(is_expert & (seg_end <= b0), 1.0, 0.0), axis=-1, keepdims=True)
        real = jnp.maximum(jnp.minimum(real_end, b0 + MOE_BLOCK) - jnp.maximum(seg_start, b0), 0.0)
        n_real = jnp.sum(jnp.where(is_expert, real, 0.0), axis=-1, keepdims=True)
        expert = jnp.minimum(expert, N_EXPERTS - 1.0)
        later = is_expert & (cnt[0:1, :] > 0.0) & (bl.astype(F32) > expert)
        nxt = jnp.min(jnp.where(later, bl.astype(F32), float(N_EXPERTS)), axis=-1, keepdims=True)
        nxt = jnp.where(nxt < N_EXPERTS, nxt, expert)
        blocks_ref[...] = jnp.where(bl == 0, expert, jnp.where(bl == 1, n_real, jnp.where(bl == 2, nxt, 0.0))
                                    ).astype(jnp.int32)

    sub = before_ref.shape[0]
    run = cnt_ref[0:1, :]
    pos = []
    for j in range(tm // sub):
        mj = member[j * sub:(j + 1) * sub]
        pos.append(run + jnp.dot(before_ref[...], mj.astype(BF16), preferred_element_type=F32))
        run = run + jnp.sum(mj, axis=0, keepdims=True)
    pos = jnp.concatenate(pos, axis=0)
    d0 = jnp.sum(jnp.where(lane == e0, pos, 0.0), axis=-1, keepdims=True)
    d1 = jnp.sum(jnp.where(lane == e1, pos, 0.0), axis=-1, keepdims=True)
    dest = jnp.where(lane == 0, d0, jnp.where(lane == 1, d1, 0.0))
    dest_ref[...] = dest.T[:SUBLANES].astype(jnp.int32)
    cnt_ref[...] = jnp.broadcast_to(run, cnt_ref.shape)


def _slots(rt, tile_counts, n_blocks, tm):
    t = rt.shape[0]
    return pl.pallas_call(
        _slots_kernel,
        grid=(t // tm,),
        in_specs=[pl.BlockSpec((tm, LANES), lambda i: (i, 0)),
                  pl.BlockSpec(tile_counts.shape, lambda i: (0, 0))],
        out_specs=[pl.BlockSpec((SUBLANES, tm), lambda i: (0, i)),
                   pl.BlockSpec((n_blocks, LANES), lambda i: (0, 0))],
        out_shape=[jax.ShapeDtypeStruct((SUBLANES, t), jnp.int32),
                   jax.ShapeDtypeStruct((n_blocks, LANES), jnp.int32)],
        scratch_shapes=[pltpu.VMEM((SUBLANES, LANES), F32), pltpu.VMEM((LANES, LANES), BF16)],
        compiler_params=_cparams("arbitrary"),
        name="moe_slots",
    )(rt, tile_counts)


def _sc_mesh():
    return plsc.VectorSubcoreMesh(core_axis_name="c", subcore_axis_name="s")


def _sc_worker():
    return lax.axis_index("s") * SC_CORES + lax.axis_index("c")


def _sc_scatter_rows(src, idx, n_rows):
    t = src.shape[0]
    per_w = t // SC_WORKERS
    n_win = per_w // SC_WINDOW
    assert per_w % SC_WINDOW == 0 and n_win % SC_SLOTS == 0, "every worker walks whole rings of windows"
    idx = idx.reshape(TOP_K, SC_WORKERS, n_win, SC_WINDOW)

    @functools.partial(
        pl.kernel, mesh=_sc_mesh(),
        out_type=jax.ShapeDtypeStruct((n_rows,) + src.shape[1:], src.dtype),
        scratch_types=[pltpu.VMEM((TOP_K, n_win, SC_WINDOW), jnp.int32),
                       pltpu.VMEM((SC_SLOTS, SC_WINDOW) + src.shape[1:], src.dtype),
                       pltpu.SemaphoreType.DMA((SC_SLOTS,)), pltpu.SemaphoreType.DMA((SC_SLOTS,))],
        compiler_params=pltpu.CompilerParams(use_tc_tiling_on_sc=True),
        name="sc_dispatch")
    def scatter(src_hbm, idx_hbm, out_hbm, idx_v, rows_v, lsem, ssem):
        wid = _sc_worker()
        base = wid * per_w
        for k in range(TOP_K):
            pltpu.sync_copy(idx_hbm.at[k, wid], idx_v.at[k])

        def load(w, slot):
            return pltpu.make_async_copy(src_hbm.at[pl.ds(base + w * SC_WINDOW, SC_WINDOW)], rows_v.at[slot],
                                         lsem.at[slot])

        def put(w, slot, k):
            return pltpu.make_async_copy(rows_v.at[slot], out_hbm.at[idx_v.at[k, w]], ssem.at[slot])

        for s in range(SC_SLOTS - 1):
            load(s, s).start()

        @pl.loop(0, n_win, step=SC_SLOTS)
        def _(w0):
            for s in range(SC_SLOTS):
                w = w0 + s
                prev = (s - 1) % SC_SLOTS

                @pl.when(w + SC_SLOTS - 1 < n_win)
                def _():
                    @pl.when(w >= 1)
                    def _():
                        for k in range(TOP_K):
                            put(w - 1, prev, k).wait()

                    load(w + SC_SLOTS - 1, prev).start()

                load(w, s).wait()
                for k in range(TOP_K):
                    put(w, s, k).start()

        for s in range(SC_SLOTS):
            for k in range(TOP_K):
                put(n_win - SC_SLOTS + s, s, k).wait()

    return scatter(src, idx)


def _sc_gather_rows(table, idx):
    b = idx.shape[0]
    per_w = b // SC_WORKERS
    n_win = per_w // SC_WINDOW
    assert per_w % SC_WINDOW == 0 and n_win % SC_SLOTS == 0, "every worker walks whole rings of windows"
    idx = idx.reshape(SC_WORKERS, n_win, SC_WINDOW)

    @functools.partial(
        pl.kernel, mesh=_sc_mesh(),
        out_type=jax.ShapeDtypeStruct((b,) + table.shape[1:], table.dtype),
        scratch_types=[pltpu.VMEM((n_win, SC_WINDOW), jnp.int32),
                       pltpu.VMEM((SC_SLOTS, SC_WINDOW) + table.shape[1:], table.dtype),
                       pltpu.SemaphoreType.DMA((SC_SLOTS,)), pltpu.SemaphoreType.DMA((SC_SLOTS,))],
        compiler_params=pltpu.CompilerParams(use_tc_tiling_on_sc=True),
        name="sc_combine_gather")
    def gather(table_hbm, idx_hbm, out_hbm, idx_v, rows_v, gsem, psem):
        wid = _sc_worker()
        base = wid * per_w
        pltpu.sync_copy(idx_hbm.at[wid], idx_v)

        def get(w, slot):
            return pltpu.make_async_copy(table_hbm.at[idx_v.at[w]], rows_v.at[slot], gsem.at[slot])

        def put(w, slot):
            return pltpu.make_async_copy(rows_v.at[slot], out_hbm.at[pl.ds(base + w * SC_WINDOW, SC_WINDOW)],
                                         psem.at[slot])

        for s in range(SC_SLOTS - 1):
            get(s, s).start()

        @pl.loop(0, n_win, step=SC_SLOTS)
        def _(w0):
            for s in range(SC_SLOTS):
                w = w0 + s
                prev = (s - 1) % SC_SLOTS

                @pl.when(w + SC_SLOTS - 1 < n_win)
                def _():
                    @pl.when(w >= 1)
                    def _():
                        put(w - 1, prev).wait()

                    get(w + SC_SLOTS - 1, prev).start()

                get(w, s).wait()
                put(w, s).start()

        for s in range(SC_SLOTS):
            put(n_win - SC_SLOTS + s, s).wait()

    return gather(table, idx)


def _expert_weight_copies(e, slot, w_hbm, w_buf, sem):
    return [pltpu.make_async_copy(w.at[e], buf.at[slot], sem.at[slot, i]) for i, (w, buf) in enumerate(zip(w_hbm, w_buf))]


def _experts_kernel(be_ref, nv_ref, nx_ref, xs_ref, wg_hbm, wu_hbm, wd_hbm, ys_ref,
                    wgf_ref, wuf_ref, wdf_ref, wgb_ref, wub_ref, wdb_ref, slot_ref, sem):
    b = pl.program_id(0)
    e = be_ref[b]
    active = nv_ref[b] > 0
    w_hbm = (wg_hbm, wu_hbm, wd_hbm)
    w_buf = (wgf_ref, wuf_ref, wdf_ref)

    @pl.when(b == 0)
    def _():
        slot_ref[0] = 0
        for c in _expert_weight_copies(e, 0, w_hbm, w_buf, sem):
            c.start()

    @pl.when(active & ((b == 0) | (be_ref[jnp.maximum(b - 1, 0)] != e)))
    def _():
        slot = slot_ref[0]
        for c in _expert_weight_copies(e, slot, w_hbm, w_buf, sem):
            c.wait()
        wgb_ref[...] = wgf_ref[slot].astype(BF16)
        wub_ref[...] = wuf_ref[slot].astype(BF16)
        wdb_ref[...] = wdf_ref[slot].astype(BF16)
        nxt = nx_ref[b]

        @pl.when(nxt != e)
        def _():
            for c in _expert_weight_copies(nxt, 1 - slot, w_hbm, w_buf, sem):
                c.start()

        slot_ref[0] = 1 - slot

    @pl.when(active)
    def _():
        row = lax.broadcasted_iota(jnp.int32, (MOE_BLOCK, D_MODEL), 0)
        xb = jnp.where(row < nv_ref[b], _load_token_rows(xs_ref, MOE_BLOCK, BF16), 0.0)
        hg = jnp.dot(xb, wgb_ref[...], preferred_element_type=F32)
        hu = jnp.dot(xb, wub_ref[...], preferred_element_type=F32)
        hb = (_silu(hg) * hu).astype(BF16)
        _store_token_rows(ys_ref, jnp.dot(hb, wdb_ref[...], preferred_element_type=F32))

    @pl.when(jnp.logical_not(active))
    def _():
        ys_ref[...] = jnp.zeros_like(ys_ref)


def _experts(block_expert, n_valid, next_expert, xs, wg, wu, wd):
    nb = xs.shape[0] // (MOE_BLOCK * ROW_TILES)
    blk = pl.BlockSpec((MOE_BLOCK * ROW_TILES, LANES), lambda b, be, nv, nx: (b, 0))
    hbm = pl.BlockSpec(memory_space=pl.ANY)
    return pl.pallas_call(
        _experts_kernel,
        grid_spec=pltpu.PrefetchScalarGridSpec(
            num_scalar_prefetch=3,
            grid=(nb,),
            in_specs=[blk, hbm, hbm, hbm],
            out_specs=blk,
            scratch_shapes=[pltpu.VMEM((2, D_MODEL, D_EXPERT), F32), pltpu.VMEM((2, D_MODEL, D_EXPERT), F32),
                            pltpu.VMEM((2, D_EXPERT, D_MODEL), F32),
                            pltpu.VMEM((D_MODEL, D_EXPERT), BF16), pltpu.VMEM((D_MODEL, D_EXPERT), BF16),
                            pltpu.VMEM((D_EXPERT, D_MODEL), BF16),
                            pltpu.SMEM((1,), jnp.int32), pltpu.SemaphoreType.DMA((2, 3))]),
        out_shape=jax.ShapeDtypeStruct(xs.shape, xs.dtype),
        compiler_params=_cparams("arbitrary"),
        name="moe_experts",
    )(block_expert, n_valid, next_expert, xs, wg, wu, wd)


def _combine_kernel(rt_ref, p_ref, h_hbm, y_hbm, gp_ref, wpg_ref, wp_ref, gf_ref, o_ref,
                    h_ring, y0_ring, y1_ring, sem, *, sub):
    s = pl.program_id(0)
    n_steps = pl.num_programs(0)
    tm = o_ref.shape[0]
    n_sub = tm // sub

    def copies(step, slot):
        y_rows = tm * ROW_TILES
        return [pltpu.make_async_copy(h_hbm.at[pl.ds(step * tm, tm)], h_ring.at[slot], sem.at[slot, 0]),
                pltpu.make_async_copy(y_hbm.at[pl.ds(step * y_rows, y_rows)], y0_ring.at[slot], sem.at[slot, 1]),
                pltpu.make_async_copy(y_hbm.at[pl.ds((step + n_steps) * y_rows, y_rows)], y1_ring.at[slot],
                                      sem.at[slot, 2])]

    @pl.when(s == 0)
    def _():
        for ahead in range(COMBINE_SLOTS - 1):
            for c in copies(ahead, ahead):
                c.start()

    @pl.when(s + COMBINE_SLOTS - 1 < n_steps)
    def _():
        for c in copies(s + COMBINE_SLOTS - 1, lax.rem(s + COMBINE_SLOTS - 1, COMBINE_SLOTS)):
            c.start()

    slot = lax.rem(s, COMBINE_SLOTS)
    for c in copies(s, slot):
        c.wait()
    h_ref, y0_ref, y1_ref = h_ring.at[slot], y0_ring.at[slot], y1_ring.at[slot]

    def residual(r):
        rows = slice(r * sub, (r + 1) * sub)
        rt = rt_ref[rows, :]
        h = h_ref[rows, :] + (_load_token_rows(y0_ref, sub, F32, r * sub) * rt[:, 2:3]
                              + _load_token_rows(y1_ref, sub, F32, r * sub) * rt[:, 3:4])
        return h, _rms(h, gp_ref[...]).astype(BF16)

    def products(r, xn):
        return (jnp.dot(xn, wpg_ref[...], preferred_element_type=F32),
                _dot(p_ref[r * sub:(r + 1) * sub, :], wp_ref[...]))

    def finish(r, h, gate_lin, ple):
        h = h + ple * _sigmoid(gate_lin)
        o_ref[r * sub:(r + 1) * sub, :] = _rms(h, gf_ref[...])

    h, xn = residual(0)
    for r in range(n_sub):
        gate_lin, ple = products(r, xn)
        if r + 1 < n_sub:
            h_next, xn = residual(r + 1)
        finish(r, h, gate_lin, ple)
        if r + 1 < n_sub:
            h = h_next


def _combine(h1, rt, p2, y, gp, wpg, wp, gf, tm):
    t = h1.shape[0]
    nt = t // tm
    tile = lambda n: pl.BlockSpec((tm, n), lambda i: (i, 0))
    full = lambda a, b: pl.BlockSpec((a, b), lambda i: (0, 0))
    hbm = pl.BlockSpec(memory_space=pl.ANY)
    assert nt >= COMBINE_SLOTS - 1
    return pl.pallas_call(
        functools.partial(_combine_kernel, sub=256),
        grid=(nt,),
        in_specs=[tile(LANES), tile(D_PLE), hbm, hbm,
                  full(1, D_MODEL), full(D_MODEL, D_MODEL), full(D_PLE, D_MODEL), full(1, D_MODEL)],
        out_specs=tile(D_MODEL),
        out_shape=jax.ShapeDtypeStruct((t, D_MODEL), F32),
        scratch_shapes=[pltpu.VMEM((COMBINE_SLOTS, tm, D_MODEL), F32),
                        pltpu.VMEM((COMBINE_SLOTS, tm * ROW_TILES, LANES), y.dtype),
                        pltpu.VMEM((COMBINE_SLOTS, tm * ROW_TILES, LANES), y.dtype),
                        pltpu.SemaphoreType.DMA((COMBINE_SLOTS, 3))],
        compiler_params=_cparams("arbitrary"),
        name="moe_combine_ple",
    )(rt, p2, h1, y, gp, wpg, wp, gf)


def _lane_row(vals, offset):
    return jnp.zeros((1, LANES), F32).at[0, offset:offset + vals.shape[0]].set(vals)


def kernel(x, p, norm_mix, w_in, lru_conv_w, lru_conv_b, lru_wa, lru_ba, lru_wi, lru_bi, lru_lambda,
           lru_out_norm, gdn_conv_w, gdn_a_log, gdn_dt_bias, gdn_out_norm, w_out, norm_ffn,
           w_router_group, b_router_group, w_router_expert, b_router_expert, w_exp_gate, w_exp_up,
           w_exp_down, norm_ple, w_ple_gate, w_ple, norm_final):
    bsz, seq, d = x.shape
    t = bsz * seq
    depth = w_in.shape[0]
    assert depth == 1, "the final norm is fused into the last layer's combine kernel"
    n_blocks = -(-t * TOP_K // MOE_BLOCK) + N_EXPERTS
    row = lambda v: v.reshape(1, -1).astype(F32)
    h = x.reshape(t, d).astype(F32)
    for l in range(depth):
        b_gates = jnp.concatenate([lru_ba[l], lru_bi[l]]).reshape(1, -1)
        proj_gdn, y_lru = _proj_lru(h, row(norm_mix[l]), w_in[l], lru_conv_w[l], row(lru_conv_b[l]), lru_wa[l],
                                    lru_wi[l], b_gates, row(lru_lambda[l]), row(lru_out_norm[l]), seq, 512)
        cw = gdn_conv_w[l].reshape(CONV_W, 3, D_GDN).transpose(1, 0, 2)
        w_r = jnp.pad(jnp.concatenate([w_router_group[l], w_router_expert[l]], axis=1),
                      ((0, 0), (0, LANES - N_GROUPS - N_EXPERTS)))
        w_r_hi = w_r.astype(BF16)
        w_r_lo = (w_r - w_r_hi.astype(F32)).astype(BF16)
        w_r = jnp.concatenate([w_r_hi, w_r_hi, w_r_lo], axis=0)
        b_r = _lane_row(jnp.concatenate([b_router_group[l], b_router_expert[l]]), 0)
        h1, xn2, rt, tile_counts = _gdn_router(
            proj_gdn, cw, _lane_row(gdn_a_log[l], GDN_HEADS), _lane_row(gdn_dt_bias[l], GDN_HEADS),
            row(gdn_out_norm[l]), h, y_lru, w_out[l].astype(BF16), row(norm_ffn[l]), w_r, b_r, seq, 512)
        dest8, blocks = _slots(rt, tile_counts, n_blocks, 2048)
        dest = dest8[:TOP_K]
        block_expert, n_valid, next_expert = blocks[:, 0], blocks[:, 1], blocks[:, 2]
        tiles = lambda a: a.reshape(-1, ROW_TILES, LANES)
        xs = _sc_scatter_rows(tiles(xn2), dest, n_blocks * MOE_BLOCK)
        ys = _experts(block_expert, n_valid, next_expert, xs.reshape(-1, LANES), w_exp_gate[l], w_exp_up[l],
                      w_exp_down[l])
        y = _sc_gather_rows(tiles(ys), dest.reshape(-1))
        h = _combine(h1, rt, p[l].reshape(t, -1).astype(F32), y.reshape(-1, LANES), row(norm_ple[l]),
                     w_ple_gate[l].astype(BF16), w_ple[l].astype(BF16), row(norm_final), 1024)
    return h.reshape(bsz, seq, d).astype(x.dtype)
```

```python
import functools

import jax
import jax.numpy as jnp
from jax import lax
from jax.experimental import pallas as pl
from jax.experimental.pallas import tpu as pltpu
from jax.experimental.pallas import tpu_sc as plsc

D_MODEL = 1024
D_LRU = 512
LRU_BLOCKS = 8
LRU_BLOCK_W = D_LRU // LRU_BLOCKS
LRU_C = 8.0
D_GDN = 512
GDN_HEADS = 4
GDN_HEAD_DIM = D_GDN // GDN_HEADS
CONV_W = 4
CHUNK = 64
N_GROUPS = 4
EXPERTS_PER_GROUP = 8
N_EXPERTS = N_GROUPS * EXPERTS_PER_GROUP
TOP_K = 2
D_EXPERT = 512
MOE_BLOCK = 512
D_PLE = 256
EPS = 1e-6

LANES = 128
SUBLANES = 8
ROW_TILES = D_MODEL // (2 * LANES)
SC_CORES = 2
SC_SUBCORES = 16
SC_WORKERS = SC_CORES * SC_SUBCORES
SC_WINDOW = 32
SC_SLOTS = 4
COMBINE_SLOTS = 3
PROJ_COLS = 2 * D_LRU + 4 * D_GDN + LANES
VMEM_LIMIT = 56 * 1024 * 1024

BF16 = jnp.bfloat16
F32 = jnp.float32


def _cparams(*sem):
    return pltpu.CompilerParams(dimension_semantics=sem, vmem_limit_bytes=VMEM_LIMIT)


def _rms(x, g):
    return x * lax.rsqrt(jnp.mean(x * x, axis=-1, keepdims=True) + EPS) * g


def _sigmoid(x):
    return 0.5 * jnp.tanh(0.5 * x) + 0.5


def _silu(x):
    return x * _sigmoid(x)


def _dot(a, b):
    return jnp.dot(a.astype(BF16), b.astype(BF16), preferred_element_type=F32)


def _dot_nt(a, b):
    return lax.dot_general(a.astype(BF16), b.astype(BF16), (((1,), (1,)), ((), ())),
                           preferred_element_type=F32)


def _dot_tn(a, b):
    return lax.dot_general(a.astype(BF16), b.astype(BF16), (((0,), (0,)), ((), ())),
                           preferred_element_type=F32)


def _conv_scratch(rows, channels):
    return pltpu.VMEM((channels // LANES, 2 * (SUBLANES + rows), LANES), F32)


def _conv_reset(xp_ref):
    for k in range(xp_ref.shape[0]):
        xp_ref.at[k][pl.ds(0, SUBLANES, stride=2), :] = jnp.zeros((SUBLANES, LANES), F32)


def _causal_conv(xp_ref, x_ref, w, r0, rows, col0=0):
    out = []
    for k in range(xp_ref.shape[0]):
        lanes = slice(k * LANES, (k + 1) * LANES)
        slab = xp_ref.at[k]
        xk = x_ref[r0:r0 + rows, col0 + k * LANES:col0 + (k + 1) * LANES]
        slab[pl.ds(2 * (SUBLANES + r0), rows, stride=2), :] = xk
        acc = xk * w[CONV_W - 1:CONV_W, lanes]
        for j in range(1, CONV_W):
            acc = acc + slab[pl.ds(2 * (SUBLANES + r0 - j), rows, stride=2), :] * w[CONV_W - 1 - j:CONV_W - j, lanes]
        out.append(acc)
    return jnp.concatenate(out, axis=1)


def _conv_carry(xp_ref, x_ref, col0=0):
    ts = x_ref.shape[0]
    for k in range(xp_ref.shape[0]):
        xp_ref.at[k][pl.ds(0, SUBLANES, stride=2), :] = x_ref[ts - SUBLANES:, col0 + k * LANES:col0 + (k + 1) * LANES]


def _interleave(*stages):
    live = list(stages)
    while live:
        for g in list(live):
            try:
                next(g)
            except StopIteration:
                live.remove(g)


def _store_token_rows(ref, x, row0=0):
    rows = x.shape[0]
    bits = lax.bitcast_convert_type(x.astype(BF16).astype(F32), jnp.uint32)
    words = (bits[:, :D_MODEL // 2] >> 16) | bits[:, D_MODEL // 2:]
    for j in range(ROW_TILES):
        ref[pl.ds(row0 * ROW_TILES + j, rows, stride=ROW_TILES), :] = words[:, j * LANES:(j + 1) * LANES]


def _load_token_rows(ref, rows, dtype, row0=0):
    words = [ref[pl.ds(row0 * ROW_TILES + j, rows, stride=ROW_TILES), :] for j in range(ROW_TILES)]
    lo = [lax.bitcast_convert_type(w << 16, F32).astype(dtype) for w in words]
    hi = [lax.bitcast_convert_type(w & jnp.uint32(0xFFFF0000), F32).astype(dtype) for w in words]
    return jnp.concatenate(lo + hi, axis=1)


def _proj_lru_kernel(x_ref, gin_ref, win_ref, cw_ref, cb_ref, wa_ref, wi_ref, bg_ref, lam_ref, og_ref,
                     gdn_ref, o_ref, pj_ref, tail_ref, h_ref, wg_ref, wb_ref, *, tiles_per_seq, col_tile):
    g = pl.program_id(0)
    ts = x_ref.shape[0]
    pj_new = pj_ref.at[g % 2]
    pj = pj_ref.at[(g + 1) % 2]

    @pl.when(g == 0)
    def _():
        pj_ref[...] = jnp.zeros_like(pj_ref)
        in_cols = win_ref.shape[1]
        whole = in_cols // LANES * LANES
        for c0 in range(0, whole, col_tile):
            wb_ref[:, c0:c0 + col_tile] = win_ref[:, c0:c0 + col_tile].astype(BF16)
        wb_ref[:, whole:] = jnp.zeros((D_MODEL, PROJ_COLS - whole), BF16)
        wb_ref[:, whole:in_cols] = win_ref[:, whole:].astype(BF16)
        wg_ref[...] = jnp.zeros_like(wg_ref)
        for n in range(LRU_BLOCKS):
            blk = slice(n * LRU_BLOCK_W, (n + 1) * LRU_BLOCK_W)
            wg_ref[blk, blk] = wa_ref[n].astype(BF16)
            wg_ref[blk, D_LRU + n * LRU_BLOCK_W:D_LRU + (n + 1) * LRU_BLOCK_W] = wi_ref[n].astype(BF16)

    @pl.when((g == 0) | (lax.rem(g - 1, tiles_per_seq) == 0))
    def _():
        _conv_reset(tail_ref)
        h_ref[...] = jnp.zeros_like(h_ref)

    done = []

    def after_projection(v):
        return v + jnp.concatenate([done[max(len(done) - 2, 0)]] * (v.shape[-1] // LANES), axis=-1)

    def project():
        xn = _rms(x_ref[...], gin_ref[...]).astype(BF16)
        yield
        for c0 in range(0, PROJ_COLS, col_tile):
            c1 = min(c0 + col_tile, PROJ_COLS)
            y = jnp.dot(xn, wb_ref[:, c0:c1], preferred_element_type=F32)
            if c1 <= 2 * D_LRU:
                pj_new[:, c0:c1] = y
            else:
                gdn_ref[:, c0 - 2 * D_LRU:c1 - 2 * D_LRU] = y
            bits = lax.bitcast_convert_type(y[ts - 1:, c1 - c0 - LANES:], jnp.uint32)
            done.append(lax.bitcast_convert_type((bits >> 16) >> 16, F32))
            yield

    def recur():
        xc = _causal_conv(tail_ref, pj, cw_ref[...], 0, ts) + cb_ref[...]
        _conv_carry(tail_ref, pj)
        yield
        gates = _sigmoid(_dot(xc, wg_ref[...]) + after_projection(bg_ref[...]))
        r = gates[:, :D_LRU]
        i = gates[:, D_LRU:]
        yield
        log_a = LRU_C * r * jax.nn.log_sigmoid(after_projection(lam_ref[...]))
        a = jnp.exp(log_a)
        th = jnp.tanh(log_a)
        u = jnp.sqrt(-2.0 * th) * lax.rsqrt(1.0 - th) * (i * xc)
        yield
        a = a.reshape(ts // SUBLANES, SUBLANES, D_LRU)
        u = u.reshape(ts // SUBLANES, SUBLANES, D_LRU)
        row = lax.broadcasted_iota(jnp.int32, a.shape, 1)
        d = 1
        while d < SUBLANES:
            keep = row >= d
            a_prev = jnp.where(keep, pltpu.roll(a, d, axis=1), 1.0)
            u_prev = jnp.where(keep, pltpu.roll(u, d, axis=1), 0.0)
            u = a * u_prev + u
            a = a * a_prev
            d *= 2
            yield
        carry = after_projection(h_ref[...])
        groups = []
        for n in range(ts // SUBLANES):
            groups.append(a[n] * carry + u[n])
            carry = groups[-1][SUBLANES - 1:]
        h = jnp.concatenate(groups, axis=0)
        h_ref[...] = carry
        yield
        y = h * jax.nn.gelu(pj[:, D_LRU:])
        o_ref[...] = _rms(y, after_projection(og_ref[...])).astype(o_ref.dtype)

    _interleave(project(), recur())


def _proj_lru(x2, gin, win, cw, cb, wa, wi, bg, lam, og, seq, ts):
    t = x2.shape[0]
    n_tiles = t // ts
    this = lambda g: (jnp.minimum(g, n_tiles - 1), 0)
    last = lambda g: (jnp.maximum(g - 1, 0), 0)
    const = lambda *shape: pl.BlockSpec(shape, lambda g: (0,) * len(shape))
    gdn_cols = PROJ_COLS - 2 * D_LRU
    return pl.pallas_call(
        functools.partial(_proj_lru_kernel, tiles_per_seq=seq // ts, col_tile=D_LRU),
        grid=(n_tiles + 1,),
        in_specs=[pl.BlockSpec((ts, D_MODEL), this), const(1, D_MODEL),
                  pl.BlockSpec(win.shape, lambda g: (0, 0), pipeline_mode=pl.Buffered(1)),
                  const(CONV_W, D_LRU), const(1, D_LRU),
                  const(LRU_BLOCKS, LRU_BLOCK_W, LRU_BLOCK_W), const(LRU_BLOCKS, LRU_BLOCK_W, LRU_BLOCK_W),
                  const(1, 2 * D_LRU), const(1, D_LRU), const(1, D_LRU)],
        out_specs=[pl.BlockSpec((ts, gdn_cols), this), pl.BlockSpec((ts, D_LRU), last)],
        out_shape=[jax.ShapeDtypeStruct((t, gdn_cols), F32), jax.ShapeDtypeStruct((t, D_LRU), BF16)],
        scratch_shapes=[pltpu.VMEM((2, ts, 2 * D_LRU), F32), _conv_scratch(ts, D_LRU), pltpu.VMEM((1, D_LRU), F32),
                        pltpu.VMEM((D_LRU, 2 * D_LRU), BF16), pltpu.VMEM((D_MODEL, PROJ_COLS), BF16)],
        compiler_params=_cparams("arbitrary"),
        name="in_proj_rglru",
    )(x2, gin, win, cw, cb, wa, wi, bg, lam, og)


def _gdn_router_kernel(pj_ref, cw_ref, alog_ref, dtb_ref, og_ref,
                       x_ref, yl_ref, wo_ref, gf_ref, wr_ref, br_ref, h_ref, xn_ref, rt_ref, cnt_ref,
                       qt_ref, kt_ref, vt_ref, qs_ref, ks_ref, vs_ref, bs_ref, gc_ref, gct_ref, st_ref, yg_ref,
                       *, tiles_per_seq, group_chunks, prep_rows, sub):
    g = pl.program_id(0)
    ts = pj_ref.shape[0]
    dk = GDN_HEAD_DIM
    nc = ts // CHUNK
    n_sub = ts // sub
    q0, k0, v0, z0, ba0 = 0, D_GDN, 2 * D_GDN, 3 * D_GDN, 4 * D_GDN
    yg_new = yg_ref.at[g % 2]
    yg = yg_ref.at[(g + 1) % 2]

    @pl.when(g == 0)
    def _():
        yg_ref[...] = jnp.zeros_like(yg_ref)

    @pl.when(lax.rem(g, tiles_per_seq) == 0)
    def _():
        for tail_ref in (qt_ref, kt_ref, vt_ref):
            _conv_reset(tail_ref)
        st_ref[...] = jnp.zeros_like(st_ref)

    done = []
    hs = {}
    counts = []

    def after_projection(v):
        if not done:
            return v
        return v + jnp.concatenate([done[-1]] * (v.shape[-1] // LANES), axis=-1)

    def project_out():
        for r in range(n_sub):
            rows = slice(r * sub, (r + 1) * sub)
            h = x_ref[rows, :] + jnp.dot(yl_ref[rows, :], wo_ref[:D_LRU, :], preferred_element_type=F32) \
                + jnp.dot(yg[rows, :], wo_ref[D_LRU:, :], preferred_element_type=F32)
            h_ref[rows, :] = h
            hs[r] = h
            bits = lax.bitcast_convert_type(h[sub - 1:, D_MODEL - LANES:], jnp.uint32)
            done.append(lax.bitcast_convert_type((bits >> 16) >> 16, F32))
            yield

    def route(r, h):
        xn = _rms(h, gf_ref[...])
        _store_token_rows(xn_ref, xn, r * sub)
        xh = xn.astype(BF16)
        xl = (xn - xh.astype(F32)).astype(BF16)
        logits = jnp.dot(jnp.concatenate([xh, xl, xh], axis=1), wr_ref[...],
                         preferred_element_type=F32) + br_ref[...]
        lane = lax.broadcasted_iota(jnp.int32, logits.shape, 1).astype(F32)
        big = jnp.float32(2 * LANES)
        ninf = jnp.float32(-jnp.inf)

        def top1(vals):
            m = jnp.max(vals, axis=-1, keepdims=True)
            return m, jnp.min(jnp.where(vals == m, lane, big), axis=-1, keepdims=True)

        gl = jnp.where(lane < N_GROUPS, logits, ninf)
        gmax, gsel = top1(gl)
        p_group = 1.0 / jnp.sum(jnp.exp(gl - gmax), axis=-1, keepdims=True)
        lo = N_GROUPS + EXPERTS_PER_GROUP * gsel
        el = jnp.where((lane >= lo) & (lane < lo + EXPERTS_PER_GROUP), logits, ninf)
        m1, i1 = top1(el)
        m2, i2 = top1(jnp.where(lane == i1, ninf, el))
        rr = jnp.exp(m2 - m1)
        g1 = p_group / (1.0 + rr)
        g2 = p_group * rr / (1.0 + rr)
        rt_ref[r * sub:(r + 1) * sub, :] = jnp.where(
            lane == 0, i1 - N_GROUPS,
            jnp.where(lane == 1, i2 - N_GROUPS, jnp.where(lane == 2, g1, jnp.where(lane == 3, g2, 0.0))))
        member = ((lane == i1 - N_GROUPS) | (lane == i2 - N_GROUPS)).astype(F32)
        counts.append(jnp.sum(member, axis=0, keepdims=True))

    def route_subtiles(subtiles):
        for r in subtiles:
            route(r, hs[r])
            yield

    ri = lax.broadcasted_iota(jnp.int32, (CHUNK, CHUNK), 0)
    ci = lax.broadcasted_iota(jnp.int32, (CHUNK, CHUNK), 1)
    causal = ri >= ci
    strict = ri > ci
    tril = causal.astype(F32)
    eye = (ri == ci).astype(F32)
    og = og_ref[...]

    def l2n(x, scale):
        parts = []
        for h in range(GDN_HEADS):
            xh = x[:, h * dk:(h + 1) * dk]
            parts.append(xh * (lax.rsqrt(jnp.sum(xh * xh, axis=-1, keepdims=True) + EPS) * scale))
        return jnp.concatenate(parts, axis=1)

    def prepare(r0, r1):
        for p0 in range(r0, r1, prep_rows):
            rows = slice(p0, p0 + prep_rows)
            conv = lambda tail_ref, part, col0: _silu(
                _causal_conv(tail_ref, pj_ref, after_projection(cw_ref[part]), p0, prep_rows, col0))
            qs_ref[rows, :] = l2n(conv(qt_ref, 0, q0), dk ** -0.5)
            yield
            ks_ref[rows, :] = l2n(conv(kt_ref, 1, k0), 1.0)
            yield
            vs_ref[rows, :] = conv(vt_ref, 2, v0)
            ba = pj_ref[rows, ba0:ba0 + LANES]
            bs_ref[rows, :] = _sigmoid(ba)
            g = -jnp.exp(alog_ref[...]) * jax.nn.softplus(ba + dtb_ref[...])
            for c0 in range(0, prep_rows, CHUNK):
                gc_ref[p0 + c0:p0 + c0 + CHUNK, :] = jnp.dot(tril, g[c0:c0 + CHUNK], precision=lax.Precision.HIGHEST,
                                                             preferred_element_type=F32)
            gct_ref[:, rows] = gc_ref[rows, :].T
            yield

    terms = {}

    def chunk_terms(pairs):
        n = range(len(pairs))
        rows = [slice(c * CHUNK, (c + 1) * CHUNK) for c, _ in pairs]
        cols = [slice(h * dk, (h + 1) * dk) for _, h in pairs]
        gl = [GDN_HEADS + h for _, h in pairs]
        kh = [ks_ref[rows[i], cols[i]] for i in n]
        kb = [kh[i] * bs_ref[rows[i], pairs[i][1]:pairs[i][1] + 1] for i in n]
        r = [_dot_nt(jnp.concatenate([kb[i], qs_ref[rows[i], cols[i]]], axis=0), kh[i]) for i in n]
        yield
        gcol = [gc_ref[rows[i], gl[i]:gl[i] + 1] for i in n]
        decay = []
        for i in n:
            diff = gcol[i] - gct_ref[gl[i]:gl[i] + 1, rows[i]]
            decay.append(jnp.where(causal, jnp.exp(jnp.where(causal, diff, 0.0)), 0.0))
        a = [jnp.where(strict, r[i][:CHUNK] * decay[i], 0.0) for i in n]
        qk = [(r[i][CHUNK:] * decay[i]).astype(BF16) for i in n]
        tinv = [eye - a[i] for i in n]
        p = 2
        while p < CHUNK:
            a = [_dot(a[i], a[i]) for i in n]
            yield
            tinv = [tinv[i] + _dot(tinv[i], a[i]) for i in n]
            yield
            p *= 2
        eg = [jnp.exp(gcol[i]) for i in n]
        rhs = [jnp.concatenate([vs_ref[rows[i], cols[i]] * bs_ref[rows[i], pairs[i][1]:pairs[i][1] + 1],
                                kb[i] * eg[i]], axis=1) for i in n]
        uw = [_dot(tinv[i], rhs[i]).astype(BF16) for i in n]
        yield
        qk_uw = [jnp.dot(qk[i], uw[i], preferred_element_type=F32) for i in n]
        glast = [gc_ref[(c + 1) * CHUNK - 1:(c + 1) * CHUNK, gl[i]:gl[i] + 1] for i, (c, _) in enumerate(pairs)]
        kd_uw = [_dot_tn(kh[i] * jnp.exp(glast[i] - gcol[i]), uw[i]) for i in n]
        yield
        for i in n:
            lhs = jnp.concatenate([kd_uw[i][:, dk:], qs_ref[rows[i], cols[i]] * eg[i] - qk_uw[i][:, dk:]],
                                  axis=0).astype(BF16)
            terms[pairs[i]] = (lhs, kd_uw[i][:, :dk], qk_uw[i][:, :dk], jnp.exp(glast[i]))

    state = [st_ref[h] for h in range(GDN_HEADS)]

    def advance(chunks):
        for c in chunks:
            rows = slice(c * CHUNK, (c + 1) * CHUNK)
            r = [jnp.dot(terms[c, h][0], state[h].astype(BF16), preferred_element_type=F32)
                 for h in range(GDN_HEADS)]
            for h in range(GDN_HEADS):
                cols = slice(h * dk, (h + 1) * dk)
                _, c_add, o_add, egl = terms[c, h]
                o = r[h][dk:] + o_add
                state[h] = egl * state[h] - r[h][:dk] + c_add
                zh = pj_ref[rows, z0 + h * dk:z0 + (h + 1) * dk]
                yg_new[rows, cols] = (_rms(o, og) * _silu(zh)).astype(yg_new.dtype)
            yield

    groups = [range(c0, c0 + group_chunks) for c0 in range(0, nc, group_chunks)]
    pairs_of = lambda chunks: [(c, h) for c in chunks for h in range(GDN_HEADS)]
    span = lambda chunks: (chunks[0] * CHUNK, (chunks[-1] + 1) * CHUNK)
    per_phase = max(1, n_sub // (len(groups) + 1))
    _interleave(prepare(*span(groups[0])), project_out())
    for i, chunks in enumerate(groups):
        side = [route_subtiles(range(i * per_phase, (i + 1) * per_phase))]
        if i + 1 < len(groups):
            side.append(prepare(*span(groups[i + 1])))
        if i > 0:
            side.append(advance(groups[i - 1]))
        _interleave(chunk_terms(pairs_of(chunks)), *side)
    _interleave(advance(groups[-1]), route_subtiles(range(len(groups) * per_phase, n_sub)))
    for h in range(GDN_HEADS):
        st_ref[h] = state[h]
    for tail_ref, col0 in ((qt_ref, q0), (kt_ref, k0), (vt_ref, v0)):
        _conv_carry(tail_ref, pj_ref, col0)
    cnt_ref[...] = jnp.broadcast_to(sum(counts), cnt_ref.shape)


def _gdn_router(proj, cw, alog, dtb, og, x2, yl, wo, gf, wr, br, seq, ts):
    t = x2.shape[0]
    n_tiles = t // ts
    this = lambda g: (jnp.minimum(g, n_tiles - 1), 0)
    last = lambda g: (jnp.maximum(g - 1, 0), 0)
    const = lambda *shape: pl.BlockSpec(shape, lambda g: (0,) * len(shape))
    return pl.pallas_call(
        functools.partial(_gdn_router_kernel, tiles_per_seq=seq // ts, group_chunks=4, prep_rows=128, sub=256),
        grid=(n_tiles + 1,),
        in_specs=[pl.BlockSpec((ts, proj.shape[1]), this),
                  const(3, CONV_W, D_GDN), const(1, LANES), const(1, LANES), const(1, GDN_HEAD_DIM),
                  pl.BlockSpec((ts, D_MODEL), last), pl.BlockSpec((ts, D_LRU), last), const(D_MODEL, D_MODEL),
                  const(1, D_MODEL), const(3 * D_MODEL, LANES), const(1, LANES)],
        out_specs=[pl.BlockSpec((ts, D_MODEL), last), pl.BlockSpec((ts * ROW_TILES, LANES), last),
                   pl.BlockSpec((ts, LANES), last), pl.BlockSpec((SUBLANES, LANES), last)],
        out_shape=[jax.ShapeDtypeStruct((t, D_MODEL), F32),
                   jax.ShapeDtypeStruct((t * ROW_TILES, LANES), jnp.uint32),
                   jax.ShapeDtypeStruct((t, LANES), F32),
                   jax.ShapeDtypeStruct((n_tiles * SUBLANES, LANES), F32)],
        scratch_shapes=[_conv_scratch(ts, D_GDN)] * 3
        + [pltpu.VMEM((ts, D_GDN), F32)] * 3
        + [pltpu.VMEM((ts, LANES), F32)] * 2
        + [pltpu.VMEM((LANES, ts), F32)]
        + [pltpu.VMEM((GDN_HEADS, GDN_HEAD_DIM, GDN_HEAD_DIM), F32)]
        + [pltpu.VMEM((2, ts, D_GDN), BF16)],
        compiler_params=_cparams("arbitrary"),
        name="gdn_out_router",
    )(proj, cw, alog, dtb, og, x2, yl, wo, gf, wr, br)


def _slots_kernel(rt_ref, tc_ref, dest_ref, blocks_ref, cnt_ref, before_ref):
    i = pl.program_id(0)
    tm = rt_ref.shape[0]
    rt = rt_ref[...]
    lane = lax.broadcasted_iota(jnp.int32, rt.shape, 1)
    e0 = rt[:, 0:1].astype(jnp.int32)
    e1 = rt[:, 1:2].astype(jnp.int32)
    member = ((lane == e0) | (lane == e1)).astype(F32)

    @pl.when(i == 0)
    def _():
        ri = lax.broadcasted_iota(jnp.int32, before_ref.shape, 0)
        ci = lax.broadcasted_iota(jnp.int32, before_ref.shape, 1)
        before_ref[...] = (ri > ci).astype(BF16)
        cnt = jnp.broadcast_to(jnp.sum(tc_ref[...], axis=0, keepdims=True) / SUBLANES, cnt_ref.shape)
        padded = jnp.ceil(cnt / MOE_BLOCK) * MOE_BLOCK
        l8 = lax.broadcasted_iota(jnp.int32, cnt.shape, 1)
        incl = padded
        d = 1
        while d < LANES:
            incl = incl + jnp.where(l8 >= d, pltpu.roll(incl, d, axis=1), 0.0)
            d *= 2
        cnt_ref[...] = incl - padded
        seg_start = (incl - padded)[0:1, :]
        seg_end = incl[0:1, :]
        real_end = seg_start + cnt[0:1, :]
        bl = lax.broadcasted_iota(jnp.int32, blocks_ref.shape, 1)
        b0 = (lax.broadcasted_iota(jnp.int32, blocks_ref.shape, 0) * MOE_BLOCK).astype(F32)
        is_expert = bl < N_EXPERTS
        expert = jnp.sum(jnp.where(is_expert & (seg_end <= b0), 1.0, 0.0), axis=-1, keepdims=True)
        real = jnp.maximum(jnp.minimum(real_end, b0 + MOE_BLOCK) - jnp.maximum(seg_start, b0), 0.0)
        n_real = jnp.sum(jnp.where(is_expert, real, 0.0), axis=-1, keepdims=True)
        expert = jnp.minimum(expert, N_EXPERTS - 1.0)
        later = is_expert & (cnt[0:1, :] > 0.0) & (bl.astype(F32) > expert)
        nxt = jnp.min(jnp.where(later, bl.astype(F32), float(N_EXPERTS)), axis=-1, keepdims=True)
        nxt = jnp.where(nxt < N_EXPERTS, nxt, expert)
        blocks_ref[...] = jnp.where(bl == 0, expert, jnp.where(bl == 1, n_real, jnp.where(bl == 2, nxt, 0.0))
                                    ).astype(jnp.int32)

    sub = before_ref.shape[0]
    run = cnt_ref[0:1, :]
    pos = []
    for j in range(tm // sub):
        mj = member[j * sub:(j + 1) * sub]
        pos.append(run + jnp.dot(before_ref[...], mj.astype(BF16), preferred_element_type=F32))
        run = run + jnp.sum(mj, axis=0, keepdims=True)
    pos = jnp.concatenate(pos, axis=0)
    d0 = jnp.sum(jnp.where(lane == e0, pos, 0.0), axis=-1, keepdims=True)
    d1 = jnp.sum(jnp.where(lane == e1, pos, 0.0), axis=-1, keepdims=True)
    dest = jnp.where(lane == 0, d0, jnp.where(lane == 1, d1, 0.0))
    dest_ref[...] = dest.T[:SUBLANES].astype(jnp.int32)
    cnt_ref[...] = jnp.broadcast_to(run, cnt_ref.shape)


def _slots(rt, tile_counts, n_blocks, tm):
    t = rt.shape[0]
    return pl.pallas_call(
        _slots_kernel,
        grid=(t // tm,),
        in_specs=[pl.BlockSpec((tm, LANES), lambda i: (i, 0)),
                  pl.BlockSpec(tile_counts.shape, lambda i: (0, 0))],
        out_specs=[pl.BlockSpec((SUBLANES, tm), lambda i: (0, i)),
                   pl.BlockSpec((n_blocks, LANES), lambda i: (0, 0))],
        out_shape=[jax.ShapeDtypeStruct((SUBLANES, t), jnp.int32),
                   jax.ShapeDtypeStruct((n_blocks, LANES), jnp.int32)],
        scratch_shapes=[pltpu.VMEM((SUBLANES, LANES), F32), pltpu.VMEM((LANES, LANES), BF16)],
        compiler_params=_cparams("arbitrary"),
        name="moe_slots",
    )(rt, tile_counts)


def _sc_mesh():
    return plsc.VectorSubcoreMesh(core_axis_name="c", subcore_axis_name="s")


def _sc_worker():
    return lax.axis_index("s") * SC_CORES + lax.axis_index("c")


def _sc_scatter_rows(src, idx, n_rows):
    t = src.shape[0]
    per_w = t // SC_WORKERS
    n_win = per_w // SC_WINDOW
    assert per_w % SC_WINDOW == 0 and n_win % SC_SLOTS == 0, "every worker walks whole rings of windows"
    idx = idx.reshape(TOP_K, SC_WORKERS, n_win, SC_WINDOW)

    @functools.partial(
        pl.kernel, mesh=_sc_mesh(),
        out_type=jax.ShapeDtypeStruct((n_rows,) + src.shape[1:], src.dtype),
        scratch_types=[pltpu.VMEM((TOP_K, n_win, SC_WINDOW), jnp.int32),
                       pltpu.VMEM((SC_SLOTS, SC_WINDOW) + src.shape[1:], src.dtype),
                       pltpu.SemaphoreType.DMA((SC_SLOTS,)), pltpu.SemaphoreType.DMA((SC_SLOTS,))],
        compiler_params=pltpu.CompilerParams(use_tc_tiling_on_sc=True),
        name="sc_dispatch")
    def scatter(src_hbm, idx_hbm, out_hbm, idx_v, rows_v, lsem, ssem):
        wid = _sc_worker()
        base = wid * per_w
        for k in range(TOP_K):
            pltpu.sync_copy(idx_hbm.at[k, wid], idx_v.at[k])

        def load(w, slot):
            return pltpu.make_async_copy(src_hbm.at[pl.ds(base + w * SC_WINDOW, SC_WINDOW)], rows_v.at[slot],
                                         lsem.at[slot])

        def put(w, slot, k):
            return pltpu.make_async_copy(rows_v.at[slot], out_hbm.at[idx_v.at[k, w]], ssem.at[slot])

        for s in range(SC_SLOTS - 1):
            load(s, s).start()

        @pl.loop(0, n_win, step=SC_SLOTS)
        def _(w0):
            for s in range(SC_SLOTS):
                w = w0 + s
                prev = (s - 1) % SC_SLOTS

                @pl.when(w + SC_SLOTS - 1 < n_win)
                def _():
                    @pl.when(w >= 1)
                    def _():
                        for k in range(TOP_K):
                            put(w - 1, prev, k).wait()

                    load(w + SC_SLOTS - 1, prev).start()

                load(w, s).wait()
                for k in range(TOP_K):
                    put(w, s, k).start()

        for s in range(SC_SLOTS):
            for k in range(TOP_K):
                put(n_win - SC_SLOTS + s, s, k).wait()

    return scatter(src, idx)


def _sc_gather_rows(table, idx):
    b = idx.shape[0]
    per_w = b // SC_WORKERS
    n_win = per_w // SC_WINDOW
    assert per_w % SC_WINDOW == 0 and n_win % SC_SLOTS == 0, "every worker walks whole rings of windows"
    idx = idx.reshape(SC_WORKERS, n_win, SC_WINDOW)

    @functools.partial(
        pl.kernel, mesh=_sc_mesh(),
        out_type=jax.ShapeDtypeStruct((b,) + table.shape[1:], table.dtype),
        scratch_types=[pltpu.VMEM((n_win, SC_WINDOW), jnp.int32),
                       pltpu.VMEM((SC_SLOTS, SC_WINDOW) + table.shape[1:], table.dtype),
                       pltpu.SemaphoreType.DMA((SC_SLOTS,)), pltpu.SemaphoreType.DMA((SC_SLOTS,))],
        compiler_params=pltpu.CompilerParams(use_tc_tiling_on_sc=True),
        name="sc_combine_gather")
    def gather(table_hbm, idx_hbm, out_hbm, idx_v, rows_v, gsem, psem):
        wid = _sc_worker()
        base = wid * per_w
        pltpu.sync_copy(idx_hbm.at[wid], idx_v)

        def get(w, slot):
            return pltpu.make_async_copy(table_hbm.at[idx_v.at[w]], rows_v.at[slot], gsem.at[slot])

        def put(w, slot):
            return pltpu.make_async_copy(rows_v.at[slot], out_hbm.at[pl.ds(base + w * SC_WINDOW, SC_WINDOW)],
                                         psem.at[slot])

        for s in range(SC_SLOTS - 1):
            get(s, s).start()

        @pl.loop(0, n_win, step=SC_SLOTS)
        def _(w0):
            for s in range(SC_SLOTS):
                w = w0 + s
                prev = (s - 1) % SC_SLOTS

                @pl.when(w + SC_SLOTS - 1 < n_win)
                def _():
                    @pl.when(w >= 1)
                    def _():
                        put(w - 1, prev).wait()

                    get(w + SC_SLOTS - 1, prev).start()

                get(w, s).wait()
                put(w, s).start()

        for s in range(SC_SLOTS):
            put(n_win - SC_SLOTS + s, s).wait()

    return gather(table, idx)


def _expert_weight_copies(e, slot, w_hbm, w_buf, sem):
    return [pltpu.make_async_copy(w.at[e], buf.at[slot], sem.at[slot, i]) for i, (w, buf) in enumerate(zip(w_hbm, w_buf))]


def _experts_kernel(be_ref, nv_ref, nx_ref, xs_ref, wg_hbm, wu_hbm, wd_hbm, ys_ref,
                    wgf_ref, wuf_ref, wdf_ref, wgb_ref, wub_ref, wdb_ref, slot_ref, sem):
    b = pl.program_id(0)
    e = be_ref[b]
    active = nv_ref[b] > 0
    w_hbm = (wg_hbm, wu_hbm, wd_hbm)
    w_buf = (wgf_ref, wuf_ref, wdf_ref)

    @pl.when(b == 0)
    def _():
        slot_ref[0] = 0
        for c in _expert_weight_copies(e, 0, w_hbm, w_buf, sem):
            c.start()

    @pl.when(active & ((b == 0) | (be_ref[jnp.maximum(b - 1, 0)] != e)))
    def _():
        slot = slot_ref[0]
        for c in _expert_weight_copies(e, slot, w_hbm, w_buf, sem):
            c.wait()
        wgb_ref[...] = wgf_ref[slot].astype(BF16)
        wub_ref[...] = wuf_ref[slot].astype(BF16)
        wdb_ref[...] = wdf_ref[slot].astype(BF16)
        nxt = nx_ref[b]

        @pl.when(nxt != e)
        def _():
            for c in _expert_weight_copies(nxt, 1 - slot, w_hbm, w_buf, sem):
                c.start(priority=1)

        slot_ref[0] = 1 - slot

    @pl.when(active)
    def _():
        row = lax.broadcasted_iota(jnp.int32, (MOE_BLOCK, D_MODEL), 0)
        xb = jnp.where(row < nv_ref[b], _load_token_rows(xs_ref, MOE_BLOCK, BF16), 0.0)
        hg = jnp.dot(xb, wgb_ref[...], preferred_element_type=F32)
        hu = jnp.dot(xb, wub_ref[...], preferred_element_type=F32)
        hb = (_silu(hg) * hu).astype(BF16)
        _store_token_rows(ys_ref, jnp.dot(hb, wdb_ref[...], preferred_element_type=F32))

    @pl.when(jnp.logical_not(active))
    def _():
        ys_ref[...] = jnp.zeros_like(ys_ref)


def _experts(block_expert, n_valid, next_expert, xs, wg, wu, wd):
    nb = xs.shape[0] // (MOE_BLOCK * ROW_TILES)
    blk = pl.BlockSpec((MOE_BLOCK * ROW_TILES, LANES), lambda b, be, nv, nx: (b, 0))
    hbm = pl.BlockSpec(memory_space=pl.ANY)
    return pl.pallas_call(
        _experts_kernel,
        grid_spec=pltpu.PrefetchScalarGridSpec(
            num_scalar_prefetch=3,
            grid=(nb,),
            in_specs=[blk, hbm, hbm, hbm],
            out_specs=blk,
            scratch_shapes=[pltpu.VMEM((2, D_MODEL, D_EXPERT), F32), pltpu.VMEM((2, D_MODEL, D_EXPERT), F32),
                            pltpu.VMEM((2, D_EXPERT, D_MODEL), F32),
                            pltpu.VMEM((D_MODEL, D_EXPERT), BF16), pltpu.VMEM((D_MODEL, D_EXPERT), BF16),
                            pltpu.VMEM((D_EXPERT, D_MODEL), BF16),
                            pltpu.SMEM((1,), jnp.int32), pltpu.SemaphoreType.DMA((2, 3))]),
        out_shape=jax.ShapeDtypeStruct(xs.shape, xs.dtype),
        compiler_params=_cparams("arbitrary"),
        name="moe_experts",
    )(block_expert, n_valid, next_expert, xs, wg, wu, wd)


def _combine_kernel(rt_ref, p_ref, h_hbm, y_hbm, gp_ref, wpg_ref, wp_ref, gf_ref, o_ref,
                    h_ring, y0_ring, y1_ring, sem, *, sub):
    s = pl.program_id(0)
    n_steps = pl.num_programs(0)
    tm = o_ref.shape[0]
    n_sub = tm // sub

    def copies(step, slot):
        y_rows = tm * ROW_TILES
        return [pltpu.make_async_copy(h_hbm.at[pl.ds(step * tm, tm)], h_ring.at[slot], sem.at[slot, 0]),
                pltpu.make_async_copy(y_hbm.at[pl.ds(step * y_rows, y_rows)], y0_ring.at[slot], sem.at[slot, 1]),
                pltpu.make_async_copy(y_hbm.at[pl.ds((step + n_steps) * y_rows, y_rows)], y1_ring.at[slot],
                                      sem.at[slot, 2])]

    @pl.when(s == 0)
    def _():
        for ahead in range(COMBINE_SLOTS - 1):
            for c in copies(ahead, ahead):
                c.start()

    @pl.when(s + COMBINE_SLOTS - 1 < n_steps)
    def _():
        for c in copies(s + COMBINE_SLOTS - 1, lax.rem(s + COMBINE_SLOTS - 1, COMBINE_SLOTS)):
            c.start()

    slot = lax.rem(s, COMBINE_SLOTS)
    for c in copies(s, slot):
        c.wait()
    h_ref, y0_ref, y1_ref = h_ring.at[slot], y0_ring.at[slot], y1_ring.at[slot]

    def residual(r):
        rows = slice(r * sub, (r + 1) * sub)
        rt = rt_ref[rows, :]
        h = h_ref[rows, :] + (_load_token_rows(y0_ref, sub, F32, r * sub) * rt[:, 2:3]
                              + _load_token_rows(y1_ref, sub, F32, r * sub) * rt[:, 3:4])
        return h, _rms(h, gp_ref[...]).astype(BF16)

    def products(r, xn):
        return (jnp.dot(xn, wpg_ref[...], preferred_element_type=F32),
                _dot(p_ref[r * sub:(r + 1) * sub, :], wp_ref[...]))

    def finish(r, h, gate_lin, ple):
        h = h + ple * _sigmoid(gate_lin)
        o_ref[r * sub:(r + 1) * sub, :] = _rms(h, gf_ref[...])

    h, xn = residual(0)
    for r in range(n_sub):
        gate_lin, ple = products(r, xn)
        if r + 1 < n_sub:
            h_next, xn = residual(r + 1)
        finish(r, h, gate_lin, ple)
        if r + 1 < n_sub:
            h = h_next


def _combine(h1, rt, p2, y, gp, wpg, wp, gf, tm):
    t = h1.shape[0]
    nt = t // tm
    tile = lambda n: pl.BlockSpec((tm, n), lambda i: (i, 0))
    full = lambda a, b: pl.BlockSpec((a, b), lambda i: (0, 0))
    hbm = pl.BlockSpec(memory_space=pl.ANY)
    assert nt >= COMBINE_SLOTS - 1
    return pl.pallas_call(
        functools.partial(_combine_kernel, sub=256),
        grid=(nt,),
        in_specs=[tile(LANES), tile(D_PLE), hbm, hbm,
                  full(1, D_MODEL), full(D_MODEL, D_MODEL), full(D_PLE, D_MODEL), full(1, D_MODEL)],
        out_specs=tile(D_MODEL),
        out_shape=jax.ShapeDtypeStruct((t, D_MODEL), F32),
        scratch_shapes=[pltpu.VMEM((COMBINE_SLOTS, tm, D_MODEL), F32),
                        pltpu.VMEM((COMBINE_SLOTS, tm * ROW_TILES, LANES), y.dtype),
                        pltpu.VMEM((COMBINE_SLOTS, tm * ROW_TILES, LANES), y.dtype),
                        pltpu.SemaphoreType.DMA((COMBINE_SLOTS, 3))],
        compiler_params=_cparams("arbitrary"),
        name="moe_combine_ple",
    )(rt, p2, h1, y, gp, wpg, wp, gf)


def _lane_row(vals, offset):
    return jnp.zeros((1, LANES), F32).at[0, offset:offset + vals.shape[0]].set(vals)


def kernel(x, p, norm_mix, w_in, lru_conv_w, lru_conv_b, lru_wa, lru_ba, lru_wi, lru_bi, lru_lambda,
           lru_out_norm, gdn_conv_w, gdn_a_log, gdn_dt_bias, gdn_out_norm, w_out, norm_ffn,
           w_router_group, b_router_group, w_router_expert, b_router_expert, w_exp_gate, w_exp_up,
           w_exp_down, norm_ple, w_ple_gate, w_ple, norm_final):
    bsz, seq, d = x.shape
    t = bsz * seq
    depth = w_in.shape[0]
    assert depth == 1, "the final norm is fused into the last layer's combine kernel"
    n_blocks = -(-t * TOP_K // MOE_BLOCK) + N_EXPERTS
    row = lambda v: v.reshape(1, -1).astype(F32)
    h = x.reshape(t, d).astype(F32)
    for l in range(depth):
        b_gates = jnp.concatenate([lru_ba[l], lru_bi[l]]).reshape(1, -1)
        proj_gdn, y_lru = _proj_lru(h, row(norm_mix[l]), w_in[l], lru_conv_w[l], row(lru_conv_b[l]), lru_wa[l],
                                    lru_wi[l], b_gates, row(lru_lambda[l]), row(lru_out_norm[l]), seq, 512)
        cw = gdn_conv_w[l].reshape(CONV_W, 3, D_GDN).transpose(1, 0, 2)
        w_r = jnp.pad(jnp.concatenate([w_router_group[l], w_router_expert[l]], axis=1),
                      ((0, 0), (0, LANES - N_GROUPS - N_EXPERTS)))
        w_r_hi = w_r.astype(BF16)
        w_r_lo = (w_r - w_r_hi.astype(F32)).astype(BF16)
        w_r = jnp.concatenate([w_r_hi, w_r_hi, w_r_lo], axis=0)
        b_r = _lane_row(jnp.concatenate([b_router_group[l], b_router_expert[l]]), 0)
        h1, xn2, rt, tile_counts = _gdn_router(
            proj_gdn, cw, _lane_row(gdn_a_log[l], GDN_HEADS), _lane_row(gdn_dt_bias[l], GDN_HEADS),
            row(gdn_out_norm[l]), h, y_lru, w_out[l].astype(BF16), row(norm_ffn[l]), w_r, b_r, seq, 512)
        dest8, blocks = _slots(rt, tile_counts, n_blocks, 2048)
        dest = dest8[:TOP_K]
        block_expert, n_valid, next_expert = blocks[:, 0], blocks[:, 1], blocks[:, 2]
        tiles = lambda a: a.reshape(-1, ROW_TILES, LANES)
        xs = _sc_scatter_rows(tiles(xn2), dest, n_blocks * MOE_BLOCK)
        ys = _experts(block_expert, n_valid, next_expert, xs.reshape(-1, LANES), w_exp_gate[l], w_exp_up[l],
                      w_exp_down[l])
        y = _sc_gather_rows(tiles(ys), dest.reshape(-1))
        h = _combine(h1, rt, p[l].reshape(t, -1).astype(F32), y.reshape(-1, LANES), row(norm_ple[l]),
                     w_ple_gate[l].astype(BF16), w_ple[l].astype(BF16), row(norm_final), 1024)
    return h.reshape(bsz, seq, d).astype(x.dtype)
```

```python
import functools

import jax
import jax.numpy as jnp
from jax import lax
from jax.experimental import pallas as pl
from jax.experimental.pallas import tpu as pltpu
from jax.experimental.pallas import tpu_sc as plsc

D_MODEL = 1024
D_LRU = 512
LRU_BLOCKS = 8
LRU_BLOCK_W = D_LRU // LRU_BLOCKS
LRU_C = 8.0
D_GDN = 512
GDN_HEADS = 4
GDN_HEAD_DIM = D_GDN // GDN_HEADS
CONV_W = 4
CHUNK = 64
N_GROUPS = 4
EXPERTS_PER_GROUP = 8
N_EXPERTS = N_GROUPS * EXPERTS_PER_GROUP
TOP_K = 2
D_EXPERT = 512
MOE_BLOCK = 512
D_PLE = 256
EPS = 1e-6

LANES = 128
SUBLANES = 8
ROW_TILES = D_MODEL // (2 * LANES)
SC_CORES = 2
SC_SUBCORES = 16
SC_WORKERS = SC_CORES * SC_SUBCORES
SC_WINDOW = 32
SC_SLOTS = 4
COMBINE_SLOTS = 3
PROJ_COLS = 2 * D_LRU + 4 * D_GDN + LANES
VMEM_LIMIT = 56 * 1024 * 1024

BF16 = jnp.bfloat16
F32 = jnp.float32


def _cparams(*sem):
    return pltpu.CompilerParams(dimension_semantics=sem, vmem_limit_bytes=VMEM_LIMIT)


def _rms(x, g):
    return x * lax.rsqrt(jnp.mean(x * x, axis=-1, keepdims=True) + EPS) * g


def _sigmoid(x):
    return 0.5 * jnp.tanh(0.5 * x) + 0.5


def _silu(x):
    return x * _sigmoid(x)


def _dot(a, b):
    return jnp.dot(a.astype(BF16), b.astype(BF16), preferred_element_type=F32)


def _dot_nt(a, b):
    return lax.dot_general(a.astype(BF16), b.astype(BF16), (((1,), (1,)), ((), ())),
                           preferred_element_type=F32)


def _dot_tn(a, b):
    return lax.dot_general(a.astype(BF16), b.astype(BF16), (((0,), (0,)), ((), ())),
                           preferred_element_type=F32)


def _conv_scratch(rows, channels):
    return pltpu.VMEM((channels // LANES, 2 * (SUBLANES + rows), LANES), F32)


def _conv_reset(xp_ref):
    for k in range(xp_ref.shape[0]):
        xp_ref.at[k][pl.ds(0, SUBLANES, stride=2), :] = jnp.zeros((SUBLANES, LANES), F32)


def _causal_conv(xp_ref, x_ref, w, r0, rows, col0=0):
    out = []
    for k in range(xp_ref.shape[0]):
        lanes = slice(k * LANES, (k + 1) * LANES)
        slab = xp_ref.at[k]
        xk = x_ref[r0:r0 + rows, col0 + k * LANES:col0 + (k + 1) * LANES]
        slab[pl.ds(2 * (SUBLANES + r0), rows, stride=2), :] = xk
        acc = xk * w[CONV_W - 1:CONV_W, lanes]
        for j in range(1, CONV_W):
            acc = acc + slab[pl.ds(2 * (SUBLANES + r0 - j), rows, stride=2), :] * w[CONV_W - 1 - j:CONV_W - j, lanes]
        out.append(acc)
    return jnp.concatenate(out, axis=1)


def _conv_carry(xp_ref, x_ref, col0=0):
    ts = x_ref.shape[0]
    for k in range(xp_ref.shape[0]):
        xp_ref.at[k][pl.ds(0, SUBLANES, stride=2), :] = x_ref[ts - SUBLANES:, col0 + k * LANES:col0 + (k + 1) * LANES]


def _interleave(*stages):
    live = list(stages)
    while live:
        for g in list(live):
            try:
                next(g)
            except StopIteration:
                live.remove(g)


def _store_token_rows(ref, x, row0=0):
    rows = x.shape[0]
    bits = lax.bitcast_convert_type(x.astype(BF16).astype(F32), jnp.uint32)
    words = (bits[:, :D_MODEL // 2] >> 16) | bits[:, D_MODEL // 2:]
    for j in range(ROW_TILES):
        ref[pl.ds(row0 * ROW_TILES + j, rows, stride=ROW_TILES), :] = words[:, j * LANES:(j + 1) * LANES]


def _load_token_rows(ref, rows, dtype, row0=0):
    words = [ref[pl.ds(row0 * ROW_TILES + j, rows, stride=ROW_TILES), :] for j in range(ROW_TILES)]
    lo = [lax.bitcast_convert_type(w << 16, F32).astype(dtype) for w in words]
    hi = [lax.bitcast_convert_type(w & jnp.uint32(0xFFFF0000), F32).astype(dtype) for w in words]
    return jnp.concatenate(lo + hi, axis=1)


def _proj_lru_kernel(x_ref, gin_ref, win_ref, cw_ref, cb_ref, wa_ref, wi_ref, bg_ref, lam_ref, og_ref,
                     gdn_ref, o_ref, pj_ref, tail_ref, h_ref, wg_ref, wb_ref, *, tiles_per_seq, col_tile):
    g = pl.program_id(0)
    ts = x_ref.shape[0]
    pj_new = pj_ref.at[g % 2]
    pj = pj_ref.at[(g + 1) % 2]

    @pl.when(g == 0)
    def _():
        pj_ref[...] = jnp.zeros_like(pj_ref)
        in_cols = win_ref.shape[1]
        whole = in_cols // LANES * LANES
        for c0 in range(0, whole, col_tile):
            wb_ref[:, c0:c0 + col_tile] = win_ref[:, c0:c0 + col_tile].astype(BF16)
        wb_ref[:, whole:] = jnp.zeros((D_MODEL, PROJ_COLS - whole), BF16)
        wb_ref[:, whole:in_cols] = win_ref[:, whole:].astype(BF16)
        wg_ref[...] = jnp.zeros_like(wg_ref)
        for n in range(LRU_BLOCKS):
            blk = slice(n * LRU_BLOCK_W, (n + 1) * LRU_BLOCK_W)
            wg_ref[blk, blk] = wa_ref[n].astype(BF16)
            wg_ref[blk, D_LRU + n * LRU_BLOCK_W:D_LRU + (n + 1) * LRU_BLOCK_W] = wi_ref[n].astype(BF16)

    @pl.when((g == 0) | (lax.rem(g - 1, tiles_per_seq) == 0))
    def _():
        _conv_reset(tail_ref)
        h_ref[...] = jnp.zeros_like(h_ref)

    done = []

    def after_projection(v):
        return v + jnp.concatenate([done[max(len(done) - 2, 0)]] * (v.shape[-1] // LANES), axis=-1)

    def project():
        xn = _rms(x_ref[...], gin_ref[...]).astype(BF16)
        yield
        for c0 in range(0, PROJ_COLS, col_tile):
            c1 = min(c0 + col_tile, PROJ_COLS)
            y = jnp.dot(xn, wb_ref[:, c0:c1], preferred_element_type=F32)
            if c1 <= 2 * D_LRU:
                pj_new[:, c0:c1] = y
            else:
                gdn_ref[:, c0 - 2 * D_LRU:c1 - 2 * D_LRU] = y
            bits = lax.bitcast_convert_type(y[ts - 1:, c1 - c0 - LANES:], jnp.uint32)
            done.append(lax.bitcast_convert_type((bits >> 16) >> 16, F32))
            yield

    def recur():
        xc = _causal_conv(tail_ref, pj, cw_ref[...], 0, ts) + cb_ref[...]
        _conv_carry(tail_ref, pj)
        yield
        gates = _sigmoid(_dot(xc, wg_ref[...]) + after_projection(bg_ref[...]))
        r = gates[:, :D_LRU]
        i = gates[:, D_LRU:]
        yield
        log_a = LRU_C * r * jax.nn.log_sigmoid(after_projection(lam_ref[...]))
        a = jnp.exp(log_a)
        th = jnp.tanh(log_a)
        u = jnp.sqrt(-2.0 * th) * lax.rsqrt(1.0 - th) * (i * xc)
        yield
        a = a.reshape(ts // SUBLANES, SUBLANES, D_LRU)
        u = u.reshape(ts // SUBLANES, SUBLANES, D_LRU)
        row = lax.broadcasted_iota(jnp.int32, a.shape, 1)
        d = 1
        while d < SUBLANES:
            keep = row >= d
            a_prev = jnp.where(keep, pltpu.roll(a, d, axis=1), 1.0)
            u_prev = jnp.where(keep, pltpu.roll(u, d, axis=1), 0.0)
            u = a * u_prev + u
            a = a * a_prev
            d *= 2
            yield
        carry = after_projection(h_ref[...])
        groups = []
        for n in range(ts // SUBLANES):
            groups.append(a[n] * carry + u[n])
            carry = groups[-1][SUBLANES - 1:]
        h = jnp.concatenate(groups, axis=0)
        h_ref[...] = carry
        yield
        y = h * jax.nn.gelu(pj[:, D_LRU:])
        o_ref[...] = _rms(y, after_projection(og_ref[...])).astype(o_ref.dtype)

    _interleave(project(), recur())


def _proj_lru(x2, gin, win, cw, cb, wa, wi, bg, lam, og, seq, ts):
    t = x2.shape[0]
    n_tiles = t // ts
    this = lambda g: (jnp.minimum(g, n_tiles - 1), 0)
    last = lambda g: (jnp.maximum(g - 1, 0), 0)
    const = lambda *shape: pl.BlockSpec(shape, lambda g: (0,) * len(shape))
    gdn_cols = PROJ_COLS - 2 * D_LRU
    return pl.pallas_call(
        functools.partial(_proj_lru_kernel, tiles_per_seq=seq // ts, col_tile=D_LRU),
        grid=(n_tiles + 1,),
        in_specs=[pl.BlockSpec((ts, D_MODEL), this), const(1, D_MODEL),
                  pl.BlockSpec(win.shape, lambda g: (0, 0), pipeline_mode=pl.Buffered(1)),
                  const(CONV_W, D_LRU), const(1, D_LRU),
                  const(LRU_BLOCKS, LRU_BLOCK_W, LRU_BLOCK_W), const(LRU_BLOCKS, LRU_BLOCK_W, LRU_BLOCK_W),
                  const(1, 2 * D_LRU), const(1, D_LRU), const(1, D_LRU)],
        out_specs=[pl.BlockSpec((ts, gdn_cols), this), pl.BlockSpec((ts, D_LRU), last)],
        out_shape=[jax.ShapeDtypeStruct((t, gdn_cols), F32), jax.ShapeDtypeStruct((t, D_LRU), BF16)],
        scratch_shapes=[pltpu.VMEM((2, ts, 2 * D_LRU), F32), _conv_scratch(ts, D_LRU), pltpu.VMEM((1, D_LRU), F32),
                        pltpu.VMEM((D_LRU, 2 * D_LRU), BF16), pltpu.VMEM((D_MODEL, PROJ_COLS), BF16)],
        compiler_params=_cparams("arbitrary"),
        name="in_proj_rglru",
    )(x2, gin, win, cw, cb, wa, wi, bg, lam, og)


def _gdn_router_kernel(pj_ref, cw_ref, alog_ref, dtb_ref, og_ref,
                       x_ref, yl_ref, wo_ref, gf_ref, wr_ref, br_ref, h_ref, xn_ref, rt_ref, cnt_ref,
                       qt_ref, kt_ref, vt_ref, qs_ref, ks_ref, vs_ref, bs_ref, gc_ref, gct_ref, st_ref, yg_ref,
                       *, tiles_per_seq, group_chunks, prep_rows, sub):
    g = pl.program_id(0)
    ts = pj_ref.shape[0]
    dk = GDN_HEAD_DIM
    nc = ts // CHUNK
    n_sub = ts // sub
    q0, k0, v0, z0, ba0 = 0, D_GDN, 2 * D_GDN, 3 * D_GDN, 4 * D_GDN
    yg_new = yg_ref.at[g % 2]
    yg = yg_ref.at[(g + 1) % 2]

    @pl.when(g == 0)
    def _():
        yg_ref[...] = jnp.zeros_like(yg_ref)

    @pl.when(lax.rem(g, tiles_per_seq) == 0)
    def _():
        for tail_ref in (qt_ref, kt_ref, vt_ref):
            _conv_reset(tail_ref)
        st_ref[...] = jnp.zeros_like(st_ref)

    done = []
    hs = {}
    counts = []

    def after_projection(v):
        if not done:
            return v
        return v + jnp.concatenate([done[-1]] * (v.shape[-1] // LANES), axis=-1)

    def project_out():
        for r in range(n_sub):
            rows = slice(r * sub, (r + 1) * sub)
            h = x_ref[rows, :] + jnp.dot(yl_ref[rows, :], wo_ref[:D_LRU, :], preferred_element_type=F32) \
                + jnp.dot(yg[rows, :], wo_ref[D_LRU:, :], preferred_element_type=F32)
            h_ref[rows, :] = h
            hs[r] = h
            bits = lax.bitcast_convert_type(h[sub - 1:, D_MODEL - LANES:], jnp.uint32)
            done.append(lax.bitcast_convert_type((bits >> 16) >> 16, F32))
            yield

    def route(r, h):
        xn = _rms(h, gf_ref[...])
        _store_token_rows(xn_ref, xn, r * sub)
        xh = xn.astype(BF16)
        xl = (xn - xh.astype(F32)).astype(BF16)
        logits = jnp.dot(jnp.concatenate([xh, xl, xh], axis=1), wr_ref[...],
                         preferred_element_type=F32) + br_ref[...]
        lane = lax.broadcasted_iota(jnp.int32, logits.shape, 1).astype(F32)
        big = jnp.float32(2 * LANES)
        ninf = jnp.float32(-jnp.inf)

        def top1(vals):
            m = jnp.max(vals, axis=-1, keepdims=True)
            return m, jnp.min(jnp.where(vals == m, lane, big), axis=-1, keepdims=True)

        gl = jnp.where(lane < N_GROUPS, logits, ninf)
        gmax, gsel = top1(gl)
        p_group = 1.0 / jnp.sum(jnp.exp(gl - gmax), axis=-1, keepdims=True)
        lo = N_GROUPS + EXPERTS_PER_GROUP * gsel
        el = jnp.where((lane >= lo) & (lane < lo + EXPERTS_PER_GROUP), logits, ninf)
        m1, i1 = top1(el)
        m2, i2 = top1(jnp.where(lane == i1, ninf, el))
        rr = jnp.exp(m2 - m1)
        g1 = p_group / (1.0 + rr)
        g2 = p_group * rr / (1.0 + rr)
        rt_ref[r * sub:(r + 1) * sub, :] = jnp.where(
            lane == 0, i1 - N_GROUPS,
            jnp.where(lane == 1, i2 - N_GROUPS, jnp.where(lane == 2, g1, jnp.where(lane == 3, g2, 0.0))))
        member = ((lane == i1 - N_GROUPS) | (lane == i2 - N_GROUPS)).astype(F32)
        counts.append(jnp.sum(member, axis=0, keepdims=True))

    def route_subtiles(subtiles):
        for r in subtiles:
            route(r, hs[r])
            yield

    ri = lax.broadcasted_iota(jnp.int32, (CHUNK, CHUNK), 0)
    ci = lax.broadcasted_iota(jnp.int32, (CHUNK, CHUNK), 1)
    causal = ri >= ci
    strict = ri > ci
    tril = causal.astype(F32)
    eye = (ri == ci).astype(F32)
    og = og_ref[...]

    def l2n(x, scale):
        parts = []
        for h in range(GDN_HEADS):
            xh = x[:, h * dk:(h + 1) * dk]
            parts.append(xh * (lax.rsqrt(jnp.sum(xh * xh, axis=-1, keepdims=True) + EPS) * scale))
        return jnp.concatenate(parts, axis=1)

    def prepare(r0, r1):
        for p0 in range(r0, r1, prep_rows):
            rows = slice(p0, p0 + prep_rows)
            conv = lambda tail_ref, part, col0: _silu(
                _causal_conv(tail_ref, pj_ref, after_projection(cw_ref[part]), p0, prep_rows, col0))
            qs_ref[rows, :] = l2n(conv(qt_ref, 0, q0), dk ** -0.5)
            yield
            ks_ref[rows, :] = l2n(conv(kt_ref, 1, k0), 1.0)
            yield
            vs_ref[rows, :] = conv(vt_ref, 2, v0)
            ba = pj_ref[rows, ba0:ba0 + LANES]
            bs_ref[rows, :] = _sigmoid(ba)
            g = -jnp.exp(alog_ref[...]) * jax.nn.softplus(ba + dtb_ref[...])
            for c0 in range(0, prep_rows, CHUNK):
                gc_ref[p0 + c0:p0 + c0 + CHUNK, :] = jnp.dot(tril, g[c0:c0 + CHUNK], precision=lax.Precision.HIGHEST,
                                                             preferred_element_type=F32)
            gct_ref[:, rows] = gc_ref[rows, :].T
            yield

    terms = {}

    def chunk_terms(pairs):
        n = range(len(pairs))
        rows = [slice(c * CHUNK, (c + 1) * CHUNK) for c, _ in pairs]
        cols = [slice(h * dk, (h + 1) * dk) for _, h in pairs]
        gl = [GDN_HEADS + h for _, h in pairs]
        kh = [ks_ref[rows[i], cols[i]] for i in n]
        kb = [kh[i] * bs_ref[rows[i], pairs[i][1]:pairs[i][1] + 1] for i in n]
        r = [_dot_nt(jnp.concatenate([kb[i], qs_ref[rows[i], cols[i]]], axis=0), kh[i]) for i in n]
        yield
        gcol = [gc_ref[rows[i], gl[i]:gl[i] + 1] for i in n]
        decay = []
        for i in n:
            diff = gcol[i] - gct_ref[gl[i]:gl[i] + 1, rows[i]]
            decay.append(jnp.where(causal, jnp.exp(jnp.where(causal, diff, 0.0)), 0.0))
        a = [jnp.where(strict, r[i][:CHUNK] * decay[i], 0.0) for i in n]
        qk = [(r[i][CHUNK:] * decay[i]).astype(BF16) for i in n]
        tinv = [eye - a[i] for i in n]
        p = 2
        while p < CHUNK:
            a = [_dot(a[i], a[i]) for i in n]
            yield
            tinv = [tinv[i] + _dot(tinv[i], a[i]) for i in n]
            yield
            p *= 2
        eg = [jnp.exp(gcol[i]) for i in n]
        rhs = [jnp.concatenate([vs_ref[rows[i], cols[i]] * bs_ref[rows[i], pairs[i][1]:pairs[i][1] + 1],
                                kb[i] * eg[i]], axis=1) for i in n]
        uw = [_dot(tinv[i], rhs[i]).astype(BF16) for i in n]
        yield
        qk_uw = [jnp.dot(qk[i], uw[i], preferred_element_type=F32) for i in n]
        glast = [gc_ref[(c + 1) * CHUNK - 1:(c + 1) * CHUNK, gl[i]:gl[i] + 1] for i, (c, _) in enumerate(pairs)]
        kd_uw = [_dot_tn(kh[i] * jnp.exp(glast[i] - gcol[i]), uw[i]) for i in n]
        yield
        for i in n:
            lhs = jnp.concatenate([kd_uw[i][:, dk:], qs_ref[rows[i], cols[i]] * eg[i] - qk_uw[i][:, dk:]],
                                  axis=0).astype(BF16)
            terms[pairs[i]] = (lhs, kd_uw[i][:, :dk], qk_uw[i][:, :dk], jnp.exp(glast[i]))

    state = [st_ref[h] for h in range(GDN_HEADS)]

    def advance(chunks):
        for c in chunks:
            rows = slice(c * CHUNK, (c + 1) * CHUNK)
            r = [jnp.dot(terms[c, h][0], state[h].astype(BF16), preferred_element_type=F32)
                 for h in range(GDN_HEADS)]
            for h in range(GDN_HEADS):
                cols = slice(h * dk, (h + 1) * dk)
                _, c_add, o_add, egl = terms[c, h]
                o = r[h][dk:] + o_add
                state[h] = egl * state[h] - r[h][:dk] + c_add
                zh = pj_ref[rows, z0 + h * dk:z0 + (h + 1) * dk]
                yg_new[rows, cols] = (_rms(o, og) * _silu(zh)).astype(yg_new.dtype)
            yield

    groups = [range(c0, c0 + group_chunks) for c0 in range(0, nc, group_chunks)]
    pairs_of = lambda chunks: [(c, h) for c in chunks for h in range(GDN_HEADS)]
    span = lambda chunks: (chunks[0] * CHUNK, (chunks[-1] + 1) * CHUNK)
    per_phase = max(1, n_sub // (len(groups) + 1))
    _interleave(prepare(*span(groups[0])), project_out())
    for i, chunks in enumerate(groups):
        side = [route_subtiles(range(i * per_phase, (i + 1) * per_phase))]
        if i + 1 < len(groups):
            side.append(prepare(*span(groups[i + 1])))
        if i > 0:
            side.append(advance(groups[i - 1]))
        _interleave(chunk_terms(pairs_of(chunks)), *side)
    _interleave(advance(groups[-1]), route_subtiles(range(len(groups) * per_phase, n_sub)))
    for h in range(GDN_HEADS):
        st_ref[h] = state[h]
    for tail_ref, col0 in ((qt_ref, q0), (kt_ref, k0), (vt_ref, v0)):
        _conv_carry(tail_ref, pj_ref, col0)
    cnt_ref[...] = jnp.broadcast_to(sum(counts), cnt_ref.shape)


def _gdn_router(proj, cw, alog, dtb, og, x2, yl, wo, gf, wr, br, seq, ts):
    t = x2.shape[0]
    n_tiles = t // ts
    this = lambda g: (jnp.minimum(g, n_tiles - 1), 0)
    last = lambda g: (jnp.maximum(g - 1, 0), 0)
    const = lambda *shape: pl.BlockSpec(shape, lambda g: (0,) * len(shape))
    return pl.pallas_call(
        functools.partial(_gdn_router_kernel, tiles_per_seq=seq // ts, group_chunks=4, prep_rows=128, sub=256),
        grid=(n_tiles + 1,),
        in_specs=[pl.BlockSpec((ts, proj.shape[1]), this),
                  const(3, CONV_W, D_GDN), const(1, LANES), const(1, LANES), const(1, GDN_HEAD_DIM),
                  pl.BlockSpec((ts, D_MODEL), last), pl.BlockSpec((ts, D_LRU), last), const(D_MODEL, D_MODEL),
                  const(1, D_MODEL), const(3 * D_MODEL, LANES), const(1, LANES)],
        out_specs=[pl.BlockSpec((ts, D_MODEL), last), pl.BlockSpec((ts * ROW_TILES, LANES), last),
                   pl.BlockSpec((ts, LANES), last), pl.BlockSpec((SUBLANES, LANES), last)],
        out_shape=[jax.ShapeDtypeStruct((t, D_MODEL), F32),
                   jax.ShapeDtypeStruct((t * ROW_TILES, LANES), jnp.uint32),
                   jax.ShapeDtypeStruct((t, LANES), F32),
                   jax.ShapeDtypeStruct((n_tiles * SUBLANES, LANES), F32)],
        scratch_shapes=[_conv_scratch(ts, D_GDN)] * 3
        + [pltpu.VMEM((ts, D_GDN), F32)] * 3
        + [pltpu.VMEM((ts, LANES), F32)] * 2
        + [pltpu.VMEM((LANES, ts), F32)]
        + [pltpu.VMEM((GDN_HEADS, GDN_HEAD_DIM, GDN_HEAD_DIM), F32)]
        + [pltpu.VMEM((2, ts, D_GDN), BF16)],
        compiler_params=_cparams("arbitrary"),
        name="gdn_out_router",
    )(proj, cw, alog, dtb, og, x2, yl, wo, gf, wr, br)


def _slots_kernel(rt_ref, tc_ref, dest_ref, blocks_ref, cnt_ref, before_ref):
    i = pl.program_id(0)
    tm = rt_ref.shape[0]
    rt = rt_ref[...]
    lane = lax.broadcasted_iota(jnp.int32, rt.shape, 1)
    e0 = rt[:, 0:1].astype(jnp.int32)
    e1 = rt[:, 1:2].astype(jnp.int32)
    member = ((lane == e0) | (lane == e1)).astype(F32)

    @pl.when(i == 0)
    def _():
        ri = lax.broadcasted_iota(jnp.int32, before_ref.shape, 0)
        ci = lax.broadcasted_iota(jnp.int32, before_ref.shape, 1)
        before_ref[...] = (ri > ci).astype(BF16)
        cnt = jnp.broadcast_to(jnp.sum(tc_ref[...], axis=0, keepdims=True) / SUBLANES, cnt_ref.shape)
        padded = jnp.ceil(cnt / MOE_BLOCK) * MOE_BLOCK
        l8 = lax.broadcasted_iota(jnp.int32, cnt.shape, 1)
        incl = padded
        d = 1
        while d < LANES:
            incl = incl + jnp.where(l8 >= d, pltpu.roll(incl, d, axis=1), 0.0)
            d *= 2
        cnt_ref[...] = incl - padded
        seg_start = (incl - padded)[0:1, :]
        seg_end = incl[0:1, :]
        real_end = seg_start + cnt[0:1, :]
        bl = lax.broadcasted_iota(jnp.int32, blocks_ref.shape, 1)
        b0 = (lax.broadcasted_iota(jnp.int32, blocks_ref.shape, 0) * MOE_BLOCK).astype(F32)
        is_expert = bl < N_EXPERTS
        expert = jnp.sum(jnp.where(is_expert & (seg_end <= b0), 1.0, 0.0), axis=-1, keepdims=True)
        real = jnp.maximum(jnp.minimum(real_end, b0 + MOE_BLOCK) - jnp.maximum(seg_start, b0), 0.0)
        n_real = jnp.sum(jnp.where(is_expert, real, 0.0), axis=-1, keepdims=True)
        expert = jnp.minimum(expert, N_EXPERTS - 1.0)
        later = is_expert & (cnt[0:1, :] > 0.0) & (bl.astype(F32) > expert)
        nxt = jnp.min(jnp.where(later, bl.astype(F32), float(N_EXPERTS)), axis=-1, keepdims=True)
        nxt = jnp.where(nxt < N_EXPERTS, nxt, expert)
        blocks_ref[...] = jnp.where(bl == 0, expert, jnp.where(bl == 1, n_real, jnp.where(bl == 2, nxt, 0.0))
                                    ).astype(jnp.int32)

    sub = before_ref.shape[0]
    run = cnt_ref[0:1, :]
    pos = []
    for j in range(tm // sub):
        mj = member[j * sub:(j + 1) * sub]
        pos.append(run + jnp.dot(before_ref[...], mj.astype(BF16), preferred_element_type=F32))
        run = run + jnp.sum(mj, axis=0, keepdims=True)
    pos = jnp.concatenate(pos, axis=0)
    d0 = jnp.sum(jnp.where(lane == e0, pos, 0.0), axis=-1, keepdims=True)
    d1 = jnp.sum(jnp.where(lane == e1, pos, 0.0), axis=-1, keepdims=True)
    dest = jnp.where(lane == 0, d0, jnp.where(lane == 1, d1, 0.0))
    dest_ref[...] = dest.T[:SUBLANES].astype(jnp.int32)
    cnt_ref[...] = jnp.broadcast_to(run, cnt_ref.shape)


def _slots(rt, tile_counts, n_blocks, tm):
    t = rt.shape[0]
    return pl.pallas_call(
        _slots_kernel,
        grid=(t // tm,),
        in_specs=[pl.BlockSpec((tm, LANES), lambda i: (i, 0)),
                  pl.BlockSpec(tile_counts.shape, lambda i: (0, 0))],
        out_specs=[pl.BlockSpec((SUBLANES, tm), lambda i: (0, i)),
                   pl.BlockSpec((n_blocks, LANES), lambda i: (0, 0))],
        out_shape=[jax.ShapeDtypeStruct((SUBLANES, t), jnp.int32),
                   jax.ShapeDtypeStruct((n_blocks, LANES), jnp.int32)],
        scratch_shapes=[pltpu.VMEM((SUBLANES, LANES), F32), pltpu.VMEM((LANES, LANES), BF16)],
        compiler_params=_cparams("arbitrary"),
        name="moe_slots",
    )(rt, tile_counts)


def _sc_mesh():
    return plsc.VectorSubcoreMesh(core_axis_name="c", subcore_axis_name="s")


def _sc_worker():
    return lax.axis_index("s") * SC_CORES + lax.axis_index("c")


def _sc_scatter_rows(src, idx, n_rows):
    t = src.shape[0]
    per_w = t // SC_WORKERS
    n_win = per_w // SC_WINDOW
    assert per_w % SC_WINDOW == 0 and n_win % SC_SLOTS == 0, "every worker walks whole rings of windows"
    idx = idx.reshape(TOP_K, SC_WORKERS, n_win, SC_WINDOW)

    @functools.partial(
        pl.kernel, mesh=_sc_mesh(),
        out_type=jax.ShapeDtypeStruct((n_rows,) + src.shape[1:], src.dtype),
        scratch_types=[pltpu.VMEM((TOP_K, n_win, SC_WINDOW), jnp.int32),
                       pltpu.VMEM((SC_SLOTS, SC_WINDOW) + src.shape[1:], src.dtype),
                       pltpu.SemaphoreType.DMA((SC_SLOTS,)), pltpu.SemaphoreType.DMA((SC_SLOTS,))],
        compiler_params=pltpu.CompilerParams(use_tc_tiling_on_sc=True),
        name="sc_dispatch")
    def scatter(src_hbm, idx_hbm, out_hbm, idx_v, rows_v, lsem, ssem):
        wid = _sc_worker()
        base = wid * per_w
        for k in range(TOP_K):
            pltpu.sync_copy(idx_hbm.at[k, wid], idx_v.at[k])

        def load(w, slot):
            return pltpu.make_async_copy(src_hbm.at[pl.ds(base + w * SC_WINDOW, SC_WINDOW)], rows_v.at[slot],
                                         lsem.at[slot])

        def put(w, slot, k):
            return pltpu.make_async_copy(rows_v.at[slot], out_hbm.at[idx_v.at[k, w]], ssem.at[slot])

        for s in range(SC_SLOTS - 1):
            load(s, s).start()

        @pl.loop(0, n_win, step=SC_SLOTS)
        def _(w0):
            for s in range(SC_SLOTS):
                w = w0 + s
                prev = (s - 1) % SC_SLOTS

                @pl.when(w + SC_SLOTS - 1 < n_win)
                def _():
                    @pl.when(w >= 1)
                    def _():
                        for k in range(TOP_K):
                            put(w - 1, prev, k).wait()

                    load(w + SC_SLOTS - 1, prev).start()

                load(w, s).wait()
                for k in range(TOP_K):
                    put(w, s, k).start()

        for s in range(SC_SLOTS):
            for k in range(TOP_K):
                put(n_win - SC_SLOTS + s, s, k).wait()

    return scatter(src, idx)


def _sc_gather_rows(table, idx):
    b = idx.shape[0]
    per_w = b // SC_WORKERS
    n_win = per_w // SC_WINDOW
    assert per_w % SC_WINDOW == 0 and n_win % SC_SLOTS == 0, "every worker walks whole rings of windows"
    idx = idx.reshape(SC_WORKERS, n_win, SC_WINDOW)

    @functools.partial(
        pl.kernel, mesh=_sc_mesh(),
        out_type=jax.ShapeDtypeStruct((b,) + table.shape[1:], table.dtype),
        scratch_types=[pltpu.VMEM((n_win, SC_WINDOW), jnp.int32),
                       pltpu.VMEM((SC_SLOTS, SC_WINDOW) + table.shape[1:], table.dtype),
                       pltpu.SemaphoreType.DMA((SC_SLOTS,)), pltpu.SemaphoreType.DMA((SC_SLOTS,))],
        compiler_params=pltpu.CompilerParams(use_tc_tiling_on_sc=True),
        name="sc_combine_gather")
    def gather(table_hbm, idx_hbm, out_hbm, idx_v, rows_v, gsem, psem):
        wid = _sc_worker()
        base = wid * per_w
        pltpu.sync_copy(idx_hbm.at[wid], idx_v)

        def get(w, slot):
            return pltpu.make_async_copy(table_hbm.at[idx_v.at[w]], rows_v.at[slot], gsem.at[slot])

        def put(w, slot):
            return pltpu.make_async_copy(rows_v.at[slot], out_hbm.at[pl.ds(base + w * SC_WINDOW, SC_WINDOW)],
                                         psem.at[slot])

        for s in range(SC_SLOTS - 1):
            get(s, s).start()

        @pl.loop(0, n_win, step=SC_SLOTS)
        def _(w0):
            for s in range(SC_SLOTS):
                w = w0 + s
                prev = (s - 1) % SC_SLOTS

                @pl.when(w + SC_SLOTS - 1 < n_win)
                def _():
                    @pl.when(w >= 1)
                    def _():
                        put(w - 1, prev).wait()

                    get(w + SC_SLOTS - 1, prev).start()

                get(w, s).wait()
                put(w, s).start()

        for s in range(SC_SLOTS):
            put(n_win - SC_SLOTS + s, s).wait()

    return gather(table, idx)


def _expert_weight_copies(e, slot, w_hbm, w_buf, sem):
    return [pltpu.make_async_copy(w.at[e], buf.at[slot], sem.at[slot, i]) for i, (w, buf) in enumerate(zip(w_hbm, w_buf))]


def _experts_kernel(be_ref, nv_ref, nx_ref, xs_ref, wg_hbm, wu_hbm, wd_hbm, ys_ref,
                    wgf_ref, wuf_ref, wdf_ref, wgb_ref, wub_ref, wdb_ref, slot_ref, sem):
    b = pl.program_id(0)
    e = be_ref[b]
    active = nv_ref[b] > 0
    w_hbm = (wg_hbm, wu_hbm, wd_hbm)
    w_buf = (wgf_ref, wuf_ref, wdf_ref)

    @pl.when(b == 0)
    def _():
        slot_ref[0] = 0
        for c in _expert_weight_copies(e, 0, w_hbm, w_buf, sem):
            c.start()

    first = (b == 0) | (be_ref[jnp.maximum(b - 1, 0)] != e)

    def block(slot=None):
        row = lax.broadcasted_iota(jnp.int32, (MOE_BLOCK, D_MODEL), 0)
        xb = jnp.where(row < nv_ref[b], _load_token_rows(xs_ref, MOE_BLOCK, BF16), 0.0)
        if slot is not None:
            wgb_ref[...] = wgf_ref[slot].astype(BF16)
        hg = jnp.dot(xb, wgb_ref[...], preferred_element_type=F32)
        if slot is not None:
            wub_ref[...] = wuf_ref[slot].astype(BF16)
        hu = jnp.dot(xb, wub_ref[...], preferred_element_type=F32)
        if slot is not None:
            wdb_ref[...] = wdf_ref[slot].astype(BF16)
        hb = (_silu(hg) * hu).astype(BF16)
        _store_token_rows(ys_ref, jnp.dot(hb, wdb_ref[...], preferred_element_type=F32))

    @pl.when(active & first)
    def _():
        slot = slot_ref[0]
        for c in _expert_weight_copies(e, slot, w_hbm, w_buf, sem):
            c.wait()
        nxt = nx_ref[b]

        @pl.when(nxt != e)
        def _():
            for c in _expert_weight_copies(nxt, 1 - slot, w_hbm, w_buf, sem):
                c.start()

        slot_ref[0] = 1 - slot
        block(slot)

    @pl.when(active & jnp.logical_not(first))
    def _():
        block()

    @pl.when(jnp.logical_not(active))
    def _():
        ys_ref[...] = jnp.zeros_like(ys_ref)


def _experts(block_expert, n_valid, next_expert, xs, wg, wu, wd):
    nb = xs.shape[0] // (MOE_BLOCK * ROW_TILES)
    blk = pl.BlockSpec((MOE_BLOCK * ROW_TILES, LANES), lambda b, be, nv, nx: (b, 0))
    hbm = pl.BlockSpec(memory_space=pl.ANY)
    return pl.pallas_call(
        _experts_kernel,
        grid_spec=pltpu.PrefetchScalarGridSpec(
            num_scalar_prefetch=3,
            grid=(nb,),
            in_specs=[blk, hbm, hbm, hbm],
            out_specs=blk,
            scratch_shapes=[pltpu.VMEM((2, D_MODEL, D_EXPERT), F32), pltpu.VMEM((2, D_MODEL, D_EXPERT), F32),
                            pltpu.VMEM((2, D_EXPERT, D_MODEL), F32),
                            pltpu.VMEM((D_MODEL, D_EXPERT), BF16), pltpu.VMEM((D_MODEL, D_EXPERT), BF16),
                            pltpu.VMEM((D_EXPERT, D_MODEL), BF16),
                            pltpu.SMEM((1,), jnp.int32), pltpu.SemaphoreType.DMA((2, 3))]),
        out_shape=jax.ShapeDtypeStruct(xs.shape, xs.dtype),
        compiler_params=_cparams("arbitrary"),
        name="moe_experts",
    )(block_expert, n_valid, next_expert, xs, wg, wu, wd)


def _combine_kernel(rt_ref, p_ref, h_hbm, y_hbm, gp_ref, wpg_ref, wp_ref, gf_ref, o_ref,
                    h_ring, y0_ring, y1_ring, sem, *, sub):
    s = pl.program_id(0)
    n_steps = pl.num_programs(0)
    tm = o_ref.shape[0]
    n_sub = tm // sub

    def copies(step, slot):
        y_rows = tm * ROW_TILES
        return [pltpu.make_async_copy(h_hbm.at[pl.ds(step * tm, tm)], h_ring.at[slot], sem.at[slot, 0]),
                pltpu.make_async_copy(y_hbm.at[pl.ds(step * y_rows, y_rows)], y0_ring.at[slot], sem.at[slot, 1]),
                pltpu.make_async_copy(y_hbm.at[pl.ds((step + n_steps) * y_rows, y_rows)], y1_ring.at[slot],
                                      sem.at[slot, 2])]

    @pl.when(s == 0)
    def _():
        for ahead in range(COMBINE_SLOTS - 1):
            for c in copies(ahead, ahead):
                c.start()

    @pl.when(s + COMBINE_SLOTS - 1 < n_steps)
    def _():
        for c in copies(s + COMBINE_SLOTS - 1, lax.rem(s + COMBINE_SLOTS - 1, COMBINE_SLOTS)):
            c.start()

    slot = lax.rem(s, COMBINE_SLOTS)
    for c in copies(s, slot):
        c.wait()
    h_ref, y0_ref, y1_ref = h_ring.at[slot], y0_ring.at[slot], y1_ring.at[slot]

    def residual(r):
        rows = slice(r * sub, (r + 1) * sub)
        rt = rt_ref[rows, :]
        h = h_ref[rows, :] + (_load_token_rows(y0_ref, sub, F32, r * sub) * rt[:, 2:3]
                              + _load_token_rows(y1_ref, sub, F32, r * sub) * rt[:, 3:4])
        return h, _rms(h, gp_ref[...]).astype(BF16)

    def products(r, xn):
        return (jnp.dot(xn, wpg_ref[...], preferred_element_type=F32),
                _dot(p_ref[r * sub:(r + 1) * sub, :], wp_ref[...]))

    def finish(r, h, gate_lin, ple):
        h = h + ple * _sigmoid(gate_lin)
        o_ref[r * sub:(r + 1) * sub, :] = _rms(h, gf_ref[...])

    h, xn = residual(0)
    for r in range(n_sub):
        gate_lin, ple = products(r, xn)
        if r + 1 < n_sub:
            h_next, xn = residual(r + 1)
        finish(r, h, gate_lin, ple)
        if r + 1 < n_sub:
            h = h_next


def _combine(h1, rt, p2, y, gp, wpg, wp, gf, tm):
    t = h1.shape[0]
    nt = t // tm
    tile = lambda n: pl.BlockSpec((tm, n), lambda i: (i, 0))
    full = lambda a, b: pl.BlockSpec((a, b), lambda i: (0, 0))
    hbm = pl.BlockSpec(memory_space=pl.ANY)
    assert nt >= COMBINE_SLOTS - 1
    return pl.pallas_call(
        functools.partial(_combine_kernel, sub=256),
        grid=(nt,),
        in_specs=[tile(LANES), tile(D_PLE), hbm, hbm,
                  full(1, D_MODEL), full(D_MODEL, D_MODEL), full(D_PLE, D_MODEL), full(1, D_MODEL)],
        out_specs=tile(D_MODEL),
        out_shape=jax.ShapeDtypeStruct((t, D_MODEL), F32),
        scratch_shapes=[pltpu.VMEM((COMBINE_SLOTS, tm, D_MODEL), F32),
                        pltpu.VMEM((COMBINE_SLOTS, tm * ROW_TILES, LANES), y.dtype),
                        pltpu.VMEM((COMBINE_SLOTS, tm * ROW_TILES, LANES), y.dtype),
                        pltpu.SemaphoreType.DMA((COMBINE_SLOTS, 3))],
        compiler_params=_cparams("arbitrary"),
        name="moe_combine_ple",
    )(rt, p2, h1, y, gp, wpg, wp, gf)


def _lane_row(vals, offset):
    return jnp.zeros((1, LANES), F32).at[0, offset:offset + vals.shape[0]].set(vals)


def kernel(x, p, norm_mix, w_in, lru_conv_w, lru_conv_b, lru_wa, lru_ba, lru_wi, lru_bi, lru_lambda,
           lru_out_norm, gdn_conv_w, gdn_a_log, gdn_dt_bias, gdn_out_norm, w_out, norm_ffn,
           w_router_group, b_router_group, w_router_expert, b_router_expert, w_exp_gate, w_exp_up,
           w_exp_down, norm_ple, w_ple_gate, w_ple, norm_final):
    bsz, seq, d = x.shape
    t = bsz * seq
    depth = w_in.shape[0]
    assert depth == 1, "the final norm is fused into the last layer's combine kernel"
    n_blocks = -(-t * TOP_K // MOE_BLOCK) + N_EXPERTS
    row = lambda v: v.reshape(1, -1).astype(F32)
    h = x.reshape(t, d).astype(F32)
    for l in range(depth):
        b_gates = jnp.concatenate([lru_ba[l], lru_bi[l]]).reshape(1, -1)
        proj_gdn, y_lru = _proj_lru(h, row(norm_mix[l]), w_in[l], lru_conv_w[l], row(lru_conv_b[l]), lru_wa[l],
                                    lru_wi[l], b_gates, row(lru_lambda[l]), row(lru_out_norm[l]), seq, 512)
        cw = gdn_conv_w[l].reshape(CONV_W, 3, D_GDN).transpose(1, 0, 2)
        w_r = jnp.pad(jnp.concatenate([w_router_group[l], w_router_expert[l]], axis=1),
                      ((0, 0), (0, LANES - N_GROUPS - N_EXPERTS)))
        w_r_hi = w_r.astype(BF16)
        w_r_lo = (w_r - w_r_hi.astype(F32)).astype(BF16)
        w_r = jnp.concatenate([w_r_hi, w_r_hi, w_r_lo], axis=0)
        b_r = _lane_row(jnp.concatenate([b_router_group[l], b_router_expert[l]]), 0)
        h1, xn2, rt, tile_counts = _gdn_router(
            proj_gdn, cw, _lane_row(gdn_a_log[l], GDN_HEADS), _lane_row(gdn_dt_bias[l], GDN_HEADS),
            row(gdn_out_norm[l]), h, y_lru, w_out[l].astype(BF16), row(norm_ffn[l]), w_r, b_r, seq, 512)
        dest8, blocks = _slots(rt, tile_counts, n_blocks, 2048)
        dest = dest8[:TOP_K]
        block_expert, n_valid, next_expert = blocks[:, 0], blocks[:, 1], blocks[:, 2]
        tiles = lambda a: a.reshape(-1, ROW_TILES, LANES)
        xs = _sc_scatter_rows(tiles(xn2), dest, n_blocks * MOE_BLOCK)
        ys = _experts(block_expert, n_valid, next_expert, xs.reshape(-1, LANES), w_exp_gate[l], w_exp_up[l],
                      w_exp_down[l])
        y = _sc_gather_rows(tiles(ys), dest.reshape(-1))
        h = _combine(h1, rt, p[l].reshape(t, -1).astype(F32), y.reshape(-1, LANES), row(norm_ple[l]),
                     w_ple_gate[l].astype(BF16), w_ple[l].astype(BF16), row(norm_final), 1024)
    return h.reshape(bsz, seq, d).astype(x.dtype)
```
